```python
import jax, jax.numpy as jnp
from jax import lax
import numpy as np

D_MODEL = 1024
BATCH = 8
SEQ = 4096
DEPTH = 1
DEC_BATCH = 32
DEC_SEQ = 1
PAST_LEN = 16384
PAGE_SIZE = 128

HEAD_DIM = 64
ATTN_GROUPS = ((128, 1), (512, 4), (2048, 16))
N_GROUPS = len(ATTN_GROUPS)
HEADS_PER_GROUP = (D_MODEL // 2) // HEAD_DIM
N_ATTN_HEADS = N_GROUPS * HEADS_PER_GROUP
ATTN_WIDTH = HEADS_PER_GROUP * HEAD_DIM
QKV_WIDTH = N_ATTN_HEADS * HEAD_DIM
ATTN_BLOCK = 128
POOL_WIDTH = D_MODEL // 2
POOL_WINDOWS = (2, 4, 8, 16)
POOL_GROUP = POOL_WIDTH // len(POOL_WINDOWS)
POOL_STATE = max(POOL_WINDOWS) - 1
IN_COLS = 3 * QKV_WIDTH + POOL_WIDTH + 2 * D_MODEL
N_BUCKETS = 32
REL_MAX_DIST = 2048
N_EXPERTS = 256
TOP_K = 8
N_EXPERT_GROUPS = 8
TOPK_GROUPS = 4
EXPERT_HIDDEN = D_MODEL // 4
SHARED_HIDDEN = D_MODEL // 4
ROUTED_SCALE = 2.5
MOE_BLOCK = 128
MOE_MIN_BLOCK = 8
EPS = 1e-6

kernel_name = 'hybrid_dilated_attn_pool_moe_adaln_step'


def t5_bucket(dist):
    exact = N_BUCKETS // 2
    d = np.asarray(dist)
    large = exact + (np.log(np.maximum(d, 1) / exact) / np.log(REL_MAX_DIST / exact) * (N_BUCKETS - exact)).astype(np.int32)
    large = np.minimum(large, N_BUCKETS - 1)
    return np.where(d < exact, d, large).astype(np.int32)


def group_bias(rel_bias, gi):
    window, dil = ATTN_GROUPS[gi]
    bkt = t5_bucket(np.arange(window // dil + 1) * dil)
    return rel_bias[bkt, gi * HEADS_PER_GROUP:(gi + 1) * HEADS_PER_GROUP].T.astype(jnp.float32)


def rmsnorm(x, g):
    xf = x.astype(jnp.float32)
    y = xf * lax.rsqrt(jnp.mean(xf * xf, axis=-1, keepdims=True) + EPS) * g.astype(jnp.float32)
    return y.astype(x.dtype)


def qk_norm(t, gain):
    tf = t.astype(jnp.float32)
    y = tf * lax.rsqrt(jnp.mean(tf * tf, axis=-1, keepdims=True) + EPS) * gain.astype(jnp.float32)[:, None, :]
    return y.astype(t.dtype)


def adaln(c, w, b):
    mod = jax.nn.silu(c) @ w + b
    return tuple(m[:, None, :] for m in jnp.split(mod, 6, axis=-1))


def modulate(h, shift, scale):
    return h * (1 + scale) + shift


def attn_prompt(q, k, v, bias_j, dil):
    B, S, H, E = q.shape
    n = bias_j.shape[1] - 1
    blk = ATTN_BLOCK
    span = dil * blk
    s_pad = -(-S // span) * span
    L = s_pad // dil
    nb = L // blk

    def to_classes(t):
        t = jnp.pad(t.astype(jnp.float32), ((0, 0), (0, s_pad - S), (0, 0), (0, 0)))
        return t.reshape(B, L, dil, H, E).transpose(0, 2, 1, 3, 4).reshape(B, dil, nb, blk, H, E)

    def with_prev(t):
        prev = jnp.pad(t, ((0, 0), (0, 0), (1, 0), (0, 0), (0, 0), (0, 0)))[:, :, :-1]
        return jnp.concatenate([prev, t], axis=3)

    qc = to_classes(q)
    kk = with_prev(to_classes(k))
    vv = with_prev(to_classes(v))
    s = jnp.einsum('brnqhe,brnkhe->brnhqk', qc, kk) * (E ** -0.5)
    a = np.arange(blk)[:, None]
    c = np.arange(2 * blk)[None, :]
    j = blk + a - c
    band = (j >= 0) & (j <= n)
    starts = np.arange(nb)[:, None, None] * blk
    valid = band[None] & (starts + a[None] - j[None] >= 0)
    bias = bias_j[:, np.clip(j, 0, n)]
    s = jnp.where(valid[None, None, :, None], s + bias, -jnp.inf)
    m = jnp.max(s, axis=-1, keepdims=True)
    p = jnp.exp(s - m)
    l = jnp.sum(p, axis=-1)
    o = jnp.einsum('brnhqk,brnkhe->brnqhe', p, vv) / jnp.swapaxes(l, -1, -2)[..., None]
    lse = jnp.swapaxes(m[..., 0] + jnp.log(l), -1, -2)
    o = o.reshape(B, dil, L, H, E).transpose(0, 2, 1, 3, 4).reshape(B, s_pad, H, E)[:, :S]
    lse = lse.reshape(B, dil, L, H).transpose(0, 2, 1, 3).reshape(B, s_pad, H)[:, :S]
    return o, lse


def attn_sample(q, k_new, v_new, k_buf, v_buf, bias_j, window, dil):
    R = k_buf.shape[1]
    T = q.shape[1]
    n = bias_j.shape[1] - 1
    kext = jnp.concatenate([k_buf, k_new.astype(k_buf.dtype)], axis=1)
    vext = jnp.concatenate([v_buf, v_new.astype(v_buf.dtype)], axis=1)
    idx = R + np.arange(T)[:, None] - dil * np.arange(n + 1)[None, :]
    valid = idx >= 0
    idx_c = np.maximum(idx, 0)
    kg = kext[:, idx_c].astype(jnp.float32)
    vg = vext[:, idx_c].astype(jnp.float32)
    s = jnp.einsum('bthe,btjhe->bthj', q.astype(jnp.float32), kg) * (q.shape[-1] ** -0.5) + bias_j
    s = jnp.where(valid[None, :, None, :], s, -jnp.inf)
    m = jnp.max(s, axis=-1, keepdims=True)
    p = jnp.exp(s - m)
    l = jnp.sum(p, axis=-1)
    o = jnp.einsum('bthj,btjhe->bthe', p, vg) / l[..., None]
    lse = m[..., 0] + jnp.log(l)
    keep = min(window, R + T)
    return o, lse, kext[:, -keep:], vext[:, -keep:]


def combine_groups(outs, lses):
    w = jax.nn.softmax(jnp.stack(lses), axis=0)
    return jnp.einsum('gbthe,gbth->bthe', jnp.stack(outs), w)


def pool_branch(u_prev, u, pos0, pool_w, pool_scale):
    B, T, C = u.shape
    P = u_prev.shape[1]
    ext = jnp.concatenate([u_prev, u], axis=1)
    csz = jnp.pad(jnp.cumsum(ext.astype(jnp.float32), axis=1), ((0, 0), (1, 0), (0, 0)))
    pos = pos0 + jnp.arange(T)
    means = []
    for gi, w in enumerate(POOL_WINDOWS):
        sl = slice(gi * POOL_GROUP, (gi + 1) * POOL_GROUP)
        tot = csz[:, P + 1:P + 1 + T, sl] - csz[:, P + 1 - w:P + 1 - w + T, sl]
        cnt = jnp.minimum(pos + 1, w).astype(jnp.float32)
        means.append(tot / cnt[None, :, None])
    pooled = jnp.concatenate(means, axis=-1) - u.astype(jnp.float32)
    mixed = jnp.einsum('btgc,gcd->btgd', pooled.reshape(B, T, len(POOL_WINDOWS), POOL_GROUP),
                       pool_w.astype(jnp.float32)).reshape(B, T, C) * pool_scale.astype(jnp.float32)
    return mixed.astype(u.dtype), ext[:, -P:]


def token_mixer(h, pos0, k_bufs, v_bufs, pool_prev, w_in, q_gain, k_gain, bias_js, pool_w, pool_scale, w_br_a, w_br_b, w_out):
    B, T, _ = h.shape
    z = h @ w_in
    shp = (B, T, N_GROUPS, HEADS_PER_GROUP, HEAD_DIM)
    q = qk_norm(z[..., :QKV_WIDTH].reshape(shp), q_gain)
    k = qk_norm(z[..., QKV_WIDTH:2 * QKV_WIDTH].reshape(shp), k_gain)
    v = z[..., 2 * QKV_WIDTH:3 * QKV_WIDTH].reshape(shp)
    u = z[..., 3 * QKV_WIDTH:3 * QKV_WIDTH + POOL_WIDTH]
    gates = z[..., 3 * QKV_WIDTH + POOL_WIDTH:]
    outs, lses, new_k, new_v = [], [], [], []
    for gi, (window, dil) in enumerate(ATTN_GROUPS):
        if k_bufs is None:
            o, lse = attn_prompt(q[:, :, gi], k[:, :, gi], v[:, :, gi], bias_js[gi], dil)
            keep = min(window, T)
            nk, nv = k[:, T - keep:, gi], v[:, T - keep:, gi]
        else:
            o, lse, nk, nv = attn_sample(q[:, :, gi], k[:, :, gi], v[:, :, gi], k_bufs[gi], v_bufs[gi], bias_js[gi], window, dil)
        outs.append(o)
        lses.append(lse)
        new_k.append(nk)
        new_v.append(nv)
    o_a = combine_groups(outs, lses).reshape(B, T, ATTN_WIDTH).astype(h.dtype)
    if pool_prev is None:
        pool_prev = jnp.zeros((B, POOL_STATE, POOL_WIDTH), h.dtype)
    y_b, new_pool = pool_branch(pool_prev, u, pos0, pool_w, pool_scale)
    g_a, g_b = jnp.split(gates, 2, axis=-1)
    merged = jax.nn.sigmoid(g_a) * (o_a @ w_br_a) + jax.nn.sigmoid(g_b) * (y_b @ w_br_b)
    return merged @ w_out, new_k, new_v, new_pool


def swiglu(h, wg, wu, wd):
    return (jax.nn.silu(h @ wg) * (h @ wu)) @ wd


def routed_experts(h, idx, wts, w_gate, w_up, w_down):
    n_tok, d = h.shape
    n_exp = w_gate.shape[0]
    n_assign = n_tok * TOP_K
    blk = MOE_BLOCK if n_assign >= MOE_BLOCK * n_exp else MOE_MIN_BLOCK
    n_blocks = -(-n_assign // blk) + n_exp
    flat_e = idx.reshape(-1)
    order = jnp.argsort(flat_e)
    se = flat_e[order]
    stok = (order // TOP_K).astype(jnp.int32)
    sw = wts.reshape(-1)[order]
    counts = jnp.bincount(flat_e, length=n_exp)
    offs = jnp.cumsum(counts) - counts
    pcounts = (counts + blk - 1) // blk * blk
    pends = jnp.cumsum(pcounts)
    dest = pends[se] - pcounts[se] + jnp.arange(n_assign) - offs[se]
    slot_tok = jnp.full((n_blocks * blk,), n_tok, jnp.int32).at[dest].set(stok)
    slot_w = jnp.zeros((n_blocks * blk,), jnp.float32).at[dest].set(sw)
    blk_exp = jnp.minimum(jnp.searchsorted(pends, jnp.arange(n_blocks) * blk, side='right'), n_exp - 1)
    h_pad = jnp.concatenate([h, jnp.zeros((1, d), h.dtype)], axis=0)

    def expert_block(args):
        tok, wb, e = args
        xb = h_pad[tok]
        y = swiglu(xb, w_gate[e], w_up[e], w_down[e])
        return y.astype(jnp.float32) * wb[:, None]

    yb = lax.map(expert_block, (slot_tok.reshape(n_blocks, blk), slot_w.reshape(n_blocks, blk), blk_exp))
    return jax.ops.segment_sum(yb.reshape(-1, d), slot_tok, num_segments=n_tok + 1)[:n_tok]


def moe(h, router_w, router_bias, w_gate, w_up, w_down, s_gate, s_up, s_down):
    n_tok = h.shape[0]
    scores = jax.nn.sigmoid(h.astype(jnp.float32) @ router_w.astype(jnp.float32))
    sel = scores + router_bias.astype(jnp.float32)
    grp_score = jnp.sum(lax.top_k(sel.reshape(n_tok, N_EXPERT_GROUPS, -1), 2)[0], axis=-1)
    top_g = lax.top_k(grp_score, TOPK_GROUPS)[1]
    gmask = jnp.any(top_g[..., None] == jnp.arange(N_EXPERT_GROUPS), axis=-2)
    masked = jnp.where(jnp.repeat(gmask, N_EXPERTS // N_EXPERT_GROUPS, axis=1), sel, -jnp.inf)
    idx = lax.top_k(masked, TOP_K)[1]
    wts = jnp.take_along_axis(scores, idx, axis=1)
    wts = wts / jnp.sum(wts, axis=-1, keepdims=True) * ROUTED_SCALE
    routed = routed_experts(h, idx, wts, w_gate, w_up, w_down)
    shared = swiglu(h, s_gate, s_up, s_down).astype(jnp.float32)
    return (routed + shared).astype(h.dtype)


def setup_inputs(seed: int = 0) -> dict:
    key = jax.random.key(seed)
    ks = iter(jax.random.split(key, 40))

    def nrm(shape, scale=1.0):
        return jax.random.normal(next(ks), shape, jnp.float32) * scale

    rows = [min(w, PAST_LEN) for w, _ in ATTN_GROUPS]

    def kv(r):
        return (DEPTH, DEC_BATCH, r, HEADS_PER_GROUP, HEAD_DIM)

    return {
        'x_prompt': nrm((BATCH, SEQ, D_MODEL)),
        'x_sample': nrm((DEC_BATCH, DEC_SEQ, D_MODEL)),
        'cache_k_w128': nrm(kv(rows[0])),
        'cache_v_w128': nrm(kv(rows[0])),
        'cache_k_w512': nrm(kv(rows[1])),
        'cache_v_w512': nrm(kv(rows[1])),
        'cache_k_w2048': nrm(kv(rows[2])),
        'cache_v_w2048': nrm(kv(rows[2])),
        'state_pool': nrm((DEPTH, DEC_BATCH, POOL_STATE, POOL_WIDTH)),
        'c_prompt': nrm((BATCH, D_MODEL)),
        'c_sample': nrm((DEC_BATCH, D_MODEL)),
        'ada_w': nrm((DEPTH, D_MODEL, 6 * D_MODEL), 0.5 * D_MODEL ** -0.5),
        'ada_b': nrm((DEPTH, 6 * D_MODEL), 0.01),
        'norm1': 1.0 + nrm((DEPTH, D_MODEL), 0.1),
        'norm2': 1.0 + nrm((DEPTH, D_MODEL), 0.1),
        'w_in': nrm((DEPTH, D_MODEL, IN_COLS), D_MODEL ** -0.5),
        'q_gain': 1.0 + nrm((DEPTH, N_GROUPS, HEAD_DIM), 0.1),
        'k_gain': 1.0 + nrm((DEPTH, N_GROUPS, HEAD_DIM), 0.1),
        'rel_bias': nrm((N_BUCKETS, N_ATTN_HEADS), 0.5),
        'pool_w': nrm((DEPTH, len(POOL_WINDOWS), POOL_GROUP, POOL_GROUP), POOL_GROUP ** -0.5),
        'pool_scale': 1.0 + nrm((DEPTH, POOL_WIDTH), 0.1),
        'w_br_a': nrm((DEPTH, ATTN_WIDTH, D_MODEL), ATTN_WIDTH ** -0.5),
        'w_br_b': nrm((DEPTH, POOL_WIDTH, D_MODEL), POOL_WIDTH ** -0.5),
        'w_out': nrm((DEPTH, D_MODEL, D_MODEL), D_MODEL ** -0.5),
        'router_w': nrm((DEPTH, D_MODEL, N_EXPERTS), D_MODEL ** -0.5),
        'router_bias': nrm((DEPTH, N_EXPERTS), 0.01),
        'exp_w_gate': nrm((DEPTH, N_EXPERTS, D_MODEL, EXPERT_HIDDEN), D_MODEL ** -0.5),
        'exp_w_up': nrm((DEPTH, N_EXPERTS, D_MODEL, EXPERT_HIDDEN), D_MODEL ** -0.5),
        'exp_w_down': nrm((DEPTH, N_EXPERTS, EXPERT_HIDDEN, D_MODEL), EXPERT_HIDDEN ** -0.5),
        'sh_w_gate': nrm((DEPTH, D_MODEL, SHARED_HIDDEN), D_MODEL ** -0.5),
        'sh_w_up': nrm((DEPTH, D_MODEL, SHARED_HIDDEN), D_MODEL ** -0.5),
        'sh_w_down': nrm((DEPTH, SHARED_HIDDEN, D_MODEL), SHARED_HIDDEN ** -0.5),
    }


def reference(x_prompt, x_sample, cache_k_w128, cache_v_w128, cache_k_w512, cache_v_w512, cache_k_w2048, cache_v_w2048,
              state_pool, c_prompt, c_sample, ada_w, ada_b, norm1, norm2, w_in, q_gain, k_gain, rel_bias, pool_w,
              pool_scale, w_br_a, w_br_b, w_out, router_w, router_bias, exp_w_gate, exp_w_up, exp_w_down,
              sh_w_gate, sh_w_up, sh_w_down):
    bias_js = [group_bias(rel_bias, gi) for gi in range(N_GROUPS)]
    k_caches = (cache_k_w128, cache_k_w512, cache_k_w2048)
    v_caches = (cache_v_w128, cache_v_w512, cache_v_w2048)
    xp, xs = x_prompt, x_sample
    n_prompt = xp.shape[0] * xp.shape[1]
    pk = [[] for _ in range(N_GROUPS)]
    pv = [[] for _ in range(N_GROUPS)]
    sk = [[] for _ in range(N_GROUPS)]
    sv = [[] for _ in range(N_GROUPS)]
    ppool, spool = [], []
    for l in range(DEPTH):
        mod_p = adaln(c_prompt, ada_w[l], ada_b[l])
        mod_s = adaln(c_sample, ada_w[l], ada_b[l])
        mix_w = (w_in[l], q_gain[l], k_gain[l], bias_js, pool_w[l], pool_scale[l], w_br_a[l], w_br_b[l], w_out[l])
        hp = modulate(rmsnorm(xp, norm1[l]), mod_p[0], mod_p[1])
        yp, nk, nv, npool = token_mixer(hp, 0, None, None, None, *mix_w)
        for gi in range(N_GROUPS):
            pk[gi].append(nk[gi])
            pv[gi].append(nv[gi])
        ppool.append(npool)
        hs = modulate(rmsnorm(xs, norm1[l]), mod_s[0], mod_s[1])
        ys, nk, nv, npool = token_mixer(hs, PAST_LEN, [c[l] for c in k_caches], [c[l] for c in v_caches],
                                        state_pool[l], *mix_w)
        for gi in range(N_GROUPS):
            sk[gi].append(nk[gi])
            sv[gi].append(nv[gi])
        spool.append(npool)
        xp = xp + mod_p[2] * yp
        xs = xs + mod_s[2] * ys
        h2 = jnp.concatenate([
            modulate(rmsnorm(xp, norm2[l]), mod_p[3], mod_p[4]).reshape(-1, D_MODEL),
            modulate(rmsnorm(xs, norm2[l]), mod_s[3], mod_s[4]).reshape(-1, D_MODEL)], axis=0)
        f = moe(h2, router_w[l], router_bias[l], exp_w_gate[l], exp_w_up[l], exp_w_down[l],
                sh_w_gate[l], sh_w_up[l], sh_w_down[l])
        xp = xp + mod_p[5] * f[:n_prompt].reshape(xp.shape)
        xs = xs + mod_s[5] * f[n_prompt:].reshape(xs.shape)
    p_k_w128, p_k_w512, p_k_w2048 = (jnp.stack(a) for a in pk)
    p_v_w128, p_v_w512, p_v_w2048 = (jnp.stack(a) for a in pv)
    s_k_w128, s_k_w512, s_k_w2048 = (jnp.stack(a) for a in sk)
    s_v_w128, s_v_w512, s_v_w2048 = (jnp.stack(a) for a in sv)
    p_pool = jnp.stack(ppool)
    s_pool = jnp.stack(spool)
    return (xp, xs, p_k_w128, p_v_w128, p_k_w512, p_v_w512, p_k_w2048, p_v_w2048, p_pool,
            s_k_w128, s_v_w128, s_k_w512, s_v_w512, s_k_w2048, s_v_w2048, s_pool)
```

```python
import functools

import numpy as np
import jax
import jax.numpy as jnp
from jax import lax
from jax.experimental import pallas as pl
from jax.experimental.pallas import tpu as pltpu

F32 = jnp.float32
BF16 = jnp.bfloat16
HIGHEST = lax.Precision.HIGHEST

HEAD_DIM = 64
HEADS = 8
GROUP_W = HEADS * HEAD_DIM
ATTN_GROUPS = ((128, 1), (512, 4), (2048, 16))
N_GROUPS = len(ATTN_GROUPS)
ATTN_STEPS = 128
ATTN_BLOCK = 128
POOL_WINDOWS = (2, 4, 8, 16)
POOL_GROUP = 128
POOL_STATE = 15
N_BUCKETS = 32
REL_MAX_DIST = 2048
N_EXPERTS = 256
TOP_K = 8
N_EXPERT_GROUPS = 8
TOPK_GROUPS = 4
ROUTED_SCALE = 2.5
EPS = 1e-6
NEG = -1e30

LANES = 128
MXU_DIM = 256
VMEM_LIMIT = 56 * 1024 * 1024

ROW_TILE = 256
EXPERT_TILE = 256
COMBINE_TILE = 128
SAMPLE_PAD = 256


def _sigmoid(x):
    return 1.0 / (1.0 + jnp.exp(-x))


def _params(sem):
    return pltpu.CompilerParams(dimension_semantics=sem, vmem_limit_bytes=VMEM_LIMIT)


def _ada_body(c_ref, w_ref, b_ref, o_ref):
    c = c_ref[...]
    s = c * _sigmoid(c)
    o_ref[...] = jnp.dot(s, w_ref[...], precision=HIGHEST, preferred_element_type=F32) + b_ref[...]


def _ada_call(c, w, b):
    n, d = c.shape
    cols = w.shape[1]
    tn = 512
    return pl.pallas_call(
        _ada_body,
        grid=(cols // tn,),
        in_specs=[pl.BlockSpec((n, d), lambda j: (0, 0)),
                  pl.BlockSpec((d, tn), lambda j: (0, j)),
                  pl.BlockSpec((1, tn), lambda j: (0, j))],
        out_specs=pl.BlockSpec((n, tn), lambda j: (0, j)),
        out_shape=jax.ShapeDtypeStruct((n, cols), F32),
        compiler_params=_params(("arbitrary",)),
        name="adaln",
    )(c, w, b.reshape(1, cols))


def _inproj_body(x_ref, shift_ref, scale_ref, g_ref, w_ref, qg_ref, kg_ref, bd_ref,
                 q_ref, k_ref, v_ref, u_ref, gt_ref):
    x = x_ref[0]
    ms = jnp.mean(x * x, axis=-1, keepdims=True)
    h = x * lax.rsqrt(ms + EPS) * g_ref[...]
    h = h * (1.0 + scale_ref[0]) + shift_ref[0]
    hb = h.astype(BF16)
    qkv = N_GROUPS * GROUP_W
    n_chunks = w_ref.shape[1] // GROUP_W
    for c in range(n_chunks):
        z = jnp.dot(hb, w_ref[:, c * GROUP_W:(c + 1) * GROUP_W], preferred_element_type=F32)
        if c < 2 * N_GROUPS:
            zz = (z * z).astype(BF16)
            hms = jnp.concatenate(
                [jnp.dot(zz[:, s:s + MXU_DIM], bd_ref[...], preferred_element_type=F32)
                 for s in range(0, GROUP_W, MXU_DIM)], axis=1)
            g = c % N_GROUPS
            gain = (qg_ref if c < N_GROUPS else kg_ref)[:, g * GROUP_W:(g + 1) * GROUP_W]
            y = z * lax.rsqrt(hms + EPS) * gain
            if c < N_GROUPS:
                q_ref[0, :, g * GROUP_W:(g + 1) * GROUP_W] = y.astype(BF16)
            else:
                k_ref[0, :, g * GROUP_W:(g + 1) * GROUP_W] = y
        elif c < 3 * N_GROUPS:
            g = c - 2 * N_GROUPS
            v_ref[0, :, g * GROUP_W:(g + 1) * GROUP_W] = z
        elif c == 3 * N_GROUPS:
            u_ref[0] = z
        else:
            g = c - 3 * N_GROUPS - 1
            gt_ref[0, :, g * GROUP_W:(g + 1) * GROUP_W] = z.astype(BF16)
    del qkv


def _mod_spec(mod, tm, d):
    if mod.shape[1] == 1:
        return pl.BlockSpec((1, 1, d), lambda b, i: (b, 0, 0))
    return pl.BlockSpec((1, tm, d), lambda b, i: (b, i, 0))


def _inproj_call(x, shift, scale, g1, w_bf, qg, kg, bd):
    bsz, s, d = x.shape
    tm = min(ROW_TILE, s)
    qkv = N_GROUPS * GROUP_W
    n_gate = w_bf.shape[1] - 3 * qkv - GROUP_W
    const = lambda b, i: (0, 0)
    row = lambda width: pl.BlockSpec((1, tm, width), lambda b, i: (b, i, 0))
    return pl.pallas_call(
        _inproj_body,
        grid=(bsz, s // tm),
        in_specs=[row(d), _mod_spec(shift, tm, d), _mod_spec(scale, tm, d),
                  pl.BlockSpec((1, d), const),
                  pl.BlockSpec(w_bf.shape, const),
                  pl.BlockSpec((1, qkv), const), pl.BlockSpec((1, qkv), const),
                  pl.BlockSpec((MXU_DIM, MXU_DIM), const)],
        out_specs=[row(qkv), row(qkv), row(qkv), row(GROUP_W), row(n_gate)],
        out_shape=[jax.ShapeDtypeStruct((bsz, s, qkv), BF16),
                   jax.ShapeDtypeStruct((bsz, s, qkv), F32),
                   jax.ShapeDtypeStruct((bsz, s, qkv), F32),
                   jax.ShapeDtypeStruct((bsz, s, GROUP_W), F32),
                   jax.ShapeDtypeStruct((bsz, s, n_gate), BF16)],
        compiler_params=_params(("arbitrary", "arbitrary")),
        name="inproj",
    )(x, shift, scale, g1, w_bf, qg, kg, bd)


def _attn_body(q_ref, kp_ref, kc_ref, vp_ref, vc_ref, bias_ref, o_ref, lse_ref):
    first = pl.program_id(2) == 0
    blk = ATTN_BLOCK
    qf = q_ref[0].astype(F32)
    k2 = jnp.concatenate([kp_ref[0], kc_ref[0]], axis=0).astype(BF16)
    v2 = jnp.concatenate([vp_ref[0], vc_ref[0]], axis=0).astype(BF16)
    hq = MXU_DIM // HEAD_DIM
    lane_head = lax.broadcasted_iota(jnp.int32, (blk, MXU_DIM), 1) // HEAD_DIM
    col = lax.broadcasted_iota(jnp.int32, (hq * blk, 2 * blk), 1)
    lane_l = lax.broadcasted_iota(jnp.int32, (blk, LANES), 1)
    lse_tile = jnp.zeros((blk, LANES), F32)
    outs = []
    for sl in range(GROUP_W // MXU_DIM):
        cs = slice(sl * MXU_DIM, (sl + 1) * MXU_DIM)
        qq = qf[:, cs]
        qs = jnp.concatenate([jnp.where(lane_head == h, qq, 0.0) for h in range(hq)], axis=0).astype(BF16)
        s = lax.dot_general(qs, k2[:, cs], (((1,), (1,)), ((), ())), preferred_element_type=F32)
        s = s + bias_ref[sl]
        s = jnp.where(jnp.logical_and(first, col < blk), NEG, s)
        m = jnp.max(s, axis=-1, keepdims=True)
        p = jnp.exp(s - m)
        l = jnp.sum(p, axis=-1, keepdims=True)
        o4 = jnp.dot(p.astype(BF16), v2[:, cs], preferred_element_type=F32) / l
        oq = jnp.zeros((blk, MXU_DIM), F32)
        lse = m + jnp.log(l)
        for h in range(hq):
            oq = jnp.where(lane_head == h, o4[h * blk:(h + 1) * blk], oq)
            lse_tile = jnp.where(lane_l == sl * hq + h, lse[h * blk:(h + 1) * blk], lse_tile)
        outs.append(oq)
    o_ref[0] = jnp.concatenate(outs, axis=1).astype(BF16)
    lse_ref[0] = lse_tile


def _attn_call(q, k, v, bias_tab, g, dil):
    bsz, s, qkv = q.shape
    blk = ATTN_BLOCK
    ll = s // dil
    nb = ll // blk
    ng = qkv // GROUP_W
    view = lambda t: t.reshape(bsz, ll, dil * t.shape[-1])
    cur = pl.BlockSpec((1, blk, GROUP_W), lambda b, r, i: (b, i, r * ng + g))
    prev = pl.BlockSpec((1, blk, GROUP_W), lambda b, r, i: (b, jnp.maximum(i - 1, 0), r * ng + g))
    o, lse = pl.pallas_call(
        _attn_body,
        grid=(bsz, dil, nb),
        in_specs=[cur, prev, cur, prev, cur,
                  pl.BlockSpec(bias_tab.shape, lambda b, r, i: (0, 0, 0))],
        out_specs=[pl.BlockSpec((1, blk, GROUP_W), lambda b, r, i: (b, i, r)),
                   pl.BlockSpec((1, blk, LANES), lambda b, r, i: (b, i, r))],
        out_shape=[jax.ShapeDtypeStruct((bsz, ll, dil * GROUP_W), BF16),
                   jax.ShapeDtypeStruct((bsz, ll, dil * LANES), F32)],
        compiler_params=_params(("arbitrary", "arbitrary", "arbitrary")),
        name=f"attn_g{g}",
    )(view(q), view(k), view(k), view(v), view(v), bias_tab)
    return o.reshape(bsz, s, GROUP_W), lse.reshape(bsz, s, LANES)


def _decode_body(q_ref, kn_ref, vn_ref, u_ref, pool_ref, kc0, vc0, kc1, vc1, kc2, vc2,
                 bdec_ref, bnew_ref, ho_ref, ex_ref, pw_ref, oa_ref, pooled_ref):
    kcs = (kc0, kc1, kc2)
    vcs = (vc0, vc1, vc2)
    q = q_ref[0].astype(F32)
    kn = kn_ref[0]
    vn = vn_ref[0]
    ho = ho_ref[...]
    ex = ex_ref[...]
    n = ATTN_STEPS
    outs, lses = [], []
    for g in range(N_GROUPS):
        cs = slice(g * GROUP_W, (g + 1) * GROUP_W)
        qg = q[:, cs]
        kc = kcs[g][0]
        vc = vcs[g][0]
        prod = jnp.concatenate([kc * qg, jnp.broadcast_to(kn[:, cs] * qg, (8, GROUP_W))], axis=0)
        sc = jnp.dot(prod, ho, precision=HIGHEST, preferred_element_type=F32)
        s_old = sc[:n] + bdec_ref[g]
        s_new = sc[n:n + 1] + bnew_ref[g]
        m = jnp.maximum(jnp.max(s_old, axis=0, keepdims=True), s_new)
        p_old = jnp.exp(s_old - m)
        p_new = jnp.exp(s_new - m)
        l = jnp.sum(p_old, axis=0, keepdims=True) + p_new
        stack = jnp.concatenate([p_old, jnp.broadcast_to(p_new, (8, LANES)),
                                 jnp.broadcast_to(l, (8, LANES))], axis=0)
        wide = jnp.dot(stack, ex, precision=HIGHEST, preferred_element_type=F32)
        o = (jnp.sum(wide[:n] * vc, axis=0, keepdims=True) + wide[n:n + 1] * vn[:, cs]) / wide[n + 8:n + 9]
        outs.append(o)
        lses.append(m + jnp.log(l))
    mx = jnp.maximum(jnp.maximum(lses[0], lses[1]), lses[2])
    es = [jnp.exp(t - mx) for t in lses]
    den = es[0] + es[1] + es[2]
    wstack = jnp.concatenate([jnp.broadcast_to(e / den, (8, LANES)) for e in es], axis=0)
    wwide = jnp.dot(wstack, ex, precision=HIGHEST, preferred_element_type=F32)
    oa = wwide[0:1] * outs[0] + wwide[8:9] * outs[1] + wwide[16:17] * outs[2]
    oa_ref[0] = oa
    u = u_ref[0]
    stored = jnp.sum(pool_ref[0] * pw_ref[0:POOL_STATE, :], axis=0, keepdims=True)
    pooled_ref[0] = stored + u * pw_ref[POOL_STATE:POOL_STATE + 1, :] - u


def _decode_call(q, kn, vn, u, pool_prev, k_caches, v_caches, bdec, bnew, ho, ex, pw):
    db = pool_prev.shape[0]
    qkv = q.shape[-1]
    row = lambda t: pl.BlockSpec((1, 1, t.shape[-1]), lambda b: (b, 0, 0))
    const = lambda t: pl.BlockSpec(t.shape, lambda b: (0,) * t.ndim)
    cache_specs, cache_args = [], []
    for g, (_, dil) in enumerate(ATTN_GROUPS):
        for c in (k_caches[g], v_caches[g]):
            rows = c.shape[1]
            cache_args.append(c.reshape(db, rows // dil, dil * GROUP_W))
            cache_specs.append(pl.BlockSpec((1, ATTN_STEPS, GROUP_W), lambda b: (b, 0, 0)))
    q3, kn3, vn3, u3 = (t.reshape(t.shape[0], 1, t.shape[-1]) for t in (q, kn, vn, u))
    del qkv
    return pl.pallas_call(
        _decode_body,
        grid=(db,),
        in_specs=[row(q3), row(kn3), row(vn3), row(u3),
                  pl.BlockSpec((1,) + pool_prev.shape[1:], lambda b: (b, 0, 0))] + cache_specs +
                 [const(bdec), const(bnew), const(ho), const(ex), const(pw)],
        out_specs=[pl.BlockSpec((1, 1, GROUP_W), lambda b: (b, 0, 0)),
                   pl.BlockSpec((1, 1, GROUP_W), lambda b: (b, 0, 0))],
        out_shape=[jax.ShapeDtypeStruct((db, 1, GROUP_W), F32),
                   jax.ShapeDtypeStruct((db, 1, GROUP_W), F32)],
        compiler_params=_params(("arbitrary",)),
        name="decode_mix",
    )(q3, kn3, vn3, u3, pool_prev, *cache_args, bdec, bnew, ho, ex, pw)


def _expand_heads(w, ex_bf):
    hi = w.astype(BF16)
    lo = (w - hi.astype(F32)).astype(BF16)
    return (jnp.dot(hi, ex_bf, preferred_element_type=F32) +
            jnp.dot(lo, ex_bf, preferred_element_type=F32))


def _merge_tail(x, oa, pooled, gates, gate_msa, shift2, scale2, g2, wa, wb, wo, pw_ref, ps,
                x1_ref, h2_ref):
    mixed = jnp.concatenate(
        [jnp.dot(pooled[:, j * POOL_GROUP:(j + 1) * POOL_GROUP].astype(BF16), pw_ref[j],
                 preferred_element_type=F32) for j in range(len(POOL_WINDOWS))], axis=1) * ps
    d = x.shape[-1]
    ga = gates[:, :d].astype(F32)
    gb = gates[:, d:].astype(F32)
    merged = (_sigmoid(ga) * jnp.dot(oa.astype(BF16), wa, preferred_element_type=F32) +
              _sigmoid(gb) * jnp.dot(mixed.astype(BF16), wb, preferred_element_type=F32))
    y = jnp.dot(merged.astype(BF16), wo, preferred_element_type=F32)
    x1 = x + gate_msa * y
    x1_ref[0] = x1
    ms = jnp.mean(x1 * x1, axis=-1, keepdims=True)
    h2 = x1 * lax.rsqrt(ms + EPS) * g2
    h2_ref[0] = h2 * (1.0 + scale2) + shift2


def _merge_prompt_body(x_ref, gt_ref, o0, o1, o2, l0, l1, l2, u_ref, uh_ref,
                       gm_ref, sh_ref, sc_ref, g2_ref, wa_ref, wb_ref, wo_ref, pw_ref, ps_ref, ex_ref,
                       x1_ref, h2_ref):
    i = pl.program_id(1)
    tm = x_ref.shape[1]
    ls = (l0[0], l1[0], l2[0])
    mx = jnp.maximum(jnp.maximum(ls[0], ls[1]), ls[2])
    es = [jnp.exp(t - mx) for t in ls]
    den = es[0] + es[1] + es[2]
    ex = ex_ref[...]
    oa = jnp.zeros((tm, GROUP_W), F32)
    for e, o in zip(es, (o0, o1, o2)):
        oa = oa + _expand_heads(e / den, ex) * o[0].astype(F32)
    u = u_ref[0]
    halo = jnp.where(i == 0, 0.0, uh_ref[0])
    ext = jnp.concatenate([halo, u], axis=0)
    hw = halo.shape[0]
    acc = ext
    parts = []
    for j, w in enumerate(POOL_WINDOWS):
        acc = acc[:, POOL_GROUP * (1 if j else 0):]
        acc = acc + pltpu.roll(acc, w // 2, axis=0)
        parts.append(acc[hw:, :POOL_GROUP])
    tot = jnp.concatenate(parts, axis=1)
    lane = lax.broadcasted_iota(jnp.int32, (tm, GROUP_W), 1)
    wcol = jnp.left_shift(2, lane // POOL_GROUP)
    pos = i * tm + lax.broadcasted_iota(jnp.int32, (tm, GROUP_W), 0)
    cnt = jnp.minimum(pos + 1, wcol).astype(F32)
    pooled = tot / cnt - u
    _merge_tail(x_ref[0], oa, pooled, gt_ref[0], gm_ref[0], sh_ref[0], sc_ref[0], g2_ref[...],
                wa_ref[...], wb_ref[...], wo_ref[...], pw_ref, ps_ref[...], x1_ref, h2_ref)


def _merge_sample_body(x_ref, gt_ref, oa_ref, pooled_ref,
                       gm_ref, sh_ref, sc_ref, g2_ref, wa_ref, wb_ref, wo_ref, pw_ref, ps_ref,
                       x1_ref, h2_ref):
    _merge_tail(x_ref[0], oa_ref[0], pooled_ref[0], gt_ref[0], gm_ref[0], sh_ref[0], sc_ref[0], g2_ref[...],
                wa_ref[...], wb_ref[...], wo_ref[...], pw_ref, ps_ref[...], x1_ref, h2_ref)


def _merge_call(x, gates, mixer_inputs, mods, g2, wa, wb, wo, pw, ps, ex_bf, prompt):
    bsz, s, d = x.shape
    tm = min(ROW_TILE, s)
    row = lambda t: pl.BlockSpec((1, tm, t.shape[-1]), lambda b, i: (b, i, 0))
    const = lambda t: pl.BlockSpec(t.shape, lambda b, i: (0,) * t.ndim)
    gm, sh, sc = mods
    in_specs = [row(x), row(gates)]
    args = [x, gates]
    if prompt:
        outs, lses, u = mixer_inputs
        hw = 16
        in_specs += [row(t) for t in outs] + [row(t) for t in lses] + [
            row(u), pl.BlockSpec((1, hw, u.shape[-1]), lambda b, i: (b, jnp.maximum(i * (tm // hw) - 1, 0), 0))]
        args += list(outs) + list(lses) + [u, u]
        body = _merge_prompt_body
    else:
        in_specs += [row(t) for t in mixer_inputs]
        args += list(mixer_inputs)
        body = _merge_sample_body
    in_specs += [_mod_spec(m, tm, d) for m in (gm, sh, sc)]
    in_specs += [const(t) for t in (g2, wa, wb, wo, pw, ps)]
    args += [gm, sh, sc, g2, wa, wb, wo, pw, ps]
    if prompt:
        in_specs.append(const(ex_bf))
        args.append(ex_bf)
    return pl.pallas_call(
        body,
        grid=(bsz, s // tm),
        in_specs=in_specs,
        out_specs=[row(x), row(x)],
        out_shape=[jax.ShapeDtypeStruct((bsz, s, d), F32), jax.ShapeDtypeStruct((bsz, s, d), F32)],
        compiler_params=_params(("arbitrary", "arbitrary")),
        name="merge_prompt" if prompt else "merge_sample",
    )(*args)


def _router_body(h_ref, rw_ref, rb_ref, idx_ref, wt_ref):
    tm = h_ref.shape[0]
    ne = rw_ref.shape[1]
    gsz = ne // N_EXPERT_GROUPS
    logits = jnp.dot(h_ref[...], rw_ref[...], precision=HIGHEST, preferred_element_type=F32)
    scores = _sigmoid(logits)
    sel = scores + rb_ref[...]
    lane = lax.broadcasted_iota(jnp.int32, (tm, ne), 1)
    lanef = lane.astype(F32)
    grp = lane // gsz
    ninf = -jnp.inf
    far = float(ne)

    def first_max(x):
        m = jnp.max(x, axis=-1, keepdims=True)
        at = jnp.min(jnp.where(x == m, lanef, far), axis=-1, keepdims=True)
        return m, at

    gs = []
    for g in range(N_EXPERT_GROUPS):
        xg = jnp.where(grp == g, sel, ninf)
        m1, a1 = first_max(xg)
        m2 = jnp.max(jnp.where(lanef == a1, ninf, xg), axis=-1, keepdims=True)
        gs.append(m1 + m2)
    allowed = jnp.zeros((tm, ne), jnp.bool_)
    for g in range(N_EXPERT_GROUPS):
        ahead = jnp.zeros((tm, 1), F32)
        for o in range(N_EXPERT_GROUPS):
            if o == g:
                continue
            beats = (gs[o] > gs[g]) | ((gs[o] == gs[g]) & (o < g))
            ahead = ahead + beats.astype(F32)
        allowed = allowed | ((grp == g) & (ahead < TOPK_GROUPS))
    masked = jnp.where(allowed, sel, ninf)
    lane_o = lax.broadcasted_iota(jnp.int32, (tm, LANES), 1)
    idx_t = jnp.zeros((tm, LANES), F32)
    wt_t = jnp.zeros((tm, LANES), F32)
    wsum = jnp.zeros((tm, 1), F32)
    for k in range(TOP_K):
        _, at = first_max(masked)
        hit = lanef == at
        wk = jnp.sum(jnp.where(hit, scores, 0.0), axis=-1, keepdims=True)
        masked = jnp.where(hit, ninf, masked)
        idx_t = jnp.where(lane_o == k, at, idx_t)
        wt_t = jnp.where(lane_o == k, wk, wt_t)
        wsum = wsum + wk
    idx_ref[...] = idx_t.astype(jnp.int32)
    wt_ref[...] = wt_t / wsum * ROUTED_SCALE


def _router_call(h2, rw, rb):
    nt, d = h2.shape
    tm = ROW_TILE
    ne = rw.shape[1]
    return pl.pallas_call(
        _router_body,
        grid=(nt // tm,),
        in_specs=[pl.BlockSpec((tm, d), lambda i: (i, 0)),
                  pl.BlockSpec((d, ne), lambda i: (0, 0)),
                  pl.BlockSpec((1, ne), lambda i: (0, 0))],
        out_specs=[pl.BlockSpec((tm, LANES), lambda i: (i, 0)),
                   pl.BlockSpec((tm, LANES), lambda i: (i, 0))],
        out_shape=[jax.ShapeDtypeStruct((nt, LANES), jnp.int32),
                   jax.ShapeDtypeStruct((nt, LANES), F32)],
        compiler_params=_params(("arbitrary",)),
        name="router",
    )(h2, rw, rb.reshape(1, ne))


def _row_copy(src_hbm, row, dst, j, sem):
    return pltpu.make_async_copy(src_hbm.at[pl.ds(row, 1), :], dst.at[pl.ds(j, 1), :], sem)


def _issue_rows(idx_ref, src_hbm, dst, sem, n):
    def one(j, carry):
        _row_copy(src_hbm, idx_ref[0, 0, j], dst, j, sem).start()
        return carry
    lax.fori_loop(0, n, one, 0, unroll=8)


def _wait_rows(src_hbm, dst, sem, n):
    pltpu.make_async_copy(src_hbm.at[pl.ds(0, n), :], dst, sem).wait()


def _expert_body(be_ref, nu_ref, tok_ref, tokn_ref, h_hbm, wg_ref, wu_ref, wd_ref, y_ref,
                 xbuf, sem, wgb, wub, wdb):
    i = pl.program_id(0)
    n_used = nu_ref[0]
    slot = i % 2
    te = xbuf.shape[1]

    @pl.when(i == 0)
    def _():
        _issue_rows(tok_ref, h_hbm, xbuf.at[0], sem.at[0], te)

    @pl.when(i + 1 < n_used)
    def _():
        _issue_rows(tokn_ref, h_hbm, xbuf.at[1 - slot], sem.at[1 - slot], te)

    @pl.when(i < n_used)
    def _():
        _wait_rows(h_hbm, xbuf.at[slot], sem.at[slot], te)
        changed = jnp.logical_or(i == 0, be_ref[i] != be_ref[jnp.maximum(i - 1, 0)])

        @pl.when(changed)
        def _():
            wgb[...] = wg_ref[0].astype(BF16)
            wub[...] = wu_ref[0].astype(BF16)
            wdb[...] = wd_ref[0].astype(BF16)

        x = xbuf[slot].astype(BF16)
        gt = jnp.dot(x, wgb[...], preferred_element_type=F32)
        up = jnp.dot(x, wub[...], preferred_element_type=F32)
        act = (gt * _sigmoid(gt)) * up
        y_ref[...] = jnp.dot(act.astype(BF16), wdb[...], preferred_element_type=F32)

    @pl.when(i >= n_used)
    def _():
        y_ref[...] = jnp.zeros_like(y_ref)


def _expert_call(blk_exp, n_used, slot_tok, h2, wg, wu, wd):
    n_blocks = blk_exp.shape[0]
    te = EXPERT_TILE
    nt, d = h2.shape
    hid = wg.shape[-1]
    tok3 = slot_tok.reshape(n_blocks, 1, te)
    grid_spec = pltpu.PrefetchScalarGridSpec(
        num_scalar_prefetch=2,
        grid=(n_blocks,),
        in_specs=[
            pl.BlockSpec((1, 1, te), lambda i, be, nu: (i, 0, 0), memory_space=pltpu.SMEM),
            pl.BlockSpec((1, 1, te), lambda i, be, nu: (jnp.minimum(i + 1, n_blocks - 1), 0, 0),
                         memory_space=pltpu.SMEM),
            pl.BlockSpec(memory_space=pl.ANY),
            pl.BlockSpec((1, d, hid), lambda i, be, nu: (be[i], 0, 0)),
            pl.BlockSpec((1, d, hid), lambda i, be, nu: (be[i], 0, 0)),
            pl.BlockSpec((1, hid, d), lambda i, be, nu: (be[i], 0, 0)),
        ],
        out_specs=pl.BlockSpec((te, d), lambda i, be, nu: (i, 0)),
        scratch_shapes=[pltpu.VMEM((2, te, d), F32), pltpu.SemaphoreType.DMA((2,)),
                        pltpu.VMEM((d, hid), BF16), pltpu.VMEM((d, hid), BF16), pltpu.VMEM((hid, d), BF16)],
    )
    return pl.pallas_call(
        _expert_body,
        grid_spec=grid_spec,
        out_shape=jax.ShapeDtypeStruct((n_blocks * te, d), F32),
        compiler_params=_params(("arbitrary",)),
        name="expert_ffn",
    )(blk_exp, n_used, tok3, tok3, h2, wg, wu, wd)


def _combine_body(pos_ref, posn_ref, y_hbm, wt_ref, h_ref, x1_ref, gate_ref, sg_ref, su_ref, sd_ref,
                  out_ref, ybuf, sem):
    nb, ni = pl.num_programs(0), pl.num_programs(1)
    step = pl.program_id(0) * ni + pl.program_id(1)
    slot = step % 2
    tc = h_ref.shape[0]
    rows = tc * TOP_K

    @pl.when(step == 0)
    def _():
        _issue_rows(pos_ref, y_hbm, ybuf.at[0], sem.at[0], rows)

    @pl.when(step + 1 < nb * ni)
    def _():
        _issue_rows(posn_ref, y_hbm, ybuf.at[1 - slot], sem.at[1 - slot], rows)

    _wait_rows(y_hbm, ybuf.at[slot], sem.at[slot], rows)
    wt = wt_ref[...]
    routed = jnp.zeros(out_ref.shape[1:], F32)
    for k in range(TOP_K):
        routed = routed + wt[:, k:k + 1] * ybuf[slot, k * tc:(k + 1) * tc, :]
    hb = h_ref[...].astype(BF16)
    gt = jnp.dot(hb, sg_ref[...], preferred_element_type=F32)
    up = jnp.dot(hb, su_ref[...], preferred_element_type=F32)
    shared = jnp.dot(((gt * _sigmoid(gt)) * up).astype(BF16), sd_ref[...], preferred_element_type=F32)
    out_ref[0] = x1_ref[0] + gate_ref[0] * (routed + shared)


def _combine_call(pos_tiles, tile_off, y_sorted, wts, h2, x1, gate, sg, su, sd):
    bsz, s, d = x1.shape
    tc = min(COMBINE_TILE, s)
    ni = s // tc
    n_tiles = pos_tiles.shape[0]
    lin = lambda b, i: tile_off + b * ni + i
    const = lambda t: pl.BlockSpec(t.shape, lambda b, i: (0,) * t.ndim)
    return pl.pallas_call(
        _combine_body,
        grid=(bsz, ni),
        in_specs=[
            pl.BlockSpec((1, 1, tc * TOP_K), lambda b, i: (lin(b, i), 0, 0), memory_space=pltpu.SMEM),
            pl.BlockSpec((1, 1, tc * TOP_K), lambda b, i: (jnp.minimum(lin(b, i) + 1, n_tiles - 1), 0, 0),
                         memory_space=pltpu.SMEM),
            pl.BlockSpec(memory_space=pl.ANY),
            pl.BlockSpec((tc, LANES), lambda b, i: (lin(b, i), 0)),
            pl.BlockSpec((tc, d), lambda b, i: (lin(b, i), 0)),
            pl.BlockSpec((1, tc, d), lambda b, i: (b, i, 0)),
            _mod_spec(gate, tc, d),
            const(sg), const(su), const(sd),
        ],
        out_specs=pl.BlockSpec((1, tc, d), lambda b, i: (b, i, 0)),
        out_shape=jax.ShapeDtypeStruct((bsz, s, d), F32),
        scratch_shapes=[pltpu.VMEM((2, tc * TOP_K, d), F32), pltpu.SemaphoreType.DMA((2,))],
        compiler_params=_params(("arbitrary", "arbitrary")),
        name="combine",
    )(pos_tiles, pos_tiles, y_sorted, wts, h2, x1, gate, sg, su, sd)


def _t5_bucket(dist):
    exact = N_BUCKETS // 2
    d = np.asarray(dist)
    large = exact + (np.log(np.maximum(d, 1) / exact) / np.log(REL_MAX_DIST / exact) * (N_BUCKETS - exact)).astype(np.int32)
    large = np.minimum(large, N_BUCKETS - 1)
    return np.where(d < exact, d, large).astype(np.int32)


def _group_bias(rel_bias, gi):
    window, dil = ATTN_GROUPS[gi]
    bkt = _t5_bucket(np.arange(window // dil + 1) * dil)
    return rel_bias[bkt, gi * HEADS:(gi + 1) * HEADS].T.astype(F32)


def _prompt_bias_table(bias_j):
    blk = ATTN_BLOCK
    n = bias_j.shape[1] - 1
    a = np.arange(blk)[:, None]
    c = np.arange(2 * blk)[None, :]
    j = blk + a - c
    band = (j >= 0) & (j <= n)
    tab = jnp.where(band[None], bias_j[:, np.clip(j, 0, n)], NEG)
    hq = MXU_DIM // HEAD_DIM
    return tab.reshape(HEADS // hq, hq * blk, 2 * blk)


def _decode_bias_tables(bias_j):
    n = ATTN_STEPS
    old = jnp.zeros((n, LANES), F32).at[:, :HEADS].set(bias_j[:, n - np.arange(n)].T)
    new = jnp.zeros((1, LANES), F32).at[:, :HEADS].set(bias_j[:, 0][None, :])
    return old, new


def _sort_assignments(idx):
    nt = idx.shape[0]
    te = EXPERT_TILE
    na = nt * TOP_K
    n_blocks = -(-na // te) + N_EXPERTS
    flat_e = idx.reshape(-1)
    order = jnp.argsort(flat_e)
    se = flat_e[order]
    stok = (order // TOP_K).astype(jnp.int32)
    counts = jnp.bincount(flat_e, length=N_EXPERTS)
    offs = jnp.cumsum(counts) - counts
    pcounts = (counts + te - 1) // te * te
    pends = jnp.cumsum(pcounts)
    dest = (pends[se] - pcounts[se] + jnp.arange(na) - offs[se]).astype(jnp.int32)
    slot_tok = jnp.zeros((n_blocks * te,), jnp.int32).at[dest].set(stok)
    blk_exp = jnp.minimum(jnp.searchsorted(pends, jnp.arange(n_blocks) * te, side='right'),
                          N_EXPERTS - 1).astype(jnp.int32)
    n_used = (pends[-1] // te).astype(jnp.int32).reshape(1)
    pos = jnp.zeros((na,), jnp.int32).at[order].set(dest)
    return slot_tok, blk_exp, n_used, pos


def kernel(x_prompt, x_sample, cache_k_w128, cache_v_w128, cache_k_w512, cache_v_w512, cache_k_w2048, cache_v_w2048, state_pool, c_prompt, c_sample, ada_w, ada_b, norm1, norm2, w_in, q_gain, k_gain, rel_bias, pool_w, pool_scale, w_br_a, w_br_b, w_out, router_w, router_bias, exp_w_gate, exp_w_up, exp_w_down, sh_w_gate, sh_w_up, sh_w_down):
    depth = ada_w.shape[0]
    bsz, seq, d = x_prompt.shape
    db = x_sample.shape[0]
    assert x_sample.shape[1] == 1 and db <= SAMPLE_PAD and seq % (ATTN_BLOCK * ATTN_GROUPS[-1][1]) == 0
    k_caches = (cache_k_w128, cache_k_w512, cache_k_w2048)
    v_caches = (cache_v_w128, cache_v_w512, cache_v_w2048)
    for c, (w, dil) in zip(k_caches, ATTN_GROUPS):
        assert c.shape[2] == w == ATTN_STEPS * dil

    bias_js = [_group_bias(rel_bias, g) for g in range(N_GROUPS)]
    bias_tabs = [_prompt_bias_table(b) for b in bias_js]
    dec = [_decode_bias_tables(b) for b in bias_js]
    bdec = jnp.stack([t[0] for t in dec])
    bnew = jnp.stack([t[1] for t in dec])
    head_of_col = np.arange(GROUP_W) // HEAD_DIM
    ho = jnp.asarray(head_of_col[:, None] == np.arange(LANES)[None, :], F32)
    ex = ho.T
    ex_bf = ex.astype(BF16)
    bd = jnp.asarray((np.arange(MXU_DIM)[:, None] // HEAD_DIM == np.arange(MXU_DIM)[None, :] // HEAD_DIM)
                     / HEAD_DIM, BF16)
    wcols = np.repeat(np.asarray(POOL_WINDOWS), POOL_GROUP)
    pw_dec = jnp.asarray(np.where(np.arange(POOL_STATE + 1)[:, None] >= POOL_STATE + 1 - wcols[None, :],
                                  1.0 / wcols[None, :], 0.0), F32)

    xp = x_prompt
    xs = jnp.pad(x_sample.reshape(db, d), ((0, SAMPLE_PAD - db), (0, 0))).reshape(1, SAMPLE_PAD, d)
    n_prompt = bsz * seq
    outs = {name: [] for name in ('pk', 'pv', 'sk', 'sv')}
    outs['pk'] = [[] for _ in range(N_GROUPS)]
    outs['pv'] = [[] for _ in range(N_GROUPS)]
    outs['sk'] = [[] for _ in range(N_GROUPS)]
    outs['sv'] = [[] for _ in range(N_GROUPS)]
    ppool, spool = [], []
    for l in range(depth):
        c_all = jnp.concatenate([c_prompt, c_sample], axis=0)
        mod = _ada_call(c_all, ada_w[l], ada_b[l])
        mod_p = [m.reshape(bsz, 1, d) for m in jnp.split(mod[:bsz], 6, axis=-1)]
        mod_s = [jnp.pad(m, ((0, SAMPLE_PAD - db), (0, 0))).reshape(1, SAMPLE_PAD, d)
                 for m in jnp.split(mod[bsz:], 6, axis=-1)]
        w_in_bf = w_in[l].astype(BF16)
        qg = (jnp.tile(q_gain[l], (1, HEADS)) * (HEAD_DIM ** -0.5)).reshape(1, -1)
        kg = jnp.tile(k_gain[l], (1, HEADS)).reshape(1, -1)
        g1 = norm1[l].reshape(1, d)
        g2 = norm2[l].reshape(1, d)
        wa, wb, wo = (w_br_a[l].astype(BF16), w_br_b[l].astype(BF16), w_out[l].astype(BF16))
        pw = pool_w[l].astype(BF16)
        ps = pool_scale[l].reshape(1, -1)

        q, k, v, u, gates = _inproj_call(xp, mod_p[0], mod_p[1], g1, w_in_bf, qg, kg, bd)
        o_g, lse_g = [], []
        for g, (window, dil) in enumerate(ATTN_GROUPS):
            o, lse = _attn_call(q, k, v, bias_tabs[g], g, dil)
            o_g.append(o)
            lse_g.append(lse)
            keep = min(window, seq)
            cs = slice(g * GROUP_W, (g + 1) * GROUP_W)
            outs['pk'][g].append(k[:, seq - keep:, cs].reshape(bsz, keep, HEADS, HEAD_DIM))
            outs['pv'][g].append(v[:, seq - keep:, cs].reshape(bsz, keep, HEADS, HEAD_DIM))
        ppool.append(u[:, seq - POOL_STATE:])
        x1p, h2p = _merge_call(xp, gates, (o_g, lse_g, u), (mod_p[2], mod_p[3], mod_p[4]),
                               g2, wa, wb, wo, pw, ps, ex_bf, prompt=True)

        qs, ks, vs, us, gates_s = _inproj_call(xs, mod_s[0], mod_s[1], g1, w_in_bf, qg, kg, bd)
        kcl = [c[l].reshape(db, c.shape[2], GROUP_W) for c in k_caches]
        vcl = [c[l].reshape(db, c.shape[2], GROUP_W) for c in v_caches]
        oa_s, pooled_s = _decode_call(qs[0, :db], ks[0, :db], vs[0, :db], us[0, :db], state_pool[l],
                                      kcl, vcl, bdec, bnew, ho, ex, pw_dec)
        padrows = lambda t: jnp.pad(t.reshape(db, GROUP_W), ((0, SAMPLE_PAD - db), (0, 0))).reshape(1, SAMPLE_PAD, GROUP_W)
        x1s, h2s = _merge_call(xs, gates_s, (padrows(oa_s), padrows(pooled_s)), (mod_s[2], mod_s[3], mod_s[4]),
                               g2, wa, wb, wo, pw, ps, ex_bf, prompt=False)
        for g in range(N_GROUPS):
            cs = slice(g * GROUP_W, (g + 1) * GROUP_W)
            nk = jnp.concatenate([kcl[g][:, 1:], ks[0, :db, None, cs]], axis=1)
            nv = jnp.concatenate([vcl[g][:, 1:], vs[0, :db, None, cs]], axis=1)
            outs['sk'][g].append(nk.reshape(db, -1, HEADS, HEAD_DIM))
            outs['sv'][g].append(nv.reshape(db, -1, HEADS, HEAD_DIM))
        spool.append(jnp.concatenate([state_pool[l][:, 1:], us[0, :db, None, :]], axis=1))

        h2 = jnp.concatenate([h2p.reshape(n_prompt, d), h2s.reshape(SAMPLE_PAD, d)], axis=0)
        idx, wts = _router_call(h2, router_w[l], router_bias[l])
        slot_tok, blk_exp, n_used, pos = _sort_assignments(idx[:, :TOP_K])
        y_sorted = _expert_call(blk_exp, n_used, slot_tok, h2, exp_w_gate[l], exp_w_up[l], exp_w_down[l])
        nt = h2.shape[0]
        tc = COMBINE_TILE
        pos_tiles = pos.reshape(nt // tc, tc, TOP_K).transpose(0, 2, 1).reshape(nt // tc, 1, TOP_K * tc)
        sg, su, sd = sh_w_gate[l].astype(BF16), sh_w_up[l].astype(BF16), sh_w_down[l].astype(BF16)
        xp = _combine_call(pos_tiles, 0, y_sorted, wts, h2, x1p, mod_p[5], sg, su, sd)
        xs = _combine_call(pos_tiles, n_prompt // tc, y_sorted, wts, h2, x1s, mod_s[5], sg, su, sd)

    stack = lambda parts: jnp.stack(parts)
    res = [xp, xs[0, :db].reshape(db, 1, d)]
    for g in range(N_GROUPS):
        res += [stack(outs['pk'][g]), stack(outs['pv'][g])]
    res.append(stack(ppool))
    for g in range(N_GROUPS):
        res += [stack(outs['sk'][g]), stack(outs['sv'][g])]
    res.append(stack(spool))
    return tuple(res)
```

```python
import functools

import numpy as np
import jax
import jax.numpy as jnp
from jax import lax
from jax.experimental import pallas as pl
from jax.experimental.pallas import tpu as pltpu

F32 = jnp.float32
BF16 = jnp.bfloat16
HIGHEST = lax.Precision.HIGHEST

HEAD_DIM = 64
HEADS = 8
GROUP_W = HEADS * HEAD_DIM
ATTN_GROUPS = ((128, 1), (512, 4), (2048, 16))
N_GROUPS = len(ATTN_GROUPS)
ATTN_STEPS = 128
ATTN_BLOCK = 128
POOL_WINDOWS = (2, 4, 8, 16)
POOL_GROUP = 128
POOL_STATE = 15
N_BUCKETS = 32
REL_MAX_DIST = 2048
N_EXPERTS = 256
TOP_K = 8
N_EXPERT_GROUPS = 8
TOPK_GROUPS = 4
ROUTED_SCALE = 2.5
EPS = 1e-6
NEG = -1e30

LANES = 128
MXU_DIM = 256
VMEM_LIMIT = 56 * 1024 * 1024

ROW_TILE = 256
EXPERT_TILE = 256
COMBINE_TILE = 128
SAMPLE_PAD = 256


def _sigmoid(x):
    return 1.0 / (1.0 + jnp.exp(-x))


def _params(sem):
    return pltpu.CompilerParams(dimension_semantics=sem, vmem_limit_bytes=VMEM_LIMIT)


def _ada_body(c_ref, w_ref, b_ref, o_ref):
    c = c_ref[...]
    s = c * _sigmoid(c)
    o_ref[...] = jnp.dot(s, w_ref[...], precision=HIGHEST, preferred_element_type=F32) + b_ref[...]


def _ada_call(c, w, b):
    n, d = c.shape
    cols = w.shape[1]
    tn = 512
    return pl.pallas_call(
        _ada_body,
        grid=(cols // tn,),
        in_specs=[pl.BlockSpec((n, d), lambda j: (0, 0)),
                  pl.BlockSpec((d, tn), lambda j: (0, j)),
                  pl.BlockSpec((1, tn), lambda j: (0, j))],
        out_specs=pl.BlockSpec((n, tn), lambda j: (0, j)),
        out_shape=jax.ShapeDtypeStruct((n, cols), F32),
        compiler_params=_params(("arbitrary",)),
        name="adaln",
    )(c, w, b.reshape(1, cols))


def _inproj_body(x_ref, shift_ref, scale_ref, g_ref, w_ref, qg_ref, kg_ref, bd_ref,
                 q_ref, k_ref, v_ref, u_ref, gt_ref):
    x = x_ref[0]
    ms = jnp.mean(x * x, axis=-1, keepdims=True)
    h = x * lax.rsqrt(ms + EPS) * g_ref[...]
    h = h * (1.0 + scale_ref[0]) + shift_ref[0]
    hb = h.astype(BF16)
    qkv = N_GROUPS * GROUP_W
    n_chunks = w_ref.shape[1] // GROUP_W
    for c in range(n_chunks):
        z = jnp.dot(hb, w_ref[:, c * GROUP_W:(c + 1) * GROUP_W], preferred_element_type=F32)
        if c < 2 * N_GROUPS:
            zz = (z * z).astype(BF16)
            hms = jnp.concatenate(
                [jnp.dot(zz[:, s:s + MXU_DIM], bd_ref[...], preferred_element_type=F32)
                 for s in range(0, GROUP_W, MXU_DIM)], axis=1)
            g = c % N_GROUPS
            gain = (qg_ref if c < N_GROUPS else kg_ref)[:, g * GROUP_W:(g + 1) * GROUP_W]
            y = z * lax.rsqrt(hms + EPS) * gain
            if c < N_GROUPS:
                q_ref[0, :, g * GROUP_W:(g + 1) * GROUP_W] = y.astype(BF16)
            else:
                k_ref[0, :, g * GROUP_W:(g + 1) * GROUP_W] = y
        elif c < 3 * N_GROUPS:
            g = c - 2 * N_GROUPS
            v_ref[0, :, g * GROUP_W:(g + 1) * GROUP_W] = z
        elif c == 3 * N_GROUPS:
            u_ref[0] = z
        else:
            g = c - 3 * N_GROUPS - 1
            gt_ref[0, :, g * GROUP_W:(g + 1) * GROUP_W] = z.astype(BF16)
    del qkv


def _mod_spec(mod, tm, d):
    if mod.shape[1] == 1:
        return pl.BlockSpec((1, 1, d), lambda b, i: (b, 0, 0))
    return pl.BlockSpec((1, tm, d), lambda b, i: (b, i, 0))


def _inproj_call(x, shift, scale, g1, w_bf, qg, kg, bd):
    bsz, s, d = x.shape
    tm = min(ROW_TILE, s)
    qkv = N_GROUPS * GROUP_W
    n_gate = w_bf.shape[1] - 3 * qkv - GROUP_W
    const = lambda b, i: (0, 0)
    row = lambda width: pl.BlockSpec((1, tm, width), lambda b, i: (b, i, 0))
    return pl.pallas_call(
        _inproj_body,
        grid=(bsz, s // tm),
        in_specs=[row(d), _mod_spec(shift, tm, d), _mod_spec(scale, tm, d),
                  pl.BlockSpec((1, d), const),
                  pl.BlockSpec(w_bf.shape, const),
                  pl.BlockSpec((1, qkv), const), pl.BlockSpec((1, qkv), const),
                  pl.BlockSpec((MXU_DIM, MXU_DIM), const)],
        out_specs=[row(qkv), row(qkv), row(qkv), row(GROUP_W), row(n_gate)],
        out_shape=[jax.ShapeDtypeStruct((bsz, s, qkv), BF16),
                   jax.ShapeDtypeStruct((bsz, s, qkv), F32),
                   jax.ShapeDtypeStruct((bsz, s, qkv), F32),
                   jax.ShapeDtypeStruct((bsz, s, GROUP_W), F32),
                   jax.ShapeDtypeStruct((bsz, s, n_gate), BF16)],
        compiler_params=_params(("arbitrary", "arbitrary")),
        name="inproj",
    )(x, shift, scale, g1, w_bf, qg, kg, bd)


def _attn_body(q_ref, kp_ref, kc_ref, vp_ref, vc_ref, bias_ref, o_ref, lse_ref):
    first = pl.program_id(2) == 0
    blk = ATTN_BLOCK
    qf = q_ref[0].astype(F32)
    k2 = jnp.concatenate([kp_ref[0], kc_ref[0]], axis=0).astype(BF16)
    v2 = jnp.concatenate([vp_ref[0], vc_ref[0]], axis=0).astype(BF16)
    hq = MXU_DIM // HEAD_DIM
    lane_head = lax.broadcasted_iota(jnp.int32, (blk, MXU_DIM), 1) // HEAD_DIM
    col = lax.broadcasted_iota(jnp.int32, (hq * blk, 2 * blk), 1)
    lane_l = lax.broadcasted_iota(jnp.int32, (blk, LANES), 1)
    lse_tile = jnp.zeros((blk, LANES), F32)
    outs = []
    for sl in range(GROUP_W // MXU_DIM):
        cs = slice(sl * MXU_DIM, (sl + 1) * MXU_DIM)
        qq = qf[:, cs]
        qs = jnp.concatenate([jnp.where(lane_head == h, qq, 0.0) for h in range(hq)], axis=0).astype(BF16)
        s = lax.dot_general(qs, k2[:, cs], (((1,), (1,)), ((), ())), preferred_element_type=F32)
        s = s + bias_ref[sl]
        s = jnp.where(jnp.logical_and(first, col < blk), NEG, s)
        m = jnp.max(s, axis=-1, keepdims=True)
        p = jnp.exp(s - m)
        l = jnp.sum(p, axis=-1, keepdims=True)
        o4 = jnp.dot(p.astype(BF16), v2[:, cs], preferred_element_type=F32) / l
        oq = jnp.zeros((blk, MXU_DIM), F32)
        lse = m + jnp.log(l)
        for h in range(hq):
            oq = jnp.where(lane_head == h, o4[h * blk:(h + 1) * blk], oq)
            lse_tile = jnp.where(lane_l == sl * hq + h, lse[h * blk:(h + 1) * blk], lse_tile)
        outs.append(oq)
    o_ref[0] = jnp.concatenate(outs, axis=1).astype(BF16)
    lse_ref[0] = lse_tile


def _attn_call(q, k, v, bias_tab, g, dil):
    bsz, s, qkv = q.shape
    blk = ATTN_BLOCK
    ll = s // dil
    nb = ll // blk
    ng = qkv // GROUP_W
    view = lambda t: t.reshape(bsz, ll, dil * t.shape[-1])
    cur = pl.BlockSpec((1, blk, GROUP_W), lambda b, r, i: (b, i, r * ng + g))
    prev = pl.BlockSpec((1, blk, GROUP_W), lambda b, r, i: (b, jnp.maximum(i - 1, 0), r * ng + g))
    o, lse = pl.pallas_call(
        _attn_body,
        grid=(bsz, dil, nb),
        in_specs=[cur, prev, cur, prev, cur,
                  pl.BlockSpec(bias_tab.shape, lambda b, r, i: (0, 0, 0))],
        out_specs=[pl.BlockSpec((1, blk, GROUP_W), lambda b, r, i: (b, i, r)),
                   pl.BlockSpec((1, blk, LANES), lambda b, r, i: (b, i, r))],
        out_shape=[jax.ShapeDtypeStruct((bsz, ll, dil * GROUP_W), BF16),
                   jax.ShapeDtypeStruct((bsz, ll, dil * LANES), F32)],
        compiler_params=_params(("arbitrary", "arbitrary", "arbitrary")),
        name=f"attn_g{g}",
    )(view(q), view(k), view(k), view(v), view(v), bias_tab)
    return o.reshape(bsz, s, GROUP_W), lse.reshape(bsz, s, LANES)


def _decode_body(q_ref, kn_ref, vn_ref, u_ref, pool_ref, kc0, vc0, kc1, vc1, kc2, vc2,
                 bdec_ref, bnew_ref, ho_ref, ex_ref, pw_ref, oa_ref, pooled_ref):
    kcs = (kc0, kc1, kc2)
    vcs = (vc0, vc1, vc2)
    q = q_ref[0].astype(F32)
    kn = kn_ref[0]
    vn = vn_ref[0]
    ho = ho_ref[...]
    ex = ex_ref[...]
    n = ATTN_STEPS
    outs, lses = [], []
    for g in range(N_GROUPS):
        cs = slice(g * GROUP_W, (g + 1) * GROUP_W)
        qg = q[:, cs]
        kc = kcs[g][0]
        vc = vcs[g][0]
        prod = jnp.concatenate([kc * qg, jnp.broadcast_to(kn[:, cs] * qg, (8, GROUP_W))], axis=0)
        sc = jnp.dot(prod, ho, precision=HIGHEST, preferred_element_type=F32)
        s_old = sc[:n] + bdec_ref[g]
        s_new = sc[n:n + 1] + bnew_ref[g]
        m = jnp.maximum(jnp.max(s_old, axis=0, keepdims=True), s_new)
        p_old = jnp.exp(s_old - m)
        p_new = jnp.exp(s_new - m)
        l = jnp.sum(p_old, axis=0, keepdims=True) + p_new
        stack = jnp.concatenate([p_old, jnp.broadcast_to(p_new, (8, LANES)),
                                 jnp.broadcast_to(l, (8, LANES))], axis=0)
        wide = jnp.dot(stack, ex, precision=HIGHEST, preferred_element_type=F32)
        o = (jnp.sum(wide[:n] * vc, axis=0, keepdims=True) + wide[n:n + 1] * vn[:, cs]) / wide[n + 8:n + 9]
        outs.append(o)
        lses.append(m + jnp.log(l))
    mx = jnp.maximum(jnp.maximum(lses[0], lses[1]), lses[2])
    es = [jnp.exp(t - mx) for t in lses]
    den = es[0] + es[1] + es[2]
    wstack = jnp.concatenate([jnp.broadcast_to(e / den, (8, LANES)) for e in es], axis=0)
    wwide = jnp.dot(wstack, ex, precision=HIGHEST, preferred_element_type=F32)
    oa = wwide[0:1] * outs[0] + wwide[8:9] * outs[1] + wwide[16:17] * outs[2]
    oa_ref[0] = oa
    u = u_ref[0]
    stored = jnp.sum(pool_ref[0] * pw_ref[0:POOL_STATE, :], axis=0, keepdims=True)
    pooled_ref[0] = stored + u * pw_ref[POOL_STATE:POOL_STATE + 1, :] - u


def _decode_call(q, kn, vn, u, pool_prev, k_caches, v_caches, bdec, bnew, ho, ex, pw):
    db = pool_prev.shape[0]
    qkv = q.shape[-1]
    row = lambda t: pl.BlockSpec((1, 1, t.shape[-1]), lambda b: (b, 0, 0))
    const = lambda t: pl.BlockSpec(t.shape, lambda b: (0,) * t.ndim)
    cache_specs, cache_args = [], []
    for g, (_, dil) in enumerate(ATTN_GROUPS):
        for c in (k_caches[g], v_caches[g]):
            rows = c.shape[1]
            cache_args.append(c.reshape(db, rows // dil, dil * GROUP_W))
            cache_specs.append(pl.BlockSpec((1, ATTN_STEPS, GROUP_W), lambda b: (b, 0, 0)))
    q3, kn3, vn3, u3 = (t.reshape(t.shape[0], 1, t.shape[-1]) for t in (q, kn, vn, u))
    del qkv
    return pl.pallas_call(
        _decode_body,
        grid=(db,),
        in_specs=[row(q3), row(kn3), row(vn3), row(u3),
                  pl.BlockSpec((1,) + pool_prev.shape[1:], lambda b: (b, 0, 0))] + cache_specs +
                 [const(bdec), const(bnew), const(ho), const(ex), const(pw)],
        out_specs=[pl.BlockSpec((1, 1, GROUP_W), lambda b: (b, 0, 0)),
                   pl.BlockSpec((1, 1, GROUP_W), lambda b: (b, 0, 0))],
        out_shape=[jax.ShapeDtypeStruct((db, 1, GROUP_W), F32),
                   jax.ShapeDtypeStruct((db, 1, GROUP_W), F32)],
        compiler_params=_params(("arbitrary",)),
        name="decode_mix",
    )(q3, kn3, vn3, u3, pool_prev, *cache_args, bdec, bnew, ho, ex, pw)


def _expand_heads(w, ex_bf):
    hi = w.astype(BF16)
    lo = (w - hi.astype(F32)).astype(BF16)
    return (jnp.dot(hi, ex_bf, preferred_element_type=F32) +
            jnp.dot(lo, ex_bf, preferred_element_type=F32))


def _merge_tail(x, oa, pooled, gates, gate_msa, shift2, scale2, g2, wa, wb, wo, pw_ref, ps,
                x1_ref, h2_ref):
    mixed = jnp.concatenate(
        [jnp.dot(pooled[:, j * POOL_GROUP:(j + 1) * POOL_GROUP].astype(BF16), pw_ref[j],
                 preferred_element_type=F32) for j in range(len(POOL_WINDOWS))], axis=1) * ps
    d = x.shape[-1]
    ga = gates[:, :d].astype(F32)
    gb = gates[:, d:].astype(F32)
    merged = (_sigmoid(ga) * jnp.dot(oa.astype(BF16), wa, preferred_element_type=F32) +
              _sigmoid(gb) * jnp.dot(mixed.astype(BF16), wb, preferred_element_type=F32))
    y = jnp.dot(merged.astype(BF16), wo, preferred_element_type=F32)
    x1 = x + gate_msa * y
    x1_ref[0] = x1
    ms = jnp.mean(x1 * x1, axis=-1, keepdims=True)
    h2 = x1 * lax.rsqrt(ms + EPS) * g2
    h2_ref[0] = h2 * (1.0 + scale2) + shift2


def _merge_prompt_body(x_ref, gt_ref, o0, o1, o2, l0, l1, l2, u_ref, uh_ref,
                       gm_ref, sh_ref, sc_ref, g2_ref, wa_ref, wb_ref, wo_ref, pw_ref, ps_ref, ex_ref,
                       x1_ref, h2_ref):
    i = pl.program_id(1)
    tm = x_ref.shape[1]
    ls = (l0[0], l1[0], l2[0])
    mx = jnp.maximum(jnp.maximum(ls[0], ls[1]), ls[2])
    es = [jnp.exp(t - mx) for t in ls]
    den = es[0] + es[1] + es[2]
    ex = ex_ref[...]
    oa = jnp.zeros((tm, GROUP_W), F32)
    for e, o in zip(es, (o0, o1, o2)):
        oa = oa + _expand_heads(e / den, ex) * o[0].astype(F32)
    u = u_ref[0]
    halo = jnp.where(i == 0, 0.0, uh_ref[0])
    ext = jnp.concatenate([halo, u], axis=0)
    hw = halo.shape[0]
    acc = ext
    parts = []
    for j, w in enumerate(POOL_WINDOWS):
        acc = acc[:, POOL_GROUP * (1 if j else 0):]
        acc = acc + pltpu.roll(acc, w // 2, axis=0)
        parts.append(acc[hw:, :POOL_GROUP])
    tot = jnp.concatenate(parts, axis=1)
    lane = lax.broadcasted_iota(jnp.int32, (tm, GROUP_W), 1)
    wcol = jnp.left_shift(2, lane // POOL_GROUP)
    pos = i * tm + lax.broadcasted_iota(jnp.int32, (tm, GROUP_W), 0)
    cnt = jnp.minimum(pos + 1, wcol).astype(F32)
    pooled = tot / cnt - u
    _merge_tail(x_ref[0], oa, pooled, gt_ref[0], gm_ref[0], sh_ref[0], sc_ref[0], g2_ref[...],
                wa_ref[...], wb_ref[...], wo_ref[...], pw_ref, ps_ref[...], x1_ref, h2_ref)


def _merge_sample_body(x_ref, gt_ref, oa_ref, pooled_ref,
                       gm_ref, sh_ref, sc_ref, g2_ref, wa_ref, wb_ref, wo_ref, pw_ref, ps_ref,
                       x1_ref, h2_ref):
    _merge_tail(x_ref[0], oa_ref[0], pooled_ref[0], gt_ref[0], gm_ref[0], sh_ref[0], sc_ref[0], g2_ref[...],
                wa_ref[...], wb_ref[...], wo_ref[...], pw_ref, ps_ref[...], x1_ref, h2_ref)


def _merge_call(x, gates, mixer_inputs, mods, g2, wa, wb, wo, pw, ps, ex_bf, prompt):
    bsz, s, d = x.shape
    tm = min(ROW_TILE, s)
    row = lambda t: pl.BlockSpec((1, tm, t.shape[-1]), lambda b, i: (b, i, 0))
    const = lambda t: pl.BlockSpec(t.shape, lambda b, i: (0,) * t.ndim)
    gm, sh, sc = mods
    in_specs = [row(x), row(gates)]
    args = [x, gates]
    if prompt:
        outs, lses, u = mixer_inputs
        hw = 16
        in_specs += [row(t) for t in outs] + [row(t) for t in lses] + [
            row(u), pl.BlockSpec((1, hw, u.shape[-1]), lambda b, i: (b, jnp.maximum(i * (tm // hw) - 1, 0), 0))]
        args += list(outs) + list(lses) + [u, u]
        body = _merge_prompt_body
    else:
        in_specs += [row(t) for t in mixer_inputs]
        args += list(mixer_inputs)
        body = _merge_sample_body
    in_specs += [_mod_spec(m, tm, d) for m in (gm, sh, sc)]
    in_specs += [const(t) for t in (g2, wa, wb, wo, pw, ps)]
    args += [gm, sh, sc, g2, wa, wb, wo, pw, ps]
    if prompt:
        in_specs.append(const(ex_bf))
        args.append(ex_bf)
    return pl.pallas_call(
        body,
        grid=(bsz, s // tm),
        in_specs=in_specs,
        out_specs=[row(x), row(x)],
        out_shape=[jax.ShapeDtypeStruct((bsz, s, d), F32), jax.ShapeDtypeStruct((bsz, s, d), F32)],
        compiler_params=_params(("arbitrary", "arbitrary")),
        name="merge_prompt" if prompt else "merge_sample",
    )(*args)


def _router_body(h_ref, rw_ref, rb_ref, tri_ref, idx_ref, wt_ref, rank_ref, cnt_ref, carry):
    tm = h_ref.shape[0]

    @pl.when(pl.program_id(0) == 0)
    def _():
        carry[...] = jnp.zeros_like(carry)

    ne = rw_ref.shape[1]
    gsz = ne // N_EXPERT_GROUPS
    logits = jnp.dot(h_ref[...], rw_ref[...], precision=HIGHEST, preferred_element_type=F32)
    scores = _sigmoid(logits)
    sel = scores + rb_ref[...]
    lane = lax.broadcasted_iota(jnp.int32, (tm, ne), 1)
    lanef = lane.astype(F32)
    grp = lane // gsz
    ninf = -jnp.inf
    far = float(ne)

    def first_max(x):
        m = jnp.max(x, axis=-1, keepdims=True)
        at = jnp.min(jnp.where(x == m, lanef, far), axis=-1, keepdims=True)
        return m, at

    gs = []
    for g in range(N_EXPERT_GROUPS):
        xg = jnp.where(grp == g, sel, ninf)
        m1, a1 = first_max(xg)
        m2 = jnp.max(jnp.where(lanef == a1, ninf, xg), axis=-1, keepdims=True)
        gs.append(m1 + m2)
    allowed = jnp.zeros((tm, ne), jnp.bool_)
    for g in range(N_EXPERT_GROUPS):
        ahead = jnp.zeros((tm, 1), F32)
        for o in range(N_EXPERT_GROUPS):
            if o == g:
                continue
            beats = (gs[o] > gs[g]) | ((gs[o] == gs[g]) & (o < g))
            ahead = ahead + beats.astype(F32)
        allowed = allowed | ((grp == g) & (ahead < TOPK_GROUPS))
    masked = jnp.where(allowed, sel, ninf)
    lane_o = lax.broadcasted_iota(jnp.int32, (tm, LANES), 1)
    idx_t = jnp.zeros((tm, LANES), F32)
    wt_t = jnp.zeros((tm, LANES), F32)
    wsum = jnp.zeros((tm, 1), F32)
    picked = jnp.zeros((tm, ne), F32)
    ats = []
    for k in range(TOP_K):
        _, at = first_max(masked)
        hit = lanef == at
        wk = jnp.sum(jnp.where(hit, scores, 0.0), axis=-1, keepdims=True)
        masked = jnp.where(hit, ninf, masked)
        picked = picked + hit.astype(F32)
        ats.append(at)
        idx_t = jnp.where(lane_o == k, at, idx_t)
        wt_t = jnp.where(lane_o == k, wk, wt_t)
        wsum = wsum + wk
    idx_ref[...] = idx_t.astype(jnp.int32)
    wt_ref[...] = wt_t / wsum * ROUTED_SCALE
    before = jnp.dot(tri_ref[...], picked.astype(BF16), preferred_element_type=F32) + carry[...]
    rank_t = jnp.zeros((tm, LANES), F32)
    for k in range(TOP_K):
        rk = jnp.sum(jnp.where(lanef == ats[k], before, 0.0), axis=-1, keepdims=True)
        rank_t = jnp.where(lane_o == k, rk, rank_t)
    rank_ref[...] = rank_t.astype(jnp.int32)
    carry[...] = carry[...] + jnp.sum(picked, axis=0, keepdims=True)
    cnt_ref[...] = carry[...].astype(jnp.int32)


def _router_call(h2, rw, rb):
    nt, d = h2.shape
    tm = ROW_TILE
    ne = rw.shape[1]
    tri = jnp.asarray(np.tril(np.ones((tm, tm), np.float32), -1), BF16)
    tile = pl.BlockSpec((tm, LANES), lambda i: (i, 0))
    return pl.pallas_call(
        _router_body,
        grid=(nt // tm,),
        in_specs=[pl.BlockSpec((tm, d), lambda i: (i, 0)),
                  pl.BlockSpec((d, ne), lambda i: (0, 0)),
                  pl.BlockSpec((1, ne), lambda i: (0, 0)),
                  pl.BlockSpec((tm, tm), lambda i: (0, 0))],
        out_specs=[tile, tile, tile, pl.BlockSpec((1, ne), lambda i: (0, 0))],
        out_shape=[jax.ShapeDtypeStruct((nt, LANES), jnp.int32),
                   jax.ShapeDtypeStruct((nt, LANES), F32),
                   jax.ShapeDtypeStruct((nt, LANES), jnp.int32),
                   jax.ShapeDtypeStruct((1, ne), jnp.int32)],
        scratch_shapes=[pltpu.VMEM((1, ne), F32)],
        compiler_params=_params(("arbitrary",)),
        name="router",
    )(h2, rw, rb.reshape(1, ne), tri)


def _slot_body(idx_ref, rank_ref, offs_ref, pos_ref):
    tm = idx_ref.shape[0]
    ne = offs_ref.shape[1]
    idx = idx_ref[...]
    rank = rank_ref[...]
    lane = lax.broadcasted_iota(jnp.int32, (tm, ne), 1)
    lane_o = lax.broadcasted_iota(jnp.int32, (tm, LANES), 1)
    offs = offs_ref[...]
    pos = jnp.zeros((tm, LANES), F32)
    for k in range(TOP_K):
        start = jnp.sum(jnp.where(lane == idx[:, k:k + 1], offs, 0.0), axis=-1, keepdims=True)
        pos = jnp.where(lane_o == k, start, pos)
    pos_ref[...] = pos.astype(jnp.int32) + rank


def _slot_call(idx, rank, offs):
    nt = idx.shape[0]
    tm = ROW_TILE
    ne = offs.shape[1]
    tile = pl.BlockSpec((tm, LANES), lambda i: (i, 0))
    return pl.pallas_call(
        _slot_body,
        grid=(nt // tm,),
        in_specs=[tile, tile, pl.BlockSpec((1, ne), lambda i: (0, 0))],
        out_specs=tile,
        out_shape=jax.ShapeDtypeStruct((nt, LANES), jnp.int32),
        compiler_params=_params(("arbitrary",)),
        name="slots",
    )(idx, rank, offs)


def _row_copy(src_hbm, row, dst, j, sem):
    return pltpu.make_async_copy(src_hbm.at[pl.ds(row, 1), :], dst.at[pl.ds(j, 1), :], sem)


def _issue_rows(idx_ref, src_hbm, dst, sem, n):
    def one(j, carry):
        _row_copy(src_hbm, idx_ref[0, 0, j], dst, j, sem).start()
        return carry
    lax.fori_loop(0, n, one, 0, unroll=8)


def _wait_rows(src_hbm, dst, sem, n):
    pltpu.make_async_copy(src_hbm.at[pl.ds(0, n), :], dst, sem).wait()


def _dispatch_body(pos_ref, h_ref, xs_hbm, sem):
    td = h_ref.shape[0]

    def one(t, carry):
        for k in range(TOP_K):
            pltpu.make_async_copy(h_ref.at[pl.ds(t, 1), :],
                                  xs_hbm.at[pl.ds(pos_ref[0, 0, t * TOP_K + k], 1), :], sem).start()
        return carry
    lax.fori_loop(0, td, one, 0, unroll=2)
    for k in range(TOP_K):
        pltpu.make_async_copy(h_ref, xs_hbm.at[pl.ds(0, td), :], sem).wait()


def _dispatch_call(pos_tok, h2):
    nt, d = h2.shape
    td = ROW_TILE
    return pl.pallas_call(
        _dispatch_body,
        grid=(nt // td,),
        in_specs=[pl.BlockSpec((1, 1, td * TOP_K), lambda i: (i, 0, 0), memory_space=pltpu.SMEM),
                  pl.BlockSpec((td, d), lambda i: (i, 0))],
        out_specs=pl.BlockSpec(memory_space=pl.ANY),
        out_shape=jax.ShapeDtypeStruct((nt * TOP_K, d), F32),
        scratch_shapes=[pltpu.SemaphoreType.DMA],
        compiler_params=_params(("arbitrary",)),
        name="dispatch",
    )(pos_tok, h2)


def _expert_body(tile_ref, exp_ref, lo_ref, hi_ref, xs_ref, wg_ref, wu_ref, wd_ref, y_ref, wgb, wub, wdb):
    s = pl.program_id(0)
    prev = jnp.maximum(s - 1, 0)
    first_visit = jnp.logical_or(s == 0, tile_ref[s] != tile_ref[prev])
    new_expert = jnp.logical_or(s == 0, exp_ref[s] != exp_ref[prev])
    lo = lo_ref[s]
    hi = hi_ref[s]

    @pl.when(new_expert)
    def _():
        wgb[...] = wg_ref[0].astype(BF16)
        wub[...] = wu_ref[0].astype(BF16)
        wdb[...] = wd_ref[0].astype(BF16)

    @pl.when(hi > lo)
    def _():
        x = xs_ref[...].astype(BF16)
        gt = jnp.dot(x, wgb[...], preferred_element_type=F32)
        up = jnp.dot(x, wub[...], preferred_element_type=F32)
        act = (gt * _sigmoid(gt)) * up
        y = jnp.dot(act.astype(BF16), wdb[...], preferred_element_type=F32)
        row = lax.broadcasted_iota(jnp.int32, y.shape, 0)
        mine = jnp.logical_and(row >= lo, row < hi)

        @pl.when(first_visit)
        def _():
            y_ref[...] = jnp.where(mine, y, 0.0)

        @pl.when(jnp.logical_not(first_visit))
        def _():
            y_ref[...] = jnp.where(mine, y, y_ref[...])


def _expert_call(visits, xs, wg, wu, wd):
    tile_id, exp_id, lo, hi = visits
    n_steps = tile_id.shape[0]
    te = EXPERT_TILE
    na, d = xs.shape
    hid = wg.shape[-1]
    grid_spec = pltpu.PrefetchScalarGridSpec(
        num_scalar_prefetch=4,
        grid=(n_steps,),
        in_specs=[
            pl.BlockSpec((te, d), lambda s, t, e, lo, hi: (t[s], 0)),
            pl.BlockSpec((1, d, hid), lambda s, t, e, lo, hi: (e[s], 0, 0)),
            pl.BlockSpec((1, d, hid), lambda s, t, e, lo, hi: (e[s], 0, 0)),
            pl.BlockSpec((1, hid, d), lambda s, t, e, lo, hi: (e[s], 0, 0)),
        ],
        out_specs=pl.BlockSpec((te, d), lambda s, t, e, lo, hi: (t[s], 0)),
        scratch_shapes=[pltpu.VMEM((d, hid), BF16), pltpu.VMEM((d, hid), BF16), pltpu.VMEM((hid, d), BF16)],
    )
    return pl.pallas_call(
        _expert_body,
        grid_spec=grid_spec,
        out_shape=jax.ShapeDtypeStruct((na, d), F32),
        compiler_params=_params(("arbitrary",)),
        name="expert_ffn",
    )(tile_id, exp_id, lo, hi, xs, wg, wu, wd)


def _combine_body(pos_ref, posn_ref, y_hbm, wt_ref, h_ref, x1_ref, gate_ref, sg_ref, su_ref, sd_ref,
                  out_ref, ybuf, sem):
    nb, ni = pl.num_programs(0), pl.num_programs(1)
    step = pl.program_id(0) * ni + pl.program_id(1)
    slot = step % 2
    tc = h_ref.shape[0]
    rows = tc * TOP_K

    @pl.when(step == 0)
    def _():
        _issue_rows(pos_ref, y_hbm, ybuf.at[0], sem.at[0], rows)

    @pl.when(step + 1 < nb * ni)
    def _():
        _issue_rows(posn_ref, y_hbm, ybuf.at[1 - slot], sem.at[1 - slot], rows)

    _wait_rows(y_hbm, ybuf.at[slot], sem.at[slot], rows)
    wt = wt_ref[...]
    routed = jnp.zeros(out_ref.shape[1:], F32)
    for k in range(TOP_K):
        routed = routed + wt[:, k:k + 1] * ybuf[slot, k * tc:(k + 1) * tc, :]
    hb = h_ref[...].astype(BF16)
    gt = jnp.dot(hb, sg_ref[...], preferred_element_type=F32)
    up = jnp.dot(hb, su_ref[...], preferred_element_type=F32)
    shared = jnp.dot(((gt * _sigmoid(gt)) * up).astype(BF16), sd_ref[...], preferred_element_type=F32)
    out_ref[0] = x1_ref[0] + gate_ref[0] * (routed + shared)


def _combine_call(pos_tiles, tile_off, y_sorted, wts, h2, x1, gate, sg, su, sd):
    bsz, s, d = x1.shape
    tc = min(COMBINE_TILE, s)
    ni = s // tc
    n_tiles = pos_tiles.shape[0]
    lin = lambda b, i: tile_off + b * ni + i
    const = lambda t: pl.BlockSpec(t.shape, lambda b, i: (0,) * t.ndim)
    return pl.pallas_call(
        _combine_body,
        grid=(bsz, ni),
        in_specs=[
            pl.BlockSpec((1, 1, tc * TOP_K), lambda b, i: (lin(b, i), 0, 0), memory_space=pltpu.SMEM),
            pl.BlockSpec((1, 1, tc * TOP_K), lambda b, i: (jnp.minimum(lin(b, i) + 1, n_tiles - 1), 0, 0),
                         memory_space=pltpu.SMEM),
            pl.BlockSpec(memory_space=pl.ANY),
            pl.BlockSpec((tc, LANES), lambda b, i: (lin(b, i), 0)),
            pl.BlockSpec((tc, d), lambda b, i: (lin(b, i), 0)),
            pl.BlockSpec((1, tc, d), lambda b, i: (b, i, 0)),
            _mod_spec(gate, tc, d),
            const(sg), const(su), const(sd),
        ],
        out_specs=pl.BlockSpec((1, tc, d), lambda b, i: (b, i, 0)),
        out_shape=jax.ShapeDtypeStruct((bsz, s, d), F32),
        scratch_shapes=[pltpu.VMEM((2, tc * TOP_K, d), F32), pltpu.SemaphoreType.DMA((2,))],
        compiler_params=_params(("arbitrary", "arbitrary")),
        name="combine",
    )(pos_tiles, pos_tiles, y_sorted, wts, h2, x1, gate, sg, su, sd)


def _t5_bucket(dist):
    exact = N_BUCKETS // 2
    d = np.asarray(dist)
    large = exact + (np.log(np.maximum(d, 1) / exact) / np.log(REL_MAX_DIST / exact) * (N_BUCKETS - exact)).astype(np.int32)
    large = np.minimum(large, N_BUCKETS - 1)
    return np.where(d < exact, d, large).astype(np.int32)


def _group_bias(rel_bias, gi):
    window, dil = ATTN_GROUPS[gi]
    bkt = _t5_bucket(np.arange(window // dil + 1) * dil)
    return rel_bias[bkt, gi * HEADS:(gi + 1) * HEADS].T.astype(F32)


def _prompt_bias_table(bias_j):
    blk = ATTN_BLOCK
    n = bias_j.shape[1] - 1
    a = np.arange(blk)[:, None]
    c = np.arange(2 * blk)[None, :]
    j = blk + a - c
    band = (j >= 0) & (j <= n)
    tab = jnp.where(band[None], bias_j[:, np.clip(j, 0, n)], NEG)
    hq = MXU_DIM // HEAD_DIM
    return tab.reshape(HEADS // hq, hq * blk, 2 * blk)


def _decode_bias_tables(bias_j):
    n = ATTN_STEPS
    old = jnp.zeros((n, LANES), F32).at[:, :HEADS].set(bias_j[:, n - np.arange(n)].T)
    new = jnp.zeros((1, LANES), F32).at[:, :HEADS].set(bias_j[:, 0][None, :])
    return old, new


def _visit_schedule(counts, na):
    te = EXPERT_TILE
    n_tiles = na // te
    n_steps = n_tiles + N_EXPERTS - 1
    ends = jnp.cumsum(counts)
    offs = ends - counts
    first_tile = offs // te
    n_vis = jnp.where(counts > 0, (ends - 1) // te - first_tile + 1, 0)
    v_end = jnp.cumsum(n_vis)
    v_start = v_end - n_vis
    step = jnp.arange(n_steps, dtype=jnp.int32)
    exp_id = jnp.minimum(jnp.sum(v_end[None, :] <= step[:, None], axis=1), N_EXPERTS - 1).astype(jnp.int32)
    live = step < v_end[-1]
    tile_id = jnp.where(live, first_tile[exp_id] + step - v_start[exp_id], n_tiles - 1).astype(jnp.int32)
    lo = jnp.where(live, jnp.clip(offs[exp_id] - tile_id * te, 0, te), 0).astype(jnp.int32)
    hi = jnp.where(live, jnp.clip(ends[exp_id] - tile_id * te, 0, te), 0).astype(jnp.int32)
    return offs, (tile_id, exp_id, lo, hi)


def kernel(x_prompt, x_sample, cache_k_w128, cache_v_w128, cache_k_w512, cache_v_w512, cache_k_w2048, cache_v_w2048, state_pool, c_prompt, c_sample, ada_w, ada_b, norm1, norm2, w_in, q_gain, k_gain, rel_bias, pool_w, pool_scale, w_br_a, w_br_b, w_out, router_w, router_bias, exp_w_gate, exp_w_up, exp_w_down, sh_w_gate, sh_w_up, sh_w_down):
    depth = ada_w.shape[0]
    bsz, seq, d = x_prompt.shape
    db = x_sample.shape[0]
    assert x_sample.shape[1] == 1 and db <= SAMPLE_PAD and seq % (ATTN_BLOCK * ATTN_GROUPS[-1][1]) == 0
    k_caches = (cache_k_w128, cache_k_w512, cache_k_w2048)
    v_caches = (cache_v_w128, cache_v_w512, cache_v_w2048)
    for c, (w, dil) in zip(k_caches, ATTN_GROUPS):
        assert c.shape[2] == w == ATTN_STEPS * dil

    bias_js = [_group_bias(rel_bias, g) for g in range(N_GROUPS)]
    bias_tabs = [_prompt_bias_table(b) for b in bias_js]
    dec = [_decode_bias_tables(b) for b in bias_js]
    bdec = jnp.stack([t[0] for t in dec])
    bnew = jnp.stack([t[1] for t in dec])
    head_of_col = np.arange(GROUP_W) // HEAD_DIM
    ho = jnp.asarray(head_of_col[:, None] == np.arange(LANES)[None, :], F32)
    ex = ho.T
    ex_bf = ex.astype(BF16)
    bd = jnp.asarray((np.arange(MXU_DIM)[:, None] // HEAD_DIM == np.arange(MXU_DIM)[None, :] // HEAD_DIM)
                     / HEAD_DIM, BF16)
    wcols = np.repeat(np.asarray(POOL_WINDOWS), POOL_GROUP)
    pw_dec = jnp.asarray(np.where(np.arange(POOL_STATE + 1)[:, None] >= POOL_STATE + 1 - wcols[None, :],
                                  1.0 / wcols[None, :], 0.0), F32)

    xp = x_prompt
    xs = jnp.pad(x_sample.reshape(db, d), ((0, SAMPLE_PAD - db), (0, 0))).reshape(1, SAMPLE_PAD, d)
    n_prompt = bsz * seq
    outs = {name: [] for name in ('pk', 'pv', 'sk', 'sv')}
    outs['pk'] = [[] for _ in range(N_GROUPS)]
    outs['pv'] = [[] for _ in range(N_GROUPS)]
    outs['sk'] = [[] for _ in range(N_GROUPS)]
    outs['sv'] = [[] for _ in range(N_GROUPS)]
    ppool, spool = [], []
    for l in range(depth):
        c_all = jnp.concatenate([c_prompt, c_sample], axis=0)
        mod = _ada_call(c_all, ada_w[l], ada_b[l])
        mod_p = [m.reshape(bsz, 1, d) for m in jnp.split(mod[:bsz], 6, axis=-1)]
        mod_s = [jnp.pad(m, ((0, SAMPLE_PAD - db), (0, 0))).reshape(1, SAMPLE_PAD, d)
                 for m in jnp.split(mod[bsz:], 6, axis=-1)]
        w_in_bf = w_in[l].astype(BF16)
        qg = (jnp.tile(q_gain[l], (1, HEADS)) * (HEAD_DIM ** -0.5)).reshape(1, -1)
        kg = jnp.tile(k_gain[l], (1, HEADS)).reshape(1, -1)
        g1 = norm1[l].reshape(1, d)
        g2 = norm2[l].reshape(1, d)
        wa, wb, wo = (w_br_a[l].astype(BF16), w_br_b[l].astype(BF16), w_out[l].astype(BF16))
        pw = pool_w[l].astype(BF16)
        ps = pool_scale[l].reshape(1, -1)

        q, k, v, u, gates = _inproj_call(xp, mod_p[0], mod_p[1], g1, w_in_bf, qg, kg, bd)
        o_g, lse_g = [], []
        for g, (window, dil) in enumerate(ATTN_GROUPS):
            o, lse = _attn_call(q, k, v, bias_tabs[g], g, dil)
            o_g.append(o)
            lse_g.append(lse)
            keep = min(window, seq)
            cs = slice(g * GROUP_W, (g + 1) * GROUP_W)
            outs['pk'][g].append(k[:, seq - keep:, cs].reshape(bsz, keep, HEADS, HEAD_DIM))
            outs['pv'][g].append(v[:, seq - keep:, cs].reshape(bsz, keep, HEADS, HEAD_DIM))
        ppool.append(u[:, seq - POOL_STATE:])
        x1p, h2p = _merge_call(xp, gates, (o_g, lse_g, u), (mod_p[2], mod_p[3], mod_p[4]),
                               g2, wa, wb, wo, pw, ps, ex_bf, prompt=True)

        qs, ks, vs, us, gates_s = _inproj_call(xs, mod_s[0], mod_s[1], g1, w_in_bf, qg, kg, bd)
        kcl = [c[l].reshape(db, c.shape[2], GROUP_W) for c in k_caches]
        vcl = [c[l].reshape(db, c.shape[2], GROUP_W) for c in v_caches]
        oa_s, pooled_s = _decode_call(qs[0, :db], ks[0, :db], vs[0, :db], us[0, :db], state_pool[l],
                                      kcl, vcl, bdec, bnew, ho, ex, pw_dec)
        padrows = lambda t: jnp.pad(t.reshape(db, GROUP_W), ((0, SAMPLE_PAD - db), (0, 0))).reshape(1, SAMPLE_PAD, GROUP_W)
        x1s, h2s = _merge_call(xs, gates_s, (padrows(oa_s), padrows(pooled_s)), (mod_s[2], mod_s[3], mod_s[4]),
                               g2, wa, wb, wo, pw, ps, ex_bf, prompt=False)
        for g in range(N_GROUPS):
            cs = slice(g * GROUP_W, (g + 1) * GROUP_W)
            nk = jnp.concatenate([kcl[g][:, 1:], ks[0, :db, None, cs]], axis=1)
            nv = jnp.concatenate([vcl[g][:, 1:], vs[0, :db, None, cs]], axis=1)
            outs['sk'][g].append(nk.reshape(db, -1, HEADS, HEAD_DIM))
            outs['sv'][g].append(nv.reshape(db, -1, HEADS, HEAD_DIM))
        spool.append(jnp.concatenate([state_pool[l][:, 1:], us[0, :db, None, :]], axis=1))

        h2 = jnp.concatenate([h2p.reshape(n_prompt, d), h2s.reshape(SAMPLE_PAD, d)], axis=0)
        nt = h2.shape[0]
        idx, wts, rank, counts = _router_call(h2, router_w[l], router_bias[l])
        offs, visits = _visit_schedule(counts[0], nt * TOP_K)
        pos = _slot_call(idx, rank, offs.astype(F32).reshape(1, -1))[:, :TOP_K]
        xs_sorted = _dispatch_call(pos.reshape(nt // ROW_TILE, 1, ROW_TILE * TOP_K), h2)
        y_sorted = _expert_call(visits, xs_sorted, exp_w_gate[l], exp_w_up[l], exp_w_down[l])
        tc = COMBINE_TILE
        pos_tiles = pos.reshape(nt // tc, tc, TOP_K).transpose(0, 2, 1).reshape(nt // tc, 1, TOP_K * tc)
        sg, su, sd = sh_w_gate[l].astype(BF16), sh_w_up[l].astype(BF16), sh_w_down[l].astype(BF16)
        xp = _combine_call(pos_tiles, 0, y_sorted, wts, h2, x1p, mod_p[5], sg, su, sd)
        xs = _combine_call(pos_tiles, n_prompt // tc, y_sorted, wts, h2, x1s, mod_s[5], sg, su, sd)

    stack = lambda parts: jnp.stack(parts)
    res = [xp, xs[0, :db].reshape(db, 1, d)]
    for g in range(N_GROUPS):
        res += [stack(outs['pk'][g]), stack(outs['pv'][g])]
    res.append(stack(ppool))
    for g in range(N_GROUPS):
        res += [stack(outs['sk'][g]), stack(outs['sv'][g])]
    res.append(stack(spool))
    return tuple(res)
```

```python
import functools

import numpy as np
import jax
import jax.numpy as jnp
from jax import lax
from jax.experimental import pallas as pl
from jax.experimental.pallas import tpu as pltpu

F32 = jnp.float32
BF16 = jnp.bfloat16
HIGHEST = lax.Precision.HIGHEST

HEAD_DIM = 64
HEADS = 8
GROUP_W = HEADS * HEAD_DIM
ATTN_GROUPS = ((128, 1), (512, 4), (2048, 16))
N_GROUPS = len(ATTN_GROUPS)
ATTN_STEPS = 128
ATTN_BLOCK = 128
POOL_WINDOWS = (2, 4, 8, 16)
POOL_GROUP = 128
POOL_STATE = 15
N_BUCKETS = 32
REL_MAX_DIST = 2048
N_EXPERTS = 256
TOP_K = 8
N_EXPERT_GROUPS = 8
TOPK_GROUPS = 4
ROUTED_SCALE = 2.5
EPS = 1e-6
NEG = -1e30

LANES = 128
MXU_DIM = 256
VMEM_LIMIT = 56 * 1024 * 1024

ROW_TILE = 256
EXPERT_TILE = 256
COMBINE_TILE = 128
SAMPLE_PAD = 256


def _sigmoid(x):
    return 1.0 / (1.0 + jnp.exp(-x))


def _params(sem):
    return pltpu.CompilerParams(dimension_semantics=sem, vmem_limit_bytes=VMEM_LIMIT)


def _ada_body(c_ref, w_ref, b_ref, o_ref):
    c = c_ref[...]
    s = c * _sigmoid(c)
    o_ref[...] = jnp.dot(s, w_ref[...], precision=HIGHEST, preferred_element_type=F32) + b_ref[...]


def _ada_call(c, w, b):
    n, d = c.shape
    cols = w.shape[1]
    tn = 512
    return pl.pallas_call(
        _ada_body,
        grid=(cols // tn,),
        in_specs=[pl.BlockSpec((n, d), lambda j: (0, 0)),
                  pl.BlockSpec((d, tn), lambda j: (0, j)),
                  pl.BlockSpec((1, tn), lambda j: (0, j))],
        out_specs=pl.BlockSpec((n, tn), lambda j: (0, j)),
        out_shape=jax.ShapeDtypeStruct((n, cols), F32),
        compiler_params=_params(("arbitrary",)),
        name="adaln",
    )(c, w, b.reshape(1, cols))


def _inproj_body(x_ref, shift_ref, scale_ref, g_ref, w_ref, qg_ref, kg_ref, bd_ref,
                 q_ref, k_ref, v_ref, u_ref, gt_ref):
    x = x_ref[0]
    ms = jnp.mean(x * x, axis=-1, keepdims=True)
    h = x * lax.rsqrt(ms + EPS) * g_ref[...]
    h = h * (1.0 + scale_ref[0]) + shift_ref[0]
    hb = h.astype(BF16)
    qkv = N_GROUPS * GROUP_W
    n_chunks = w_ref.shape[1] // GROUP_W
    for c in range(n_chunks):
        z = jnp.dot(hb, w_ref[:, c * GROUP_W:(c + 1) * GROUP_W], preferred_element_type=F32)
        if c < 2 * N_GROUPS:
            zz = (z * z).astype(BF16)
            hms = jnp.concatenate(
                [jnp.dot(zz[:, s:s + MXU_DIM], bd_ref[...], preferred_element_type=F32)
                 for s in range(0, GROUP_W, MXU_DIM)], axis=1)
            g = c % N_GROUPS
            gain = (qg_ref if c < N_GROUPS else kg_ref)[:, g * GROUP_W:(g + 1) * GROUP_W]
            y = z * lax.rsqrt(hms + EPS) * gain
            if c < N_GROUPS:
                q_ref[0, :, g * GROUP_W:(g + 1) * GROUP_W] = y.astype(BF16)
            else:
                k_ref[0, :, g * GROUP_W:(g + 1) * GROUP_W] = y
        elif c < 3 * N_GROUPS:
            g = c - 2 * N_GROUPS
            v_ref[0, :, g * GROUP_W:(g + 1) * GROUP_W] = z
        elif c == 3 * N_GROUPS:
            u_ref[0] = z
        else:
            g = c - 3 * N_GROUPS - 1
            gt_ref[0, :, g * GROUP_W:(g + 1) * GROUP_W] = z.astype(BF16)
    del qkv


def _mod_spec(mod, tm, d):
    if mod.shape[1] == 1:
        return pl.BlockSpec((1, 1, d), lambda b, i: (b, 0, 0))
    return pl.BlockSpec((1, tm, d), lambda b, i: (b, i, 0))


def _inproj_call(x, shift, scale, g1, w_bf, qg, kg, bd):
    bsz, s, d = x.shape
    tm = min(ROW_TILE, s)
    qkv = N_GROUPS * GROUP_W
    n_gate = w_bf.shape[1] - 3 * qkv - GROUP_W
    const = lambda b, i: (0, 0)
    row = lambda width: pl.BlockSpec((1, tm, width), lambda b, i: (b, i, 0))
    return pl.pallas_call(
        _inproj_body,
        grid=(bsz, s // tm),
        in_specs=[row(d), _mod_spec(shift, tm, d), _mod_spec(scale, tm, d),
                  pl.BlockSpec((1, d), const),
                  pl.BlockSpec(w_bf.shape, const),
                  pl.BlockSpec((1, qkv), const), pl.BlockSpec((1, qkv), const),
                  pl.BlockSpec((MXU_DIM, MXU_DIM), const)],
        out_specs=[row(qkv), row(qkv), row(qkv), row(GROUP_W), row(n_gate)],
        out_shape=[jax.ShapeDtypeStruct((bsz, s, qkv), BF16),
                   jax.ShapeDtypeStruct((bsz, s, qkv), F32),
                   jax.ShapeDtypeStruct((bsz, s, qkv), F32),
                   jax.ShapeDtypeStruct((bsz, s, GROUP_W), F32),
                   jax.ShapeDtypeStruct((bsz, s, n_gate), BF16)],
        compiler_params=_params(("arbitrary", "arbitrary")),
        name="inproj",
    )(x, shift, scale, g1, w_bf, qg, kg, bd)


def _attn_body(q_ref, kp_ref, kc_ref, vp_ref, vc_ref, bias_ref, o_ref, lse_ref):
    first = pl.program_id(2) == 0
    blk = ATTN_BLOCK
    qf = q_ref[0].astype(F32)
    k2 = jnp.concatenate([kp_ref[0], kc_ref[0]], axis=0).astype(BF16)
    v2 = jnp.concatenate([vp_ref[0], vc_ref[0]], axis=0).astype(BF16)
    hq = MXU_DIM // HEAD_DIM
    lane_head = lax.broadcasted_iota(jnp.int32, (blk, MXU_DIM), 1) // HEAD_DIM
    col = lax.broadcasted_iota(jnp.int32, (hq * blk, 2 * blk), 1)
    lane_l = lax.broadcasted_iota(jnp.int32, (blk, LANES), 1)
    lse_tile = jnp.zeros((blk, LANES), F32)
    outs = []
    for sl in range(GROUP_W // MXU_DIM):
        cs = slice(sl * MXU_DIM, (sl + 1) * MXU_DIM)
        qq = qf[:, cs]
        qs = jnp.concatenate([jnp.where(lane_head == h, qq, 0.0) for h in range(hq)], axis=0).astype(BF16)
        s = lax.dot_general(qs, k2[:, cs], (((1,), (1,)), ((), ())), preferred_element_type=F32)
        s = s + bias_ref[sl]
        s = jnp.where(jnp.logical_and(first, col < blk), NEG, s)
        m = jnp.max(s, axis=-1, keepdims=True)
        p = jnp.exp(s - m)
        l = jnp.sum(p, axis=-1, keepdims=True)
        o4 = jnp.dot(p.astype(BF16), v2[:, cs], preferred_element_type=F32) / l
        oq = jnp.zeros((blk, MXU_DIM), F32)
        lse = m + jnp.log(l)
        for h in range(hq):
            oq = jnp.where(lane_head == h, o4[h * blk:(h + 1) * blk], oq)
            lse_tile = jnp.where(lane_l == sl * hq + h, lse[h * blk:(h + 1) * blk], lse_tile)
        outs.append(oq)
    o_ref[0] = jnp.concatenate(outs, axis=1).astype(BF16)
    lse_ref[0] = lse_tile


def _attn_call(q, k, v, bias_tab, g, dil):
    bsz, s, qkv = q.shape
    blk = ATTN_BLOCK
    ll = s // dil
    nb = ll // blk
    ng = qkv // GROUP_W
    view = lambda t: t.reshape(bsz, ll, dil * t.shape[-1])
    cur = pl.BlockSpec((1, blk, GROUP_W), lambda b, r, i: (b, i, r * ng + g))
    prev = pl.BlockSpec((1, blk, GROUP_W), lambda b, r, i: (b, jnp.maximum(i - 1, 0), r * ng + g))
    o, lse = pl.pallas_call(
        _attn_body,
        grid=(bsz, dil, nb),
        in_specs=[cur, prev, cur, prev, cur,
                  pl.BlockSpec(bias_tab.shape, lambda b, r, i: (0, 0, 0))],
        out_specs=[pl.BlockSpec((1, blk, GROUP_W), lambda b, r, i: (b, i, r)),
                   pl.BlockSpec((1, blk, LANES), lambda b, r, i: (b, i, r))],
        out_shape=[jax.ShapeDtypeStruct((bsz, ll, dil * GROUP_W), BF16),
                   jax.ShapeDtypeStruct((bsz, ll, dil * LANES), F32)],
        compiler_params=_params(("arbitrary", "arbitrary", "arbitrary")),
        name=f"attn_g{g}",
    )(view(q), view(k), view(k), view(v), view(v), bias_tab)
    return o.reshape(bsz, s, GROUP_W), lse.reshape(bsz, s, LANES)


def _decode_body(q_ref, kn_ref, vn_ref, u_ref, pool_ref, kc0, vc0, kc1, vc1, kc2, vc2,
                 bdec_ref, bnew_ref, pw_ref, oa_ref, pooled_ref):
    kcs = (kc0, kc1, kc2)
    vcs = (vc0, vc1, vc2)
    outs, lses = [], []
    for g in range(N_GROUPS):
        hs = slice(g * HEADS, (g + 1) * HEADS)
        q = q_ref[0, hs, :]
        kn = kn_ref[0, hs, :]
        vn = vn_ref[0, hs, :]
        kc = kcs[g][0]
        vc = vcs[g][0]
        s_old = jnp.sum(kc * q[None], axis=-1, keepdims=True) + bdec_ref[g]
        s_new = jnp.sum(kn * q, axis=-1, keepdims=True) + bnew_ref[g]
        m = jnp.maximum(jnp.max(s_old, axis=0), s_new)
        p_old = jnp.exp(s_old - m[None])
        p_new = jnp.exp(s_new - m)
        l = jnp.sum(p_old, axis=0) + p_new
        outs.append((jnp.sum(p_old * vc, axis=0) + p_new * vn) / l)
        lses.append(m + jnp.log(l))
    mx = jnp.maximum(jnp.maximum(lses[0], lses[1]), lses[2])
    es = [jnp.exp(t - mx) for t in lses]
    den = es[0] + es[1] + es[2]
    oa_ref[0] = (es[0] / den) * outs[0] + (es[1] / den) * outs[1] + (es[2] / den) * outs[2]
    u = u_ref[0]
    stored = jnp.sum(pool_ref[0] * pw_ref[0:POOL_STATE, :], axis=0, keepdims=True)
    pooled_ref[0] = stored + u * pw_ref[POOL_STATE:POOL_STATE + 1, :] - u


def _decode_call(q, kn, vn, u, pool_prev, k_caches, v_caches, bdec, bnew, pw):
    db = pool_prev.shape[0]
    per_b = lambda t: pl.BlockSpec((1,) + t.shape[1:], lambda b: (b,) + (0,) * (t.ndim - 1))
    const = lambda t: pl.BlockSpec(t.shape, lambda b: (0,) * t.ndim)
    cache_specs, cache_args = [], []
    for g, (_, dil) in enumerate(ATTN_GROUPS):
        for c in (k_caches[g], v_caches[g]):
            rows = c.shape[1]
            cache_args.append(c.reshape(db, rows // dil, dil, HEADS, HEAD_DIM))
            cache_specs.append(pl.BlockSpec((1, ATTN_STEPS, None, HEADS, HEAD_DIM), lambda b: (b, 0, 0, 0, 0)))
    u3 = u.reshape(db, 1, u.shape[-1])
    return pl.pallas_call(
        _decode_body,
        grid=(db,),
        in_specs=[per_b(q), per_b(kn), per_b(vn), per_b(u3), per_b(pool_prev)] + cache_specs +
                 [const(bdec), const(bnew), const(pw)],
        out_specs=[pl.BlockSpec((1, HEADS, HEAD_DIM), lambda b: (b, 0, 0)),
                   pl.BlockSpec((1, 1, GROUP_W), lambda b: (b, 0, 0))],
        out_shape=[jax.ShapeDtypeStruct((db, HEADS, HEAD_DIM), F32),
                   jax.ShapeDtypeStruct((db, 1, GROUP_W), F32)],
        compiler_params=_params(("arbitrary",)),
        name="decode_mix",
    )(q, kn, vn, u3, pool_prev, *cache_args, bdec, bnew, pw)


def _roll_body(*refs):
    n = (len(refs) - 1) // 3
    caches, news, outs, sem = refs[:n], refs[n:2 * n], refs[2 * n:3 * n], refs[-1]
    copies = []
    for j in range(n):
        rows = caches[j].shape[1]
        copies.append(pltpu.make_async_copy(caches[j].at[:, pl.ds(1, rows - 1)],
                                            outs[j].at[:, pl.ds(0, rows - 1)], sem.at[2 * j]))
        copies.append(pltpu.make_async_copy(news[j], outs[j].at[:, pl.ds(rows - 1, 1)], sem.at[2 * j + 1]))
    for c in copies:
        c.start()
    for c in copies:
        c.wait()


def _roll_call(caches, news):
    n = len(caches)
    any_spec = pl.BlockSpec(memory_space=pl.ANY)
    return pl.pallas_call(
        _roll_body,
        in_specs=[any_spec] * (2 * n),
        out_specs=[any_spec] * n,
        out_shape=[jax.ShapeDtypeStruct(c.shape, c.dtype) for c in caches],
        scratch_shapes=[pltpu.SemaphoreType.DMA((2 * n,))],
        name="cache_roll",
    )(*caches, *news)


def _expand_heads(w, ex_bf):
    hi = w.astype(BF16)
    lo = (w - hi.astype(F32)).astype(BF16)
    return (jnp.dot(hi, ex_bf, preferred_element_type=F32) +
            jnp.dot(lo, ex_bf, preferred_element_type=F32))


def _merge_tail(x, oa, pooled, gates, gate_msa, shift2, scale2, g2, wa, wb, wo, pw_ref, ps,
                x1_ref, h2_ref):
    mixed = jnp.concatenate(
        [jnp.dot(pooled[:, j * POOL_GROUP:(j + 1) * POOL_GROUP].astype(BF16), pw_ref[j],
                 preferred_element_type=F32) for j in range(len(POOL_WINDOWS))], axis=1) * ps
    d = x.shape[-1]
    ga = gates[:, :d].astype(F32)
    gb = gates[:, d:].astype(F32)
    merged = (_sigmoid(ga) * jnp.dot(oa.astype(BF16), wa, preferred_element_type=F32) +
              _sigmoid(gb) * jnp.dot(mixed.astype(BF16), wb, preferred_element_type=F32))
    y = jnp.dot(merged.astype(BF16), wo, preferred_element_type=F32)
    x1 = x + gate_msa * y
    x1_ref[0] = x1
    ms = jnp.mean(x1 * x1, axis=-1, keepdims=True)
    h2 = x1 * lax.rsqrt(ms + EPS) * g2
    h2_ref[0] = h2 * (1.0 + scale2) + shift2


def _merge_prompt_body(x_ref, gt_ref, o0, o1, o2, l0, l1, l2, u_ref, uh_ref,
                       gm_ref, sh_ref, sc_ref, g2_ref, wa_ref, wb_ref, wo_ref, pw_ref, ps_ref, ex_ref,
                       x1_ref, h2_ref):
    i = pl.program_id(1)
    tm = x_ref.shape[1]
    ls = (l0[0], l1[0], l2[0])
    mx = jnp.maximum(jnp.maximum(ls[0], ls[1]), ls[2])
    es = [jnp.exp(t - mx) for t in ls]
    den = es[0] + es[1] + es[2]
    ex = ex_ref[...]
    oa = jnp.zeros((tm, GROUP_W), F32)
    for e, o in zip(es, (o0, o1, o2)):
        oa = oa + _expand_heads(e / den, ex) * o[0].astype(F32)
    u = u_ref[0]
    halo = jnp.where(i == 0, 0.0, uh_ref[0])
    ext = jnp.concatenate([halo, u], axis=0)
    hw = halo.shape[0]
    acc = ext
    parts = []
    for j, w in enumerate(POOL_WINDOWS):
        acc = acc[:, POOL_GROUP * (1 if j else 0):]
        acc = acc + pltpu.roll(acc, w // 2, axis=0)
        parts.append(acc[hw:, :POOL_GROUP])
    tot = jnp.concatenate(parts, axis=1)
    lane = lax.broadcasted_iota(jnp.int32, (tm, GROUP_W), 1)
    wcol = jnp.left_shift(2, lane // POOL_GROUP)
    pos = i * tm + lax.broadcasted_iota(jnp.int32, (tm, GROUP_W), 0)
    cnt = jnp.minimum(pos + 1, wcol).astype(F32)
    pooled = tot / cnt - u
    _merge_tail(x_ref[0], oa, pooled, gt_ref[0], gm_ref[0], sh_ref[0], sc_ref[0], g2_ref[...],
                wa_ref[...], wb_ref[...], wo_ref[...], pw_ref, ps_ref[...], x1_ref, h2_ref)


def _merge_sample_body(x_ref, gt_ref, oa_ref, pooled_ref,
                       gm_ref, sh_ref, sc_ref, g2_ref, wa_ref, wb_ref, wo_ref, pw_ref, ps_ref,
                       x1_ref, h2_ref):
    _merge_tail(x_ref[0], oa_ref[0], pooled_ref[0], gt_ref[0], gm_ref[0], sh_ref[0], sc_ref[0], g2_ref[...],
                wa_ref[...], wb_ref[...], wo_ref[...], pw_ref, ps_ref[...], x1_ref, h2_ref)


def _merge_call(x, gates, mixer_inputs, mods, g2, wa, wb, wo, pw, ps, ex_bf, prompt):
    bsz, s, d = x.shape
    tm = min(ROW_TILE, s)
    row = lambda t: pl.BlockSpec((1, tm, t.shape[-1]), lambda b, i: (b, i, 0))
    const = lambda t: pl.BlockSpec(t.shape, lambda b, i: (0,) * t.ndim)
    gm, sh, sc = mods
    in_specs = [row(x), row(gates)]
    args = [x, gates]
    if prompt:
        outs, lses, u = mixer_inputs
        hw = 16
        in_specs += [row(t) for t in outs] + [row(t) for t in lses] + [
            row(u), pl.BlockSpec((1, hw, u.shape[-1]), lambda b, i: (b, jnp.maximum(i * (tm // hw) - 1, 0), 0))]
        args += list(outs) + list(lses) + [u, u]
        body = _merge_prompt_body
    else:
        in_specs += [row(t) for t in mixer_inputs]
        args += list(mixer_inputs)
        body = _merge_sample_body
    in_specs += [_mod_spec(m, tm, d) for m in (gm, sh, sc)]
    in_specs += [const(t) for t in (g2, wa, wb, wo, pw, ps)]
    args += [gm, sh, sc, g2, wa, wb, wo, pw, ps]
    if prompt:
        in_specs.append(const(ex_bf))
        args.append(ex_bf)
    return pl.pallas_call(
        body,
        grid=(bsz, s // tm),
        in_specs=in_specs,
        out_specs=[row(x), row(x)],
        out_shape=[jax.ShapeDtypeStruct((bsz, s, d), F32), jax.ShapeDtypeStruct((bsz, s, d), F32)],
        compiler_params=_params(("arbitrary", "arbitrary")),
        name="merge_prompt" if prompt else "merge_sample",
    )(*args)


def _router_body(h_ref, rw_ref, rb_ref, tri_ref, idx_ref, wt_ref, rank_ref, cnt_ref, carry):
    tm = h_ref.shape[0]

    @pl.when(pl.program_id(0) == 0)
    def _():
        carry[...] = jnp.zeros_like(carry)

    ne = rw_ref.shape[1]
    gsz = ne // N_EXPERT_GROUPS
    logits = jnp.dot(h_ref[...], rw_ref[...], precision=HIGHEST, preferred_element_type=F32)
    scores = _sigmoid(logits)
    sel = scores + rb_ref[...]
    lane = lax.broadcasted_iota(jnp.int32, (tm, ne), 1)
    lanef = lane.astype(F32)
    grp = lane // gsz
    ninf = -jnp.inf
    far = float(ne)

    def first_max(x):
        m = jnp.max(x, axis=-1, keepdims=True)
        at = jnp.min(jnp.where(x == m, lanef, far), axis=-1, keepdims=True)
        return m, at

    gs = []
    for g in range(N_EXPERT_GROUPS):
        xg = jnp.where(grp == g, sel, ninf)
        m1, a1 = first_max(xg)
        m2 = jnp.max(jnp.where(lanef == a1, ninf, xg), axis=-1, keepdims=True)
        gs.append(m1 + m2)
    allowed = jnp.zeros((tm, ne), jnp.bool_)
    for g in range(N_EXPERT_GROUPS):
        ahead = jnp.zeros((tm, 1), F32)
        for o in range(N_EXPERT_GROUPS):
            if o == g:
                continue
            beats = (gs[o] > gs[g]) | ((gs[o] == gs[g]) & (o < g))
            ahead = ahead + beats.astype(F32)
        allowed = allowed | ((grp == g) & (ahead < TOPK_GROUPS))
    masked = jnp.where(allowed, sel, ninf)
    lane_o = lax.broadcasted_iota(jnp.int32, (tm, LANES), 1)
    idx_t = jnp.zeros((tm, LANES), F32)
    wt_t = jnp.zeros((tm, LANES), F32)
    wsum = jnp.zeros((tm, 1), F32)
    picked = jnp.zeros((tm, ne), F32)
    ats = []
    for k in range(TOP_K):
        _, at = first_max(masked)
        hit = lanef == at
        wk = jnp.sum(jnp.where(hit, scores, 0.0), axis=-1, keepdims=True)
        masked = jnp.where(hit, ninf, masked)
        picked = picked + hit.astype(F32)
        ats.append(at)
        idx_t = jnp.where(lane_o == k, at, idx_t)
        wt_t = jnp.where(lane_o == k, wk, wt_t)
        wsum = wsum + wk
    idx_ref[...] = idx_t.astype(jnp.int32)
    wt_ref[...] = wt_t / wsum * ROUTED_SCALE
    before = jnp.dot(tri_ref[...], picked.astype(BF16), preferred_element_type=F32) + carry[...]
    rank_t = jnp.zeros((tm, LANES), F32)
    for k in range(TOP_K):
        rk = jnp.sum(jnp.where(lanef == ats[k], before, 0.0), axis=-1, keepdims=True)
        rank_t = jnp.where(lane_o == k, rk, rank_t)
    rank_ref[...] = rank_t.astype(jnp.int32)
    carry[...] = carry[...] + jnp.sum(picked, axis=0, keepdims=True)
    cnt_ref[...] = carry[...].astype(jnp.int32)


def _router_call(h2, rw, rb):
    nt, d = h2.shape
    tm = ROW_TILE
    ne = rw.shape[1]
    tri = jnp.asarray(np.tril(np.ones((tm, tm), np.float32), -1), BF16)
    tile = pl.BlockSpec((tm, LANES), lambda i: (i, 0))
    return pl.pallas_call(
        _router_body,
        grid=(nt // tm,),
        in_specs=[pl.BlockSpec((tm, d), lambda i: (i, 0)),
                  pl.BlockSpec((d, ne), lambda i: (0, 0)),
                  pl.BlockSpec((1, ne), lambda i: (0, 0)),
                  pl.BlockSpec((tm, tm), lambda i: (0, 0))],
        out_specs=[tile, tile, tile, pl.BlockSpec((1, ne), lambda i: (0, 0))],
        out_shape=[jax.ShapeDtypeStruct((nt, LANES), jnp.int32),
                   jax.ShapeDtypeStruct((nt, LANES), F32),
                   jax.ShapeDtypeStruct((nt, LANES), jnp.int32),
                   jax.ShapeDtypeStruct((1, ne), jnp.int32)],
        scratch_shapes=[pltpu.VMEM((1, ne), F32)],
        compiler_params=_params(("arbitrary",)),
        name="router",
    )(h2, rw, rb.reshape(1, ne), tri)


def _slot_body(idx_ref, rank_ref, offs_ref, pos_ref):
    tm = idx_ref.shape[0]
    ne = offs_ref.shape[1]
    idx = idx_ref[...]
    rank = rank_ref[...]
    lane = lax.broadcasted_iota(jnp.int32, (tm, ne), 1)
    lane_o = lax.broadcasted_iota(jnp.int32, (tm, LANES), 1)
    offs = offs_ref[...]
    pos = jnp.zeros((tm, LANES), F32)
    for k in range(TOP_K):
        start = jnp.sum(jnp.where(lane == idx[:, k:k + 1], offs, 0.0), axis=-1, keepdims=True)
        pos = jnp.where(lane_o == k, start, pos)
    pos_ref[...] = pos.astype(jnp.int32) + rank


def _slot_call(idx, rank, offs):
    nt = idx.shape[0]
    tm = ROW_TILE
    ne = offs.shape[1]
    tile = pl.BlockSpec((tm, LANES), lambda i: (i, 0))
    return pl.pallas_call(
        _slot_body,
        grid=(nt // tm,),
        in_specs=[tile, tile, pl.BlockSpec((1, ne), lambda i: (0, 0))],
        out_specs=tile,
        out_shape=jax.ShapeDtypeStruct((nt, LANES), jnp.int32),
        compiler_params=_params(("arbitrary",)),
        name="slots",
    )(idx, rank, offs)


def _row_copy(src_hbm, row, dst, j, sem):
    return pltpu.make_async_copy(src_hbm.at[pl.ds(row, 1), :], dst.at[pl.ds(j, 1), :], sem)


def _issue_rows(idx_ref, src_hbm, dst, sem, n):
    def one(j, carry):
        _row_copy(src_hbm, idx_ref[0, 0, j], dst, j, sem).start()
        return carry
    lax.fori_loop(0, n, one, 0, unroll=8)


def _wait_rows(src_hbm, dst, sem, n):
    pltpu.make_async_copy(src_hbm.at[pl.ds(0, n), :], dst, sem).wait()


def _dispatch_body(pos_ref, h_ref, xs_hbm, sem):
    td = h_ref.shape[0]

    def one(t, carry):
        for k in range(TOP_K):
            pltpu.make_async_copy(h_ref.at[pl.ds(t, 1), :],
                                  xs_hbm.at[pl.ds(pos_ref[0, 0, t * TOP_K + k], 1), :], sem).start()
        return carry
    lax.fori_loop(0, td, one, 0, unroll=2)
    for k in range(TOP_K):
        pltpu.make_async_copy(h_ref, xs_hbm.at[pl.ds(0, td), :], sem).wait()


def _dispatch_call(pos_tok, h2):
    nt, d = h2.shape
    td = ROW_TILE
    return pl.pallas_call(
        _dispatch_body,
        grid=(nt // td,),
        in_specs=[pl.BlockSpec((1, 1, td * TOP_K), lambda i: (i, 0, 0), memory_space=pltpu.SMEM),
                  pl.BlockSpec((td, d), lambda i: (i, 0))],
        out_specs=pl.BlockSpec(memory_space=pl.ANY),
        out_shape=jax.ShapeDtypeStruct((nt * TOP_K, d), F32),
        scratch_shapes=[pltpu.SemaphoreType.DMA],
        compiler_params=_params(("arbitrary",)),
        name="dispatch",
    )(pos_tok, h2)


def _expert_body(tile_ref, exp_ref, lo_ref, hi_ref, xs_ref, wg_ref, wu_ref, wd_ref, y_ref, wgb, wub, wdb):
    s = pl.program_id(0)
    prev = jnp.maximum(s - 1, 0)
    first_visit = jnp.logical_or(s == 0, tile_ref[s] != tile_ref[prev])
    new_expert = jnp.logical_or(s == 0, exp_ref[s] != exp_ref[prev])
    lo = lo_ref[s]
    hi = hi_ref[s]

    @pl.when(new_expert)
    def _():
        wgb[...] = wg_ref[0].astype(BF16)
        wub[...] = wu_ref[0].astype(BF16)
        wdb[...] = wd_ref[0].astype(BF16)

    @pl.when(hi > lo)
    def _():
        x = xs_ref[...].astype(BF16)
        gt = jnp.dot(x, wgb[...], preferred_element_type=F32)
        up = jnp.dot(x, wub[...], preferred_element_type=F32)
        act = (gt * _sigmoid(gt)) * up
        y = jnp.dot(act.astype(BF16), wdb[...], preferred_element_type=F32)
        row = lax.broadcasted_iota(jnp.int32, y.shape, 0)
        mine = jnp.logical_and(row >= lo, row < hi)

        @pl.when(first_visit)
        def _():
            y_ref[...] = jnp.where(mine, y, 0.0)

        @pl.when(jnp.logical_not(first_visit))
        def _():
            y_ref[...] = jnp.where(mine, y, y_ref[...])


def _expert_call(visits, xs, wg, wu, wd):
    tile_id, exp_id, lo, hi = visits
    n_steps = tile_id.shape[0]
    te = EXPERT_TILE
    na, d = xs.shape
    hid = wg.shape[-1]
    grid_spec = pltpu.PrefetchScalarGridSpec(
        num_scalar_prefetch=4,
        grid=(n_steps,),
        in_specs=[
            pl.BlockSpec((te, d), lambda s, t, e, lo, hi: (t[s], 0)),
            pl.BlockSpec((1, d, hid), lambda s, t, e, lo, hi: (e[s], 0, 0)),
            pl.BlockSpec((1, d, hid), lambda s, t, e, lo, hi: (e[s], 0, 0)),
            pl.BlockSpec((1, hid, d), lambda s, t, e, lo, hi: (e[s], 0, 0)),
        ],
        out_specs=pl.BlockSpec((te, d), lambda s, t, e, lo, hi: (t[s], 0)),
        scratch_shapes=[pltpu.VMEM((d, hid), BF16), pltpu.VMEM((d, hid), BF16), pltpu.VMEM((hid, d), BF16)],
    )
    return pl.pallas_call(
        _expert_body,
        grid_spec=grid_spec,
        out_shape=jax.ShapeDtypeStruct((na, d), F32),
        compiler_params=_params(("arbitrary",)),
        name="expert_ffn",
    )(tile_id, exp_id, lo, hi, xs, wg, wu, wd)


def _combine_body(pos_ref, posn_ref, y_hbm, wt_ref, h_ref, x1_ref, gate_ref, sg_ref, su_ref, sd_ref,
                  out_ref, ybuf, sem):
    nb, ni = pl.num_programs(0), pl.num_programs(1)
    step = pl.program_id(0) * ni + pl.program_id(1)
    slot = step % 2
    tc = h_ref.shape[0]
    rows = tc * TOP_K

    @pl.when(step == 0)
    def _():
        _issue_rows(pos_ref, y_hbm, ybuf.at[0], sem.at[0], rows)

    @pl.when(step + 1 < nb * ni)
    def _():
        _issue_rows(posn_ref, y_hbm, ybuf.at[1 - slot], sem.at[1 - slot], rows)

    _wait_rows(y_hbm, ybuf.at[slot], sem.at[slot], rows)
    wt = wt_ref[...]
    routed = jnp.zeros(out_ref.shape[1:], F32)
    for k in range(TOP_K):
        routed = routed + wt[:, k:k + 1] * ybuf[slot, k * tc:(k + 1) * tc, :]
    hb = h_ref[...].astype(BF16)
    gt = jnp.dot(hb, sg_ref[...], preferred_element_type=F32)
    up = jnp.dot(hb, su_ref[...], preferred_element_type=F32)
    shared = jnp.dot(((gt * _sigmoid(gt)) * up).astype(BF16), sd_ref[...], preferred_element_type=F32)
    out_ref[0] = x1_ref[0] + gate_ref[0] * (routed + shared)


def _combine_call(pos_tiles, tile_off, y_sorted, wts, h2, x1, gate, sg, su, sd):
    bsz, s, d = x1.shape
    tc = min(COMBINE_TILE, s)
    ni = s // tc
    n_tiles = pos_tiles.shape[0]
    lin = lambda b, i: tile_off + b * ni + i
    const = lambda t: pl.BlockSpec(t.shape, lambda b, i: (0,) * t.ndim)
    return pl.pallas_call(
        _combine_body,
        grid=(bsz, ni),
        in_specs=[
            pl.BlockSpec((1, 1, tc * TOP_K), lambda b, i: (lin(b, i), 0, 0), memory_space=pltpu.SMEM),
            pl.BlockSpec((1, 1, tc * TOP_K), lambda b, i: (jnp.minimum(lin(b, i) + 1, n_tiles - 1), 0, 0),
                         memory_space=pltpu.SMEM),
            pl.BlockSpec(memory_space=pl.ANY),
            pl.BlockSpec((tc, LANES), lambda b, i: (lin(b, i), 0)),
            pl.BlockSpec((tc, d), lambda b, i: (lin(b, i), 0)),
            pl.BlockSpec((1, tc, d), lambda b, i: (b, i, 0)),
            _mod_spec(gate, tc, d),
            const(sg), const(su), const(sd),
        ],
        out_specs=pl.BlockSpec((1, tc, d), lambda b, i: (b, i, 0)),
        out_shape=jax.ShapeDtypeStruct((bsz, s, d), F32),
        scratch_shapes=[pltpu.VMEM((2, tc * TOP_K, d), F32), pltpu.SemaphoreType.DMA((2,))],
        compiler_params=_params(("arbitrary", "arbitrary")),
        name="combine",
    )(pos_tiles, pos_tiles, y_sorted, wts, h2, x1, gate, sg, su, sd)


def _t5_bucket(dist):
    exact = N_BUCKETS // 2
    d = np.asarray(dist)
    large = exact + (np.log(np.maximum(d, 1) / exact) / np.log(REL_MAX_DIST / exact) * (N_BUCKETS - exact)).astype(np.int32)
    large = np.minimum(large, N_BUCKETS - 1)
    return np.where(d < exact, d, large).astype(np.int32)


def _group_bias(rel_bias, gi):
    window, dil = ATTN_GROUPS[gi]
    bkt = _t5_bucket(np.arange(window // dil + 1) * dil)
    onehot = jnp.asarray(bkt[:, None] == np.arange(N_BUCKETS)[None, :], F32)
    cols = rel_bias[:, gi * HEADS:(gi + 1) * HEADS].astype(F32)
    return jnp.dot(onehot, cols, precision=HIGHEST).T


def _prompt_bias_table(bias_j):
    blk = ATTN_BLOCK
    n = bias_j.shape[1] - 1
    period = 3 * blk
    row0 = jnp.concatenate([jnp.flip(bias_j, axis=1), jnp.full((HEADS, period - n - 1), NEG, F32)], axis=1)
    flat = jnp.tile(row0, (1, blk))[:, :blk * (period - 1)]
    tab = flat.reshape(HEADS, blk, period - 1)[:, :, :2 * blk]
    hq = MXU_DIM // HEAD_DIM
    return tab.reshape(HEADS // hq, hq * blk, 2 * blk)


def _decode_bias_tables(bias_j):
    old = jnp.flip(bias_j[:, 1:], axis=1).T[:, :, None]
    new = bias_j[:, 0][:, None]
    return old, new


def _visit_schedule(counts, na):
    te = EXPERT_TILE
    n_tiles = na // te
    n_steps = n_tiles + N_EXPERTS - 1
    ends = jnp.cumsum(counts)
    offs = ends - counts
    first_tile = offs // te
    n_vis = jnp.where(counts > 0, (ends - 1) // te - first_tile + 1, 0)
    v_end = jnp.cumsum(n_vis)
    v_start = v_end - n_vis
    step = jnp.arange(n_steps, dtype=jnp.int32)
    exp_id = jnp.minimum(jnp.sum(v_end[None, :] <= step[:, None], axis=1), N_EXPERTS - 1).astype(jnp.int32)
    live = step < v_end[-1]
    tile_id = jnp.where(live, first_tile[exp_id] + step - v_start[exp_id], n_tiles - 1).astype(jnp.int32)
    lo = jnp.where(live, jnp.clip(offs[exp_id] - tile_id * te, 0, te), 0).astype(jnp.int32)
    hi = jnp.where(live, jnp.clip(ends[exp_id] - tile_id * te, 0, te), 0).astype(jnp.int32)
    return offs, (tile_id, exp_id, lo, hi)


def kernel(x_prompt, x_sample, cache_k_w128, cache_v_w128, cache_k_w512, cache_v_w512, cache_k_w2048, cache_v_w2048, state_pool, c_prompt, c_sample, ada_w, ada_b, norm1, norm2, w_in, q_gain, k_gain, rel_bias, pool_w, pool_scale, w_br_a, w_br_b, w_out, router_w, router_bias, exp_w_gate, exp_w_up, exp_w_down, sh_w_gate, sh_w_up, sh_w_down):
    depth = ada_w.shape[0]
    bsz, seq, d = x_prompt.shape
    db = x_sample.shape[0]
    assert x_sample.shape[1] == 1 and db <= SAMPLE_PAD and seq % (ATTN_BLOCK * ATTN_GROUPS[-1][1]) == 0
    k_caches = (cache_k_w128, cache_k_w512, cache_k_w2048)
    v_caches = (cache_v_w128, cache_v_w512, cache_v_w2048)
    for c, (w, dil) in zip(k_caches, ATTN_GROUPS):
        assert c.shape[2] == w == ATTN_STEPS * dil

    bias_js = [_group_bias(rel_bias, g) for g in range(N_GROUPS)]
    bias_tabs = [_prompt_bias_table(b) for b in bias_js]
    dec = [_decode_bias_tables(b) for b in bias_js]
    bdec = jnp.stack([t[0] for t in dec])
    bnew = jnp.stack([t[1] for t in dec])
    head_of_col = np.arange(GROUP_W) // HEAD_DIM
    ex_bf = jnp.asarray(np.arange(LANES)[:, None] == head_of_col[None, :], BF16)
    bd = jnp.asarray((np.arange(MXU_DIM)[:, None] // HEAD_DIM == np.arange(MXU_DIM)[None, :] // HEAD_DIM)
                     / HEAD_DIM, BF16)
    wcols = np.repeat(np.asarray(POOL_WINDOWS), POOL_GROUP)
    pw_dec = jnp.asarray(np.where(np.arange(POOL_STATE + 1)[:, None] >= POOL_STATE + 1 - wcols[None, :],
                                  1.0 / wcols[None, :], 0.0), F32)

    xp = x_prompt
    xs = jnp.pad(x_sample.reshape(db, d), ((0, SAMPLE_PAD - db), (0, 0))).reshape(1, SAMPLE_PAD, d)
    n_prompt = bsz * seq
    outs = {name: [] for name in ('pk', 'pv', 'sk', 'sv')}
    outs['pk'] = [[] for _ in range(N_GROUPS)]
    outs['pv'] = [[] for _ in range(N_GROUPS)]
    outs['sk'] = [[] for _ in range(N_GROUPS)]
    outs['sv'] = [[] for _ in range(N_GROUPS)]
    ppool, spool = [], []
    for l in range(depth):
        c_all = jnp.concatenate([c_prompt, c_sample], axis=0)
        mod = _ada_call(c_all, ada_w[l], ada_b[l])
        mod_p = [m.reshape(bsz, 1, d) for m in jnp.split(mod[:bsz], 6, axis=-1)]
        mod_s = [jnp.pad(m, ((0, SAMPLE_PAD - db), (0, 0))).reshape(1, SAMPLE_PAD, d)
                 for m in jnp.split(mod[bsz:], 6, axis=-1)]
        w_in_bf = w_in[l].astype(BF16)
        qg = (jnp.tile(q_gain[l], (1, HEADS)) * (HEAD_DIM ** -0.5)).reshape(1, -1)
        kg = jnp.tile(k_gain[l], (1, HEADS)).reshape(1, -1)
        g1 = norm1[l].reshape(1, d)
        g2 = norm2[l].reshape(1, d)
        wa, wb, wo = (w_br_a[l].astype(BF16), w_br_b[l].astype(BF16), w_out[l].astype(BF16))
        pw = pool_w[l].astype(BF16)
        ps = pool_scale[l].reshape(1, -1)

        q, k, v, u, gates = _inproj_call(xp, mod_p[0], mod_p[1], g1, w_in_bf, qg, kg, bd)
        o_g, lse_g = [], []
        for g, (window, dil) in enumerate(ATTN_GROUPS):
            o, lse = _attn_call(q, k, v, bias_tabs[g], g, dil)
            o_g.append(o)
            lse_g.append(lse)
            keep = min(window, seq)
            cs = slice(g * GROUP_W, (g + 1) * GROUP_W)
            outs['pk'][g].append(k[:, seq - keep:, cs].reshape(bsz, keep, HEADS, HEAD_DIM))
            outs['pv'][g].append(v[:, seq - keep:, cs].reshape(bsz, keep, HEADS, HEAD_DIM))
        ppool.append(u[:, seq - POOL_STATE:])
        x1p, h2p = _merge_call(xp, gates, (o_g, lse_g, u), (mod_p[2], mod_p[3], mod_p[4]),
                               g2, wa, wb, wo, pw, ps, ex_bf, prompt=True)

        qs, ks, vs, us, gates_s = _inproj_call(xs, mod_s[0], mod_s[1], g1, w_in_bf, qg, kg, bd)
        heads3 = lambda t: t[0, :db].astype(F32).reshape(db, N_GROUPS * HEADS, HEAD_DIM)
        q3, k3, v3 = heads3(qs), heads3(ks), heads3(vs)
        kcl = [c[l] for c in k_caches]
        vcl = [c[l] for c in v_caches]
        oa_s, pooled_s = _decode_call(q3, k3, v3, us[0, :db], state_pool[l], kcl, vcl, bdec, bnew, pw_dec)
        padrows = lambda t: jnp.pad(t.reshape(db, GROUP_W), ((0, SAMPLE_PAD - db), (0, 0))).reshape(1, SAMPLE_PAD, GROUP_W)
        x1s, h2s = _merge_call(xs, gates_s, (padrows(oa_s), padrows(pooled_s)), (mod_s[2], mod_s[3], mod_s[4]),
                               g2, wa, wb, wo, pw, ps, ex_bf, prompt=False)
        new_rows = [t[:, None, g * HEADS:(g + 1) * HEADS] for g in range(N_GROUPS) for t in (k3, v3)]
        rolled = _roll_call([c for g in range(N_GROUPS) for c in (kcl[g], vcl[g])], new_rows)
        for g in range(N_GROUPS):
            outs['sk'][g].append(rolled[2 * g])
            outs['sv'][g].append(rolled[2 * g + 1])
        spool.append(jnp.concatenate([state_pool[l][:, 1:], us[0, :db, None, :]], axis=1))

        h2 = jnp.concatenate([h2p.reshape(n_prompt, d), h2s.reshape(SAMPLE_PAD, d)], axis=0)
        nt = h2.shape[0]
        idx, wts, rank, counts = _router_call(h2, router_w[l], router_bias[l])
        offs, visits = _visit_schedule(counts[0], nt * TOP_K)
        pos = _slot_call(idx, rank, offs.astype(F32).reshape(1, -1))[:, :TOP_K]
        xs_sorted = _dispatch_call(pos.reshape(nt // ROW_TILE, 1, ROW_TILE * TOP_K), h2)
        y_sorted = _expert_call(visits, xs_sorted, exp_w_gate[l], exp_w_up[l], exp_w_down[l])
        tc = COMBINE_TILE
        pos_tiles = pos.reshape(nt // tc, tc, TOP_K).transpose(0, 2, 1).reshape(nt // tc, 1, TOP_K * tc)
        sg, su, sd = sh_w_gate[l].astype(BF16), sh_w_up[l].astype(BF16), sh_w_down[l].astype(BF16)
        xp = _combine_call(pos_tiles, 0, y_sorted, wts, h2, x1p, mod_p[5], sg, su, sd)
        xs = _combine_call(pos_tiles, n_prompt // tc, y_sorted, wts, h2, x1s, mod_s[5], sg, su, sd)

    stack = lambda parts: jnp.stack(parts)
    res = [xp, xs[0, :db].reshape(db, 1, d)]
    for g in range(N_GROUPS):
        res += [stack(outs['pk'][g]), stack(outs['pv'][g])]
    res.append(stack(ppool))
    for g in range(N_GROUPS):
        res += [stack(outs['sk'][g]), stack(outs['sv'][g])]
    res.append(stack(spool))
    return tuple(res)
```

```python
import functools

import numpy as np
import jax
import jax.numpy as jnp
from jax import lax
from jax.experimental import pallas as pl
from jax.experimental.pallas import tpu as pltpu

F32 = jnp.float32
BF16 = jnp.bfloat16
HIGHEST = lax.Precision.HIGHEST

HEAD_DIM = 64
HEADS = 8
GROUP_W = HEADS * HEAD_DIM
ATTN_GROUPS = ((128, 1), (512, 4), (2048, 16))
N_GROUPS = len(ATTN_GROUPS)
ATTN_STEPS = 128
ATTN_BLOCK = 128
POOL_WINDOWS = (2, 4, 8, 16)
POOL_GROUP = 128
POOL_STATE = 15
N_BUCKETS = 32
REL_MAX_DIST = 2048
N_EXPERTS = 256
TOP_K = 8
N_EXPERT_GROUPS = 8
TOPK_GROUPS = 4
ROUTED_SCALE = 2.5
EPS = 1e-6
NEG = -1e30

LANES = 128
MXU_DIM = 256
VMEM_LIMIT = 56 * 1024 * 1024

ROW_TILE = 256
EXPERT_TILE = 256
COMBINE_TILE = 128
SAMPLE_PAD = 256


def _sigmoid(x):
    return 1.0 / (1.0 + jnp.exp(-x))


def _params(sem):
    return pltpu.CompilerParams(dimension_semantics=sem, vmem_limit_bytes=VMEM_LIMIT)


def _ada_body(c_ref, w_ref, b_ref, o_ref):
    c = c_ref[...]
    s = c * _sigmoid(c)
    o_ref[...] = jnp.dot(s, w_ref[...], precision=HIGHEST, preferred_element_type=F32) + b_ref[...]


def _ada_call(c, w, b):
    n, d = c.shape
    cols = w.shape[1]
    tn = 512
    return pl.pallas_call(
        _ada_body,
        grid=(cols // tn,),
        in_specs=[pl.BlockSpec((n, d), lambda j: (0, 0)),
                  pl.BlockSpec((d, tn), lambda j: (0, j)),
                  pl.BlockSpec((1, tn), lambda j: (0, j))],
        out_specs=pl.BlockSpec((n, tn), lambda j: (0, j)),
        out_shape=jax.ShapeDtypeStruct((n, cols), F32),
        compiler_params=_params(("arbitrary",)),
        name="adaln",
    )(c, w, b.reshape(1, cols))


def _inproj_body(x_ref, shift_ref, scale_ref, g_ref, w_ref, qg_ref, kg_ref, bd_ref,
                 q_ref, k_ref, v_ref, u_ref, gt_ref):
    x = x_ref[0]
    ms = jnp.mean(x * x, axis=-1, keepdims=True)
    h = x * lax.rsqrt(ms + EPS) * g_ref[...]
    h = h * (1.0 + scale_ref[0]) + shift_ref[0]
    hb = h.astype(BF16)
    qkv = N_GROUPS * GROUP_W
    n_chunks = w_ref.shape[1] // GROUP_W
    for c in range(n_chunks):
        z = jnp.dot(hb, w_ref[:, c * GROUP_W:(c + 1) * GROUP_W], preferred_element_type=F32)
        if c < 2 * N_GROUPS:
            zz = (z * z).astype(BF16)
            hms = jnp.concatenate(
                [jnp.dot(zz[:, s:s + MXU_DIM], bd_ref[...], preferred_element_type=F32)
                 for s in range(0, GROUP_W, MXU_DIM)], axis=1)
            g = c % N_GROUPS
            gain = (qg_ref if c < N_GROUPS else kg_ref)[:, g * GROUP_W:(g + 1) * GROUP_W]
            y = z * lax.rsqrt(hms + EPS) * gain
            if c < N_GROUPS:
                q_ref[0, :, g * GROUP_W:(g + 1) * GROUP_W] = y.astype(BF16)
            else:
                k_ref[0, :, g * GROUP_W:(g + 1) * GROUP_W] = y
        elif c < 3 * N_GROUPS:
            g = c - 2 * N_GROUPS
            v_ref[0, :, g * GROUP_W:(g + 1) * GROUP_W] = z
        elif c == 3 * N_GROUPS:
            u_ref[0] = z
        else:
            g = c - 3 * N_GROUPS - 1
            gt_ref[0, :, g * GROUP_W:(g + 1) * GROUP_W] = z.astype(BF16)
    del qkv


def _mod_spec(mod, tm, d):
    if mod.shape[1] == 1:
        return pl.BlockSpec((1, 1, d), lambda b, i: (b, 0, 0))
    return pl.BlockSpec((1, tm, d), lambda b, i: (b, i, 0))


def _inproj_call(x, shift, scale, g1, w_bf, qg, kg, bd):
    bsz, s, d = x.shape
    tm = min(ROW_TILE, s)
    qkv = N_GROUPS * GROUP_W
    n_gate = w_bf.shape[1] - 3 * qkv - GROUP_W
    const = lambda b, i: (0, 0)
    row = lambda width: pl.BlockSpec((1, tm, width), lambda b, i: (b, i, 0))
    return pl.pallas_call(
        _inproj_body,
        grid=(bsz, s // tm),
        in_specs=[row(d), _mod_spec(shift, tm, d), _mod_spec(scale, tm, d),
                  pl.BlockSpec((1, d), const),
                  pl.BlockSpec(w_bf.shape, const),
                  pl.BlockSpec((1, qkv), const), pl.BlockSpec((1, qkv), const),
                  pl.BlockSpec((MXU_DIM, MXU_DIM), const)],
        out_specs=[row(qkv), row(qkv), row(qkv), row(GROUP_W), row(n_gate)],
        out_shape=[jax.ShapeDtypeStruct((bsz, s, qkv), BF16),
                   jax.ShapeDtypeStruct((bsz, s, qkv), F32),
                   jax.ShapeDtypeStruct((bsz, s, qkv), F32),
                   jax.ShapeDtypeStruct((bsz, s, GROUP_W), F32),
                   jax.ShapeDtypeStruct((bsz, s, n_gate), BF16)],
        compiler_params=_params(("arbitrary", "arbitrary")),
        name="inproj",
    )(x, shift, scale, g1, w_bf, qg, kg, bd)


def _attn_body(q_ref, kp_ref, kc_ref, vp_ref, vc_ref, bias_ref, o_ref, lse_ref):
    first = pl.program_id(2) == 0
    blk = ATTN_BLOCK
    qf = q_ref[0].astype(F32)
    k2 = jnp.concatenate([kp_ref[0], kc_ref[0]], axis=0).astype(BF16)
    v2 = jnp.concatenate([vp_ref[0], vc_ref[0]], axis=0).astype(BF16)
    hq = MXU_DIM // HEAD_DIM
    lane_head = lax.broadcasted_iota(jnp.int32, (blk, MXU_DIM), 1) // HEAD_DIM
    col = lax.broadcasted_iota(jnp.int32, (hq * blk, 2 * blk), 1)
    lane_l = lax.broadcasted_iota(jnp.int32, (blk, LANES), 1)
    lse_tile = jnp.zeros((blk, LANES), F32)
    outs = []
    for sl in range(GROUP_W // MXU_DIM):
        cs = slice(sl * MXU_DIM, (sl + 1) * MXU_DIM)
        qq = qf[:, cs]
        qs = jnp.concatenate([jnp.where(lane_head == h, qq, 0.0) for h in range(hq)], axis=0).astype(BF16)
        s = lax.dot_general(qs, k2[:, cs], (((1,), (1,)), ((), ())), preferred_element_type=F32)
        s = s + bias_ref[sl]
        s = jnp.where(jnp.logical_and(first, col < blk), NEG, s)
        m = jnp.max(s, axis=-1, keepdims=True)
        p = jnp.exp(s - m)
        l = jnp.sum(p, axis=-1, keepdims=True)
        o4 = jnp.dot(p.astype(BF16), v2[:, cs], preferred_element_type=F32) / l
        oq = jnp.zeros((blk, MXU_DIM), F32)
        lse = m + jnp.log(l)
        for h in range(hq):
            oq = jnp.where(lane_head == h, o4[h * blk:(h + 1) * blk], oq)
            lse_tile = jnp.where(lane_l == sl * hq + h, lse[h * blk:(h + 1) * blk], lse_tile)
        outs.append(oq)
    o_ref[0] = jnp.concatenate(outs, axis=1).astype(BF16)
    lse_ref[0] = lse_tile


def _attn_call(q, k, v, bias_tab, g, dil):
    bsz, s, qkv = q.shape
    blk = ATTN_BLOCK
    ll = s // dil
    nb = ll // blk
    ng = qkv // GROUP_W
    view = lambda t: t.reshape(bsz, ll, dil * t.shape[-1])
    cur = pl.BlockSpec((1, blk, GROUP_W), lambda b, r, i: (b, i, r * ng + g))
    prev = pl.BlockSpec((1, blk, GROUP_W), lambda b, r, i: (b, jnp.maximum(i - 1, 0), r * ng + g))
    o, lse = pl.pallas_call(
        _attn_body,
        grid=(bsz, dil, nb),
        in_specs=[cur, prev, cur, prev, cur,
                  pl.BlockSpec(bias_tab.shape, lambda b, r, i: (0, 0, 0))],
        out_specs=[pl.BlockSpec((1, blk, GROUP_W), lambda b, r, i: (b, i, r)),
                   pl.BlockSpec((1, blk, LANES), lambda b, r, i: (b, i, r))],
        out_shape=[jax.ShapeDtypeStruct((bsz, ll, dil * GROUP_W), BF16),
                   jax.ShapeDtypeStruct((bsz, ll, dil * LANES), F32)],
        compiler_params=_params(("arbitrary", "arbitrary", "arbitrary")),
        name=f"attn_g{g}",
    )(view(q), view(k), view(k), view(v), view(v), bias_tab)
    return o.reshape(bsz, s, GROUP_W), lse.reshape(bsz, s, LANES)


def _decode_body(q_ref, kn_ref, vn_ref, u_ref, pool_ref, kc0, vc0, kc1, vc1, kc2, vc2,
                 bdec_ref, bnew_ref, pw_ref, oa_ref, pooled_ref):
    kcs = (kc0, kc1, kc2)
    vcs = (vc0, vc1, vc2)
    outs, lses = [], []
    for g in range(N_GROUPS):
        hs = slice(g * HEADS, (g + 1) * HEADS)
        q = q_ref[0, hs, :]
        kn = kn_ref[0, hs, :]
        vn = vn_ref[0, hs, :]
        kc = kcs[g][0]
        vc = vcs[g][0]
        s_old = jnp.sum(kc * q[None], axis=-1, keepdims=True) + bdec_ref[g]
        s_new = jnp.sum(kn * q, axis=-1, keepdims=True) + bnew_ref[g]
        m = jnp.maximum(jnp.max(s_old, axis=0), s_new)
        p_old = jnp.exp(s_old - m[None])
        p_new = jnp.exp(s_new - m)
        l = jnp.sum(p_old, axis=0) + p_new
        outs.append((jnp.sum(p_old * vc, axis=0) + p_new * vn) / l)
        lses.append(m + jnp.log(l))
    mx = jnp.maximum(jnp.maximum(lses[0], lses[1]), lses[2])
    es = [jnp.exp(t - mx) for t in lses]
    den = es[0] + es[1] + es[2]
    oa_ref[0] = (es[0] / den) * outs[0] + (es[1] / den) * outs[1] + (es[2] / den) * outs[2]
    u = u_ref[0]
    stored = jnp.sum(pool_ref[0] * pw_ref[0:POOL_STATE, :], axis=0, keepdims=True)
    pooled_ref[0] = stored + u * pw_ref[POOL_STATE:POOL_STATE + 1, :] - u


def _decode_call(q, kn, vn, u, pool_prev, k_caches, v_caches, bdec, bnew, pw):
    db = pool_prev.shape[0]
    per_b = lambda t: pl.BlockSpec((1,) + t.shape[1:], lambda b: (b,) + (0,) * (t.ndim - 1))
    const = lambda t: pl.BlockSpec(t.shape, lambda b: (0,) * t.ndim)
    cache_specs, cache_args = [], []
    for g, (_, dil) in enumerate(ATTN_GROUPS):
        for c in (k_caches[g], v_caches[g]):
            rows = c.shape[1]
            cache_args.append(c.reshape(db, rows // dil, dil, HEADS, HEAD_DIM))
            cache_specs.append(pl.BlockSpec((1, ATTN_STEPS, None, HEADS, HEAD_DIM), lambda b: (b, 0, 0, 0, 0)))
    u3 = u.reshape(db, 1, u.shape[-1])
    return pl.pallas_call(
        _decode_body,
        grid=(db,),
        in_specs=[per_b(q), per_b(kn), per_b(vn), per_b(u3), per_b(pool_prev)] + cache_specs +
                 [const(bdec), const(bnew), const(pw)],
        out_specs=[pl.BlockSpec((1, HEADS, HEAD_DIM), lambda b: (b, 0, 0)),
                   pl.BlockSpec((1, 1, GROUP_W), lambda b: (b, 0, 0))],
        out_shape=[jax.ShapeDtypeStruct((db, HEADS, HEAD_DIM), F32),
                   jax.ShapeDtypeStruct((db, 1, GROUP_W), F32)],
        compiler_params=_params(("arbitrary",)),
        name="decode_mix",
    )(q, kn, vn, u3, pool_prev, *cache_args, bdec, bnew, pw)


def _roll_body(c_ref, nxt_ref, new_ref, o_ref):
    ch = c_ref.shape[1]
    last = pl.program_id(1) == pl.num_programs(1) - 1
    o_ref[0, 0:ch - 1] = c_ref[0, 1:ch]
    o_ref[0, ch - 1:ch] = jnp.where(last, new_ref[0], nxt_ref[0])


def _roll_call(cache, new):
    db, rows, nh, e = cache.shape
    ch = min(rows, 256)
    return pl.pallas_call(
        _roll_body,
        grid=(db, rows // ch),
        in_specs=[pl.BlockSpec((1, ch, nh, e), lambda b, i: (b, i, 0, 0)),
                  pl.BlockSpec((1, 1, nh, e), lambda b, i: (b, jnp.minimum((i + 1) * ch, rows - 1), 0, 0)),
                  pl.BlockSpec((1, 1, nh, e), lambda b, i: (b, 0, 0, 0))],
        out_specs=pl.BlockSpec((1, ch, nh, e), lambda b, i: (b, i, 0, 0)),
        out_shape=jax.ShapeDtypeStruct(cache.shape, cache.dtype),
        compiler_params=_params(("arbitrary", "arbitrary")),
        name="cache_roll",
    )(cache, cache, new)


def _expand_heads(w, ex_bf):
    hi = w.astype(BF16)
    lo = (w - hi.astype(F32)).astype(BF16)
    return (jnp.dot(hi, ex_bf, preferred_element_type=F32) +
            jnp.dot(lo, ex_bf, preferred_element_type=F32))


def _merge_tail(x, oa, pooled, gates, gate_msa, shift2, scale2, g2, wa, wb, wo, pw_ref, ps,
                x1_ref, h2_ref):
    mixed = jnp.concatenate(
        [jnp.dot(pooled[:, j * POOL_GROUP:(j + 1) * POOL_GROUP].astype(BF16), pw_ref[j],
                 preferred_element_type=F32) for j in range(len(POOL_WINDOWS))], axis=1) * ps
    d = x.shape[-1]
    ga = gates[:, :d].astype(F32)
    gb = gates[:, d:].astype(F32)
    merged = (_sigmoid(ga) * jnp.dot(oa.astype(BF16), wa, preferred_element_type=F32) +
              _sigmoid(gb) * jnp.dot(mixed.astype(BF16), wb, preferred_element_type=F32))
    y = jnp.dot(merged.astype(BF16), wo, preferred_element_type=F32)
    x1 = x + gate_msa * y
    x1_ref[0] = x1
    ms = jnp.mean(x1 * x1, axis=-1, keepdims=True)
    h2 = x1 * lax.rsqrt(ms + EPS) * g2
    h2_ref[0] = h2 * (1.0 + scale2) + shift2


def _merge_prompt_body(x_ref, gt_ref, o0, o1, o2, l0, l1, l2, u_ref, uh_ref,
                       gm_ref, sh_ref, sc_ref, g2_ref, wa_ref, wb_ref, wo_ref, pw_ref, ps_ref, ex_ref,
                       x1_ref, h2_ref):
    i = pl.program_id(1)
    tm = x_ref.shape[1]
    ls = (l0[0], l1[0], l2[0])
    mx = jnp.maximum(jnp.maximum(ls[0], ls[1]), ls[2])
    es = [jnp.exp(t - mx) for t in ls]
    den = es[0] + es[1] + es[2]
    ex = ex_ref[...]
    oa = jnp.zeros((tm, GROUP_W), F32)
    for e, o in zip(es, (o0, o1, o2)):
        oa = oa + _expand_heads(e / den, ex) * o[0].astype(F32)
    u = u_ref[0]
    halo = jnp.where(i == 0, 0.0, uh_ref[0])
    ext = jnp.concatenate([halo, u], axis=0)
    hw = halo.shape[0]
    acc = ext
    parts = []
    for j, w in enumerate(POOL_WINDOWS):
        acc = acc[:, POOL_GROUP * (1 if j else 0):]
        acc = acc + pltpu.roll(acc, w // 2, axis=0)
        parts.append(acc[hw:, :POOL_GROUP])
    tot = jnp.concatenate(parts, axis=1)
    lane = lax.broadcasted_iota(jnp.int32, (tm, GROUP_W), 1)
    wcol = jnp.left_shift(2, lane // POOL_GROUP)
    pos = i * tm + lax.broadcasted_iota(jnp.int32, (tm, GROUP_W), 0)
    cnt = jnp.minimum(pos + 1, wcol).astype(F32)
    pooled = tot / cnt - u
    _merge_tail(x_ref[0], oa, pooled, gt_ref[0], gm_ref[0], sh_ref[0], sc_ref[0], g2_ref[...],
                wa_ref[...], wb_ref[...], wo_ref[...], pw_ref, ps_ref[...], x1_ref, h2_ref)


def _merge_sample_body(x_ref, gt_ref, oa_ref, pooled_ref,
                       gm_ref, sh_ref, sc_ref, g2_ref, wa_ref, wb_ref, wo_ref, pw_ref, ps_ref,
                       x1_ref, h2_ref):
    _merge_tail(x_ref[0], oa_ref[0], pooled_ref[0], gt_ref[0], gm_ref[0], sh_ref[0], sc_ref[0], g2_ref[...],
                wa_ref[...], wb_ref[...], wo_ref[...], pw_ref, ps_ref[...], x1_ref, h2_ref)


def _merge_call(x, gates, mixer_inputs, mods, g2, wa, wb, wo, pw, ps, ex_bf, prompt):
    bsz, s, d = x.shape
    tm = min(ROW_TILE, s)
    row = lambda t: pl.BlockSpec((1, tm, t.shape[-1]), lambda b, i: (b, i, 0))
    const = lambda t: pl.BlockSpec(t.shape, lambda b, i: (0,) * t.ndim)
    gm, sh, sc = mods
    in_specs = [row(x), row(gates)]
    args = [x, gates]
    if prompt:
        outs, lses, u = mixer_inputs
        hw = 16
        in_specs += [row(t) for t in outs] + [row(t) for t in lses] + [
            row(u), pl.BlockSpec((1, hw, u.shape[-1]), lambda b, i: (b, jnp.maximum(i * (tm // hw) - 1, 0), 0))]
        args += list(outs) + list(lses) + [u, u]
        body = _merge_prompt_body
    else:
        in_specs += [row(t) for t in mixer_inputs]
        args += list(mixer_inputs)
        body = _merge_sample_body
    in_specs += [_mod_spec(m, tm, d) for m in (gm, sh, sc)]
    in_specs += [const(t) for t in (g2, wa, wb, wo, pw, ps)]
    args += [gm, sh, sc, g2, wa, wb, wo, pw, ps]
    if prompt:
        in_specs.append(const(ex_bf))
        args.append(ex_bf)
    return pl.pallas_call(
        body,
        grid=(bsz, s // tm),
        in_specs=in_specs,
        out_specs=[row(x), row(x)],
        out_shape=[jax.ShapeDtypeStruct((bsz, s, d), F32), jax.ShapeDtypeStruct((bsz, s, d), F32)],
        compiler_params=_params(("arbitrary", "arbitrary")),
        name="merge_prompt" if prompt else "merge_sample",
    )(*args)


def _router_body(h_ref, rw_ref, rb_ref, tri_ref, idx_ref, wt_ref, rank_ref, cnt_ref, carry):
    tm = h_ref.shape[0]

    @pl.when(pl.program_id(0) == 0)
    def _():
        carry[...] = jnp.zeros_like(carry)

    ne = rw_ref.shape[1]
    gsz = ne // N_EXPERT_GROUPS
    logits = jnp.dot(h_ref[...], rw_ref[...], precision=HIGHEST, preferred_element_type=F32)
    scores = _sigmoid(logits)
    sel = scores + rb_ref[...]
    lane = lax.broadcasted_iota(jnp.int32, (tm, ne), 1)
    lanef = lane.astype(F32)
    grp = lane // gsz
    ninf = -jnp.inf
    far = float(ne)

    def first_max(x):
        m = jnp.max(x, axis=-1, keepdims=True)
        at = jnp.min(jnp.where(x == m, lanef, far), axis=-1, keepdims=True)
        return m, at

    gs = []
    for g in range(N_EXPERT_GROUPS):
        xg = jnp.where(grp == g, sel, ninf)
        m1, a1 = first_max(xg)
        m2 = jnp.max(jnp.where(lanef == a1, ninf, xg), axis=-1, keepdims=True)
        gs.append(m1 + m2)
    allowed = jnp.zeros((tm, ne), jnp.bool_)
    for g in range(N_EXPERT_GROUPS):
        ahead = jnp.zeros((tm, 1), F32)
        for o in range(N_EXPERT_GROUPS):
            if o == g:
                continue
            beats = (gs[o] > gs[g]) | ((gs[o] == gs[g]) & (o < g))
            ahead = ahead + beats.astype(F32)
        allowed = allowed | ((grp == g) & (ahead < TOPK_GROUPS))
    masked = jnp.where(allowed, sel, ninf)
    lane_o = lax.broadcasted_iota(jnp.int32, (tm, LANES), 1)
    idx_t = jnp.zeros((tm, LANES), F32)
    wt_t = jnp.zeros((tm, LANES), F32)
    wsum = jnp.zeros((tm, 1), F32)
    picked = jnp.zeros((tm, ne), F32)
    ats = []
    for k in range(TOP_K):
        _, at = first_max(masked)
        hit = lanef == at
        wk = jnp.sum(jnp.where(hit, scores, 0.0), axis=-1, keepdims=True)
        masked = jnp.where(hit, ninf, masked)
        picked = picked + hit.astype(F32)
        ats.append(at)
        idx_t = jnp.where(lane_o == k, at, idx_t)
        wt_t = jnp.where(lane_o == k, wk, wt_t)
        wsum = wsum + wk
    idx_ref[...] = idx_t.astype(jnp.int32)
    wt_ref[...] = wt_t / wsum * ROUTED_SCALE
    before = jnp.dot(tri_ref[...], picked.astype(BF16), preferred_element_type=F32) + carry[...]
    rank_t = jnp.zeros((tm, LANES), F32)
    for k in range(TOP_K):
        rk = jnp.sum(jnp.where(lanef == ats[k], before, 0.0), axis=-1, keepdims=True)
        rank_t = jnp.where(lane_o == k, rk, rank_t)
    rank_ref[...] = rank_t.astype(jnp.int32)
    carry[...] = carry[...] + jnp.sum(picked, axis=0, keepdims=True)
    cnt_ref[...] = carry[...].astype(jnp.int32)


def _router_call(h2, rw, rb):
    nt, d = h2.shape
    tm = ROW_TILE
    ne = rw.shape[1]
    tri = jnp.asarray(np.tril(np.ones((tm, tm), np.float32), -1), BF16)
    tile = pl.BlockSpec((tm, LANES), lambda i: (i, 0))
    return pl.pallas_call(
        _router_body,
        grid=(nt // tm,),
        in_specs=[pl.BlockSpec((tm, d), lambda i: (i, 0)),
                  pl.BlockSpec((d, ne), lambda i: (0, 0)),
                  pl.BlockSpec((1, ne), lambda i: (0, 0)),
                  pl.BlockSpec((tm, tm), lambda i: (0, 0))],
        out_specs=[tile, tile, tile, pl.BlockSpec((1, ne), lambda i: (0, 0))],
        out_shape=[jax.ShapeDtypeStruct((nt, LANES), jnp.int32),
                   jax.ShapeDtypeStruct((nt, LANES), F32),
                   jax.ShapeDtypeStruct((nt, LANES), jnp.int32),
                   jax.ShapeDtypeStruct((1, ne), jnp.int32)],
        scratch_shapes=[pltpu.VMEM((1, ne), F32)],
        compiler_params=_params(("arbitrary",)),
        name="router",
    )(h2, rw, rb.reshape(1, ne), tri)


def _slot_body(idx_ref, rank_ref, offs_ref, pos_ref):
    tm = idx_ref.shape[0]
    ne = offs_ref.shape[1]
    idx = idx_ref[...]
    rank = rank_ref[...]
    lane = lax.broadcasted_iota(jnp.int32, (tm, ne), 1)
    lane_o = lax.broadcasted_iota(jnp.int32, (tm, LANES), 1)
    offs = offs_ref[...]
    pos = jnp.zeros((tm, LANES), F32)
    for k in range(TOP_K):
        start = jnp.sum(jnp.where(lane == idx[:, k:k + 1], offs, 0.0), axis=-1, keepdims=True)
        pos = jnp.where(lane_o == k, start, pos)
    pos_ref[...] = pos.astype(jnp.int32) + rank


def _slot_call(idx, rank, offs):
    nt = idx.shape[0]
    tm = ROW_TILE
    ne = offs.shape[1]
    tile = pl.BlockSpec((tm, LANES), lambda i: (i, 0))
    return pl.pallas_call(
        _slot_body,
        grid=(nt // tm,),
        in_specs=[tile, tile, pl.BlockSpec((1, ne), lambda i: (0, 0))],
        out_specs=tile,
        out_shape=jax.ShapeDtypeStruct((nt, LANES), jnp.int32),
        compiler_params=_params(("arbitrary",)),
        name="slots",
    )(idx, rank, offs)


def _row_copy(src_hbm, row, dst, j, sem):
    return pltpu.make_async_copy(src_hbm.at[pl.ds(row, 1), :], dst.at[pl.ds(j, 1), :], sem)


def _issue_rows(idx_ref, src_hbm, dst, sem, n):
    def one(j, carry):
        _row_copy(src_hbm, idx_ref[0, 0, j], dst, j, sem).start()
        return carry
    lax.fori_loop(0, n, one, 0, unroll=8)


def _wait_rows(src_hbm, dst, sem, n):
    pltpu.make_async_copy(src_hbm.at[pl.ds(0, n), :], dst, sem).wait()


def _dispatch_body(pos_ref, h_ref, xs_hbm, sem):
    td = h_ref.shape[0]

    def one(t, carry):
        for k in range(TOP_K):
            pltpu.make_async_copy(h_ref.at[pl.ds(t, 1), :],
                                  xs_hbm.at[pl.ds(pos_ref[0, 0, t * TOP_K + k], 1), :], sem).start()
        return carry
    lax.fori_loop(0, td, one, 0, unroll=2)
    for k in range(TOP_K):
        pltpu.make_async_copy(h_ref, xs_hbm.at[pl.ds(0, td), :], sem).wait()


def _dispatch_call(pos_tok, h2):
    nt, d = h2.shape
    td = ROW_TILE
    return pl.pallas_call(
        _dispatch_body,
        grid=(nt // td,),
        in_specs=[pl.BlockSpec((1, 1, td * TOP_K), lambda i: (i, 0, 0), memory_space=pltpu.SMEM),
                  pl.BlockSpec((td, d), lambda i: (i, 0))],
        out_specs=pl.BlockSpec(memory_space=pl.ANY),
        out_shape=jax.ShapeDtypeStruct((nt * TOP_K, d), F32),
        scratch_shapes=[pltpu.SemaphoreType.DMA],
        compiler_params=_params(("arbitrary",)),
        name="dispatch",
    )(pos_tok, h2)


def _expert_body(tile_ref, exp_ref, lo_ref, hi_ref, xs_ref, wg_ref, wu_ref, wd_ref, y_ref, wgb, wub, wdb):
    s = pl.program_id(0)
    prev = jnp.maximum(s - 1, 0)
    first_visit = jnp.logical_or(s == 0, tile_ref[s] != tile_ref[prev])
    new_expert = jnp.logical_or(s == 0, exp_ref[s] != exp_ref[prev])
    lo = lo_ref[s]
    hi = hi_ref[s]

    @pl.when(new_expert)
    def _():
        wgb[...] = wg_ref[0].astype(BF16)
        wub[...] = wu_ref[0].astype(BF16)
        wdb[...] = wd_ref[0].astype(BF16)

    @pl.when(hi > lo)
    def _():
        x = xs_ref[...].astype(BF16)
        gt = jnp.dot(x, wgb[...], preferred_element_type=F32)
        up = jnp.dot(x, wub[...], preferred_element_type=F32)
        act = (gt * _sigmoid(gt)) * up
        y = jnp.dot(act.astype(BF16), wdb[...], preferred_element_type=F32)
        row = lax.broadcasted_iota(jnp.int32, y.shape, 0)
        mine = jnp.logical_and(row >= lo, row < hi)

        @pl.when(first_visit)
        def _():
            y_ref[...] = jnp.where(mine, y, 0.0)

        @pl.when(jnp.logical_not(first_visit))
        def _():
            y_ref[...] = jnp.where(mine, y, y_ref[...])


def _expert_call(visits, xs, wg, wu, wd):
    tile_id, exp_id, lo, hi = visits
    n_steps = tile_id.shape[0]
    te = EXPERT_TILE
    na, d = xs.shape
    hid = wg.shape[-1]
    grid_spec = pltpu.PrefetchScalarGridSpec(
        num_scalar_prefetch=4,
        grid=(n_steps,),
        in_specs=[
            pl.BlockSpec((te, d), lambda s, t, e, lo, hi: (t[s], 0)),
            pl.BlockSpec((1, d, hid), lambda s, t, e, lo, hi: (e[s], 0, 0)),
            pl.BlockSpec((1, d, hid), lambda s, t, e, lo, hi: (e[s], 0, 0)),
            pl.BlockSpec((1, hid, d), lambda s, t, e, lo, hi: (e[s], 0, 0)),
        ],
        out_specs=pl.BlockSpec((te, d), lambda s, t, e, lo, hi: (t[s], 0)),
        scratch_shapes=[pltpu.VMEM((d, hid), BF16), pltpu.VMEM((d, hid), BF16), pltpu.VMEM((hid, d), BF16)],
    )
    return pl.pallas_call(
        _expert_body,
        grid_spec=grid_spec,
        out_shape=jax.ShapeDtypeStruct((na, d), F32),
        compiler_params=_params(("arbitrary",)),
        name="expert_ffn",
    )(tile_id, exp_id, lo, hi, xs, wg, wu, wd)


def _combine_body(pos_ref, posn_ref, y_hbm, wt_ref, h_ref, x1_ref, gate_ref, sg_ref, su_ref, sd_ref,
                  out_ref, ybuf, sem):
    nb, ni = pl.num_programs(0), pl.num_programs(1)
    step = pl.program_id(0) * ni + pl.program_id(1)
    slot = step % 2
    tc = h_ref.shape[0]
    rows = tc * TOP_K

    @pl.when(step == 0)
    def _():
        _issue_rows(pos_ref, y_hbm, ybuf.at[0], sem.at[0], rows)

    @pl.when(step + 1 < nb * ni)
    def _():
        _issue_rows(posn_ref, y_hbm, ybuf.at[1 - slot], sem.at[1 - slot], rows)

    _wait_rows(y_hbm, ybuf.at[slot], sem.at[slot], rows)
    wt = wt_ref[...]
    routed = jnp.zeros(out_ref.shape[1:], F32)
    for k in range(TOP_K):
        routed = routed + wt[:, k:k + 1] * ybuf[slot, k * tc:(k + 1) * tc, :]
    hb = h_ref[...].astype(BF16)
    gt = jnp.dot(hb, sg_ref[...], preferred_element_type=F32)
    up = jnp.dot(hb, su_ref[...], preferred_element_type=F32)
    shared = jnp.dot(((gt * _sigmoid(gt)) * up).astype(BF16), sd_ref[...], preferred_element_type=F32)
    out_ref[0] = x1_ref[0] + gate_ref[0] * (routed + shared)


def _combine_call(pos_tiles, tile_off, y_sorted, wts, h2, x1, gate, sg, su, sd):
    bsz, s, d = x1.shape
    tc = min(COMBINE_TILE, s)
    ni = s // tc
    n_tiles = pos_tiles.shape[0]
    lin = lambda b, i: tile_off + b * ni + i
    const = lambda t: pl.BlockSpec(t.shape, lambda b, i: (0,) * t.ndim)
    return pl.pallas_call(
        _combine_body,
        grid=(bsz, ni),
        in_specs=[
            pl.BlockSpec((1, 1, tc * TOP_K), lambda b, i: (lin(b, i), 0, 0), memory_space=pltpu.SMEM),
            pl.BlockSpec((1, 1, tc * TOP_K), lambda b, i: (jnp.minimum(lin(b, i) + 1, n_tiles - 1), 0, 0),
                         memory_space=pltpu.SMEM),
            pl.BlockSpec(memory_space=pl.ANY),
            pl.BlockSpec((tc, LANES), lambda b, i: (lin(b, i), 0)),
            pl.BlockSpec((tc, d), lambda b, i: (lin(b, i), 0)),
            pl.BlockSpec((1, tc, d), lambda b, i: (b, i, 0)),
            _mod_spec(gate, tc, d),
            const(sg), const(su), const(sd),
        ],
        out_specs=pl.BlockSpec((1, tc, d), lambda b, i: (b, i, 0)),
        out_shape=jax.ShapeDtypeStruct((bsz, s, d), F32),
        scratch_shapes=[pltpu.VMEM((2, tc * TOP_K, d), F32), pltpu.SemaphoreType.DMA((2,))],
        compiler_params=_params(("arbitrary", "arbitrary")),
        name="combine",
    )(pos_tiles, pos_tiles, y_sorted, wts, h2, x1, gate, sg, su, sd)


def _t5_bucket(dist):
    exact = N_BUCKETS // 2
    d = np.asarray(dist)
    large = exact + (np.log(np.maximum(d, 1) / exact) / np.log(REL_MAX_DIST / exact) * (N_BUCKETS - exact)).astype(np.int32)
    large = np.minimum(large, N_BUCKETS - 1)
    return np.where(d < exact, d, large).astype(np.int32)


def _group_bias(rel_bias, gi):
    window, dil = ATTN_GROUPS[gi]
    bkt = _t5_bucket(np.arange(window // dil + 1) * dil)
    onehot = jnp.asarray(bkt[:, None] == np.arange(N_BUCKETS)[None, :], F32)
    cols = rel_bias[:, gi * HEADS:(gi + 1) * HEADS].astype(F32)
    return jnp.dot(onehot, cols, precision=HIGHEST).T


def _prompt_bias_table(bias_j):
    blk = ATTN_BLOCK
    n = bias_j.shape[1] - 1
    period = 3 * blk
    row0 = jnp.concatenate([jnp.flip(bias_j, axis=1), jnp.full((HEADS, period - n - 1), NEG, F32)], axis=1)
    flat = jnp.tile(row0, (1, blk))[:, :blk * (period - 1)]
    tab = flat.reshape(HEADS, blk, period - 1)[:, :, :2 * blk]
    hq = MXU_DIM // HEAD_DIM
    return tab.reshape(HEADS // hq, hq * blk, 2 * blk)


def _decode_bias_tables(bias_j):
    old = jnp.flip(bias_j[:, 1:], axis=1).T[:, :, None]
    new = bias_j[:, 0][:, None]
    return old, new


def _visit_schedule(counts, na):
    te = EXPERT_TILE
    n_tiles = na // te
    n_steps = n_tiles + N_EXPERTS - 1
    ends = jnp.cumsum(counts)
    offs = ends - counts
    first_tile = offs // te
    n_vis = jnp.where(counts > 0, (ends - 1) // te - first_tile + 1, 0)
    v_end = jnp.cumsum(n_vis)
    v_start = v_end - n_vis
    step = jnp.arange(n_steps, dtype=jnp.int32)
    exp_id = jnp.minimum(jnp.sum(v_end[None, :] <= step[:, None], axis=1), N_EXPERTS - 1).astype(jnp.int32)
    live = step < v_end[-1]
    tile_id = jnp.where(live, first_tile[exp_id] + step - v_start[exp_id], n_tiles - 1).astype(jnp.int32)
    lo = jnp.where(live, jnp.clip(offs[exp_id] - tile_id * te, 0, te), 0).astype(jnp.int32)
    hi = jnp.where(live, jnp.clip(ends[exp_id] - tile_id * te, 0, te), 0).astype(jnp.int32)
    return offs, (tile_id, exp_id, lo, hi)


def kernel(x_prompt, x_sample, cache_k_w128, cache_v_w128, cache_k_w512, cache_v_w512, cache_k_w2048, cache_v_w2048, state_pool, c_prompt, c_sample, ada_w, ada_b, norm1, norm2, w_in, q_gain, k_gain, rel_bias, pool_w, pool_scale, w_br_a, w_br_b, w_out, router_w, router_bias, exp_w_gate, exp_w_up, exp_w_down, sh_w_gate, sh_w_up, sh_w_down):
    depth = ada_w.shape[0]
    bsz, seq, d = x_prompt.shape
    db = x_sample.shape[0]
    assert x_sample.shape[1] == 1 and db <= SAMPLE_PAD and seq % (ATTN_BLOCK * ATTN_GROUPS[-1][1]) == 0
    k_caches = (cache_k_w128, cache_k_w512, cache_k_w2048)
    v_caches = (cache_v_w128, cache_v_w512, cache_v_w2048)
    for c, (w, dil) in zip(k_caches, ATTN_GROUPS):
        assert c.shape[2] == w == ATTN_STEPS * dil

    bias_js = [_group_bias(rel_bias, g) for g in range(N_GROUPS)]
    bias_tabs = [_prompt_bias_table(b) for b in bias_js]
    dec = [_decode_bias_tables(b) for b in bias_js]
    bdec = jnp.stack([t[0] for t in dec])
    bnew = jnp.stack([t[1] for t in dec])
    head_of_col = np.arange(GROUP_W) // HEAD_DIM
    ex_bf = jnp.asarray(np.arange(LANES)[:, None] == head_of_col[None, :], BF16)
    bd = jnp.asarray((np.arange(MXU_DIM)[:, None] // HEAD_DIM == np.arange(MXU_DIM)[None, :] // HEAD_DIM)
                     / HEAD_DIM, BF16)
    wcols = np.repeat(np.asarray(POOL_WINDOWS), POOL_GROUP)
    pw_dec = jnp.asarray(np.where(np.arange(POOL_STATE + 1)[:, None] >= POOL_STATE + 1 - wcols[None, :],
                                  1.0 / wcols[None, :], 0.0), F32)

    xp = x_prompt
    xs = jnp.pad(x_sample.reshape(db, d), ((0, SAMPLE_PAD - db), (0, 0))).reshape(1, SAMPLE_PAD, d)
    n_prompt = bsz * seq
    outs = {name: [] for name in ('pk', 'pv', 'sk', 'sv')}
    outs['pk'] = [[] for _ in range(N_GROUPS)]
    outs['pv'] = [[] for _ in range(N_GROUPS)]
    outs['sk'] = [[] for _ in range(N_GROUPS)]
    outs['sv'] = [[] for _ in range(N_GROUPS)]
    ppool, spool = [], []
    for l in range(depth):
        c_all = jnp.concatenate([c_prompt, c_sample], axis=0)
        mod = _ada_call(c_all, ada_w[l], ada_b[l])
        mod_p = [m.reshape(bsz, 1, d) for m in jnp.split(mod[:bsz], 6, axis=-1)]
        mod_s = [jnp.pad(m, ((0, SAMPLE_PAD - db), (0, 0))).reshape(1, SAMPLE_PAD, d)
                 for m in jnp.split(mod[bsz:], 6, axis=-1)]
        w_in_bf = w_in[l].astype(BF16)
        qg = (jnp.tile(q_gain[l], (1, HEADS)) * (HEAD_DIM ** -0.5)).reshape(1, -1)
        kg = jnp.tile(k_gain[l], (1, HEADS)).reshape(1, -1)
        g1 = norm1[l].reshape(1, d)
        g2 = norm2[l].reshape(1, d)
        wa, wb, wo = (w_br_a[l].astype(BF16), w_br_b[l].astype(BF16), w_out[l].astype(BF16))
        pw = pool_w[l].astype(BF16)
        ps = pool_scale[l].reshape(1, -1)

        q, k, v, u, gates = _inproj_call(xp, mod_p[0], mod_p[1], g1, w_in_bf, qg, kg, bd)
        o_g, lse_g = [], []
        for g, (window, dil) in enumerate(ATTN_GROUPS):
            o, lse = _attn_call(q, k, v, bias_tabs[g], g, dil)
            o_g.append(o)
            lse_g.append(lse)
            keep = min(window, seq)
            cs = slice(g * GROUP_W, (g + 1) * GROUP_W)
            outs['pk'][g].append(k[:, seq - keep:, cs].reshape(bsz, keep, HEADS, HEAD_DIM))
            outs['pv'][g].append(v[:, seq - keep:, cs].reshape(bsz, keep, HEADS, HEAD_DIM))
        ppool.append(u[:, seq - POOL_STATE:])
        x1p, h2p = _merge_call(xp, gates, (o_g, lse_g, u), (mod_p[2], mod_p[3], mod_p[4]),
                               g2, wa, wb, wo, pw, ps, ex_bf, prompt=True)

        qs, ks, vs, us, gates_s = _inproj_call(xs, mod_s[0], mod_s[1], g1, w_in_bf, qg, kg, bd)
        heads3 = lambda t: t[0, :db].astype(F32).reshape(db, N_GROUPS * HEADS, HEAD_DIM)
        q3, k3, v3 = heads3(qs), heads3(ks), heads3(vs)
        kcl = [c[l] for c in k_caches]
        vcl = [c[l] for c in v_caches]
        oa_s, pooled_s = _decode_call(q3, k3, v3, us[0, :db], state_pool[l], kcl, vcl, bdec, bnew, pw_dec)
        padrows = lambda t: jnp.pad(t.reshape(db, GROUP_W), ((0, SAMPLE_PAD - db), (0, 0))).reshape(1, SAMPLE_PAD, GROUP_W)
        x1s, h2s = _merge_call(xs, gates_s, (padrows(oa_s), padrows(pooled_s)), (mod_s[2], mod_s[3], mod_s[4]),
                               g2, wa, wb, wo, pw, ps, ex_bf, prompt=False)
        for g in range(N_GROUPS):
            hs = slice(g * HEADS, (g + 1) * HEADS)
            outs['sk'][g].append(_roll_call(kcl[g], k3[:, None, hs]))
            outs['sv'][g].append(_roll_call(vcl[g], v3[:, None, hs]))
        spool.append(jnp.concatenate([state_pool[l][:, 1:], us[0, :db, None, :]], axis=1))

        h2 = jnp.concatenate([h2p.reshape(n_prompt, d), h2s.reshape(SAMPLE_PAD, d)], axis=0)
        nt = h2.shape[0]
        idx, wts, rank, counts = _router_call(h2, router_w[l], router_bias[l])
        offs, visits = _visit_schedule(counts[0], nt * TOP_K)
        pos = _slot_call(idx, rank, offs.astype(F32).reshape(1, -1))[:, :TOP_K]
        xs_sorted = _dispatch_call(pos.reshape(nt // ROW_TILE, 1, ROW_TILE * TOP_K), h2)
        y_sorted = _expert_call(visits, xs_sorted, exp_w_gate[l], exp_w_up[l], exp_w_down[l])
        tc = COMBINE_TILE
        pos_tiles = pos.reshape(nt // tc, tc, TOP_K).transpose(0, 2, 1).reshape(nt // tc, 1, TOP_K * tc)
        sg, su, sd = sh_w_gate[l].astype(BF16), sh_w_up[l].astype(BF16), sh_w_down[l].astype(BF16)
        xp = _combine_call(pos_tiles, 0, y_sorted, wts, h2, x1p, mod_p[5], sg, su, sd)
        xs = _combine_call(pos_tiles, n_prompt // tc, y_sorted, wts, h2, x1s, mod_s[5], sg, su, sd)

    stack = lambda parts: jnp.stack(parts)
    res = [xp, xs[0, :db].reshape(db, 1, d)]
    for g in range(N_GROUPS):
        res += [stack(outs['pk'][g]), stack(outs['pv'][g])]
    res.append(stack(ppool))
    for g in range(N_GROUPS):
        res += [stack(outs['sk'][g]), stack(outs['sv'][g])]
    res.append(stack(spool))
    return tuple(res)
```

```python
import functools

import numpy as np
import jax
import jax.numpy as jnp
from jax import lax
from jax.experimental import pallas as pl
from jax.experimental.pallas import tpu as pltpu

F32 = jnp.float32
BF16 = jnp.bfloat16
HIGHEST = lax.Precision.HIGHEST

HEAD_DIM = 64
HEADS = 8
GROUP_W = HEADS * HEAD_DIM
ATTN_GROUPS = ((128, 1), (512, 4), (2048, 16))
N_GROUPS = len(ATTN_GROUPS)
ATTN_STEPS = 128
ATTN_BLOCK = 128
POOL_WINDOWS = (2, 4, 8, 16)
POOL_GROUP = 128
POOL_STATE = 15
N_BUCKETS = 32
REL_MAX_DIST = 2048
N_EXPERTS = 256
TOP_K = 8
N_EXPERT_GROUPS = 8
TOPK_GROUPS = 4
ROUTED_SCALE = 2.5
EPS = 1e-6
NEG = -1e30

LANES = 128
MXU_DIM = 256
VMEM_LIMIT = 56 * 1024 * 1024

ROW_TILE = 256
EXPERT_TILE = 256
COMBINE_TILE = 128
SAMPLE_PAD = 256


def _sigmoid(x):
    return 1.0 / (1.0 + jnp.exp(-x))


def _params(sem):
    return pltpu.CompilerParams(dimension_semantics=sem, vmem_limit_bytes=VMEM_LIMIT)


def _ada_body(c_ref, w_ref, b_ref, o_ref):
    c = c_ref[...]
    s = c * _sigmoid(c)
    o_ref[...] = jnp.dot(s, w_ref[...], precision=HIGHEST, preferred_element_type=F32) + b_ref[...]


def _ada_call(c, w, b):
    n, d = c.shape
    cols = w.shape[1]
    tn = 512
    return pl.pallas_call(
        _ada_body,
        grid=(cols // tn,),
        in_specs=[pl.BlockSpec((n, d), lambda j: (0, 0)),
                  pl.BlockSpec((d, tn), lambda j: (0, j)),
                  pl.BlockSpec((1, tn), lambda j: (0, j))],
        out_specs=pl.BlockSpec((n, tn), lambda j: (0, j)),
        out_shape=jax.ShapeDtypeStruct((n, cols), F32),
        compiler_params=_params(("arbitrary",)),
        name="adaln",
    )(c, w, b.reshape(1, cols))


def _inproj_body(x_ref, shift_ref, scale_ref, g_ref, w_ref, qg_ref, kg_ref, bd_ref,
                 q_ref, k_ref, v_ref, u_ref, gt_ref):
    x = x_ref[0]
    ms = jnp.mean(x * x, axis=-1, keepdims=True)
    h = x * lax.rsqrt(ms + EPS) * g_ref[...]
    h = h * (1.0 + scale_ref[0]) + shift_ref[0]
    hb = h.astype(BF16)
    qkv = N_GROUPS * GROUP_W
    n_chunks = w_ref.shape[1] // GROUP_W
    for c in range(n_chunks):
        z = jnp.dot(hb, w_ref[:, c * GROUP_W:(c + 1) * GROUP_W], preferred_element_type=F32)
        if c < 2 * N_GROUPS:
            zz = (z * z).astype(BF16)
            hms = jnp.concatenate(
                [jnp.dot(zz[:, s:s + MXU_DIM], bd_ref[...], preferred_element_type=F32)
                 for s in range(0, GROUP_W, MXU_DIM)], axis=1)
            g = c % N_GROUPS
            gain = (qg_ref if c < N_GROUPS else kg_ref)[:, g * GROUP_W:(g + 1) * GROUP_W]
            y = z * lax.rsqrt(hms + EPS) * gain
            if c < N_GROUPS:
                q_ref[0, :, g * GROUP_W:(g + 1) * GROUP_W] = y
            else:
                k_ref[0, :, g * GROUP_W:(g + 1) * GROUP_W] = y
        elif c < 3 * N_GROUPS:
            g = c - 2 * N_GROUPS
            v_ref[0, :, g * GROUP_W:(g + 1) * GROUP_W] = z
        elif c == 3 * N_GROUPS:
            u_ref[0] = z
        else:
            g = c - 3 * N_GROUPS - 1
            gt_ref[0, :, g * GROUP_W:(g + 1) * GROUP_W] = z.astype(BF16)
    del qkv


def _mod_spec(mod, tm, d):
    if mod.shape[1] == 1:
        return pl.BlockSpec((1, 1, d), lambda b, i: (b, 0, 0))
    return pl.BlockSpec((1, tm, d), lambda b, i: (b, i, 0))


def _inproj_call(x, shift, scale, g1, w_bf, qg, kg, bd):
    bsz, s, d = x.shape
    tm = min(ROW_TILE, s)
    qkv = N_GROUPS * GROUP_W
    n_gate = w_bf.shape[1] - 3 * qkv - GROUP_W
    const = lambda b, i: (0, 0)
    row = lambda width: pl.BlockSpec((1, tm, width), lambda b, i: (b, i, 0))
    return pl.pallas_call(
        _inproj_body,
        grid=(bsz, s // tm),
        in_specs=[row(d), _mod_spec(shift, tm, d), _mod_spec(scale, tm, d),
                  pl.BlockSpec((1, d), const),
                  pl.BlockSpec(w_bf.shape, const),
                  pl.BlockSpec((1, qkv), const), pl.BlockSpec((1, qkv), const),
                  pl.BlockSpec((MXU_DIM, MXU_DIM), const)],
        out_specs=[row(qkv), row(qkv), row(qkv), row(GROUP_W), row(n_gate)],
        out_shape=[jax.ShapeDtypeStruct((bsz, s, qkv), F32),
                   jax.ShapeDtypeStruct((bsz, s, qkv), F32),
                   jax.ShapeDtypeStruct((bsz, s, qkv), F32),
                   jax.ShapeDtypeStruct((bsz, s, GROUP_W), F32),
                   jax.ShapeDtypeStruct((bsz, s, n_gate), BF16)],
        compiler_params=_params(("arbitrary", "arbitrary")),
        name="inproj",
    )(x, shift, scale, g1, w_bf, qg, kg, bd)


ATTN_ROWS = ATTN_BLOCK * ATTN_GROUPS[-1][1]
HEADS_PER_SLAB = LANES // HEAD_DIM


def _attn_body(*refs):
    ng = N_GROUPS
    qkv = refs[:3 * ng]
    bias_ref, o_ref = refs[3 * ng], refs[3 * ng + 1]
    scr = refs[3 * ng + 2:]
    first = pl.program_id(2) == 0
    blk = ATTN_BLOCK
    pb = ATTN_ROWS
    hs = HEADS_PER_SLAB
    lane = lax.broadcasted_iota(jnp.int32, (blk, LANES), 1)
    col = lax.broadcasted_iota(jnp.int32, (hs * blk, 2 * blk), 1)
    for g, (_, dil) in enumerate(ATTN_GROUPS):
        q_ref, k_ref, v_ref = qkv[3 * g:3 * g + 3]
        kext, vext, og, lg = scr[4 * g:4 * g + 4]
        span = blk * dil
        bias = bias_ref[g, 0]

        @pl.when(first)
        def _():
            kext[0:span, :] = jnp.zeros((span, LANES), F32)
            vext[0:span, :] = jnp.zeros((span, LANES), F32)
        kext[span:span + pb, :] = k_ref[0]
        vext[span:span + pb, :] = v_ref[0]

        def unit(u, carry):
            qb = u // dil
            base = qb * span + (u % dil)
            rows = lambda start: pl.ds(start, blk, stride=dil) if dil > 1 else pl.ds(pl.multiple_of(start, blk), blk)
            q = q_ref[0, rows(base), :]
            k2 = jnp.concatenate([kext[rows(base), :], kext[rows(base + span), :]], axis=0).astype(BF16)
            v2 = jnp.concatenate([vext[rows(base), :], vext[rows(base + span), :]], axis=0).astype(BF16)
            qs = jnp.concatenate([jnp.where(lane // HEAD_DIM == h, q, 0.0) for h in range(hs)], axis=0).astype(BF16)
            s = lax.dot_general(qs, k2, (((1,), (1,)), ((), ())), preferred_element_type=F32) + bias
            s = jnp.where(jnp.logical_and(jnp.logical_and(first, qb == 0), col < blk), NEG, s)
            m = jnp.max(s, axis=-1, keepdims=True)
            p = jnp.exp(s - m)
            l = jnp.sum(p, axis=-1, keepdims=True)
            o2 = jnp.dot(p.astype(BF16), v2, preferred_element_type=F32) / l
            lse = m + jnp.log(l)
            o = o2[0:blk]
            lw = jnp.broadcast_to(lse[0:blk], (blk, LANES))
            for h in range(1, hs):
                o = jnp.where(lane // HEAD_DIM == h, o2[h * blk:(h + 1) * blk], o)
                lw = jnp.where(lane // HEAD_DIM == h, lse[h * blk:(h + 1) * blk], lw)
            og[rows(base), :] = o
            lg[rows(base), :] = lw
            return carry
        lax.fori_loop(0, pb // blk, unit, 0, unroll=2)
        kext[0:span, :] = kext[pb:pb + span, :]
        vext[0:span, :] = vext[pb:pb + span, :]

    def merge(c, carry):
        rs = pl.ds(pl.multiple_of(c * blk, blk), blk)
        ls = [scr[4 * g + 3][rs, :] for g in range(ng)]
        mx = jnp.maximum(jnp.maximum(ls[0], ls[1]), ls[2])
        es = [jnp.exp(t - mx) for t in ls]
        den = es[0] + es[1] + es[2]
        acc = (es[0] / den) * scr[2][rs, :]
        for g in range(1, ng):
            acc = acc + (es[g] / den) * scr[4 * g + 2][rs, :]
        o_ref[0, rs, :] = acc.astype(o_ref.dtype)
        return carry
    lax.fori_loop(0, pb // blk, merge, 0)


def _attn_call(q, k, v, bias_tabs):
    bsz, s, qkv = q.shape
    pb = ATTN_ROWS
    n_slab = GROUP_W // LANES
    in_specs, args = [], []
    for g in range(N_GROUPS):
        spec = pl.BlockSpec((1, pb, LANES), lambda b, c, i, g=g: (b, i, g * n_slab + c))
        in_specs += [spec, spec, spec]
        args += [q, k, v]
    in_specs.append(pl.BlockSpec((N_GROUPS, 1) + bias_tabs.shape[2:], lambda b, c, i: (0, c, 0, 0)))
    scratch = []
    for _, dil in ATTN_GROUPS:
        ext = pltpu.VMEM((ATTN_BLOCK * dil + pb, LANES), F32)
        scratch += [ext, ext, pltpu.VMEM((pb, LANES), F32), pltpu.VMEM((pb, LANES), F32)]
    return pl.pallas_call(
        _attn_body,
        grid=(bsz, n_slab, s // pb),
        in_specs=in_specs,
        out_specs=pl.BlockSpec((1, pb, LANES), lambda b, c, i: (b, i, c)),
        out_shape=jax.ShapeDtypeStruct((bsz, s, GROUP_W), BF16),
        scratch_shapes=scratch,
        compiler_params=_params(("arbitrary", "arbitrary", "arbitrary")),
        name="attn",
    )(*args, bias_tabs)


def _decode_body(q_ref, kn_ref, vn_ref, u_ref, pool_ref, kc0, vc0, kc1, vc1, kc2, vc2,
                 bdec_ref, bnew_ref, pw_ref, oa_ref, pooled_ref):
    kcs = (kc0, kc1, kc2)
    vcs = (vc0, vc1, vc2)
    outs, lses = [], []
    for g in range(N_GROUPS):
        hs = slice(g * HEADS, (g + 1) * HEADS)
        q = q_ref[0, hs, :]
        kn = kn_ref[0, hs, :]
        vn = vn_ref[0, hs, :]
        kc = kcs[g][0]
        vc = vcs[g][0]
        s_old = jnp.sum(kc * q[None], axis=-1, keepdims=True) + bdec_ref[g]
        s_new = jnp.sum(kn * q, axis=-1, keepdims=True) + bnew_ref[g]
        m = jnp.maximum(jnp.max(s_old, axis=0), s_new)
        p_old = jnp.exp(s_old - m[None])
        p_new = jnp.exp(s_new - m)
        l = jnp.sum(p_old, axis=0) + p_new
        outs.append((jnp.sum(p_old * vc, axis=0) + p_new * vn) / l)
        lses.append(m + jnp.log(l))
    mx = jnp.maximum(jnp.maximum(lses[0], lses[1]), lses[2])
    es = [jnp.exp(t - mx) for t in lses]
    den = es[0] + es[1] + es[2]
    oa_ref[0] = (es[0] / den) * outs[0] + (es[1] / den) * outs[1] + (es[2] / den) * outs[2]
    u = u_ref[0]
    stored = jnp.sum(pool_ref[0] * pw_ref[0:POOL_STATE, :], axis=0, keepdims=True)
    pooled_ref[0] = stored + u * pw_ref[POOL_STATE:POOL_STATE + 1, :] - u


def _decode_call(q, kn, vn, u, pool_prev, k_caches, v_caches, bdec, bnew, pw):
    db = pool_prev.shape[0]
    per_b = lambda t: pl.BlockSpec((1,) + t.shape[1:], lambda b: (b,) + (0,) * (t.ndim - 1))
    const = lambda t: pl.BlockSpec(t.shape, lambda b: (0,) * t.ndim)
    cache_specs, cache_args = [], []
    for g, (_, dil) in enumerate(ATTN_GROUPS):
        for c in (k_caches[g], v_caches[g]):
            rows = c.shape[1]
            cache_args.append(c.reshape(db, rows // dil, dil, HEADS, HEAD_DIM))
            cache_specs.append(pl.BlockSpec((1, ATTN_STEPS, None, HEADS, HEAD_DIM), lambda b: (b, 0, 0, 0, 0)))
    u3 = u.reshape(db, 1, u.shape[-1])
    return pl.pallas_call(
        _decode_body,
        grid=(db,),
        in_specs=[per_b(q), per_b(kn), per_b(vn), per_b(u3), per_b(pool_prev)] + cache_specs +
                 [const(bdec), const(bnew), const(pw)],
        out_specs=[pl.BlockSpec((1, HEADS, HEAD_DIM), lambda b: (b, 0, 0)),
                   pl.BlockSpec((1, 1, GROUP_W), lambda b: (b, 0, 0))],
        out_shape=[jax.ShapeDtypeStruct((db, HEADS, HEAD_DIM), F32),
                   jax.ShapeDtypeStruct((db, 1, GROUP_W), F32)],
        compiler_params=_params(("arbitrary",)),
        name="decode_mix",
    )(q, kn, vn, u3, pool_prev, *cache_args, bdec, bnew, pw)


def _roll_body(c_ref, nxt_ref, new_ref, o_ref):
    ch = c_ref.shape[1]
    last = pl.program_id(1) == pl.num_programs(1) - 1
    o_ref[0, 0:ch - 1] = c_ref[0, 1:ch]
    o_ref[0, ch - 1:ch] = jnp.where(last, new_ref[0], nxt_ref[0])


def _roll_call(cache, new):
    db, rows, nh, e = cache.shape
    ch = min(rows, 256)
    return pl.pallas_call(
        _roll_body,
        grid=(db, rows // ch),
        in_specs=[pl.BlockSpec((1, ch, nh, e), lambda b, i: (b, i, 0, 0)),
                  pl.BlockSpec((1, 1, nh, e), lambda b, i: (b, jnp.minimum((i + 1) * ch, rows - 1), 0, 0)),
                  pl.BlockSpec((1, 1, nh, e), lambda b, i: (b, 0, 0, 0))],
        out_specs=pl.BlockSpec((1, ch, nh, e), lambda b, i: (b, i, 0, 0)),
        out_shape=jax.ShapeDtypeStruct(cache.shape, cache.dtype),
        compiler_params=_params(("arbitrary", "arbitrary")),
        name="cache_roll",
    )(cache, cache, new)


def _merge_tail(x, oa, pooled, gates, gate_msa, shift2, scale2, g2, wa, wb, wo, pw_ref, ps,
                x1_ref, h2_ref):
    mixed = jnp.concatenate(
        [jnp.dot(pooled[:, j * POOL_GROUP:(j + 1) * POOL_GROUP].astype(BF16), pw_ref[j],
                 preferred_element_type=F32) for j in range(len(POOL_WINDOWS))], axis=1) * ps
    d = x.shape[-1]
    ga = gates[:, :d].astype(F32)
    gb = gates[:, d:].astype(F32)
    merged = (_sigmoid(ga) * jnp.dot(oa.astype(BF16), wa, preferred_element_type=F32) +
              _sigmoid(gb) * jnp.dot(mixed.astype(BF16), wb, preferred_element_type=F32))
    y = jnp.dot(merged.astype(BF16), wo, preferred_element_type=F32)
    x1 = x + gate_msa * y
    x1_ref[0] = x1
    ms = jnp.mean(x1 * x1, axis=-1, keepdims=True)
    h2 = x1 * lax.rsqrt(ms + EPS) * g2
    h2_ref[0] = h2 * (1.0 + scale2) + shift2


def _merge_prompt_body(x_ref, gt_ref, oa_ref, u_ref, uh_ref,
                       gm_ref, sh_ref, sc_ref, g2_ref, wa_ref, wb_ref, wo_ref, pw_ref, ps_ref,
                       x1_ref, h2_ref):
    i = pl.program_id(1)
    tm = x_ref.shape[1]
    oa = oa_ref[0]
    u = u_ref[0]
    halo = jnp.where(i == 0, 0.0, uh_ref[0])
    ext = jnp.concatenate([halo, u], axis=0)
    hw = halo.shape[0]
    acc = ext
    parts = []
    for j, w in enumerate(POOL_WINDOWS):
        acc = acc[:, POOL_GROUP * (1 if j else 0):]
        acc = acc + pltpu.roll(acc, w // 2, axis=0)
        parts.append(acc[hw:, :POOL_GROUP])
    tot = jnp.concatenate(parts, axis=1)
    lane = lax.broadcasted_iota(jnp.int32, (tm, GROUP_W), 1)
    wcol = jnp.left_shift(2, lane // POOL_GROUP)
    pos = i * tm + lax.broadcasted_iota(jnp.int32, (tm, GROUP_W), 0)
    cnt = jnp.minimum(pos + 1, wcol).astype(F32)
    pooled = tot / cnt - u
    _merge_tail(x_ref[0], oa, pooled, gt_ref[0], gm_ref[0], sh_ref[0], sc_ref[0], g2_ref[...],
                wa_ref[...], wb_ref[...], wo_ref[...], pw_ref, ps_ref[...], x1_ref, h2_ref)


def _merge_sample_body(x_ref, gt_ref, oa_ref, pooled_ref,
                       gm_ref, sh_ref, sc_ref, g2_ref, wa_ref, wb_ref, wo_ref, pw_ref, ps_ref,
                       x1_ref, h2_ref):
    _merge_tail(x_ref[0], oa_ref[0], pooled_ref[0], gt_ref[0], gm_ref[0], sh_ref[0], sc_ref[0], g2_ref[...],
                wa_ref[...], wb_ref[...], wo_ref[...], pw_ref, ps_ref[...], x1_ref, h2_ref)


def _merge_call(x, gates, mixer_inputs, mods, g2, wa, wb, wo, pw, ps, prompt):
    bsz, s, d = x.shape
    tm = min(ROW_TILE, s)
    row = lambda t: pl.BlockSpec((1, tm, t.shape[-1]), lambda b, i: (b, i, 0))
    const = lambda t: pl.BlockSpec(t.shape, lambda b, i: (0,) * t.ndim)
    gm, sh, sc = mods
    in_specs = [row(x), row(gates)]
    args = [x, gates]
    if prompt:
        oa, u = mixer_inputs
        hw = 16
        in_specs += [row(oa), row(u),
                     pl.BlockSpec((1, hw, u.shape[-1]), lambda b, i: (b, jnp.maximum(i * (tm // hw) - 1, 0), 0))]
        args += [oa, u, u]
        body = _merge_prompt_body
    else:
        in_specs += [row(t) for t in mixer_inputs]
        args += list(mixer_inputs)
        body = _merge_sample_body
    in_specs += [_mod_spec(m, tm, d) for m in (gm, sh, sc)]
    in_specs += [const(t) for t in (g2, wa, wb, wo, pw, ps)]
    args += [gm, sh, sc, g2, wa, wb, wo, pw, ps]
    return pl.pallas_call(
        body,
        grid=(bsz, s // tm),
        in_specs=in_specs,
        out_specs=[row(x), row(x)],
        out_shape=[jax.ShapeDtypeStruct((bsz, s, d), F32), jax.ShapeDtypeStruct((bsz, s, d), F32)],
        compiler_params=_params(("arbitrary", "arbitrary")),
        name="merge_prompt" if prompt else "merge_sample",
    )(*args)


def _router_body(h_ref, rw_ref, rb_ref, tri_ref, idx_ref, wt_ref, rank_ref, cnt_ref, carry):
    tm = h_ref.shape[0]

    @pl.when(pl.program_id(0) == 0)
    def _():
        carry[...] = jnp.zeros_like(carry)

    ne = rw_ref.shape[1]
    gsz = ne // N_EXPERT_GROUPS
    logits = jnp.dot(h_ref[...], rw_ref[...], precision=HIGHEST, preferred_element_type=F32)
    scores = _sigmoid(logits)
    sel = scores + rb_ref[...]
    lane = lax.broadcasted_iota(jnp.int32, (tm, ne), 1)
    lanef = lane.astype(F32)
    grp = lane // gsz
    ninf = -jnp.inf
    far = float(ne)

    def first_max(x):
        m = jnp.max(x, axis=-1, keepdims=True)
        at = jnp.min(jnp.where(x == m, lanef, far), axis=-1, keepdims=True)
        return m, at

    gs = []
    for g in range(N_EXPERT_GROUPS):
        xg = jnp.where(grp == g, sel, ninf)
        m1, a1 = first_max(xg)
        m2 = jnp.max(jnp.where(lanef == a1, ninf, xg), axis=-1, keepdims=True)
        gs.append(m1 + m2)
    allowed = jnp.zeros((tm, ne), jnp.bool_)
    for g in range(N_EXPERT_GROUPS):
        ahead = jnp.zeros((tm, 1), F32)
        for o in range(N_EXPERT_GROUPS):
            if o == g:
                continue
            beats = (gs[o] > gs[g]) | ((gs[o] == gs[g]) & (o < g))
            ahead = ahead + beats.astype(F32)
        allowed = allowed | ((grp == g) & (ahead < TOPK_GROUPS))
    masked = jnp.where(allowed, sel, ninf)
    lane_o = lax.broadcasted_iota(jnp.int32, (tm, LANES), 1)
    idx_t = jnp.zeros((tm, LANES), F32)
    wt_t = jnp.zeros((tm, LANES), F32)
    wsum = jnp.zeros((tm, 1), F32)
    picked = jnp.zeros((tm, ne), F32)
    ats = []
    for k in range(TOP_K):
        _, at = first_max(masked)
        hit = lanef == at
        wk = jnp.sum(jnp.where(hit, scores, 0.0), axis=-1, keepdims=True)
        masked = jnp.where(hit, ninf, masked)
        picked = picked + hit.astype(F32)
        ats.append(at)
        idx_t = jnp.where(lane_o == k, at, idx_t)
        wt_t = jnp.where(lane_o == k, wk, wt_t)
        wsum = wsum + wk
    idx_ref[...] = idx_t.astype(jnp.int32)
    wt_ref[...] = wt_t / wsum * ROUTED_SCALE
    before = jnp.dot(tri_ref[...], picked.astype(BF16), preferred_element_type=F32) + carry[...]
    rank_t = jnp.zeros((tm, LANES), F32)
    for k in range(TOP_K):
        rk = jnp.sum(jnp.where(lanef == ats[k], before, 0.0), axis=-1, keepdims=True)
        rank_t = jnp.where(lane_o == k, rk, rank_t)
    rank_ref[...] = rank_t.astype(jnp.int32)
    carry[...] = carry[...] + jnp.sum(picked, axis=0, keepdims=True)
    cnt_ref[...] = carry[...].astype(jnp.int32)


def _router_call(h2, rw, rb):
    nt, d = h2.shape
    tm = ROW_TILE
    ne = rw.shape[1]
    tri = jnp.asarray(np.tril(np.ones((tm, tm), np.float32), -1), BF16)
    tile = pl.BlockSpec((tm, LANES), lambda i: (i, 0))
    return pl.pallas_call(
        _router_body,
        grid=(nt // tm,),
        in_specs=[pl.BlockSpec((tm, d), lambda i: (i, 0)),
                  pl.BlockSpec((d, ne), lambda i: (0, 0)),
                  pl.BlockSpec((1, ne), lambda i: (0, 0)),
                  pl.BlockSpec((tm, tm), lambda i: (0, 0))],
        out_specs=[tile, tile, tile, pl.BlockSpec((1, ne), lambda i: (0, 0))],
        out_shape=[jax.ShapeDtypeStruct((nt, LANES), jnp.int32),
                   jax.ShapeDtypeStruct((nt, LANES), F32),
                   jax.ShapeDtypeStruct((nt, LANES), jnp.int32),
                   jax.ShapeDtypeStruct((1, ne), jnp.int32)],
        scratch_shapes=[pltpu.VMEM((1, ne), F32)],
        compiler_params=_params(("arbitrary",)),
        name="router",
    )(h2, rw, rb.reshape(1, ne), tri)


def _slot_body(idx_ref, rank_ref, offs_ref, pos_ref):
    tm = idx_ref.shape[0]
    ne = offs_ref.shape[1]
    idx = idx_ref[...]
    rank = rank_ref[...]
    lane = lax.broadcasted_iota(jnp.int32, (tm, ne), 1)
    lane_o = lax.broadcasted_iota(jnp.int32, (tm, LANES), 1)
    offs = offs_ref[...]
    pos = jnp.zeros((tm, LANES), F32)
    for k in range(TOP_K):
        start = jnp.sum(jnp.where(lane == idx[:, k:k + 1], offs, 0.0), axis=-1, keepdims=True)
        pos = jnp.where(lane_o == k, start, pos)
    pos_ref[...] = pos.astype(jnp.int32) + rank


def _slot_call(idx, rank, offs):
    nt = idx.shape[0]
    tm = ROW_TILE
    ne = offs.shape[1]
    tile = pl.BlockSpec((tm, LANES), lambda i: (i, 0))
    return pl.pallas_call(
        _slot_body,
        grid=(nt // tm,),
        in_specs=[tile, tile, pl.BlockSpec((1, ne), lambda i: (0, 0))],
        out_specs=tile,
        out_shape=jax.ShapeDtypeStruct((nt, LANES), jnp.int32),
        compiler_params=_params(("arbitrary",)),
        name="slots",
    )(idx, rank, offs)


def _row_copy(src_hbm, row, dst, j, sem):
    return pltpu.make_async_copy(src_hbm.at[pl.ds(row, 1), :], dst.at[pl.ds(j, 1), :], sem)


def _issue_rows(idx_ref, src_hbm, dst, sem, n):
    def one(j, carry):
        _row_copy(src_hbm, idx_ref[0, 0, j], dst, j, sem).start()
        return carry
    lax.fori_loop(0, n, one, 0, unroll=8)


def _wait_rows(src_hbm, dst, sem, n):
    pltpu.make_async_copy(src_hbm.at[pl.ds(0, n), :], dst, sem).wait()


def _dispatch_body(pos_ref, h_ref, xs_hbm, sem):
    td = h_ref.shape[0]

    def one(t, carry):
        for k in range(TOP_K):
            pltpu.make_async_copy(h_ref.at[pl.ds(t, 1), :],
                                  xs_hbm.at[pl.ds(pos_ref[0, 0, t * TOP_K + k], 1), :], sem).start()
        return carry
    lax.fori_loop(0, td, one, 0, unroll=2)
    for k in range(TOP_K):
        pltpu.make_async_copy(h_ref, xs_hbm.at[pl.ds(0, td), :], sem).wait()


def _dispatch_call(pos_tok, h2):
    nt, d = h2.shape
    td = ROW_TILE
    return pl.pallas_call(
        _dispatch_body,
        grid=(nt // td,),
        in_specs=[pl.BlockSpec((1, 1, td * TOP_K), lambda i: (i, 0, 0), memory_space=pltpu.SMEM),
                  pl.BlockSpec((td, d), lambda i: (i, 0))],
        out_specs=pl.BlockSpec(memory_space=pl.ANY),
        out_shape=jax.ShapeDtypeStruct((nt * TOP_K, d), F32),
        scratch_shapes=[pltpu.SemaphoreType.DMA],
        compiler_params=_params(("arbitrary",)),
        name="dispatch",
    )(pos_tok, h2)


def _expert_body(tile_ref, exp_ref, lo_ref, hi_ref, xs_ref, wg_ref, wu_ref, wd_ref, y_ref, wgb, wub, wdb):
    s = pl.program_id(0)
    prev = jnp.maximum(s - 1, 0)
    first_visit = jnp.logical_or(s == 0, tile_ref[s] != tile_ref[prev])
    new_expert = jnp.logical_or(s == 0, exp_ref[s] != exp_ref[prev])
    lo = lo_ref[s]
    hi = hi_ref[s]

    @pl.when(new_expert)
    def _():
        wgb[...] = wg_ref[0].astype(BF16)
        wub[...] = wu_ref[0].astype(BF16)
        wdb[...] = wd_ref[0].astype(BF16)

    @pl.when(hi > lo)
    def _():
        x = xs_ref[...].astype(BF16)
        gt = jnp.dot(x, wgb[...], preferred_element_type=F32)
        up = jnp.dot(x, wub[...], preferred_element_type=F32)
        act = (gt * _sigmoid(gt)) * up
        y = jnp.dot(act.astype(BF16), wdb[...], preferred_element_type=F32)
        row = lax.broadcasted_iota(jnp.int32, y.shape, 0)
        mine = jnp.logical_and(row >= lo, row < hi)

        @pl.when(first_visit)
        def _():
            y_ref[...] = jnp.where(mine, y, 0.0)

        @pl.when(jnp.logical_not(first_visit))
        def _():
            y_ref[...] = jnp.where(mine, y, y_ref[...])


def _expert_call(visits, xs, wg, wu, wd):
    tile_id, exp_id, lo, hi = visits
    n_steps = tile_id.shape[0]
    te = EXPERT_TILE
    na, d = xs.shape
    hid = wg.shape[-1]
    grid_spec = pltpu.PrefetchScalarGridSpec(
        num_scalar_prefetch=4,
        grid=(n_steps,),
        in_specs=[
            pl.BlockSpec((te, d), lambda s, t, e, lo, hi: (t[s], 0)),
            pl.BlockSpec((1, d, hid), lambda s, t, e, lo, hi: (e[s], 0, 0)),
            pl.BlockSpec((1, d, hid), lambda s, t, e, lo, hi: (e[s], 0, 0)),
            pl.BlockSpec((1, hid, d), lambda s, t, e, lo, hi: (e[s], 0, 0)),
        ],
        out_specs=pl.BlockSpec((te, d), lambda s, t, e, lo, hi: (t[s], 0)),
        scratch_shapes=[pltpu.VMEM((d, hid), BF16), pltpu.VMEM((d, hid), BF16), pltpu.VMEM((hid, d), BF16)],
    )
    return pl.pallas_call(
        _expert_body,
        grid_spec=grid_spec,
        out_shape=jax.ShapeDtypeStruct((na, d), F32),
        compiler_params=_params(("arbitrary",)),
        name="expert_ffn",
    )(tile_id, exp_id, lo, hi, xs, wg, wu, wd)


def _combine_body(pos_ref, posn_ref, y_hbm, wt_ref, h_ref, x1_ref, gate_ref, sg_ref, su_ref, sd_ref,
                  out_ref, ybuf, sem):
    nb, ni = pl.num_programs(0), pl.num_programs(1)
    step = pl.program_id(0) * ni + pl.program_id(1)
    slot = step % 2
    tc = h_ref.shape[0]
    rows = tc * TOP_K

    @pl.when(step == 0)
    def _():
        _issue_rows(pos_ref, y_hbm, ybuf.at[0], sem.at[0], rows)

    @pl.when(step + 1 < nb * ni)
    def _():
        _issue_rows(posn_ref, y_hbm, ybuf.at[1 - slot], sem.at[1 - slot], rows)

    _wait_rows(y_hbm, ybuf.at[slot], sem.at[slot], rows)
    wt = wt_ref[...]
    routed = jnp.zeros(out_ref.shape[1:], F32)
    for k in range(TOP_K):
        routed = routed + wt[:, k:k + 1] * ybuf[slot, k * tc:(k + 1) * tc, :]
    hb = h_ref[...].astype(BF16)
    gt = jnp.dot(hb, sg_ref[...], preferred_element_type=F32)
    up = jnp.dot(hb, su_ref[...], preferred_element_type=F32)
    shared = jnp.dot(((gt * _sigmoid(gt)) * up).astype(BF16), sd_ref[...], preferred_element_type=F32)
    out_ref[0] = x1_ref[0] + gate_ref[0] * (routed + shared)


def _combine_call(pos_tiles, tile_off, y_sorted, wts, h2, x1, gate, sg, su, sd):
    bsz, s, d = x1.shape
    tc = min(COMBINE_TILE, s)
    ni = s // tc
    n_tiles = pos_tiles.shape[0]
    lin = lambda b, i: tile_off + b * ni + i
    const = lambda t: pl.BlockSpec(t.shape, lambda b, i: (0,) * t.ndim)
    return pl.pallas_call(
        _combine_body,
        grid=(bsz, ni),
        in_specs=[
            pl.BlockSpec((1, 1, tc * TOP_K), lambda b, i: (lin(b, i), 0, 0), memory_space=pltpu.SMEM),
            pl.BlockSpec((1, 1, tc * TOP_K), lambda b, i: (jnp.minimum(lin(b, i) + 1, n_tiles - 1), 0, 0),
                         memory_space=pltpu.SMEM),
            pl.BlockSpec(memory_space=pl.ANY),
            pl.BlockSpec((tc, LANES), lambda b, i: (lin(b, i), 0)),
            pl.BlockSpec((tc, d), lambda b, i: (lin(b, i), 0)),
            pl.BlockSpec((1, tc, d), lambda b, i: (b, i, 0)),
            _mod_spec(gate, tc, d),
            const(sg), const(su), const(sd),
        ],
        out_specs=pl.BlockSpec((1, tc, d), lambda b, i: (b, i, 0)),
        out_shape=jax.ShapeDtypeStruct((bsz, s, d), F32),
        scratch_shapes=[pltpu.VMEM((2, tc * TOP_K, d), F32), pltpu.SemaphoreType.DMA((2,))],
        compiler_params=_params(("arbitrary", "arbitrary")),
        name="combine",
    )(pos_tiles, pos_tiles, y_sorted, wts, h2, x1, gate, sg, su, sd)


def _t5_bucket(dist):
    exact = N_BUCKETS // 2
    d = np.asarray(dist)
    large = exact + (np.log(np.maximum(d, 1) / exact) / np.log(REL_MAX_DIST / exact) * (N_BUCKETS - exact)).astype(np.int32)
    large = np.minimum(large, N_BUCKETS - 1)
    return np.where(d < exact, d, large).astype(np.int32)


def _group_bias(rel_bias, gi):
    window, dil = ATTN_GROUPS[gi]
    bkt = _t5_bucket(np.arange(window // dil + 1) * dil)
    onehot = jnp.asarray(bkt[:, None] == np.arange(N_BUCKETS)[None, :], F32)
    cols = rel_bias[:, gi * HEADS:(gi + 1) * HEADS].astype(F32)
    return jnp.dot(onehot, cols, precision=HIGHEST).T


def _prompt_bias_table(bias_j):
    blk = ATTN_BLOCK
    n = bias_j.shape[1] - 1
    period = 3 * blk
    row0 = jnp.concatenate([jnp.flip(bias_j, axis=1), jnp.full((HEADS, period - n - 1), NEG, F32)], axis=1)
    flat = jnp.tile(row0, (1, blk))[:, :blk * (period - 1)]
    tab = flat.reshape(HEADS, blk, period - 1)[:, :, :2 * blk]
    return tab.reshape(HEADS // HEADS_PER_SLAB, HEADS_PER_SLAB * blk, 2 * blk)


def _decode_bias_tables(bias_j):
    old = jnp.flip(bias_j[:, 1:], axis=1).T[:, :, None]
    new = bias_j[:, 0][:, None]
    return old, new


def _visit_schedule(counts, na):
    te = EXPERT_TILE
    n_tiles = na // te
    n_steps = n_tiles + N_EXPERTS - 1
    ends = jnp.cumsum(counts)
    offs = ends - counts
    first_tile = offs // te
    n_vis = jnp.where(counts > 0, (ends - 1) // te - first_tile + 1, 0)
    v_end = jnp.cumsum(n_vis)
    v_start = v_end - n_vis
    step = jnp.arange(n_steps, dtype=jnp.int32)
    exp_id = jnp.minimum(jnp.sum(v_end[None, :] <= step[:, None], axis=1), N_EXPERTS - 1).astype(jnp.int32)
    live = step < v_end[-1]
    tile_id = jnp.where(live, first_tile[exp_id] + step - v_start[exp_id], n_tiles - 1).astype(jnp.int32)
    lo = jnp.where(live, jnp.clip(offs[exp_id] - tile_id * te, 0, te), 0).astype(jnp.int32)
    hi = jnp.where(live, jnp.clip(ends[exp_id] - tile_id * te, 0, te), 0).astype(jnp.int32)
    return offs, (tile_id, exp_id, lo, hi)


def kernel(x_prompt, x_sample, cache_k_w128, cache_v_w128, cache_k_w512, cache_v_w512, cache_k_w2048, cache_v_w2048, state_pool, c_prompt, c_sample, ada_w, ada_b, norm1, norm2, w_in, q_gain, k_gain, rel_bias, pool_w, pool_scale, w_br_a, w_br_b, w_out, router_w, router_bias, exp_w_gate, exp_w_up, exp_w_down, sh_w_gate, sh_w_up, sh_w_down):
    depth = ada_w.shape[0]
    bsz, seq, d = x_prompt.shape
    db = x_sample.shape[0]
    assert x_sample.shape[1] == 1 and db <= SAMPLE_PAD and seq % (ATTN_BLOCK * ATTN_GROUPS[-1][1]) == 0
    k_caches = (cache_k_w128, cache_k_w512, cache_k_w2048)
    v_caches = (cache_v_w128, cache_v_w512, cache_v_w2048)
    for c, (w, dil) in zip(k_caches, ATTN_GROUPS):
        assert c.shape[2] == w == ATTN_STEPS * dil

    bias_js = [_group_bias(rel_bias, g) for g in range(N_GROUPS)]
    bias_tabs = jnp.stack([_prompt_bias_table(b) for b in bias_js])
    dec = [_decode_bias_tables(b) for b in bias_js]
    bdec = jnp.stack([t[0] for t in dec])
    bnew = jnp.stack([t[1] for t in dec])
    bd = jnp.asarray((np.arange(MXU_DIM)[:, None] // HEAD_DIM == np.arange(MXU_DIM)[None, :] // HEAD_DIM)
                     / HEAD_DIM, BF16)
    wcols = np.repeat(np.asarray(POOL_WINDOWS), POOL_GROUP)
    pw_dec = jnp.asarray(np.where(np.arange(POOL_STATE + 1)[:, None] >= POOL_STATE + 1 - wcols[None, :],
                                  1.0 / wcols[None, :], 0.0), F32)

    xp = x_prompt
    xs = jnp.pad(x_sample.reshape(db, d), ((0, SAMPLE_PAD - db), (0, 0))).reshape(1, SAMPLE_PAD, d)
    n_prompt = bsz * seq
    outs = {name: [] for name in ('pk', 'pv', 'sk', 'sv')}
    outs['pk'] = [[] for _ in range(N_GROUPS)]
    outs['pv'] = [[] for _ in range(N_GROUPS)]
    outs['sk'] = [[] for _ in range(N_GROUPS)]
    outs['sv'] = [[] for _ in range(N_GROUPS)]
    ppool, spool = [], []
    for l in range(depth):
        c_all = jnp.concatenate([c_prompt, c_sample], axis=0)
        mod = _ada_call(c_all, ada_w[l], ada_b[l])
        mod_p = [m.reshape(bsz, 1, d) for m in jnp.split(mod[:bsz], 6, axis=-1)]
        mod_s = [jnp.pad(m, ((0, SAMPLE_PAD - db), (0, 0))).reshape(1, SAMPLE_PAD, d)
                 for m in jnp.split(mod[bsz:], 6, axis=-1)]
        w_in_bf = w_in[l].astype(BF16)
        qg = (jnp.tile(q_gain[l], (1, HEADS)) * (HEAD_DIM ** -0.5)).reshape(1, -1)
        kg = jnp.tile(k_gain[l], (1, HEADS)).reshape(1, -1)
        g1 = norm1[l].reshape(1, d)
        g2 = norm2[l].reshape(1, d)
        wa, wb, wo = (w_br_a[l].astype(BF16), w_br_b[l].astype(BF16), w_out[l].astype(BF16))
        pw = pool_w[l].astype(BF16)
        ps = pool_scale[l].reshape(1, -1)

        q, k, v, u, gates = _inproj_call(xp, mod_p[0], mod_p[1], g1, w_in_bf, qg, kg, bd)
        oa_p = _attn_call(q, k, v, bias_tabs)
        for g, (window, dil) in enumerate(ATTN_GROUPS):
            keep = min(window, seq)
            cs = slice(g * GROUP_W, (g + 1) * GROUP_W)
            outs['pk'][g].append(k[:, seq - keep:, cs].reshape(bsz, keep, HEADS, HEAD_DIM))
            outs['pv'][g].append(v[:, seq - keep:, cs].reshape(bsz, keep, HEADS, HEAD_DIM))
        ppool.append(u[:, seq - POOL_STATE:])
        x1p, h2p = _merge_call(xp, gates, (oa_p, u), (mod_p[2], mod_p[3], mod_p[4]),
                               g2, wa, wb, wo, pw, ps, prompt=True)

        qs, ks, vs, us, gates_s = _inproj_call(xs, mod_s[0], mod_s[1], g1, w_in_bf, qg, kg, bd)
        heads3 = lambda t: t[0, :db].reshape(db, N_GROUPS * HEADS, HEAD_DIM)
        q3, k3, v3 = heads3(qs), heads3(ks), heads3(vs)
        kcl = [c[l] for c in k_caches]
        vcl = [c[l] for c in v_caches]
        oa_s, pooled_s = _decode_call(q3, k3, v3, us[0, :db], state_pool[l], kcl, vcl, bdec, bnew, pw_dec)
        padrows = lambda t: jnp.pad(t.reshape(db, GROUP_W), ((0, SAMPLE_PAD - db), (0, 0))).reshape(1, SAMPLE_PAD, GROUP_W)
        x1s, h2s = _merge_call(xs, gates_s, (padrows(oa_s), padrows(pooled_s)), (mod_s[2], mod_s[3], mod_s[4]),
                               g2, wa, wb, wo, pw, ps, prompt=False)
        for g in range(N_GROUPS):
            hs = slice(g * HEADS, (g + 1) * HEADS)
            outs['sk'][g].append(_roll_call(kcl[g], k3[:, None, hs]))
            outs['sv'][g].append(_roll_call(vcl[g], v3[:, None, hs]))
        spool.append(jnp.concatenate([state_pool[l][:, 1:], us[0, :db, None, :]], axis=1))

        h2 = jnp.concatenate([h2p.reshape(n_prompt, d), h2s.reshape(SAMPLE_PAD, d)], axis=0)
        nt = h2.shape[0]
        idx, wts, rank, counts = _router_call(h2, router_w[l], router_bias[l])
        offs, visits = _visit_schedule(counts[0], nt * TOP_K)
        pos = _slot_call(idx, rank, offs.astype(F32).reshape(1, -1))[:, :TOP_K]
        xs_sorted = _dispatch_call(pos.reshape(nt // ROW_TILE, 1, ROW_TILE * TOP_K), h2)
        y_sorted = _expert_call(visits, xs_sorted, exp_w_gate[l], exp_w_up[l], exp_w_down[l])
        tc = COMBINE_TILE
        pos_tiles = pos.reshape(nt // tc, tc, TOP_K).transpose(0, 2, 1).reshape(nt // tc, 1, TOP_K * tc)
        sg, su, sd = sh_w_gate[l].astype(BF16), sh_w_up[l].astype(BF16), sh_w_down[l].astype(BF16)
        xp = _combine_call(pos_tiles, 0, y_sorted, wts, h2, x1p, mod_p[5], sg, su, sd)
        xs = _combine_call(pos_tiles, n_prompt // tc, y_sorted, wts, h2, x1s, mod_s[5], sg, su, sd)

    stack = lambda parts: jnp.stack(parts)
    res = [xp, xs[0, :db].reshape(db, 1, d)]
    for g in range(N_GROUPS):
        res += [stack(outs['pk'][g]), stack(outs['pv'][g])]
    res.append(stack(ppool))
    for g in range(N_GROUPS):
        res += [stack(outs['sk'][g]), stack(outs['sv'][g])]
    res.append(stack(spool))
    return tuple(res)
```

```python
import functools

import numpy as np
import jax
import jax.numpy as jnp
from jax import lax
from jax.experimental import pallas as pl
from jax.experimental.pallas import tpu as pltpu

F32 = jnp.float32
BF16 = jnp.bfloat16
HIGHEST = lax.Precision.HIGHEST

HEAD_DIM = 64
HEADS = 8
GROUP_W = HEADS * HEAD_DIM
ATTN_GROUPS = ((128, 1), (512, 4), (2048, 16))
N_GROUPS = len(ATTN_GROUPS)
ATTN_STEPS = 128
ATTN_BLOCK = 128
POOL_WINDOWS = (2, 4, 8, 16)
POOL_GROUP = 128
POOL_STATE = 15
N_BUCKETS = 32
REL_MAX_DIST = 2048
N_EXPERTS = 256
TOP_K = 8
N_EXPERT_GROUPS = 8
TOPK_GROUPS = 4
ROUTED_SCALE = 2.5
EPS = 1e-6
NEG = -1e30

LANES = 128
MXU_DIM = 256
VMEM_LIMIT = 56 * 1024 * 1024

ROW_TILE = 256
EXPERT_TILE = 256
COMBINE_TILE = 128
SAMPLE_PAD = 256


def _sigmoid(x):
    return 1.0 / (1.0 + jnp.exp(-x))


def _params(sem):
    return pltpu.CompilerParams(dimension_semantics=sem, vmem_limit_bytes=VMEM_LIMIT)


ROW_SLABS = 8


def _load_rows(ref, n, first=0):
    return jnp.concatenate([ref[pl.ds(first * ROW_SLABS + s, n, stride=ROW_SLABS), :]
                            for s in range(ROW_SLABS)], axis=1)


def _store_rows(ref, val):
    n = val.shape[0]
    for s in range(ROW_SLABS):
        ref[pl.ds(s, n, stride=ROW_SLABS), :] = val[:, s * LANES:(s + 1) * LANES]


def _ada_body(c_ref, w_ref, b_ref, o_ref):
    c = c_ref[...]
    s = c * _sigmoid(c)
    o_ref[...] = jnp.dot(s, w_ref[...], precision=HIGHEST, preferred_element_type=F32) + b_ref[...]


def _ada_call(c, w, b):
    n, d = c.shape
    cols = w.shape[1]
    tn = 512
    return pl.pallas_call(
        _ada_body,
        grid=(cols // tn,),
        in_specs=[pl.BlockSpec((n, d), lambda j: (0, 0)),
                  pl.BlockSpec((d, tn), lambda j: (0, j)),
                  pl.BlockSpec((1, tn), lambda j: (0, j))],
        out_specs=pl.BlockSpec((n, tn), lambda j: (0, j)),
        out_shape=jax.ShapeDtypeStruct((n, cols), F32),
        compiler_params=_params(("arbitrary",)),
        name="adaln",
    )(c, w, b.reshape(1, cols))


def _inproj_body(x_ref, shift_ref, scale_ref, g_ref, w_ref, qg_ref, kg_ref, bd_ref,
                 q_ref, k_ref, v_ref, u_ref, gt_ref):
    x = x_ref[0]
    ms = jnp.mean(x * x, axis=-1, keepdims=True)
    h = x * lax.rsqrt(ms + EPS) * g_ref[...]
    h = h * (1.0 + scale_ref[0]) + shift_ref[0]
    hb = h.astype(BF16)
    qkv = N_GROUPS * GROUP_W
    n_chunks = w_ref.shape[1] // GROUP_W
    for c in range(n_chunks):
        z = jnp.dot(hb, w_ref[:, c * GROUP_W:(c + 1) * GROUP_W], preferred_element_type=F32)
        if c < 2 * N_GROUPS:
            zz = (z * z).astype(BF16)
            hms = jnp.concatenate(
                [jnp.dot(zz[:, s:s + MXU_DIM], bd_ref[...], preferred_element_type=F32)
                 for s in range(0, GROUP_W, MXU_DIM)], axis=1)
            g = c % N_GROUPS
            gain = (qg_ref if c < N_GROUPS else kg_ref)[:, g * GROUP_W:(g + 1) * GROUP_W]
            y = z * lax.rsqrt(hms + EPS) * gain
            if c < N_GROUPS:
                q_ref[0, :, g * GROUP_W:(g + 1) * GROUP_W] = y
            else:
                k_ref[0, :, g * GROUP_W:(g + 1) * GROUP_W] = y
        elif c < 3 * N_GROUPS:
            g = c - 2 * N_GROUPS
            v_ref[0, :, g * GROUP_W:(g + 1) * GROUP_W] = z
        elif c == 3 * N_GROUPS:
            u_ref[0] = z
        else:
            g = c - 3 * N_GROUPS - 1
            gt_ref[0, :, g * GROUP_W:(g + 1) * GROUP_W] = z.astype(BF16)
    del qkv


def _mod_spec(mod, tm, d):
    if mod.shape[1] == 1:
        return pl.BlockSpec((1, 1, d), lambda b, i: (b, 0, 0))
    return pl.BlockSpec((1, tm, d), lambda b, i: (b, i, 0))


def _inproj_call(x, shift, scale, g1, w_bf, qg, kg, bd):
    bsz, s, d = x.shape
    tm = min(ROW_TILE, s)
    qkv = N_GROUPS * GROUP_W
    n_gate = w_bf.shape[1] - 3 * qkv - GROUP_W
    const = lambda b, i: (0, 0)
    row = lambda width: pl.BlockSpec((1, tm, width), lambda b, i: (b, i, 0))
    return pl.pallas_call(
        _inproj_body,
        grid=(bsz, s // tm),
        in_specs=[row(d), _mod_spec(shift, tm, d), _mod_spec(scale, tm, d),
                  pl.BlockSpec((1, d), const),
                  pl.BlockSpec(w_bf.shape, const),
                  pl.BlockSpec((1, qkv), const), pl.BlockSpec((1, qkv), const),
                  pl.BlockSpec((MXU_DIM, MXU_DIM), const)],
        out_specs=[row(qkv), row(qkv), row(qkv), row(GROUP_W), row(n_gate)],
        out_shape=[jax.ShapeDtypeStruct((bsz, s, qkv), F32),
                   jax.ShapeDtypeStruct((bsz, s, qkv), F32),
                   jax.ShapeDtypeStruct((bsz, s, qkv), F32),
                   jax.ShapeDtypeStruct((bsz, s, GROUP_W), F32),
                   jax.ShapeDtypeStruct((bsz, s, n_gate), BF16)],
        compiler_params=_params(("arbitrary", "arbitrary")),
        name="inproj",
    )(x, shift, scale, g1, w_bf, qg, kg, bd)


ATTN_ROWS = ATTN_BLOCK * ATTN_GROUPS[-1][1]
HEADS_PER_SLAB = LANES // HEAD_DIM


def _attn_body(*refs):
    ng = N_GROUPS
    qkv = refs[:3 * ng]
    bias_ref, o_ref = refs[3 * ng], refs[3 * ng + 1]
    scr = refs[3 * ng + 2:]
    first = pl.program_id(2) == 0
    blk = ATTN_BLOCK
    pb = ATTN_ROWS
    hs = HEADS_PER_SLAB
    lane = lax.broadcasted_iota(jnp.int32, (blk, LANES), 1)
    col = lax.broadcasted_iota(jnp.int32, (hs * blk, 2 * blk), 1)
    for g, (_, dil) in enumerate(ATTN_GROUPS):
        q_ref, k_ref, v_ref = qkv[3 * g:3 * g + 3]
        kext, vext, og, lg = scr[4 * g:4 * g + 4]
        span = blk * dil
        bias = bias_ref[g, 0]

        @pl.when(first)
        def _():
            kext[0:span, :] = jnp.zeros((span, LANES), F32)
            vext[0:span, :] = jnp.zeros((span, LANES), F32)
        kext[span:span + pb, :] = k_ref[0]
        vext[span:span + pb, :] = v_ref[0]

        def unit(u, carry):
            qb = u // dil
            base = qb * span + (u % dil)
            rows = lambda start: pl.ds(start, blk, stride=dil) if dil > 1 else pl.ds(pl.multiple_of(start, blk), blk)
            q = q_ref[0, rows(base), :]
            k2 = jnp.concatenate([kext[rows(base), :], kext[rows(base + span), :]], axis=0).astype(BF16)
            v2 = jnp.concatenate([vext[rows(base), :], vext[rows(base + span), :]], axis=0).astype(BF16)
            qs = jnp.concatenate([jnp.where(lane // HEAD_DIM == h, q, 0.0) for h in range(hs)], axis=0).astype(BF16)
            s = lax.dot_general(qs, k2, (((1,), (1,)), ((), ())), preferred_element_type=F32) + bias
            s = jnp.where(jnp.logical_and(jnp.logical_and(first, qb == 0), col < blk), NEG, s)
            m = jnp.max(s, axis=-1, keepdims=True)
            p = jnp.exp(s - m)
            l = jnp.sum(p, axis=-1, keepdims=True)
            o2 = jnp.dot(p.astype(BF16), v2, preferred_element_type=F32) / l
            lse = m + jnp.log(l)
            o = o2[0:blk]
            lw = jnp.broadcast_to(lse[0:blk], (blk, LANES))
            for h in range(1, hs):
                o = jnp.where(lane // HEAD_DIM == h, o2[h * blk:(h + 1) * blk], o)
                lw = jnp.where(lane // HEAD_DIM == h, lse[h * blk:(h + 1) * blk], lw)
            og[rows(base), :] = o
            lg[rows(base), :] = lw
            return carry
        lax.fori_loop(0, pb // blk, unit, 0, unroll=2)
        kext[0:span, :] = kext[pb:pb + span, :]
        vext[0:span, :] = vext[pb:pb + span, :]

    def merge(c, carry):
        rs = pl.ds(pl.multiple_of(c * blk, blk), blk)
        ls = [scr[4 * g + 3][rs, :] for g in range(ng)]
        mx = jnp.maximum(jnp.maximum(ls[0], ls[1]), ls[2])
        es = [jnp.exp(t - mx) for t in ls]
        den = es[0] + es[1] + es[2]
        acc = (es[0] / den) * scr[2][rs, :]
        for g in range(1, ng):
            acc = acc + (es[g] / den) * scr[4 * g + 2][rs, :]
        o_ref[0, rs, :] = acc.astype(o_ref.dtype)
        return carry
    lax.fori_loop(0, pb // blk, merge, 0)


def _attn_call(q, k, v, bias_tabs):
    bsz, s, qkv = q.shape
    pb = ATTN_ROWS
    n_slab = GROUP_W // LANES
    in_specs, args = [], []
    for g in range(N_GROUPS):
        spec = pl.BlockSpec((1, pb, LANES), lambda b, c, i, g=g: (b, i, g * n_slab + c))
        in_specs += [spec, spec, spec]
        args += [q, k, v]
    in_specs.append(pl.BlockSpec((N_GROUPS, 1) + bias_tabs.shape[2:], lambda b, c, i: (0, c, 0, 0)))
    scratch = []
    for _, dil in ATTN_GROUPS:
        ext = pltpu.VMEM((ATTN_BLOCK * dil + pb, LANES), F32)
        scratch += [ext, ext, pltpu.VMEM((pb, LANES), F32), pltpu.VMEM((pb, LANES), F32)]
    return pl.pallas_call(
        _attn_body,
        grid=(bsz, n_slab, s // pb),
        in_specs=in_specs,
        out_specs=pl.BlockSpec((1, pb, LANES), lambda b, c, i: (b, i, c)),
        out_shape=jax.ShapeDtypeStruct((bsz, s, GROUP_W), BF16),
        scratch_shapes=scratch,
        compiler_params=_params(("arbitrary", "arbitrary", "arbitrary")),
        name="attn",
    )(*args, bias_tabs)


def _decode_body(q_ref, kn_ref, vn_ref, u_ref, pool_ref, kc0, vc0, kc1, vc1, kc2, vc2,
                 bdec_ref, bnew_ref, pw_ref, oa_ref, pooled_ref):
    kcs = (kc0, kc1, kc2)
    vcs = (vc0, vc1, vc2)
    outs, lses = [], []
    for g in range(N_GROUPS):
        hs = slice(g * HEADS, (g + 1) * HEADS)
        q = q_ref[0, hs, :]
        kn = kn_ref[0, hs, :]
        vn = vn_ref[0, hs, :]
        kc = kcs[g][0]
        vc = vcs[g][0]
        s_old = jnp.sum(kc * q[None], axis=-1, keepdims=True) + bdec_ref[g]
        s_new = jnp.sum(kn * q, axis=-1, keepdims=True) + bnew_ref[g]
        m = jnp.maximum(jnp.max(s_old, axis=0), s_new)
        p_old = jnp.exp(s_old - m[None])
        p_new = jnp.exp(s_new - m)
        l = jnp.sum(p_old, axis=0) + p_new
        outs.append((jnp.sum(p_old * vc, axis=0) + p_new * vn) / l)
        lses.append(m + jnp.log(l))
    mx = jnp.maximum(jnp.maximum(lses[0], lses[1]), lses[2])
    es = [jnp.exp(t - mx) for t in lses]
    den = es[0] + es[1] + es[2]
    oa_ref[0] = (es[0] / den) * outs[0] + (es[1] / den) * outs[1] + (es[2] / den) * outs[2]
    u = u_ref[0]
    stored = jnp.sum(pool_ref[0] * pw_ref[0:POOL_STATE, :], axis=0, keepdims=True)
    pooled_ref[0] = stored + u * pw_ref[POOL_STATE:POOL_STATE + 1, :] - u


def _decode_call(q, kn, vn, u, pool_prev, k_caches, v_caches, bdec, bnew, pw):
    db = pool_prev.shape[0]
    per_b = lambda t: pl.BlockSpec((1,) + t.shape[1:], lambda b: (b,) + (0,) * (t.ndim - 1))
    const = lambda t: pl.BlockSpec(t.shape, lambda b: (0,) * t.ndim)
    cache_specs, cache_args = [], []
    for g, (_, dil) in enumerate(ATTN_GROUPS):
        for c in (k_caches[g], v_caches[g]):
            rows = c.shape[1]
            cache_args.append(c.reshape(db, rows // dil, dil, HEADS, HEAD_DIM))
            cache_specs.append(pl.BlockSpec((1, ATTN_STEPS, None, HEADS, HEAD_DIM), lambda b: (b, 0, 0, 0, 0)))
    u3 = u.reshape(db, 1, u.shape[-1])
    return pl.pallas_call(
        _decode_body,
        grid=(db,),
        in_specs=[per_b(q), per_b(kn), per_b(vn), per_b(u3), per_b(pool_prev)] + cache_specs +
                 [const(bdec), const(bnew), const(pw)],
        out_specs=[pl.BlockSpec((1, HEADS, HEAD_DIM), lambda b: (b, 0, 0)),
                   pl.BlockSpec((1, 1, GROUP_W), lambda b: (b, 0, 0))],
        out_shape=[jax.ShapeDtypeStruct((db, HEADS, HEAD_DIM), F32),
                   jax.ShapeDtypeStruct((db, 1, GROUP_W), F32)],
        compiler_params=_params(("arbitrary",)),
        name="decode_mix",
    )(q, kn, vn, u3, pool_prev, *cache_args, bdec, bnew, pw)


def _roll_body(c_ref, nxt_ref, new_ref, o_ref):
    ch = c_ref.shape[1]
    last = pl.program_id(1) == pl.num_programs(1) - 1
    o_ref[0, 0:ch - 1] = c_ref[0, 1:ch]
    o_ref[0, ch - 1:ch] = jnp.where(last, new_ref[0], nxt_ref[0])


def _roll_call(cache, new):
    db, rows, nh, e = cache.shape
    ch = min(rows, 256)
    return pl.pallas_call(
        _roll_body,
        grid=(db, rows // ch),
        in_specs=[pl.BlockSpec((1, ch, nh, e), lambda b, i: (b, i, 0, 0)),
                  pl.BlockSpec((1, 1, nh, e), lambda b, i: (b, jnp.minimum((i + 1) * ch, rows - 1), 0, 0)),
                  pl.BlockSpec((1, 1, nh, e), lambda b, i: (b, 0, 0, 0))],
        out_specs=pl.BlockSpec((1, ch, nh, e), lambda b, i: (b, i, 0, 0)),
        out_shape=jax.ShapeDtypeStruct(cache.shape, cache.dtype),
        compiler_params=_params(("arbitrary", "arbitrary")),
        name="cache_roll",
    )(cache, cache, new)


def _merge_tail(x, oa, pooled, gates, gate_msa, shift2, scale2, g2, wa, wb, wo, pw_ref, ps,
                x1_ref, h2_ref):
    mixed = jnp.concatenate(
        [jnp.dot(pooled[:, j * POOL_GROUP:(j + 1) * POOL_GROUP].astype(BF16), pw_ref[j],
                 preferred_element_type=F32) for j in range(len(POOL_WINDOWS))], axis=1) * ps
    d = x.shape[-1]
    ga = gates[:, :d].astype(F32)
    gb = gates[:, d:].astype(F32)
    merged = (_sigmoid(ga) * jnp.dot(oa.astype(BF16), wa, preferred_element_type=F32) +
              _sigmoid(gb) * jnp.dot(mixed.astype(BF16), wb, preferred_element_type=F32))
    y = jnp.dot(merged.astype(BF16), wo, preferred_element_type=F32)
    x1 = x + gate_msa * y
    x1_ref[0] = x1
    ms = jnp.mean(x1 * x1, axis=-1, keepdims=True)
    h2 = x1 * lax.rsqrt(ms + EPS) * g2
    _store_rows(h2_ref.at[0], h2 * (1.0 + scale2) + shift2)


def _merge_prompt_body(x_ref, gt_ref, oa_ref, u_ref, uh_ref,
                       gm_ref, sh_ref, sc_ref, g2_ref, wa_ref, wb_ref, wo_ref, pw_ref, ps_ref,
                       x1_ref, h2_ref):
    i = pl.program_id(1)
    tm = x_ref.shape[1]
    oa = oa_ref[0]
    u = u_ref[0]
    halo = jnp.where(i == 0, 0.0, uh_ref[0])
    ext = jnp.concatenate([halo, u], axis=0)
    hw = halo.shape[0]
    acc = ext
    parts = []
    for j, w in enumerate(POOL_WINDOWS):
        acc = acc[:, POOL_GROUP * (1 if j else 0):]
        acc = acc + pltpu.roll(acc, w // 2, axis=0)
        parts.append(acc[hw:, :POOL_GROUP])
    tot = jnp.concatenate(parts, axis=1)
    lane = lax.broadcasted_iota(jnp.int32, (tm, GROUP_W), 1)
    wcol = jnp.left_shift(2, lane // POOL_GROUP)
    pos = i * tm + lax.broadcasted_iota(jnp.int32, (tm, GROUP_W), 0)
    cnt = jnp.minimum(pos + 1, wcol).astype(F32)
    pooled = tot / cnt - u
    _merge_tail(x_ref[0], oa, pooled, gt_ref[0], gm_ref[0], sh_ref[0], sc_ref[0], g2_ref[...],
                wa_ref[...], wb_ref[...], wo_ref[...], pw_ref, ps_ref[...], x1_ref, h2_ref)


def _merge_sample_body(x_ref, gt_ref, oa_ref, pooled_ref,
                       gm_ref, sh_ref, sc_ref, g2_ref, wa_ref, wb_ref, wo_ref, pw_ref, ps_ref,
                       x1_ref, h2_ref):
    _merge_tail(x_ref[0], oa_ref[0], pooled_ref[0], gt_ref[0], gm_ref[0], sh_ref[0], sc_ref[0], g2_ref[...],
                wa_ref[...], wb_ref[...], wo_ref[...], pw_ref, ps_ref[...], x1_ref, h2_ref)


def _merge_call(x, gates, mixer_inputs, mods, g2, wa, wb, wo, pw, ps, prompt):
    bsz, s, d = x.shape
    assert d == ROW_SLABS * LANES
    tm = min(ROW_TILE, s)
    row = lambda t: pl.BlockSpec((1, tm, t.shape[-1]), lambda b, i: (b, i, 0))
    const = lambda t: pl.BlockSpec(t.shape, lambda b, i: (0,) * t.ndim)
    gm, sh, sc = mods
    in_specs = [row(x), row(gates)]
    args = [x, gates]
    if prompt:
        oa, u = mixer_inputs
        hw = 16
        in_specs += [row(oa), row(u),
                     pl.BlockSpec((1, hw, u.shape[-1]), lambda b, i: (b, jnp.maximum(i * (tm // hw) - 1, 0), 0))]
        args += [oa, u, u]
        body = _merge_prompt_body
    else:
        in_specs += [row(t) for t in mixer_inputs]
        args += list(mixer_inputs)
        body = _merge_sample_body
    in_specs += [_mod_spec(m, tm, d) for m in (gm, sh, sc)]
    in_specs += [const(t) for t in (g2, wa, wb, wo, pw, ps)]
    args += [gm, sh, sc, g2, wa, wb, wo, pw, ps]
    return pl.pallas_call(
        body,
        grid=(bsz, s // tm),
        in_specs=in_specs,
        out_specs=[row(x), pl.BlockSpec((1, tm * ROW_SLABS, LANES), lambda b, i: (b, i, 0))],
        out_shape=[jax.ShapeDtypeStruct((bsz, s, d), F32),
                   jax.ShapeDtypeStruct((bsz, s * ROW_SLABS, LANES), F32)],
        compiler_params=_params(("arbitrary", "arbitrary")),
        name="merge_prompt" if prompt else "merge_sample",
    )(*args)


def _router_body(h_ref, rw_ref, rb_ref, tri_ref, idx_ref, wt_ref, rank_ref, cnt_ref, carry):
    tm = h_ref.shape[0] // ROW_SLABS

    @pl.when(pl.program_id(0) == 0)
    def _():
        carry[...] = jnp.zeros_like(carry)

    ne = rw_ref.shape[1]
    gsz = ne // N_EXPERT_GROUPS
    logits = jnp.dot(_load_rows(h_ref, tm), rw_ref[...], precision=HIGHEST, preferred_element_type=F32)
    scores = _sigmoid(logits)
    sel = scores + rb_ref[...]
    lane = lax.broadcasted_iota(jnp.int32, (tm, ne), 1)
    lanef = lane.astype(F32)
    grp = lane // gsz
    ninf = -jnp.inf
    far = float(ne)

    def first_max(x):
        m = jnp.max(x, axis=-1, keepdims=True)
        at = jnp.min(jnp.where(x == m, lanef, far), axis=-1, keepdims=True)
        return m, at

    gs = []
    for g in range(N_EXPERT_GROUPS):
        xg = jnp.where(grp == g, sel, ninf)
        m1, a1 = first_max(xg)
        m2 = jnp.max(jnp.where(lanef == a1, ninf, xg), axis=-1, keepdims=True)
        gs.append(m1 + m2)
    allowed = jnp.zeros((tm, ne), jnp.bool_)
    for g in range(N_EXPERT_GROUPS):
        ahead = jnp.zeros((tm, 1), F32)
        for o in range(N_EXPERT_GROUPS):
            if o == g:
                continue
            beats = (gs[o] > gs[g]) | ((gs[o] == gs[g]) & (o < g))
            ahead = ahead + beats.astype(F32)
        allowed = allowed | ((grp == g) & (ahead < TOPK_GROUPS))
    masked = jnp.where(allowed, sel, ninf)
    lane_o = lax.broadcasted_iota(jnp.int32, (tm, LANES), 1)
    idx_t = jnp.zeros((tm, LANES), F32)
    wt_t = jnp.zeros((tm, LANES), F32)
    wsum = jnp.zeros((tm, 1), F32)
    picked = jnp.zeros((tm, ne), F32)
    ats = []
    for k in range(TOP_K):
        _, at = first_max(masked)
        hit = lanef == at
        wk = jnp.sum(jnp.where(hit, scores, 0.0), axis=-1, keepdims=True)
        masked = jnp.where(hit, ninf, masked)
        picked = picked + hit.astype(F32)
        ats.append(at)
        idx_t = jnp.where(lane_o == k, at, idx_t)
        wt_t = jnp.where(lane_o == k, wk, wt_t)
        wsum = wsum + wk
    idx_ref[...] = idx_t.astype(jnp.int32)
    wt_ref[...] = wt_t / wsum * ROUTED_SCALE
    before = jnp.dot(tri_ref[...], picked.astype(BF16), preferred_element_type=F32) + carry[...]
    rank_t = jnp.zeros((tm, LANES), F32)
    for k in range(TOP_K):
        rk = jnp.sum(jnp.where(lanef == ats[k], before, 0.0), axis=-1, keepdims=True)
        rank_t = jnp.where(lane_o == k, rk, rank_t)
    rank_ref[...] = rank_t.astype(jnp.int32)
    carry[...] = carry[...] + jnp.sum(picked, axis=0, keepdims=True)
    cnt_ref[...] = carry[...].astype(jnp.int32)


def _router_call(h2, rw, rb):
    nt = h2.shape[0] // ROW_SLABS
    tm = ROW_TILE
    d, ne = rw.shape
    tri = jnp.asarray(np.tril(np.ones((tm, tm), np.float32), -1), BF16)
    tile = pl.BlockSpec((tm, LANES), lambda i: (i, 0))
    return pl.pallas_call(
        _router_body,
        grid=(nt // tm,),
        in_specs=[pl.BlockSpec((tm * ROW_SLABS, LANES), lambda i: (i, 0)),
                  pl.BlockSpec((d, ne), lambda i: (0, 0)),
                  pl.BlockSpec((1, ne), lambda i: (0, 0)),
                  pl.BlockSpec((tm, tm), lambda i: (0, 0))],
        out_specs=[tile, tile, tile, pl.BlockSpec((1, ne), lambda i: (0, 0))],
        out_shape=[jax.ShapeDtypeStruct((nt, LANES), jnp.int32),
                   jax.ShapeDtypeStruct((nt, LANES), F32),
                   jax.ShapeDtypeStruct((nt, LANES), jnp.int32),
                   jax.ShapeDtypeStruct((1, ne), jnp.int32)],
        scratch_shapes=[pltpu.VMEM((1, ne), F32)],
        compiler_params=_params(("arbitrary",)),
        name="router",
    )(h2, rw, rb.reshape(1, ne), tri)


def _slot_body(idx_ref, rank_ref, offs_ref, pos_ref):
    tm = idx_ref.shape[0]
    ne = offs_ref.shape[1]
    idx = idx_ref[...]
    rank = rank_ref[...]
    lane = lax.broadcasted_iota(jnp.int32, (tm, ne), 1)
    lane_o = lax.broadcasted_iota(jnp.int32, (tm, LANES), 1)
    offs = offs_ref[...]
    pos = jnp.zeros((tm, LANES), F32)
    for k in range(TOP_K):
        start = jnp.sum(jnp.where(lane == idx[:, k:k + 1], offs, 0.0), axis=-1, keepdims=True)
        pos = jnp.where(lane_o == k, start, pos)
    pos_ref[...] = pos.astype(jnp.int32) + rank


def _slot_call(idx, rank, offs):
    nt = idx.shape[0]
    tm = ROW_TILE
    ne = offs.shape[1]
    tile = pl.BlockSpec((tm, LANES), lambda i: (i, 0))
    return pl.pallas_call(
        _slot_body,
        grid=(nt // tm,),
        in_specs=[tile, tile, pl.BlockSpec((1, ne), lambda i: (0, 0))],
        out_specs=tile,
        out_shape=jax.ShapeDtypeStruct((nt, LANES), jnp.int32),
        compiler_params=_params(("arbitrary",)),
        name="slots",
    )(idx, rank, offs)


def _row_tile(ref, r):
    return ref.at[pl.ds(pl.multiple_of(r * ROW_SLABS, ROW_SLABS), ROW_SLABS), :]


def _dispatch_body(pos_ref, h_ref, xs_hbm, sem):
    td = h_ref.shape[0] // ROW_SLABS

    def one(t, carry):
        for k in range(TOP_K):
            pltpu.make_async_copy(_row_tile(h_ref, t), _row_tile(xs_hbm, pos_ref[0, 0, t * TOP_K + k]), sem).start()
        return carry
    lax.fori_loop(0, td, one, 0, unroll=2)
    for k in range(TOP_K):
        pltpu.make_async_copy(h_ref, xs_hbm.at[pl.ds(0, td * ROW_SLABS), :], sem).wait()


def _dispatch_call(pos_tok, h2):
    nt = h2.shape[0] // ROW_SLABS
    td = ROW_TILE
    return pl.pallas_call(
        _dispatch_body,
        grid=(nt // td,),
        in_specs=[pl.BlockSpec((1, 1, td * TOP_K), lambda i: (i, 0, 0), memory_space=pltpu.SMEM),
                  pl.BlockSpec((td * ROW_SLABS, LANES), lambda i: (i, 0))],
        out_specs=pl.BlockSpec(memory_space=pl.ANY),
        out_shape=jax.ShapeDtypeStruct((nt * TOP_K * ROW_SLABS, LANES), F32),
        scratch_shapes=[pltpu.SemaphoreType.DMA],
        compiler_params=_params(("arbitrary",)),
        name="dispatch",
    )(pos_tok, h2)


def _expert_body(tile_ref, exp_ref, lo_ref, hi_ref, xs_ref, wg_ref, wu_ref, wd_ref, y_ref, wgb, wub, wdb):
    s = pl.program_id(0)
    prev = jnp.maximum(s - 1, 0)
    first_visit = jnp.logical_or(s == 0, tile_ref[s] != tile_ref[prev])
    new_expert = jnp.logical_or(s == 0, exp_ref[s] != exp_ref[prev])
    lo = lo_ref[s]
    hi = hi_ref[s]

    @pl.when(new_expert)
    def _():
        wgb[...] = wg_ref[0].astype(BF16)
        wub[...] = wu_ref[0].astype(BF16)
        wdb[...] = wd_ref[0].astype(BF16)

    @pl.when(hi > lo)
    def _():
        te = xs_ref.shape[0] // ROW_SLABS
        x = _load_rows(xs_ref, te).astype(BF16)
        gt = jnp.dot(x, wgb[...], preferred_element_type=F32)
        up = jnp.dot(x, wub[...], preferred_element_type=F32)
        act = (gt * _sigmoid(gt)) * up
        y = jnp.dot(act.astype(BF16), wdb[...], preferred_element_type=F32)
        row = lax.broadcasted_iota(jnp.int32, y.shape, 0)
        mine = jnp.logical_and(row >= lo, row < hi)

        @pl.when(first_visit)
        def _():
            _store_rows(y_ref, jnp.where(mine, y, 0.0))

        @pl.when(jnp.logical_not(first_visit))
        def _():
            _store_rows(y_ref, jnp.where(mine, y, _load_rows(y_ref, te)))


def _expert_call(visits, xs, wg, wu, wd):
    tile_id, exp_id, lo, hi = visits
    n_steps = tile_id.shape[0]
    te = EXPERT_TILE
    d, hid = wg.shape[-2:]
    rows = pl.BlockSpec((te * ROW_SLABS, LANES), lambda s, t, e, lo, hi: (t[s], 0))
    grid_spec = pltpu.PrefetchScalarGridSpec(
        num_scalar_prefetch=4,
        grid=(n_steps,),
        in_specs=[
            rows,
            pl.BlockSpec((1, d, hid), lambda s, t, e, lo, hi: (e[s], 0, 0)),
            pl.BlockSpec((1, d, hid), lambda s, t, e, lo, hi: (e[s], 0, 0)),
            pl.BlockSpec((1, hid, d), lambda s, t, e, lo, hi: (e[s], 0, 0)),
        ],
        out_specs=rows,
        scratch_shapes=[pltpu.VMEM((d, hid), BF16), pltpu.VMEM((d, hid), BF16), pltpu.VMEM((hid, d), BF16)],
    )
    return pl.pallas_call(
        _expert_body,
        grid_spec=grid_spec,
        out_shape=jax.ShapeDtypeStruct(xs.shape, F32),
        compiler_params=_params(("arbitrary",)),
        name="expert_ffn",
    )(tile_id, exp_id, lo, hi, xs, wg, wu, wd)


def _gather_rows(idx_ref, src_hbm, dst, sem, n):
    def one(j, carry):
        pltpu.make_async_copy(_row_tile(src_hbm, idx_ref[0, 0, j]), _row_tile(dst, j), sem).start()
        return carry
    lax.fori_loop(0, n, one, 0, unroll=8)


def _combine_body(pos_ref, posn_ref, y_hbm, wt_ref, h_ref, x1_ref, gate_ref, sg_ref, su_ref, sd_ref,
                  out_ref, ybuf, sem):
    nb, ni = pl.num_programs(0), pl.num_programs(1)
    step = pl.program_id(0) * ni + pl.program_id(1)
    slot = step % 2
    tc = h_ref.shape[0] // ROW_SLABS
    rows = tc * TOP_K

    @pl.when(step == 0)
    def _():
        _gather_rows(pos_ref, y_hbm, ybuf.at[0], sem.at[0], rows)

    @pl.when(step + 1 < nb * ni)
    def _():
        _gather_rows(posn_ref, y_hbm, ybuf.at[1 - slot], sem.at[1 - slot], rows)

    pltpu.make_async_copy(y_hbm.at[pl.ds(0, rows * ROW_SLABS), :], ybuf.at[slot], sem.at[slot]).wait()
    wt = wt_ref[...]
    routed = jnp.zeros(out_ref.shape[1:], F32)
    for k in range(TOP_K):
        routed = routed + wt[:, k:k + 1] * _load_rows(ybuf.at[slot], tc, first=k * tc)
    hb = _load_rows(h_ref, tc).astype(BF16)
    gt = jnp.dot(hb, sg_ref[...], preferred_element_type=F32)
    up = jnp.dot(hb, su_ref[...], preferred_element_type=F32)
    shared = jnp.dot(((gt * _sigmoid(gt)) * up).astype(BF16), sd_ref[...], preferred_element_type=F32)
    out_ref[0] = x1_ref[0] + gate_ref[0] * (routed + shared)


def _combine_call(pos_tiles, tile_off, y_sorted, wts, h2, x1, gate, sg, su, sd):
    bsz, s, d = x1.shape
    tc = min(COMBINE_TILE, s)
    ni = s // tc
    n_tiles = pos_tiles.shape[0]
    lin = lambda b, i: tile_off + b * ni + i
    const = lambda t: pl.BlockSpec(t.shape, lambda b, i: (0,) * t.ndim)
    return pl.pallas_call(
        _combine_body,
        grid=(bsz, ni),
        in_specs=[
            pl.BlockSpec((1, 1, tc * TOP_K), lambda b, i: (lin(b, i), 0, 0), memory_space=pltpu.SMEM),
            pl.BlockSpec((1, 1, tc * TOP_K), lambda b, i: (jnp.minimum(lin(b, i) + 1, n_tiles - 1), 0, 0),
                         memory_space=pltpu.SMEM),
            pl.BlockSpec(memory_space=pl.ANY),
            pl.BlockSpec((tc, LANES), lambda b, i: (lin(b, i), 0)),
            pl.BlockSpec((tc * ROW_SLABS, LANES), lambda b, i: (lin(b, i), 0)),
            pl.BlockSpec((1, tc, d), lambda b, i: (b, i, 0)),
            _mod_spec(gate, tc, d),
            const(sg), const(su), const(sd),
        ],
        out_specs=pl.BlockSpec((1, tc, d), lambda b, i: (b, i, 0)),
        out_shape=jax.ShapeDtypeStruct((bsz, s, d), F32),
        scratch_shapes=[pltpu.VMEM((2, tc * TOP_K * ROW_SLABS, LANES), F32), pltpu.SemaphoreType.DMA((2,))],
        compiler_params=_params(("arbitrary", "arbitrary")),
        name="combine",
    )(pos_tiles, pos_tiles, y_sorted, wts, h2, x1, gate, sg, su, sd)


def _t5_bucket(dist):
    exact = N_BUCKETS // 2
    d = np.asarray(dist)
    large = exact + (np.log(np.maximum(d, 1) / exact) / np.log(REL_MAX_DIST / exact) * (N_BUCKETS - exact)).astype(np.int32)
    large = np.minimum(large, N_BUCKETS - 1)
    return np.where(d < exact, d, large).astype(np.int32)


def _group_bias(rel_bias, gi):
    window, dil = ATTN_GROUPS[gi]
    bkt = _t5_bucket(np.arange(window // dil + 1) * dil)
    onehot = jnp.asarray(bkt[:, None] == np.arange(N_BUCKETS)[None, :], F32)
    cols = rel_bias[:, gi * HEADS:(gi + 1) * HEADS].astype(F32)
    return jnp.dot(onehot, cols, precision=HIGHEST).T


def _prompt_bias_table(bias_j):
    blk = ATTN_BLOCK
    n = bias_j.shape[1] - 1
    period = 3 * blk
    row0 = jnp.concatenate([jnp.flip(bias_j, axis=1), jnp.full((HEADS, period - n - 1), NEG, F32)], axis=1)
    flat = jnp.tile(row0, (1, blk))[:, :blk * (period - 1)]
    tab = flat.reshape(HEADS, blk, period - 1)[:, :, :2 * blk]
    return tab.reshape(HEADS // HEADS_PER_SLAB, HEADS_PER_SLAB * blk, 2 * blk)


def _decode_bias_tables(bias_j):
    old = jnp.flip(bias_j[:, 1:], axis=1).T[:, :, None]
    new = bias_j[:, 0][:, None]
    return old, new


def _visit_schedule(counts, na):
    te = EXPERT_TILE
    n_tiles = na // te
    n_steps = n_tiles + N_EXPERTS - 1
    ends = jnp.cumsum(counts)
    offs = ends - counts
    first_tile = offs // te
    n_vis = jnp.where(counts > 0, (ends - 1) // te - first_tile + 1, 0)
    v_end = jnp.cumsum(n_vis)
    v_start = v_end - n_vis
    step = jnp.arange(n_steps, dtype=jnp.int32)
    exp_id = jnp.minimum(jnp.sum(v_end[None, :] <= step[:, None], axis=1), N_EXPERTS - 1).astype(jnp.int32)
    live = step < v_end[-1]
    tile_id = jnp.where(live, first_tile[exp_id] + step - v_start[exp_id], n_tiles - 1).astype(jnp.int32)
    lo = jnp.where(live, jnp.clip(offs[exp_id] - tile_id * te, 0, te), 0).astype(jnp.int32)
    hi = jnp.where(live, jnp.clip(ends[exp_id] - tile_id * te, 0, te), 0).astype(jnp.int32)
    return offs, (tile_id, exp_id, lo, hi)


def kernel(x_prompt, x_sample, cache_k_w128, cache_v_w128, cache_k_w512, cache_v_w512, cache_k_w2048, cache_v_w2048, state_pool, c_prompt, c_sample, ada_w, ada_b, norm1, norm2, w_in, q_gain, k_gain, rel_bias, pool_w, pool_scale, w_br_a, w_br_b, w_out, router_w, router_bias, exp_w_gate, exp_w_up, exp_w_down, sh_w_gate, sh_w_up, sh_w_down):
    depth = ada_w.shape[0]
    bsz, seq, d = x_prompt.shape
    db = x_sample.shape[0]
    assert x_sample.shape[1] == 1 and db <= SAMPLE_PAD and seq % (ATTN_BLOCK * ATTN_GROUPS[-1][1]) == 0
    k_caches = (cache_k_w128, cache_k_w512, cache_k_w2048)
    v_caches = (cache_v_w128, cache_v_w512, cache_v_w2048)
    for c, (w, dil) in zip(k_caches, ATTN_GROUPS):
        assert c.shape[2] == w == ATTN_STEPS * dil

    bias_js = [_group_bias(rel_bias, g) for g in range(N_GROUPS)]
    bias_tabs = jnp.stack([_prompt_bias_table(b) for b in bias_js])
    dec = [_decode_bias_tables(b) for b in bias_js]
    bdec = jnp.stack([t[0] for t in dec])
    bnew = jnp.stack([t[1] for t in dec])
    bd = jnp.asarray((np.arange(MXU_DIM)[:, None] // HEAD_DIM == np.arange(MXU_DIM)[None, :] // HEAD_DIM)
                     / HEAD_DIM, BF16)
    wcols = np.repeat(np.asarray(POOL_WINDOWS), POOL_GROUP)
    pw_dec = jnp.asarray(np.where(np.arange(POOL_STATE + 1)[:, None] >= POOL_STATE + 1 - wcols[None, :],
                                  1.0 / wcols[None, :], 0.0), F32)

    xp = x_prompt
    xs = jnp.pad(x_sample.reshape(db, d), ((0, SAMPLE_PAD - db), (0, 0))).reshape(1, SAMPLE_PAD, d)
    n_prompt = bsz * seq
    outs = {name: [] for name in ('pk', 'pv', 'sk', 'sv')}
    outs['pk'] = [[] for _ in range(N_GROUPS)]
    outs['pv'] = [[] for _ in range(N_GROUPS)]
    outs['sk'] = [[] for _ in range(N_GROUPS)]
    outs['sv'] = [[] for _ in range(N_GROUPS)]
    ppool, spool = [], []
    for l in range(depth):
        c_all = jnp.concatenate([c_prompt, c_sample], axis=0)
        mod = _ada_call(c_all, ada_w[l], ada_b[l])
        mod_p = [m.reshape(bsz, 1, d) for m in jnp.split(mod[:bsz], 6, axis=-1)]
        mod_s = [jnp.pad(m, ((0, SAMPLE_PAD - db), (0, 0))).reshape(1, SAMPLE_PAD, d)
                 for m in jnp.split(mod[bsz:], 6, axis=-1)]
        w_in_bf = w_in[l].astype(BF16)
        qg = (jnp.tile(q_gain[l], (1, HEADS)) * (HEAD_DIM ** -0.5)).reshape(1, -1)
        kg = jnp.tile(k_gain[l], (1, HEADS)).reshape(1, -1)
        g1 = norm1[l].reshape(1, d)
        g2 = norm2[l].reshape(1, d)
        wa, wb, wo = (w_br_a[l].astype(BF16), w_br_b[l].astype(BF16), w_out[l].astype(BF16))
        pw = pool_w[l].astype(BF16)
        ps = pool_scale[l].reshape(1, -1)

        q, k, v, u, gates = _inproj_call(xp, mod_p[0], mod_p[1], g1, w_in_bf, qg, kg, bd)
        oa_p = _attn_call(q, k, v, bias_tabs)
        for g, (window, dil) in enumerate(ATTN_GROUPS):
            keep = min(window, seq)
            cs = slice(g * GROUP_W, (g + 1) * GROUP_W)
            outs['pk'][g].append(k[:, seq - keep:, cs].reshape(bsz, keep, HEADS, HEAD_DIM))
            outs['pv'][g].append(v[:, seq - keep:, cs].reshape(bsz, keep, HEADS, HEAD_DIM))
        ppool.append(u[:, seq - POOL_STATE:])
        x1p, h2p = _merge_call(xp, gates, (oa_p, u), (mod_p[2], mod_p[3], mod_p[4]),
                               g2, wa, wb, wo, pw, ps, prompt=True)

        qs, ks, vs, us, gates_s = _inproj_call(xs, mod_s[0], mod_s[1], g1, w_in_bf, qg, kg, bd)
        heads3 = lambda t: t[0, :db].reshape(db, N_GROUPS * HEADS, HEAD_DIM)
        q3, k3, v3 = heads3(qs), heads3(ks), heads3(vs)
        kcl = [c[l] for c in k_caches]
        vcl = [c[l] for c in v_caches]
        oa_s, pooled_s = _decode_call(q3, k3, v3, us[0, :db], state_pool[l], kcl, vcl, bdec, bnew, pw_dec)
        padrows = lambda t: jnp.pad(t.reshape(db, GROUP_W), ((0, SAMPLE_PAD - db), (0, 0))).reshape(1, SAMPLE_PAD, GROUP_W)
        x1s, h2s = _merge_call(xs, gates_s, (padrows(oa_s), padrows(pooled_s)), (mod_s[2], mod_s[3], mod_s[4]),
                               g2, wa, wb, wo, pw, ps, prompt=False)
        for g in range(N_GROUPS):
            hs = slice(g * HEADS, (g + 1) * HEADS)
            outs['sk'][g].append(_roll_call(kcl[g], k3[:, None, hs]))
            outs['sv'][g].append(_roll_call(vcl[g], v3[:, None, hs]))
        spool.append(jnp.concatenate([state_pool[l][:, 1:], us[0, :db, None, :]], axis=1))

        h2 = jnp.concatenate([h2p.reshape(n_prompt * ROW_SLABS, LANES), h2s.reshape(SAMPLE_PAD * ROW_SLABS, LANES)], axis=0)
        nt = n_prompt + SAMPLE_PAD
        idx, wts, rank, counts = _router_call(h2, router_w[l], router_bias[l])
        offs, visits = _visit_schedule(counts[0], nt * TOP_K)
        pos = _slot_call(idx, rank, offs.astype(F32).reshape(1, -1))[:, :TOP_K]
        xs_sorted = _dispatch_call(pos.reshape(nt // ROW_TILE, 1, ROW_TILE * TOP_K), h2)
        y_sorted = _expert_call(visits, xs_sorted, exp_w_gate[l], exp_w_up[l], exp_w_down[l])
        tc = COMBINE_TILE
        pos_tiles = pos.reshape(nt // tc, tc, TOP_K).transpose(0, 2, 1).reshape(nt // tc, 1, TOP_K * tc)
        sg, su, sd = sh_w_gate[l].astype(BF16), sh_w_up[l].astype(BF16), sh_w_down[l].astype(BF16)
        xp = _combine_call(pos_tiles, 0, y_sorted, wts, h2, x1p, mod_p[5], sg, su, sd)
        xs = _combine_call(pos_tiles, n_prompt // tc, y_sorted, wts, h2, x1s, mod_s[5], sg, su, sd)

    stack = lambda parts: jnp.stack(parts)
    res = [xp, xs[0, :db].reshape(db, 1, d)]
    for g in range(N_GROUPS):
        res += [stack(outs['pk'][g]), stack(outs['pv'][g])]
    res.append(stack(ppool))
    for g in range(N_GROUPS):
        res += [stack(outs['sk'][g]), stack(outs['sv'][g])]
    res.append(stack(spool))
    return tuple(res)
```

```python
import functools

import numpy as np
import jax
import jax.numpy as jnp
from jax import lax
from jax.experimental import pallas as pl
from jax.experimental.pallas import tpu as pltpu

F32 = jnp.float32
BF16 = jnp.bfloat16
HIGHEST = lax.Precision.HIGHEST

HEAD_DIM = 64
HEADS = 8
GROUP_W = HEADS * HEAD_DIM
ATTN_GROUPS = ((128, 1), (512, 4), (2048, 16))
N_GROUPS = len(ATTN_GROUPS)
ATTN_STEPS = 128
ATTN_BLOCK = 128
POOL_WINDOWS = (2, 4, 8, 16)
POOL_GROUP = 128
POOL_STATE = 15
N_BUCKETS = 32
REL_MAX_DIST = 2048
N_EXPERTS = 256
TOP_K = 8
N_EXPERT_GROUPS = 8
TOPK_GROUPS = 4
ROUTED_SCALE = 2.5
EPS = 1e-6
NEG = -1e30

LANES = 128
MXU_DIM = 256
VMEM_LIMIT = 56 * 1024 * 1024

INPROJ_TILE = 512
ROW_TILE = 256
EXPERT_TILE = 256
COMBINE_TILE = 128
SAMPLE_PAD = 256


def _sigmoid(x):
    return 1.0 / (1.0 + jnp.exp(-x))


def _params(sem):
    return pltpu.CompilerParams(dimension_semantics=sem, vmem_limit_bytes=VMEM_LIMIT)


ROW_SLABS = 8


def _load_rows(ref, n, first=0):
    return jnp.concatenate([ref[pl.ds(first * ROW_SLABS + s, n, stride=ROW_SLABS), :]
                            for s in range(ROW_SLABS)], axis=1)


def _store_rows(ref, val):
    n = val.shape[0]
    for s in range(ROW_SLABS):
        ref[pl.ds(s, n, stride=ROW_SLABS), :] = val[:, s * LANES:(s + 1) * LANES]


def _ada_body(c_ref, w_ref, b_ref, o_ref):
    c = c_ref[...]
    s = c * _sigmoid(c)
    o_ref[...] = jnp.dot(s, w_ref[...], precision=HIGHEST, preferred_element_type=F32) + b_ref[...]


def _ada_call(c, w, b):
    n, d = c.shape
    cols = w.shape[1]
    tn = 512
    return pl.pallas_call(
        _ada_body,
        grid=(cols // tn,),
        in_specs=[pl.BlockSpec((n, d), lambda j: (0, 0)),
                  pl.BlockSpec((d, tn), lambda j: (0, j)),
                  pl.BlockSpec((1, tn), lambda j: (0, j))],
        out_specs=pl.BlockSpec((n, tn), lambda j: (0, j)),
        out_shape=jax.ShapeDtypeStruct((n, cols), F32),
        compiler_params=_params(("arbitrary",)),
        name="adaln",
    )(c, w, b.reshape(1, cols))


def _inproj_body(x_ref, shift_ref, scale_ref, g_ref, w_ref, qg_ref, kg_ref, bd_ref,
                 q_ref, k_ref, v_ref, u_ref, gt_ref):
    x = x_ref[0]
    ms = jnp.mean(x * x, axis=-1, keepdims=True)
    h = x * lax.rsqrt(ms + EPS) * g_ref[...]
    h = h * (1.0 + scale_ref[0]) + shift_ref[0]
    hb = h.astype(BF16)
    qkv = N_GROUPS * GROUP_W
    n_chunks = w_ref.shape[1] // GROUP_W
    for c in range(n_chunks):
        z = jnp.dot(hb, w_ref[:, c * GROUP_W:(c + 1) * GROUP_W], preferred_element_type=F32)
        if c < 2 * N_GROUPS:
            zz = (z * z).astype(BF16)
            hms = jnp.concatenate(
                [jnp.dot(zz[:, s:s + MXU_DIM], bd_ref[...], preferred_element_type=F32)
                 for s in range(0, GROUP_W, MXU_DIM)], axis=1)
            g = c % N_GROUPS
            gain = (qg_ref if c < N_GROUPS else kg_ref)[:, g * GROUP_W:(g + 1) * GROUP_W]
            y = z * lax.rsqrt(hms + EPS) * gain
            if c < N_GROUPS:
                q_ref[0, :, g * GROUP_W:(g + 1) * GROUP_W] = y
            else:
                k_ref[0, :, g * GROUP_W:(g + 1) * GROUP_W] = y
        elif c < 3 * N_GROUPS:
            g = c - 2 * N_GROUPS
            v_ref[0, :, g * GROUP_W:(g + 1) * GROUP_W] = z
        elif c == 3 * N_GROUPS:
            u_ref[0] = z
        else:
            g = c - 3 * N_GROUPS - 1
            gt_ref[0, :, g * GROUP_W:(g + 1) * GROUP_W] = z.astype(BF16)
    del qkv


def _mod_spec(mod, tm, d):
    if mod.shape[1] == 1:
        return pl.BlockSpec((1, 1, d), lambda b, i: (b, 0, 0))
    return pl.BlockSpec((1, tm, d), lambda b, i: (b, i, 0))


def _inproj_call(x, shift, scale, g1, w_bf, qg, kg, bd):
    bsz, s, d = x.shape
    tm = min(INPROJ_TILE, s)
    qkv = N_GROUPS * GROUP_W
    n_gate = w_bf.shape[1] - 3 * qkv - GROUP_W
    const = lambda b, i: (0, 0)
    row = lambda width: pl.BlockSpec((1, tm, width), lambda b, i: (b, i, 0))
    return pl.pallas_call(
        _inproj_body,
        grid=(bsz, s // tm),
        in_specs=[row(d), _mod_spec(shift, tm, d), _mod_spec(scale, tm, d),
                  pl.BlockSpec((1, d), const),
                  pl.BlockSpec(w_bf.shape, const, pipeline_mode=pl.Buffered(1)),
                  pl.BlockSpec((1, qkv), const), pl.BlockSpec((1, qkv), const),
                  pl.BlockSpec((MXU_DIM, MXU_DIM), const)],
        out_specs=[row(qkv), row(qkv), row(qkv), row(GROUP_W), row(n_gate)],
        out_shape=[jax.ShapeDtypeStruct((bsz, s, qkv), F32),
                   jax.ShapeDtypeStruct((bsz, s, qkv), F32),
                   jax.ShapeDtypeStruct((bsz, s, qkv), F32),
                   jax.ShapeDtypeStruct((bsz, s, GROUP_W), F32),
                   jax.ShapeDtypeStruct((bsz, s, n_gate), BF16)],
        compiler_params=_params(("arbitrary", "arbitrary")),
        name="inproj",
    )(x, shift, scale, g1, w_bf, qg, kg, bd)


ATTN_ROWS = ATTN_BLOCK * ATTN_GROUPS[-1][1]
HEADS_PER_SLAB = LANES // HEAD_DIM


def _attn_body(*refs):
    ng = N_GROUPS
    qkv = refs[:3 * ng]
    bias_ref, o_ref = refs[3 * ng], refs[3 * ng + 1]
    scr = refs[3 * ng + 2:]
    first = pl.program_id(2) == 0
    blk = ATTN_BLOCK
    pb = ATTN_ROWS
    hs = HEADS_PER_SLAB
    lane = lax.broadcasted_iota(jnp.int32, (blk, LANES), 1)
    col = lax.broadcasted_iota(jnp.int32, (hs * blk, 2 * blk), 1)
    for g, (_, dil) in enumerate(ATTN_GROUPS):
        q_ref, k_ref, v_ref = qkv[3 * g:3 * g + 3]
        kext, vext, og, lg = scr[4 * g:4 * g + 4]
        span = blk * dil
        bias = bias_ref[g, 0]

        @pl.when(first)
        def _():
            kext[0:span, :] = jnp.zeros((span, LANES), F32)
            vext[0:span, :] = jnp.zeros((span, LANES), F32)
        kext[span:span + pb, :] = k_ref[0]
        vext[span:span + pb, :] = v_ref[0]

        def unit(u, carry):
            qb = u // dil
            base = qb * span + (u % dil)
            rows = lambda start: pl.ds(start, blk, stride=dil) if dil > 1 else pl.ds(pl.multiple_of(start, blk), blk)
            q = q_ref[0, rows(base), :]
            k2 = jnp.concatenate([kext[rows(base), :], kext[rows(base + span), :]], axis=0).astype(BF16)
            v2 = jnp.concatenate([vext[rows(base), :], vext[rows(base + span), :]], axis=0).astype(BF16)
            qs = jnp.concatenate([jnp.where(lane // HEAD_DIM == h, q, 0.0) for h in range(hs)], axis=0).astype(BF16)
            s = lax.dot_general(qs, k2, (((1,), (1,)), ((), ())), preferred_element_type=F32) + bias
            s = jnp.where(jnp.logical_and(jnp.logical_and(first, qb == 0), col < blk), NEG, s)
            m = jnp.max(s, axis=-1, keepdims=True)
            p = jnp.exp(s - m)
            l = jnp.sum(p, axis=-1, keepdims=True)
            o2 = jnp.dot(p.astype(BF16), v2, preferred_element_type=F32) / l
            lse = m + jnp.log(l)
            o = o2[0:blk]
            lw = jnp.broadcast_to(lse[0:blk], (blk, LANES))
            for h in range(1, hs):
                o = jnp.where(lane // HEAD_DIM == h, o2[h * blk:(h + 1) * blk], o)
                lw = jnp.where(lane // HEAD_DIM == h, lse[h * blk:(h + 1) * blk], lw)
            og[rows(base), :] = o
            lg[rows(base), :] = lw
            return carry
        lax.fori_loop(0, pb // blk, unit, 0, unroll=2)
        kext[0:span, :] = kext[pb:pb + span, :]
        vext[0:span, :] = vext[pb:pb + span, :]

    def merge(c, carry):
        rs = pl.ds(pl.multiple_of(c * blk, blk), blk)
        ls = [scr[4 * g + 3][rs, :] for g in range(ng)]
        mx = jnp.maximum(jnp.maximum(ls[0], ls[1]), ls[2])
        es = [jnp.exp(t - mx) for t in ls]
        den = es[0] + es[1] + es[2]
        acc = (es[0] / den) * scr[2][rs, :]
        for g in range(1, ng):
            acc = acc + (es[g] / den) * scr[4 * g + 2][rs, :]
        o_ref[0, rs, :] = acc.astype(o_ref.dtype)
        return carry
    lax.fori_loop(0, pb // blk, merge, 0)


def _attn_call(q, k, v, bias_tabs):
    bsz, s, qkv = q.shape
    pb = ATTN_ROWS
    n_slab = GROUP_W // LANES
    in_specs, args = [], []
    for g in range(N_GROUPS):
        spec = pl.BlockSpec((1, pb, LANES), lambda b, c, i, g=g: (b, i, g * n_slab + c))
        in_specs += [spec, spec, spec]
        args += [q, k, v]
    in_specs.append(pl.BlockSpec((N_GROUPS, 1) + bias_tabs.shape[2:], lambda b, c, i: (0, c, 0, 0)))
    scratch = []
    for _, dil in ATTN_GROUPS:
        ext = pltpu.VMEM((ATTN_BLOCK * dil + pb, LANES), F32)
        scratch += [ext, ext, pltpu.VMEM((pb, LANES), F32), pltpu.VMEM((pb, LANES), F32)]
    return pl.pallas_call(
        _attn_body,
        grid=(bsz, n_slab, s // pb),
        in_specs=in_specs,
        out_specs=pl.BlockSpec((1, pb, LANES), lambda b, c, i: (b, i, c)),
        out_shape=jax.ShapeDtypeStruct((bsz, s, GROUP_W), BF16),
        scratch_shapes=scratch,
        compiler_params=_params(("arbitrary", "arbitrary", "arbitrary")),
        name="attn",
    )(*args, bias_tabs)


def _decode_body(q_ref, kn_ref, vn_ref, u_ref, pool_ref, kc0, vc0, kc1, vc1, kc2, vc2,
                 bdec_ref, bnew_ref, pw_ref, oa_ref, pooled_ref):
    kcs = (kc0, kc1, kc2)
    vcs = (vc0, vc1, vc2)
    outs, lses = [], []
    for g in range(N_GROUPS):
        hs = slice(g * HEADS, (g + 1) * HEADS)
        q = q_ref[0, hs, :]
        kn = kn_ref[0, hs, :]
        vn = vn_ref[0, hs, :]
        kc = kcs[g][0]
        vc = vcs[g][0]
        s_old = jnp.sum(kc * q[None], axis=-1, keepdims=True) + bdec_ref[g]
        s_new = jnp.sum(kn * q, axis=-1, keepdims=True) + bnew_ref[g]
        m = jnp.maximum(jnp.max(s_old, axis=0), s_new)
        p_old = jnp.exp(s_old - m[None])
        p_new = jnp.exp(s_new - m)
        l = jnp.sum(p_old, axis=0) + p_new
        outs.append((jnp.sum(p_old * vc, axis=0) + p_new * vn) / l)
        lses.append(m + jnp.log(l))
    mx = jnp.maximum(jnp.maximum(lses[0], lses[1]), lses[2])
    es = [jnp.exp(t - mx) for t in lses]
    den = es[0] + es[1] + es[2]
    oa_ref[0] = (es[0] / den) * outs[0] + (es[1] / den) * outs[1] + (es[2] / den) * outs[2]
    u = u_ref[0]
    stored = jnp.sum(pool_ref[0] * pw_ref[0:POOL_STATE, :], axis=0, keepdims=True)
    pooled_ref[0] = stored + u * pw_ref[POOL_STATE:POOL_STATE + 1, :] - u


def _decode_call(q, kn, vn, u, pool_prev, k_caches, v_caches, bdec, bnew, pw):
    db = pool_prev.shape[0]
    per_b = lambda t: pl.BlockSpec((1,) + t.shape[1:], lambda b: (b,) + (0,) * (t.ndim - 1))
    const = lambda t: pl.BlockSpec(t.shape, lambda b: (0,) * t.ndim)
    cache_specs, cache_args = [], []
    for g, (_, dil) in enumerate(ATTN_GROUPS):
        for c in (k_caches[g], v_caches[g]):
            rows = c.shape[1]
            cache_args.append(c.reshape(db, rows // dil, dil, HEADS, HEAD_DIM))
            cache_specs.append(pl.BlockSpec((1, ATTN_STEPS, None, HEADS, HEAD_DIM), lambda b: (b, 0, 0, 0, 0)))
    u3 = u.reshape(db, 1, u.shape[-1])
    return pl.pallas_call(
        _decode_body,
        grid=(db,),
        in_specs=[per_b(q), per_b(kn), per_b(vn), per_b(u3), per_b(pool_prev)] + cache_specs +
                 [const(bdec), const(bnew), const(pw)],
        out_specs=[pl.BlockSpec((1, HEADS, HEAD_DIM), lambda b: (b, 0, 0)),
                   pl.BlockSpec((1, 1, GROUP_W), lambda b: (b, 0, 0))],
        out_shape=[jax.ShapeDtypeStruct((db, HEADS, HEAD_DIM), F32),
                   jax.ShapeDtypeStruct((db, 1, GROUP_W), F32)],
        compiler_params=_params(("arbitrary",)),
        name="decode_mix",
    )(q, kn, vn, u3, pool_prev, *cache_args, bdec, bnew, pw)


def _roll_body(c_ref, nxt_ref, new_ref, o_ref):
    ch = c_ref.shape[1]
    last = pl.program_id(1) == pl.num_programs(1) - 1
    o_ref[0, 0:ch - 1] = c_ref[0, 1:ch]
    o_ref[0, ch - 1:ch] = jnp.where(last, new_ref[0], nxt_ref[0])


def _roll_call(cache, new):
    db, rows, nh, e = cache.shape
    ch = min(rows, 1024)
    return pl.pallas_call(
        _roll_body,
        grid=(db, rows // ch),
        in_specs=[pl.BlockSpec((1, ch, nh, e), lambda b, i: (b, i, 0, 0)),
                  pl.BlockSpec((1, 1, nh, e), lambda b, i: (b, jnp.minimum((i + 1) * ch, rows - 1), 0, 0)),
                  pl.BlockSpec((1, 1, nh, e), lambda b, i: (b, 0, 0, 0))],
        out_specs=pl.BlockSpec((1, ch, nh, e), lambda b, i: (b, i, 0, 0)),
        out_shape=jax.ShapeDtypeStruct(cache.shape, cache.dtype),
        compiler_params=_params(("arbitrary", "arbitrary")),
        name="cache_roll",
    )(cache, cache, new)


def _merge_tail(x, oa, pooled, gates, gate_msa, shift2, scale2, g2, wa, wb, wo, pw_ref, ps,
                x1_ref, h2_ref):
    mixed = jnp.concatenate(
        [jnp.dot(pooled[:, j * POOL_GROUP:(j + 1) * POOL_GROUP].astype(BF16), pw_ref[j],
                 preferred_element_type=F32) for j in range(len(POOL_WINDOWS))], axis=1) * ps
    d = x.shape[-1]
    ga = gates[:, :d].astype(F32)
    gb = gates[:, d:].astype(F32)
    merged = (_sigmoid(ga) * jnp.dot(oa.astype(BF16), wa, preferred_element_type=F32) +
              _sigmoid(gb) * jnp.dot(mixed.astype(BF16), wb, preferred_element_type=F32))
    y = jnp.dot(merged.astype(BF16), wo, preferred_element_type=F32)
    x1 = x + gate_msa * y
    x1_ref[0] = x1
    ms = jnp.mean(x1 * x1, axis=-1, keepdims=True)
    h2 = x1 * lax.rsqrt(ms + EPS) * g2
    _store_rows(h2_ref.at[0], h2 * (1.0 + scale2) + shift2)


def _merge_prompt_body(x_ref, gt_ref, oa_ref, u_ref, uh_ref,
                       gm_ref, sh_ref, sc_ref, g2_ref, wa_ref, wb_ref, wo_ref, pw_ref, ps_ref,
                       x1_ref, h2_ref):
    i = pl.program_id(1)
    tm = x_ref.shape[1]
    oa = oa_ref[0]
    u = u_ref[0]
    halo = jnp.where(i == 0, 0.0, uh_ref[0])
    ext = jnp.concatenate([halo, u], axis=0)
    hw = halo.shape[0]
    acc = ext
    parts = []
    for j, w in enumerate(POOL_WINDOWS):
        acc = acc[:, POOL_GROUP * (1 if j else 0):]
        acc = acc + pltpu.roll(acc, w // 2, axis=0)
        parts.append(acc[hw:, :POOL_GROUP])
    tot = jnp.concatenate(parts, axis=1)
    lane = lax.broadcasted_iota(jnp.int32, (tm, GROUP_W), 1)
    wcol = jnp.left_shift(2, lane // POOL_GROUP)
    pos = i * tm + lax.broadcasted_iota(jnp.int32, (tm, GROUP_W), 0)
    cnt = jnp.minimum(pos + 1, wcol).astype(F32)
    pooled = tot / cnt - u
    _merge_tail(x_ref[0], oa, pooled, gt_ref[0], gm_ref[0], sh_ref[0], sc_ref[0], g2_ref[...],
                wa_ref[...], wb_ref[...], wo_ref[...], pw_ref, ps_ref[...], x1_ref, h2_ref)


def _merge_sample_body(x_ref, gt_ref, oa_ref, pooled_ref,
                       gm_ref, sh_ref, sc_ref, g2_ref, wa_ref, wb_ref, wo_ref, pw_ref, ps_ref,
                       x1_ref, h2_ref):
    _merge_tail(x_ref[0], oa_ref[0], pooled_ref[0], gt_ref[0], gm_ref[0], sh_ref[0], sc_ref[0], g2_ref[...],
                wa_ref[...], wb_ref[...], wo_ref[...], pw_ref, ps_ref[...], x1_ref, h2_ref)


def _merge_call(x, gates, mixer_inputs, mods, g2, wa, wb, wo, pw, ps, prompt):
    bsz, s, d = x.shape
    assert d == ROW_SLABS * LANES
    tm = min(ROW_TILE, s)
    row = lambda t: pl.BlockSpec((1, tm, t.shape[-1]), lambda b, i: (b, i, 0))
    const = lambda t: pl.BlockSpec(t.shape, lambda b, i: (0,) * t.ndim)
    gm, sh, sc = mods
    in_specs = [row(x), row(gates)]
    args = [x, gates]
    if prompt:
        oa, u = mixer_inputs
        hw = 16
        in_specs += [row(oa), row(u),
                     pl.BlockSpec((1, hw, u.shape[-1]), lambda b, i: (b, jnp.maximum(i * (tm // hw) - 1, 0), 0))]
        args += [oa, u, u]
        body = _merge_prompt_body
    else:
        in_specs += [row(t) for t in mixer_inputs]
        args += list(mixer_inputs)
        body = _merge_sample_body
    in_specs += [_mod_spec(m, tm, d) for m in (gm, sh, sc)]
    in_specs += [const(t) for t in (g2, wa, wb, wo, pw, ps)]
    args += [gm, sh, sc, g2, wa, wb, wo, pw, ps]
    return pl.pallas_call(
        body,
        grid=(bsz, s // tm),
        in_specs=in_specs,
        out_specs=[row(x), pl.BlockSpec((1, tm * ROW_SLABS, LANES), lambda b, i: (b, i, 0))],
        out_shape=[jax.ShapeDtypeStruct((bsz, s, d), F32),
                   jax.ShapeDtypeStruct((bsz, s * ROW_SLABS, LANES), F32)],
        compiler_params=_params(("arbitrary", "arbitrary")),
        name="merge_prompt" if prompt else "merge_sample",
    )(*args)


def _router_body(h_ref, rw_ref, rb_ref, tri_ref, idx_ref, wt_ref, rank_ref, cnt_ref, carry):
    tm = h_ref.shape[0] // ROW_SLABS

    @pl.when(pl.program_id(0) == 0)
    def _():
        carry[...] = jnp.zeros_like(carry)

    ne = rw_ref.shape[1]
    gsz = ne // N_EXPERT_GROUPS
    logits = jnp.dot(_load_rows(h_ref, tm), rw_ref[...], precision=HIGHEST, preferred_element_type=F32)
    scores = _sigmoid(logits)
    sel = scores + rb_ref[...]
    lane = lax.broadcasted_iota(jnp.int32, (tm, ne), 1)
    lanef = lane.astype(F32)
    grp = lane // gsz
    ninf = -jnp.inf
    far = float(ne)

    def first_max(x):
        m = jnp.max(x, axis=-1, keepdims=True)
        at = jnp.min(jnp.where(x == m, lanef, far), axis=-1, keepdims=True)
        return m, at

    gs = []
    for g in range(N_EXPERT_GROUPS):
        xg = jnp.where(grp == g, sel, ninf)
        m1, a1 = first_max(xg)
        m2 = jnp.max(jnp.where(lanef == a1, ninf, xg), axis=-1, keepdims=True)
        gs.append(m1 + m2)
    allowed = jnp.zeros((tm, ne), jnp.bool_)
    for g in range(N_EXPERT_GROUPS):
        ahead = jnp.zeros((tm, 1), F32)
        for o in range(N_EXPERT_GROUPS):
            if o == g:
                continue
            beats = (gs[o] > gs[g]) | ((gs[o] == gs[g]) & (o < g))
            ahead = ahead + beats.astype(F32)
        allowed = allowed | ((grp == g) & (ahead < TOPK_GROUPS))
    masked = jnp.where(allowed, sel, ninf)
    lane_o = lax.broadcasted_iota(jnp.int32, (tm, LANES), 1)
    idx_t = jnp.zeros((tm, LANES), F32)
    wt_t = jnp.zeros((tm, LANES), F32)
    wsum = jnp.zeros((tm, 1), F32)
    picked = jnp.zeros((tm, ne), F32)
    ats = []
    for k in range(TOP_K):
        _, at = first_max(masked)
        hit = lanef == at
        wk = jnp.sum(jnp.where(hit, scores, 0.0), axis=-1, keepdims=True)
        masked = jnp.where(hit, ninf, masked)
        picked = picked + hit.astype(F32)
        ats.append(at)
        idx_t = jnp.where(lane_o == k, at, idx_t)
        wt_t = jnp.where(lane_o == k, wk, wt_t)
        wsum = wsum + wk
    idx_ref[...] = idx_t.astype(jnp.int32)
    wt_ref[...] = wt_t / wsum * ROUTED_SCALE
    before = jnp.dot(tri_ref[...], picked.astype(BF16), preferred_element_type=F32) + carry[...]
    rank_t = jnp.zeros((tm, LANES), F32)
    for k in range(TOP_K):
        rk = jnp.sum(jnp.where(lanef == ats[k], before, 0.0), axis=-1, keepdims=True)
        rank_t = jnp.where(lane_o == k, rk, rank_t)
    rank_ref[...] = rank_t.astype(jnp.int32)
    carry[...] = carry[...] + jnp.sum(picked, axis=0, keepdims=True)
    cnt_ref[...] = carry[...].astype(jnp.int32)


def _router_call(h2, rw, rb):
    nt = h2.shape[0] // ROW_SLABS
    tm = ROW_TILE
    d, ne = rw.shape
    tri = jnp.asarray(np.tril(np.ones((tm, tm), np.float32), -1), BF16)
    tile = pl.BlockSpec((tm, LANES), lambda i: (i, 0))
    return pl.pallas_call(
        _router_body,
        grid=(nt // tm,),
        in_specs=[pl.BlockSpec((tm * ROW_SLABS, LANES), lambda i: (i, 0)),
                  pl.BlockSpec((d, ne), lambda i: (0, 0)),
                  pl.BlockSpec((1, ne), lambda i: (0, 0)),
                  pl.BlockSpec((tm, tm), lambda i: (0, 0))],
        out_specs=[tile, tile, tile, pl.BlockSpec((1, ne), lambda i: (0, 0))],
        out_shape=[jax.ShapeDtypeStruct((nt, LANES), jnp.int32),
                   jax.ShapeDtypeStruct((nt, LANES), F32),
                   jax.ShapeDtypeStruct((nt, LANES), jnp.int32),
                   jax.ShapeDtypeStruct((1, ne), jnp.int32)],
        scratch_shapes=[pltpu.VMEM((1, ne), F32)],
        compiler_params=_params(("arbitrary",)),
        name="router",
    )(h2, rw, rb.reshape(1, ne), tri)


def _slot_body(idx_ref, rank_ref, offs_ref, pos_ref):
    tm = idx_ref.shape[0]
    ne = offs_ref.shape[1]
    idx = idx_ref[...]
    rank = rank_ref[...]
    lane = lax.broadcasted_iota(jnp.int32, (tm, ne), 1)
    lane_o = lax.broadcasted_iota(jnp.int32, (tm, LANES), 1)
    offs = offs_ref[...]
    pos = jnp.zeros((tm, LANES), F32)
    for k in range(TOP_K):
        start = jnp.sum(jnp.where(lane == idx[:, k:k + 1], offs, 0.0), axis=-1, keepdims=True)
        pos = jnp.where(lane_o == k, start, pos)
    pos_ref[...] = pos.astype(jnp.int32) + rank


def _slot_call(idx, rank, offs):
    nt = idx.shape[0]
    tm = ROW_TILE
    ne = offs.shape[1]
    tile = pl.BlockSpec((tm, LANES), lambda i: (i, 0))
    return pl.pallas_call(
        _slot_body,
        grid=(nt // tm,),
        in_specs=[tile, tile, pl.BlockSpec((1, ne), lambda i: (0, 0))],
        out_specs=tile,
        out_shape=jax.ShapeDtypeStruct((nt, LANES), jnp.int32),
        compiler_params=_params(("arbitrary",)),
        name="slots",
    )(idx, rank, offs)


def _row_tile(ref, r):
    return ref.at[pl.ds(pl.multiple_of(r * ROW_SLABS, ROW_SLABS), ROW_SLABS), :]


def _dispatch_body(pos_ref, h_ref, xs_hbm, sem):
    td = h_ref.shape[0] // ROW_SLABS

    def one(t, carry):
        for k in range(TOP_K):
            pltpu.make_async_copy(_row_tile(h_ref, t), _row_tile(xs_hbm, pos_ref[0, 0, t * TOP_K + k]),
                                  sem).start(priority=k % 2)
        return carry
    lax.fori_loop(0, td, one, 0, unroll=2)
    for k in range(TOP_K):
        pltpu.make_async_copy(h_ref, xs_hbm.at[pl.ds(0, td * ROW_SLABS), :], sem).wait()


def _dispatch_call(pos_tok, h2):
    nt = h2.shape[0] // ROW_SLABS
    td = ROW_TILE
    return pl.pallas_call(
        _dispatch_body,
        grid=(nt // td,),
        in_specs=[pl.BlockSpec((1, 1, td * TOP_K), lambda i: (i, 0, 0), memory_space=pltpu.SMEM),
                  pl.BlockSpec((td * ROW_SLABS, LANES), lambda i: (i, 0))],
        out_specs=pl.BlockSpec(memory_space=pl.ANY),
        out_shape=jax.ShapeDtypeStruct((nt * TOP_K * ROW_SLABS, LANES), F32),
        scratch_shapes=[pltpu.SemaphoreType.DMA],
        compiler_params=_params(("arbitrary",)),
        name="dispatch",
    )(pos_tok, h2)


def _expert_body(tile_ref, exp_ref, lo_ref, hi_ref, xs_ref, wg_ref, wu_ref, wd_ref, y_ref, wgb, wub, wdb):
    s = pl.program_id(0)
    prev = jnp.maximum(s - 1, 0)
    first_visit = jnp.logical_or(s == 0, tile_ref[s] != tile_ref[prev])
    new_expert = jnp.logical_or(s == 0, exp_ref[s] != exp_ref[prev])
    lo = lo_ref[s]
    hi = hi_ref[s]

    @pl.when(new_expert)
    def _():
        wgb[...] = wg_ref[0].astype(BF16)
        wub[...] = wu_ref[0].astype(BF16)
        wdb[...] = wd_ref[0].astype(BF16)

    @pl.when(hi > lo)
    def _():
        te = xs_ref.shape[0] // ROW_SLABS
        x = _load_rows(xs_ref, te).astype(BF16)
        gt = jnp.dot(x, wgb[...], preferred_element_type=F32)
        up = jnp.dot(x, wub[...], preferred_element_type=F32)
        act = (gt * _sigmoid(gt)) * up
        y = jnp.dot(act.astype(BF16), wdb[...], preferred_element_type=F32)
        row = lax.broadcasted_iota(jnp.int32, y.shape, 0)
        mine = jnp.logical_and(row >= lo, row < hi)

        @pl.when(first_visit)
        def _():
            _store_rows(y_ref, jnp.where(mine, y, 0.0))

        @pl.when(jnp.logical_not(first_visit))
        def _():
            _store_rows(y_ref, jnp.where(mine, y, _load_rows(y_ref, te)))


def _expert_call(visits, xs, wg, wu, wd):
    tile_id, exp_id, lo, hi = visits
    n_steps = tile_id.shape[0]
    te = EXPERT_TILE
    d, hid = wg.shape[-2:]
    rows = pl.BlockSpec((te * ROW_SLABS, LANES), lambda s, t, e, lo, hi: (t[s], 0))
    grid_spec = pltpu.PrefetchScalarGridSpec(
        num_scalar_prefetch=4,
        grid=(n_steps,),
        in_specs=[
            rows,
            pl.BlockSpec((1, d, hid), lambda s, t, e, lo, hi: (e[s], 0, 0)),
            pl.BlockSpec((1, d, hid), lambda s, t, e, lo, hi: (e[s], 0, 0)),
            pl.BlockSpec((1, hid, d), lambda s, t, e, lo, hi: (e[s], 0, 0)),
        ],
        out_specs=rows,
        scratch_shapes=[pltpu.VMEM((d, hid), BF16), pltpu.VMEM((d, hid), BF16), pltpu.VMEM((hid, d), BF16)],
    )
    return pl.pallas_call(
        _expert_body,
        grid_spec=grid_spec,
        out_shape=jax.ShapeDtypeStruct(xs.shape, F32),
        compiler_params=_params(("arbitrary",)),
        name="expert_ffn",
    )(tile_id, exp_id, lo, hi, xs, wg, wu, wd)


def _gather_rows(idx_ref, src_hbm, dst, sem, n):
    def pair(i, carry):
        for p in range(2):
            j = 2 * i + p
            pltpu.make_async_copy(_row_tile(src_hbm, idx_ref[0, 0, j]), _row_tile(dst, j), sem).start(priority=p)
        return carry
    lax.fori_loop(0, n // 2, pair, 0, unroll=4)


def _combine_body(pos_ref, posn_ref, y_hbm, wt_ref, h_ref, x1_ref, gate_ref, sg_ref, su_ref, sd_ref,
                  out_ref, ybuf, sem):
    nb, ni = pl.num_programs(0), pl.num_programs(1)
    step = pl.program_id(0) * ni + pl.program_id(1)
    slot = step % 2
    tc = h_ref.shape[0] // ROW_SLABS
    rows = tc * TOP_K

    @pl.when(step == 0)
    def _():
        _gather_rows(pos_ref, y_hbm, ybuf.at[0], sem.at[0], rows)

    @pl.when(step + 1 < nb * ni)
    def _():
        _gather_rows(posn_ref, y_hbm, ybuf.at[1 - slot], sem.at[1 - slot], rows)

    pltpu.make_async_copy(y_hbm.at[pl.ds(0, rows * ROW_SLABS), :], ybuf.at[slot], sem.at[slot]).wait()
    wt = wt_ref[...]
    routed = jnp.zeros(out_ref.shape[1:], F32)
    for k in range(TOP_K):
        routed = routed + wt[:, k:k + 1] * _load_rows(ybuf.at[slot], tc, first=k * tc)
    hb = _load_rows(h_ref, tc).astype(BF16)
    gt = jnp.dot(hb, sg_ref[...], preferred_element_type=F32)
    up = jnp.dot(hb, su_ref[...], preferred_element_type=F32)
    shared = jnp.dot(((gt * _sigmoid(gt)) * up).astype(BF16), sd_ref[...], preferred_element_type=F32)
    out_ref[0] = x1_ref[0] + gate_ref[0] * (routed + shared)


def _combine_call(pos_tiles, tile_off, y_sorted, wts, h2, x1, gate, sg, su, sd):
    bsz, s, d = x1.shape
    tc = min(COMBINE_TILE, s)
    ni = s // tc
    n_tiles = pos_tiles.shape[0]
    lin = lambda b, i: tile_off + b * ni + i
    const = lambda t: pl.BlockSpec(t.shape, lambda b, i: (0,) * t.ndim)
    return pl.pallas_call(
        _combine_body,
        grid=(bsz, ni),
        in_specs=[
            pl.BlockSpec((1, 1, tc * TOP_K), lambda b, i: (lin(b, i), 0, 0), memory_space=pltpu.SMEM),
            pl.BlockSpec((1, 1, tc * TOP_K), lambda b, i: (jnp.minimum(lin(b, i) + 1, n_tiles - 1), 0, 0),
                         memory_space=pltpu.SMEM),
            pl.BlockSpec(memory_space=pl.ANY),
            pl.BlockSpec((tc, LANES), lambda b, i: (lin(b, i), 0)),
            pl.BlockSpec((tc * ROW_SLABS, LANES), lambda b, i: (lin(b, i), 0)),
            pl.BlockSpec((1, tc, d), lambda b, i: (b, i, 0)),
            _mod_spec(gate, tc, d),
            const(sg), const(su), const(sd),
        ],
        out_specs=pl.BlockSpec((1, tc, d), lambda b, i: (b, i, 0)),
        out_shape=jax.ShapeDtypeStruct((bsz, s, d), F32),
        scratch_shapes=[pltpu.VMEM((2, tc * TOP_K * ROW_SLABS, LANES), F32), pltpu.SemaphoreType.DMA((2,))],
        compiler_params=_params(("arbitrary", "arbitrary")),
        name="combine",
    )(pos_tiles, pos_tiles, y_sorted, wts, h2, x1, gate, sg, su, sd)


def _t5_bucket(dist):
    exact = N_BUCKETS // 2
    d = np.asarray(dist)
    large = exact + (np.log(np.maximum(d, 1) / exact) / np.log(REL_MAX_DIST / exact) * (N_BUCKETS - exact)).astype(np.int32)
    large = np.minimum(large, N_BUCKETS - 1)
    return np.where(d < exact, d, large).astype(np.int32)


def _group_bias(rel_bias, gi):
    window, dil = ATTN_GROUPS[gi]
    bkt = _t5_bucket(np.arange(window // dil + 1) * dil)
    onehot = jnp.asarray(bkt[:, None] == np.arange(N_BUCKETS)[None, :], F32)
    cols = rel_bias[:, gi * HEADS:(gi + 1) * HEADS].astype(F32)
    return jnp.dot(onehot, cols, precision=HIGHEST).T


def _prompt_bias_table(bias_j):
    blk = ATTN_BLOCK
    n = bias_j.shape[1] - 1
    period = 3 * blk
    row0 = jnp.concatenate([jnp.flip(bias_j, axis=1), jnp.full((HEADS, period - n - 1), NEG, F32)], axis=1)
    flat = jnp.tile(row0, (1, blk))[:, :blk * (period - 1)]
    tab = flat.reshape(HEADS, blk, period - 1)[:, :, :2 * blk]
    return tab.reshape(HEADS // HEADS_PER_SLAB, HEADS_PER_SLAB * blk, 2 * blk)


def _decode_bias_tables(bias_j):
    old = jnp.flip(bias_j[:, 1:], axis=1).T[:, :, None]
    new = bias_j[:, 0][:, None]
    return old, new


def _visit_schedule(counts, na):
    te = EXPERT_TILE
    n_tiles = na // te
    n_steps = n_tiles + N_EXPERTS - 1
    ends = jnp.cumsum(counts)
    offs = ends - counts
    first_tile = offs // te
    n_vis = jnp.where(counts > 0, (ends - 1) // te - first_tile + 1, 0)
    v_end = jnp.cumsum(n_vis)
    v_start = v_end - n_vis
    step = jnp.arange(n_steps, dtype=jnp.int32)
    exp_id = jnp.minimum(jnp.sum(v_end[None, :] <= step[:, None], axis=1), N_EXPERTS - 1).astype(jnp.int32)
    live = step < v_end[-1]
    tile_id = jnp.where(live, first_tile[exp_id] + step - v_start[exp_id], n_tiles - 1).astype(jnp.int32)
    lo = jnp.where(live, jnp.clip(offs[exp_id] - tile_id * te, 0, te), 0).astype(jnp.int32)
    hi = jnp.where(live, jnp.clip(ends[exp_id] - tile_id * te, 0, te), 0).astype(jnp.int32)
    return offs, (tile_id, exp_id, lo, hi)


def kernel(x_prompt, x_sample, cache_k_w128, cache_v_w128, cache_k_w512, cache_v_w512, cache_k_w2048, cache_v_w2048, state_pool, c_prompt, c_sample, ada_w, ada_b, norm1, norm2, w_in, q_gain, k_gain, rel_bias, pool_w, pool_scale, w_br_a, w_br_b, w_out, router_w, router_bias, exp_w_gate, exp_w_up, exp_w_down, sh_w_gate, sh_w_up, sh_w_down):
    depth = ada_w.shape[0]
    bsz, seq, d = x_prompt.shape
    db = x_sample.shape[0]
    assert x_sample.shape[1] == 1 and db <= SAMPLE_PAD and seq % (ATTN_BLOCK * ATTN_GROUPS[-1][1]) == 0
    k_caches = (cache_k_w128, cache_k_w512, cache_k_w2048)
    v_caches = (cache_v_w128, cache_v_w512, cache_v_w2048)
    for c, (w, dil) in zip(k_caches, ATTN_GROUPS):
        assert c.shape[2] == w == ATTN_STEPS * dil

    bias_js = [_group_bias(rel_bias, g) for g in range(N_GROUPS)]
    bias_tabs = jnp.stack([_prompt_bias_table(b) for b in bias_js])
    dec = [_decode_bias_tables(b) for b in bias_js]
    bdec = jnp.stack([t[0] for t in dec])
    bnew = jnp.stack([t[1] for t in dec])
    bd = jnp.asarray((np.arange(MXU_DIM)[:, None] // HEAD_DIM == np.arange(MXU_DIM)[None, :] // HEAD_DIM)
                     / HEAD_DIM, BF16)
    wcols = np.repeat(np.asarray(POOL_WINDOWS), POOL_GROUP)
    pw_dec = jnp.asarray(np.where(np.arange(POOL_STATE + 1)[:, None] >= POOL_STATE + 1 - wcols[None, :],
                                  1.0 / wcols[None, :], 0.0), F32)

    xp = x_prompt
    xs = jnp.pad(x_sample.reshape(db, d), ((0, SAMPLE_PAD - db), (0, 0))).reshape(1, SAMPLE_PAD, d)
    n_prompt = bsz * seq
    outs = {name: [] for name in ('pk', 'pv', 'sk', 'sv')}
    outs['pk'] = [[] for _ in range(N_GROUPS)]
    outs['pv'] = [[] for _ in range(N_GROUPS)]
    outs['sk'] = [[] for _ in range(N_GROUPS)]
    outs['sv'] = [[] for _ in range(N_GROUPS)]
    ppool, spool = [], []
    for l in range(depth):
        c_all = jnp.concatenate([c_prompt, c_sample], axis=0)
        mod = _ada_call(c_all, ada_w[l], ada_b[l])
        mod_p = [m.reshape(bsz, 1, d) for m in jnp.split(mod[:bsz], 6, axis=-1)]
        mod_s = [jnp.pad(m, ((0, SAMPLE_PAD - db), (0, 0))).reshape(1, SAMPLE_PAD, d)
                 for m in jnp.split(mod[bsz:], 6, axis=-1)]
        w_in_bf = w_in[l].astype(BF16)
        qg = (jnp.tile(q_gain[l], (1, HEADS)) * (HEAD_DIM ** -0.5)).reshape(1, -1)
        kg = jnp.tile(k_gain[l], (1, HEADS)).reshape(1, -1)
        g1 = norm1[l].reshape(1, d)
        g2 = norm2[l].reshape(1, d)
        wa, wb, wo = (w_br_a[l].astype(BF16), w_br_b[l].astype(BF16), w_out[l].astype(BF16))
        pw = pool_w[l].astype(BF16)
        ps = pool_scale[l].reshape(1, -1)

        q, k, v, u, gates = _inproj_call(xp, mod_p[0], mod_p[1], g1, w_in_bf, qg, kg, bd)
        oa_p = _attn_call(q, k, v, bias_tabs)
        for g, (window, dil) in enumerate(ATTN_GROUPS):
            keep = min(window, seq)
            cs = slice(g * GROUP_W, (g + 1) * GROUP_W)
            outs['pk'][g].append(k[:, seq - keep:, cs].reshape(bsz, keep, HEADS, HEAD_DIM))
            outs['pv'][g].append(v[:, seq - keep:, cs].reshape(bsz, keep, HEADS, HEAD_DIM))
        ppool.append(u[:, seq - POOL_STATE:])
        x1p, h2p = _merge_call(xp, gates, (oa_p, u), (mod_p[2], mod_p[3], mod_p[4]),
                               g2, wa, wb, wo, pw, ps, prompt=True)

        qs, ks, vs, us, gates_s = _inproj_call(xs, mod_s[0], mod_s[1], g1, w_in_bf, qg, kg, bd)
        heads3 = lambda t: t[0, :db].reshape(db, N_GROUPS * HEADS, HEAD_DIM)
        q3, k3, v3 = heads3(qs), heads3(ks), heads3(vs)
        kcl = [c[l] for c in k_caches]
        vcl = [c[l] for c in v_caches]
        oa_s, pooled_s = _decode_call(q3, k3, v3, us[0, :db], state_pool[l], kcl, vcl, bdec, bnew, pw_dec)
        padrows = lambda t: jnp.pad(t.reshape(db, GROUP_W), ((0, SAMPLE_PAD - db), (0, 0))).reshape(1, SAMPLE_PAD, GROUP_W)
        x1s, h2s = _merge_call(xs, gates_s, (padrows(oa_s), padrows(pooled_s)), (mod_s[2], mod_s[3], mod_s[4]),
                               g2, wa, wb, wo, pw, ps, prompt=False)
        for g in range(N_GROUPS):
            hs = slice(g * HEADS, (g + 1) * HEADS)
            outs['sk'][g].append(_roll_call(kcl[g], k3[:, None, hs]))
            outs['sv'][g].append(_roll_call(vcl[g], v3[:, None, hs]))
        spool.append(jnp.concatenate([state_pool[l][:, 1:], us[0, :db, None, :]], axis=1))

        h2 = jnp.concatenate([h2p.reshape(n_prompt * ROW_SLABS, LANES), h2s.reshape(SAMPLE_PAD * ROW_SLABS, LANES)], axis=0)
        nt = n_prompt + SAMPLE_PAD
        idx, wts, rank, counts = _router_call(h2, router_w[l], router_bias[l])
        offs, visits = _visit_schedule(counts[0], nt * TOP_K)
        pos = _slot_call(idx, rank, offs.astype(F32).reshape(1, -1))[:, :TOP_K]
        xs_sorted = _dispatch_call(pos.reshape(nt // ROW_TILE, 1, ROW_TILE * TOP_K), h2)
        y_sorted = _expert_call(visits, xs_sorted, exp_w_gate[l], exp_w_up[l], exp_w_down[l])
        tc = COMBINE_TILE
        pos_tiles = pos.reshape(nt // tc, tc, TOP_K).transpose(0, 2, 1).reshape(nt // tc, 1, TOP_K * tc)
        sg, su, sd = sh_w_gate[l].astype(BF16), sh_w_up[l].astype(BF16), sh_w_down[l].astype(BF16)
        xp = _combine_call(pos_tiles, 0, y_sorted, wts, h2, x1p, mod_p[5], sg, su, sd)
        xs = _combine_call(pos_tiles, n_prompt // tc, y_sorted, wts, h2, x1s, mod_s[5], sg, su, sd)

    stack = lambda parts: jnp.stack(parts)
    res = [xp, xs[0, :db].reshape(db, 1, d)]
    for g in range(N_GROUPS):
        res += [stack(outs['pk'][g]), stack(outs['pv'][g])]
    res.append(stack(ppool))
    for g in range(N_GROUPS):
        res += [stack(outs['sk'][g]), stack(outs['sv'][g])]
    res.append(stack(spool))
    return tuple(res)
```

```python
import functools

import numpy as np
import jax
import jax.numpy as jnp
from jax import lax
from jax.experimental import pallas as pl
from jax.experimental.pallas import tpu as pltpu

F32 = jnp.float32
BF16 = jnp.bfloat16
HIGHEST = lax.Precision.HIGHEST

HEAD_DIM = 64
HEADS = 8
GROUP_W = HEADS * HEAD_DIM
ATTN_GROUPS = ((128, 1), (512, 4), (2048, 16))
N_GROUPS = len(ATTN_GROUPS)
ATTN_STEPS = 128
ATTN_BLOCK = 128
POOL_WINDOWS = (2, 4, 8, 16)
POOL_GROUP = 128
POOL_STATE = 15
N_BUCKETS = 32
REL_MAX_DIST = 2048
N_EXPERTS = 256
TOP_K = 8
N_EXPERT_GROUPS = 8
TOPK_GROUPS = 4
ROUTED_SCALE = 2.5
EPS = 1e-6
NEG = -1e30

LANES = 128
MXU_DIM = 256
VMEM_LIMIT = 56 * 1024 * 1024

INPROJ_TILE = 512
ROW_TILE = 256
EXPERT_TILE = 256
COMBINE_TILE = 128
SAMPLE_PAD = 256


def _sigmoid(x):
    return 1.0 / (1.0 + jnp.exp(-x))


def _params(sem):
    return pltpu.CompilerParams(dimension_semantics=sem, vmem_limit_bytes=VMEM_LIMIT)


ROW_SLABS = 8


def _load_rows(ref, n, first=0):
    return jnp.concatenate([ref[pl.ds(first * ROW_SLABS + s, n, stride=ROW_SLABS), :]
                            for s in range(ROW_SLABS)], axis=1)


def _store_rows(ref, val):
    n = val.shape[0]
    for s in range(ROW_SLABS):
        ref[pl.ds(s, n, stride=ROW_SLABS), :] = val[:, s * LANES:(s + 1) * LANES]


def _ada_body(c_ref, w_ref, b_ref, o_ref):
    c = c_ref[...]
    s = c * _sigmoid(c)
    o_ref[...] = jnp.dot(s, w_ref[...], precision=HIGHEST, preferred_element_type=F32) + b_ref[...]


def _ada_call(c, w, b):
    n, d = c.shape
    cols = w.shape[1]
    tn = 512
    return pl.pallas_call(
        _ada_body,
        grid=(cols // tn,),
        in_specs=[pl.BlockSpec((n, d), lambda j: (0, 0)),
                  pl.BlockSpec((d, tn), lambda j: (0, j)),
                  pl.BlockSpec((1, tn), lambda j: (0, j))],
        out_specs=pl.BlockSpec((n, tn), lambda j: (0, j)),
        out_shape=jax.ShapeDtypeStruct((n, cols), F32),
        compiler_params=_params(("arbitrary",)),
        name="adaln",
    )(c, w, b.reshape(1, cols))


def _inproj_body(x_ref, shift_ref, scale_ref, g_ref, w_ref, qg_ref, kg_ref, bd_ref,
                 q_ref, k_ref, v_ref, u_ref, gt_ref):
    x = x_ref[0]
    ms = jnp.mean(x * x, axis=-1, keepdims=True)
    h = x * lax.rsqrt(ms + EPS) * g_ref[...]
    h = h * (1.0 + scale_ref[0]) + shift_ref[0]
    hb = h.astype(BF16)
    qkv = N_GROUPS * GROUP_W
    n_chunks = w_ref.shape[1] // GROUP_W
    for c in range(n_chunks):
        z = jnp.dot(hb, w_ref[:, c * GROUP_W:(c + 1) * GROUP_W], preferred_element_type=F32)
        if c < 2 * N_GROUPS:
            zz = (z * z).astype(BF16)
            hms = jnp.concatenate(
                [jnp.dot(zz[:, s:s + MXU_DIM], bd_ref[...], preferred_element_type=F32)
                 for s in range(0, GROUP_W, MXU_DIM)], axis=1)
            g = c % N_GROUPS
            gain = (qg_ref if c < N_GROUPS else kg_ref)[:, g * GROUP_W:(g + 1) * GROUP_W]
            y = z * lax.rsqrt(hms + EPS) * gain
            if c < N_GROUPS:
                q_ref[0, :, g * GROUP_W:(g + 1) * GROUP_W] = y
            else:
                k_ref[0, :, g * GROUP_W:(g + 1) * GROUP_W] = y
        elif c < 3 * N_GROUPS:
            g = c - 2 * N_GROUPS
            v_ref[0, :, g * GROUP_W:(g + 1) * GROUP_W] = z
        elif c == 3 * N_GROUPS:
            u_ref[0] = z
        else:
            g = c - 3 * N_GROUPS - 1
            gt_ref[0, :, g * GROUP_W:(g + 1) * GROUP_W] = z.astype(BF16)
    del qkv


def _mod_spec(mod, tm, d):
    if mod.shape[1] == 1:
        return pl.BlockSpec((1, 1, d), lambda b, i: (b, 0, 0))
    return pl.BlockSpec((1, tm, d), lambda b, i: (b, i, 0))


def _inproj_call(x, shift, scale, g1, w_bf, qg, kg, bd):
    bsz, s, d = x.shape
    tm = min(INPROJ_TILE, s)
    qkv = N_GROUPS * GROUP_W
    n_gate = w_bf.shape[1] - 3 * qkv - GROUP_W
    const = lambda b, i: (0, 0)
    row = lambda width: pl.BlockSpec((1, tm, width), lambda b, i: (b, i, 0))
    return pl.pallas_call(
        _inproj_body,
        grid=(bsz, s // tm),
        in_specs=[row(d), _mod_spec(shift, tm, d), _mod_spec(scale, tm, d),
                  pl.BlockSpec((1, d), const),
                  pl.BlockSpec(w_bf.shape, const, pipeline_mode=pl.Buffered(1)),
                  pl.BlockSpec((1, qkv), const), pl.BlockSpec((1, qkv), const),
                  pl.BlockSpec((MXU_DIM, MXU_DIM), const)],
        out_specs=[row(qkv), row(qkv), row(qkv), row(GROUP_W), row(n_gate)],
        out_shape=[jax.ShapeDtypeStruct((bsz, s, qkv), F32),
                   jax.ShapeDtypeStruct((bsz, s, qkv), F32),
                   jax.ShapeDtypeStruct((bsz, s, qkv), F32),
                   jax.ShapeDtypeStruct((bsz, s, GROUP_W), F32),
                   jax.ShapeDtypeStruct((bsz, s, n_gate), BF16)],
        compiler_params=_params(("arbitrary", "arbitrary")),
        name="inproj",
    )(x, shift, scale, g1, w_bf, qg, kg, bd)


ATTN_ROWS = ATTN_BLOCK * ATTN_GROUPS[-1][1]
HEADS_PER_SLAB = LANES // HEAD_DIM


def _attn_body(*refs):
    ng = N_GROUPS
    qkv = refs[:3 * ng]
    bias_ref, o_ref = refs[3 * ng], refs[3 * ng + 1]
    scr = refs[3 * ng + 2:]
    first = pl.program_id(2) == 0
    blk = ATTN_BLOCK
    pb = ATTN_ROWS
    hs = HEADS_PER_SLAB
    lane = lax.broadcasted_iota(jnp.int32, (blk, LANES), 1)
    col = lax.broadcasted_iota(jnp.int32, (hs * blk, 2 * blk), 1)
    for g, (_, dil) in enumerate(ATTN_GROUPS):
        q_ref, k_ref, v_ref = qkv[3 * g:3 * g + 3]
        kext, vext, og, lg = scr[4 * g:4 * g + 4]
        span = blk * dil
        bias = bias_ref[g, 0]

        @pl.when(first)
        def _():
            kext[0:span, :] = jnp.zeros((span, LANES), F32)
            vext[0:span, :] = jnp.zeros((span, LANES), F32)
        kext[span:span + pb, :] = k_ref[0]
        vext[span:span + pb, :] = v_ref[0]

        def unit(u, carry):
            qb = u // dil
            base = qb * span + (u % dil)
            rows = lambda start: pl.ds(start, blk, stride=dil) if dil > 1 else pl.ds(pl.multiple_of(start, blk), blk)
            q = q_ref[0, rows(base), :]
            k2 = jnp.concatenate([kext[rows(base), :], kext[rows(base + span), :]], axis=0).astype(BF16)
            v2 = jnp.concatenate([vext[rows(base), :], vext[rows(base + span), :]], axis=0).astype(BF16)
            qs = jnp.concatenate([jnp.where(lane // HEAD_DIM == h, q, 0.0) for h in range(hs)], axis=0).astype(BF16)
            s = lax.dot_general(qs, k2, (((1,), (1,)), ((), ())), preferred_element_type=F32) + bias
            s = jnp.where(jnp.logical_and(jnp.logical_and(first, qb == 0), col < blk), NEG, s)
            m = jnp.max(s, axis=-1, keepdims=True)
            p = jnp.exp(s - m)
            l = jnp.sum(p, axis=-1, keepdims=True)
            o2 = jnp.dot(p.astype(BF16), v2, preferred_element_type=F32) / l
            lse = m + jnp.log(l)
            o = o2[0:blk]
            lw = jnp.broadcast_to(lse[0:blk], (blk, LANES))
            for h in range(1, hs):
                o = jnp.where(lane // HEAD_DIM == h, o2[h * blk:(h + 1) * blk], o)
                lw = jnp.where(lane // HEAD_DIM == h, lse[h * blk:(h + 1) * blk], lw)
            og[rows(base), :] = o
            lg[rows(base), :] = lw
            return carry
        lax.fori_loop(0, pb // blk, unit, 0, unroll=2)
        kext[0:span, :] = kext[pb:pb + span, :]
        vext[0:span, :] = vext[pb:pb + span, :]

    def merge(c, carry):
        rs = pl.ds(pl.multiple_of(c * blk, blk), blk)
        ls = [scr[4 * g + 3][rs, :] for g in range(ng)]
        mx = jnp.maximum(jnp.maximum(ls[0], ls[1]), ls[2])
        es = [jnp.exp(t - mx) for t in ls]
        den = es[0] + es[1] + es[2]
        acc = (es[0] / den) * scr[2][rs, :]
        for g in range(1, ng):
            acc = acc + (es[g] / den) * scr[4 * g + 2][rs, :]
        o_ref[0, rs, :] = acc.astype(o_ref.dtype)
        return carry
    lax.fori_loop(0, pb // blk, merge, 0)


def _attn_call(q, k, v, bias_tabs):
    bsz, s, qkv = q.shape
    pb = ATTN_ROWS
    n_slab = GROUP_W // LANES
    in_specs, args = [], []
    for g in range(N_GROUPS):
        spec = pl.BlockSpec((1, pb, LANES), lambda b, c, i, g=g: (b, i, g * n_slab + c))
        in_specs += [spec, spec, spec]
        args += [q, k, v]
    in_specs.append(pl.BlockSpec((N_GROUPS, 1) + bias_tabs.shape[2:], lambda b, c, i: (0, c, 0, 0)))
    scratch = []
    for _, dil in ATTN_GROUPS:
        ext = pltpu.VMEM((ATTN_BLOCK * dil + pb, LANES), F32)
        scratch += [ext, ext, pltpu.VMEM((pb, LANES), F32), pltpu.VMEM((pb, LANES), F32)]
    return pl.pallas_call(
        _attn_body,
        grid=(bsz, n_slab, s // pb),
        in_specs=in_specs,
        out_specs=pl.BlockSpec((1, pb, LANES), lambda b, c, i: (b, i, c)),
        out_shape=jax.ShapeDtypeStruct((bsz, s, GROUP_W), BF16),
        scratch_shapes=scratch,
        compiler_params=_params(("arbitrary", "arbitrary", "arbitrary")),
        name="attn",
    )(*args, bias_tabs)


def _decode_body(q_ref, kn_ref, vn_ref, u_ref, pool_ref, kc0, vc0, kc1, vc1, kc2, vc2,
                 bdec_ref, bnew_ref, pw_ref, oa_ref, pooled_ref):
    kcs = (kc0, kc1, kc2)
    vcs = (vc0, vc1, vc2)
    outs, lses = [], []
    for g in range(N_GROUPS):
        hs = slice(g * HEADS, (g + 1) * HEADS)
        q = q_ref[0, hs, :]
        kn = kn_ref[0, hs, :]
        vn = vn_ref[0, hs, :]
        kc = kcs[g][0]
        vc = vcs[g][0]
        s_old = jnp.sum(kc * q[None], axis=-1, keepdims=True) + bdec_ref[g]
        s_new = jnp.sum(kn * q, axis=-1, keepdims=True) + bnew_ref[g]
        m = jnp.maximum(jnp.max(s_old, axis=0), s_new)
        p_old = jnp.exp(s_old - m[None])
        p_new = jnp.exp(s_new - m)
        l = jnp.sum(p_old, axis=0) + p_new
        outs.append((jnp.sum(p_old * vc, axis=0) + p_new * vn) / l)
        lses.append(m + jnp.log(l))
    mx = jnp.maximum(jnp.maximum(lses[0], lses[1]), lses[2])
    es = [jnp.exp(t - mx) for t in lses]
    den = es[0] + es[1] + es[2]
    oa_ref[0] = (es[0] / den) * outs[0] + (es[1] / den) * outs[1] + (es[2] / den) * outs[2]
    u = u_ref[0]
    stored = jnp.sum(pool_ref[0] * pw_ref[0:POOL_STATE, :], axis=0, keepdims=True)
    pooled_ref[0] = stored + u * pw_ref[POOL_STATE:POOL_STATE + 1, :] - u


def _decode_call(q, kn, vn, u, pool_prev, k_caches, v_caches, bdec, bnew, pw):
    db = pool_prev.shape[0]
    per_b = lambda t: pl.BlockSpec((1,) + t.shape[1:], lambda b: (b,) + (0,) * (t.ndim - 1))
    const = lambda t: pl.BlockSpec(t.shape, lambda b: (0,) * t.ndim)
    cache_specs, cache_args = [], []
    for g, (_, dil) in enumerate(ATTN_GROUPS):
        for c in (k_caches[g], v_caches[g]):
            rows = c.shape[1]
            cache_args.append(c.reshape(db, rows // dil, dil, HEADS, HEAD_DIM))
            cache_specs.append(pl.BlockSpec((1, ATTN_STEPS, None, HEADS, HEAD_DIM), lambda b: (b, 0, 0, 0, 0)))
    u3 = u.reshape(db, 1, u.shape[-1])
    return pl.pallas_call(
        _decode_body,
        grid=(db,),
        in_specs=[per_b(q), per_b(kn), per_b(vn), per_b(u3), per_b(pool_prev)] + cache_specs +
                 [const(bdec), const(bnew), const(pw)],
        out_specs=[pl.BlockSpec((1, HEADS, HEAD_DIM), lambda b: (b, 0, 0)),
                   pl.BlockSpec((1, 1, GROUP_W), lambda b: (b, 0, 0))],
        out_shape=[jax.ShapeDtypeStruct((db, HEADS, HEAD_DIM), F32),
                   jax.ShapeDtypeStruct((db, 1, GROUP_W), F32)],
        compiler_params=_params(("arbitrary",)),
        name="decode_mix",
    )(q, kn, vn, u3, pool_prev, *cache_args, bdec, bnew, pw)


def _roll_body(c_ref, nxt_ref, new_ref, o_ref):
    ch = c_ref.shape[1]
    last = pl.program_id(1) == pl.num_programs(1) - 1
    o_ref[0, 0:ch - 1] = c_ref[0, 1:ch]
    o_ref[0, ch - 1:ch] = jnp.where(last, new_ref[0], nxt_ref[0])


def _roll_call(cache, new):
    db, rows, nh, e = cache.shape
    ch = min(rows, 1024)
    return pl.pallas_call(
        _roll_body,
        grid=(db, rows // ch),
        in_specs=[pl.BlockSpec((1, ch, nh, e), lambda b, i: (b, i, 0, 0)),
                  pl.BlockSpec((1, 1, nh, e), lambda b, i: (b, jnp.minimum((i + 1) * ch, rows - 1), 0, 0)),
                  pl.BlockSpec((1, 1, nh, e), lambda b, i: (b, 0, 0, 0))],
        out_specs=pl.BlockSpec((1, ch, nh, e), lambda b, i: (b, i, 0, 0)),
        out_shape=jax.ShapeDtypeStruct(cache.shape, cache.dtype),
        compiler_params=_params(("arbitrary", "arbitrary")),
        name="cache_roll",
    )(cache, cache, new)


def _merge_tail(x, oa, pooled, gates, gate_msa, shift2, scale2, g2, wa, wb, wo, pw_ref, ps,
                x1_ref, h2_ref):
    mixed = jnp.concatenate(
        [jnp.dot(pooled[:, j * POOL_GROUP:(j + 1) * POOL_GROUP].astype(BF16), pw_ref[j],
                 preferred_element_type=F32) for j in range(len(POOL_WINDOWS))], axis=1) * ps
    d = x.shape[-1]
    ga = gates[:, :d].astype(F32)
    gb = gates[:, d:].astype(F32)
    merged = (_sigmoid(ga) * jnp.dot(oa.astype(BF16), wa, preferred_element_type=F32) +
              _sigmoid(gb) * jnp.dot(mixed.astype(BF16), wb, preferred_element_type=F32))
    y = jnp.dot(merged.astype(BF16), wo, preferred_element_type=F32)
    x1 = x + gate_msa * y
    x1_ref[0] = x1
    ms = jnp.mean(x1 * x1, axis=-1, keepdims=True)
    h2 = x1 * lax.rsqrt(ms + EPS) * g2
    _store_rows(h2_ref.at[0], h2 * (1.0 + scale2) + shift2)


def _merge_prompt_body(x_ref, gt_ref, oa_ref, u_ref, uh_ref,
                       gm_ref, sh_ref, sc_ref, g2_ref, wa_ref, wb_ref, wo_ref, pw_ref, ps_ref,
                       x1_ref, h2_ref):
    i = pl.program_id(1)
    tm = x_ref.shape[1]
    oa = oa_ref[0]
    u = u_ref[0]
    halo = jnp.where(i == 0, 0.0, uh_ref[0])
    ext = jnp.concatenate([halo, u], axis=0)
    hw = halo.shape[0]
    acc = ext
    parts = []
    for j, w in enumerate(POOL_WINDOWS):
        acc = acc[:, POOL_GROUP * (1 if j else 0):]
        acc = acc + pltpu.roll(acc, w // 2, axis=0)
        parts.append(acc[hw:, :POOL_GROUP])
    tot = jnp.concatenate(parts, axis=1)
    lane = lax.broadcasted_iota(jnp.int32, (tm, GROUP_W), 1)
    wcol = jnp.left_shift(2, lane // POOL_GROUP)
    pos = i * tm + lax.broadcasted_iota(jnp.int32, (tm, GROUP_W), 0)
    cnt = jnp.minimum(pos + 1, wcol).astype(F32)
    pooled = tot / cnt - u
    _merge_tail(x_ref[0], oa, pooled, gt_ref[0], gm_ref[0], sh_ref[0], sc_ref[0], g2_ref[...],
                wa_ref[...], wb_ref[...], wo_ref[...], pw_ref, ps_ref[...], x1_ref, h2_ref)


def _merge_sample_body(x_ref, gt_ref, oa_ref, pooled_ref,
                       gm_ref, sh_ref, sc_ref, g2_ref, wa_ref, wb_ref, wo_ref, pw_ref, ps_ref,
                       x1_ref, h2_ref):
    _merge_tail(x_ref[0], oa_ref[0], pooled_ref[0], gt_ref[0], gm_ref[0], sh_ref[0], sc_ref[0], g2_ref[...],
                wa_ref[...], wb_ref[...], wo_ref[...], pw_ref, ps_ref[...], x1_ref, h2_ref)


def _merge_call(x, gates, mixer_inputs, mods, g2, wa, wb, wo, pw, ps, prompt):
    bsz, s, d = x.shape
    assert d == ROW_SLABS * LANES
    tm = min(ROW_TILE, s)
    row = lambda t: pl.BlockSpec((1, tm, t.shape[-1]), lambda b, i: (b, i, 0))
    const = lambda t: pl.BlockSpec(t.shape, lambda b, i: (0,) * t.ndim)
    gm, sh, sc = mods
    in_specs = [row(x), row(gates)]
    args = [x, gates]
    if prompt:
        oa, u = mixer_inputs
        hw = 16
        in_specs += [row(oa), row(u),
                     pl.BlockSpec((1, hw, u.shape[-1]), lambda b, i: (b, jnp.maximum(i * (tm // hw) - 1, 0), 0))]
        args += [oa, u, u]
        body = _merge_prompt_body
    else:
        in_specs += [row(t) for t in mixer_inputs]
        args += list(mixer_inputs)
        body = _merge_sample_body
    in_specs += [_mod_spec(m, tm, d) for m in (gm, sh, sc)]
    in_specs += [const(t) for t in (g2, wa, wb, wo, pw, ps)]
    args += [gm, sh, sc, g2, wa, wb, wo, pw, ps]
    return pl.pallas_call(
        body,
        grid=(bsz, s // tm),
        in_specs=in_specs,
        out_specs=[row(x), pl.BlockSpec((1, tm * ROW_SLABS, LANES), lambda b, i: (b, i, 0))],
        out_shape=[jax.ShapeDtypeStruct((bsz, s, d), F32),
                   jax.ShapeDtypeStruct((bsz, s * ROW_SLABS, LANES), F32)],
        compiler_params=_params(("arbitrary", "arbitrary")),
        name="merge_prompt" if prompt else "merge_sample",
    )(*args)


def _router_body(h_ref, rw_ref, rb_ref, tri_ref, idx_ref, wt_ref, rank_ref, cnt_ref, carry):
    tm = h_ref.shape[0] // ROW_SLABS

    @pl.when(pl.program_id(0) == 0)
    def _():
        carry[...] = jnp.zeros_like(carry)

    ne = rw_ref.shape[1]
    gsz = ne // N_EXPERT_GROUPS
    logits = jnp.dot(_load_rows(h_ref, tm), rw_ref[...], precision=HIGHEST, preferred_element_type=F32)
    scores = _sigmoid(logits)
    sel = scores + rb_ref[...]
    lane = lax.broadcasted_iota(jnp.int32, (tm, ne), 1)
    lanef = lane.astype(F32)
    grp = lane // gsz
    ninf = -jnp.inf
    far = float(ne)

    def first_max(x):
        m = jnp.max(x, axis=-1, keepdims=True)
        at = jnp.min(jnp.where(x == m, lanef, far), axis=-1, keepdims=True)
        return m, at

    gs = []
    for g in range(N_EXPERT_GROUPS):
        xg = jnp.where(grp == g, sel, ninf)
        m1, a1 = first_max(xg)
        m2 = jnp.max(jnp.where(lanef == a1, ninf, xg), axis=-1, keepdims=True)
        gs.append(m1 + m2)
    allowed = jnp.zeros((tm, ne), jnp.bool_)
    for g in range(N_EXPERT_GROUPS):
        ahead = jnp.zeros((tm, 1), F32)
        for o in range(N_EXPERT_GROUPS):
            if o == g:
                continue
            beats = (gs[o] > gs[g]) | ((gs[o] == gs[g]) & (o < g))
            ahead = ahead + beats.astype(F32)
        allowed = allowed | ((grp == g) & (ahead < TOPK_GROUPS))
    masked = jnp.where(allowed, sel, ninf)
    lane_o = lax.broadcasted_iota(jnp.int32, (tm, LANES), 1)
    idx_t = jnp.zeros((tm, LANES), F32)
    wt_t = jnp.zeros((tm, LANES), F32)
    wsum = jnp.zeros((tm, 1), F32)
    picked = jnp.zeros((tm, ne), F32)
    ats = []
    for k in range(TOP_K):
        _, at = first_max(masked)
        hit = lanef == at
        wk = jnp.sum(jnp.where(hit, scores, 0.0), axis=-1, keepdims=True)
        masked = jnp.where(hit, ninf, masked)
        picked = picked + hit.astype(F32)
        ats.append(at)
        idx_t = jnp.where(lane_o == k, at, idx_t)
        wt_t = jnp.where(lane_o == k, wk, wt_t)
        wsum = wsum + wk
    idx_ref[...] = idx_t.astype(jnp.int32)
    wt_ref[...] = wt_t / wsum * ROUTED_SCALE
    before = jnp.dot(tri_ref[...], picked.astype(BF16), preferred_element_type=F32) + carry[...]
    rank_t = jnp.zeros((tm, LANES), F32)
    for k in range(TOP_K):
        rk = jnp.sum(jnp.where(lanef == ats[k], before, 0.0), axis=-1, keepdims=True)
        rank_t = jnp.where(lane_o == k, rk, rank_t)
    rank_ref[...] = rank_t.astype(jnp.int32)
    carry[...] = carry[...] + jnp.sum(picked, axis=0, keepdims=True)
    cnt_ref[...] = carry[...].astype(jnp.int32)


def _router_call(h2, rw, rb):
    nt = h2.shape[0] // ROW_SLABS
    tm = ROW_TILE
    d, ne = rw.shape
    tri = jnp.asarray(np.tril(np.ones((tm, tm), np.float32), -1), BF16)
    tile = pl.BlockSpec((tm, LANES), lambda i: (i, 0))
    return pl.pallas_call(
        _router_body,
        grid=(nt // tm,),
        in_specs=[pl.BlockSpec((tm * ROW_SLABS, LANES), lambda i: (i, 0)),
                  pl.BlockSpec((d, ne), lambda i: (0, 0)),
                  pl.BlockSpec((1, ne), lambda i: (0, 0)),
                  pl.BlockSpec((tm, tm), lambda i: (0, 0))],
        out_specs=[tile, tile, tile, pl.BlockSpec((1, ne), lambda i: (0, 0))],
        out_shape=[jax.ShapeDtypeStruct((nt, LANES), jnp.int32),
                   jax.ShapeDtypeStruct((nt, LANES), F32),
                   jax.ShapeDtypeStruct((nt, LANES), jnp.int32),
                   jax.ShapeDtypeStruct((1, ne), jnp.int32)],
        scratch_shapes=[pltpu.VMEM((1, ne), F32)],
        compiler_params=_params(("arbitrary",)),
        name="router",
    )(h2, rw, rb.reshape(1, ne), tri)


def _slot_body(idx_ref, rank_ref, offs_ref, pos_ref):
    tm = idx_ref.shape[0]
    ne = offs_ref.shape[1]
    idx = idx_ref[...]
    rank = rank_ref[...]
    lane = lax.broadcasted_iota(jnp.int32, (tm, ne), 1)
    lane_o = lax.broadcasted_iota(jnp.int32, (tm, LANES), 1)
    offs = offs_ref[...]
    pos = jnp.zeros((tm, LANES), F32)
    for k in range(TOP_K):
        start = jnp.sum(jnp.where(lane == idx[:, k:k + 1], offs, 0.0), axis=-1, keepdims=True)
        pos = jnp.where(lane_o == k, start, pos)
    pos_ref[...] = pos.astype(jnp.int32) + rank


def _slot_call(idx, rank, offs):
    nt = idx.shape[0]
    tm = ROW_TILE
    ne = offs.shape[1]
    tile = pl.BlockSpec((tm, LANES), lambda i: (i, 0))
    return pl.pallas_call(
        _slot_body,
        grid=(nt // tm,),
        in_specs=[tile, tile, pl.BlockSpec((1, ne), lambda i: (0, 0))],
        out_specs=tile,
        out_shape=jax.ShapeDtypeStruct((nt, LANES), jnp.int32),
        compiler_params=_params(("arbitrary",)),
        name="slots",
    )(idx, rank, offs)


def _row_tile(ref, r):
    return ref.at[pl.ds(pl.multiple_of(r * ROW_SLABS, ROW_SLABS), ROW_SLABS), :]


def _dispatch_body(last_ref, pos_ref, h_ref, xs_hbm, zeros, zsem, sem):
    td = h_ref.shape[0] // ROW_SLABS
    te = zeros.shape[0] // ROW_SLABS

    @pl.when(pl.program_id(0) == 0)
    def _():
        zeros[...] = jnp.zeros_like(zeros)

        def fill_block(blk, wait):
            start = pl.multiple_of(blk * (te * ROW_SLABS), te * ROW_SLABS)
            cp = pltpu.make_async_copy(zeros, xs_hbm.at[pl.ds(start, te * ROW_SLABS), :], zsem)
            if wait:
                cp.wait()
            else:
                cp.start()

        def fill(e, wait):
            @pl.when(last_ref[e] >= 0)
            def _():
                fill_block(last_ref[e], wait)
        n_exp = last_ref.shape[0] - 1
        n_blocks = xs_hbm.shape[0] // (te * ROW_SLABS)
        for wait in (False, True):
            lax.fori_loop(0, n_exp, lambda e, c, wait=wait: (fill(e, wait), c)[1], 0)
            lax.fori_loop(last_ref[n_exp], n_blocks, lambda b, c, wait=wait: (fill_block(b, wait), c)[1], 0)

    def one(t, carry):
        for k in range(TOP_K):
            pltpu.make_async_copy(_row_tile(h_ref, t), _row_tile(xs_hbm, pos_ref[0, 0, t * TOP_K + k]),
                                  sem).start(priority=k % 2)
        return carry
    lax.fori_loop(0, td, one, 0, unroll=2)
    for k in range(TOP_K):
        pltpu.make_async_copy(h_ref, xs_hbm.at[pl.ds(0, td * ROW_SLABS), :], sem).wait()


def _dispatch_call(last_blk, pos_tok, h2, n_blocks):
    nt = h2.shape[0] // ROW_SLABS
    td = ROW_TILE
    te = EXPERT_TILE
    grid_spec = pltpu.PrefetchScalarGridSpec(
        num_scalar_prefetch=1,
        grid=(nt // td,),
        in_specs=[pl.BlockSpec((1, 1, td * TOP_K), lambda i, lb: (i, 0, 0), memory_space=pltpu.SMEM),
                  pl.BlockSpec((td * ROW_SLABS, LANES), lambda i, lb: (i, 0))],
        out_specs=pl.BlockSpec(memory_space=pl.ANY),
        scratch_shapes=[pltpu.VMEM((te * ROW_SLABS, LANES), F32), pltpu.SemaphoreType.DMA, pltpu.SemaphoreType.DMA],
    )
    return pl.pallas_call(
        _dispatch_body,
        grid_spec=grid_spec,
        out_shape=jax.ShapeDtypeStruct((n_blocks * te * ROW_SLABS, LANES), F32),
        compiler_params=_params(("arbitrary",)),
        name="dispatch",
    )(last_blk, pos_tok, h2)


def _expert_body(exp_ref, nu_ref, xs_ref, wg_ref, wu_ref, wd_ref, y_ref, wgb, wub, wdb):
    b = pl.program_id(0)
    live = b < nu_ref[0]
    new_expert = jnp.logical_or(b == 0, exp_ref[b] != exp_ref[jnp.maximum(b - 1, 0)])

    @pl.when(jnp.logical_and(live, new_expert))
    def _():
        wgb[...] = wg_ref[0].astype(BF16)
        wub[...] = wu_ref[0].astype(BF16)
        wdb[...] = wd_ref[0].astype(BF16)

    @pl.when(live)
    def _():
        te = xs_ref.shape[0] // ROW_SLABS
        x = _load_rows(xs_ref, te).astype(BF16)
        gt = jnp.dot(x, wgb[...], preferred_element_type=F32)
        up = jnp.dot(x, wub[...], preferred_element_type=F32)
        act = (gt * _sigmoid(gt)) * up
        _store_rows(y_ref, jnp.dot(act.astype(BF16), wdb[...], preferred_element_type=F32))

    @pl.when(jnp.logical_not(live))
    def _():
        y_ref[...] = jnp.zeros_like(y_ref)


def _expert_call(blk_exp, n_used, xs, wg, wu, wd):
    n_blocks = blk_exp.shape[0]
    te = EXPERT_TILE
    d, hid = wg.shape[-2:]
    used = lambda b, nu: jnp.minimum(b, nu[0] - 1)
    rows_in = pl.BlockSpec((te * ROW_SLABS, LANES), lambda b, e, nu: (used(b, nu), 0))
    wspec = lambda r, c: pl.BlockSpec((1, r, c), lambda b, e, nu: (e[used(b, nu)], 0, 0))
    grid_spec = pltpu.PrefetchScalarGridSpec(
        num_scalar_prefetch=2,
        grid=(n_blocks,),
        in_specs=[rows_in, wspec(d, hid), wspec(d, hid), wspec(hid, d)],
        out_specs=pl.BlockSpec((te * ROW_SLABS, LANES), lambda b, e, nu: (b, 0)),
        scratch_shapes=[pltpu.VMEM((d, hid), BF16), pltpu.VMEM((d, hid), BF16), pltpu.VMEM((hid, d), BF16)],
    )
    return pl.pallas_call(
        _expert_body,
        grid_spec=grid_spec,
        out_shape=jax.ShapeDtypeStruct(xs.shape, F32),
        compiler_params=_params(("arbitrary",)),
        name="expert_ffn",
    )(blk_exp, n_used, xs, wg, wu, wd)


def _gather_rows(idx_ref, src_hbm, dst, sem, n):
    def pair(i, carry):
        for p in range(2):
            j = 2 * i + p
            pltpu.make_async_copy(_row_tile(src_hbm, idx_ref[0, 0, j]), _row_tile(dst, j), sem).start(priority=p)
        return carry
    lax.fori_loop(0, n // 2, pair, 0, unroll=4)


def _combine_body(pos_ref, posn_ref, y_hbm, wt_ref, h_ref, x1_ref, gate_ref, sg_ref, su_ref, sd_ref,
                  out_ref, ybuf, sem):
    nb, ni = pl.num_programs(0), pl.num_programs(1)
    step = pl.program_id(0) * ni + pl.program_id(1)
    slot = step % 2
    tc = h_ref.shape[0] // ROW_SLABS
    rows = tc * TOP_K

    @pl.when(step == 0)
    def _():
        _gather_rows(pos_ref, y_hbm, ybuf.at[0], sem.at[0], rows)

    @pl.when(step + 1 < nb * ni)
    def _():
        _gather_rows(posn_ref, y_hbm, ybuf.at[1 - slot], sem.at[1 - slot], rows)

    pltpu.make_async_copy(y_hbm.at[pl.ds(0, rows * ROW_SLABS), :], ybuf.at[slot], sem.at[slot]).wait()
    wt = wt_ref[...]
    routed = jnp.zeros(out_ref.shape[1:], F32)
    for k in range(TOP_K):
        routed = routed + wt[:, k:k + 1] * _load_rows(ybuf.at[slot], tc, first=k * tc)
    hb = _load_rows(h_ref, tc).astype(BF16)
    gt = jnp.dot(hb, sg_ref[...], preferred_element_type=F32)
    up = jnp.dot(hb, su_ref[...], preferred_element_type=F32)
    shared = jnp.dot(((gt * _sigmoid(gt)) * up).astype(BF16), sd_ref[...], preferred_element_type=F32)
    out_ref[0] = x1_ref[0] + gate_ref[0] * (routed + shared)


def _combine_call(pos_tiles, tile_off, y_sorted, wts, h2, x1, gate, sg, su, sd):
    bsz, s, d = x1.shape
    tc = min(COMBINE_TILE, s)
    ni = s // tc
    n_tiles = pos_tiles.shape[0]
    lin = lambda b, i: tile_off + b * ni + i
    const = lambda t: pl.BlockSpec(t.shape, lambda b, i: (0,) * t.ndim)
    return pl.pallas_call(
        _combine_body,
        grid=(bsz, ni),
        in_specs=[
            pl.BlockSpec((1, 1, tc * TOP_K), lambda b, i: (lin(b, i), 0, 0), memory_space=pltpu.SMEM),
            pl.BlockSpec((1, 1, tc * TOP_K), lambda b, i: (jnp.minimum(lin(b, i) + 1, n_tiles - 1), 0, 0),
                         memory_space=pltpu.SMEM),
            pl.BlockSpec(memory_space=pl.ANY),
            pl.BlockSpec((tc, LANES), lambda b, i: (lin(b, i), 0)),
            pl.BlockSpec((tc * ROW_SLABS, LANES), lambda b, i: (lin(b, i), 0)),
            pl.BlockSpec((1, tc, d), lambda b, i: (b, i, 0)),
            _mod_spec(gate, tc, d),
            const(sg), const(su), const(sd),
        ],
        out_specs=pl.BlockSpec((1, tc, d), lambda b, i: (b, i, 0)),
        out_shape=jax.ShapeDtypeStruct((bsz, s, d), F32),
        scratch_shapes=[pltpu.VMEM((2, tc * TOP_K * ROW_SLABS, LANES), F32), pltpu.SemaphoreType.DMA((2,))],
        compiler_params=_params(("arbitrary", "arbitrary")),
        name="combine",
    )(pos_tiles, pos_tiles, y_sorted, wts, h2, x1, gate, sg, su, sd)


def _t5_bucket(dist):
    exact = N_BUCKETS // 2
    d = np.asarray(dist)
    large = exact + (np.log(np.maximum(d, 1) / exact) / np.log(REL_MAX_DIST / exact) * (N_BUCKETS - exact)).astype(np.int32)
    large = np.minimum(large, N_BUCKETS - 1)
    return np.where(d < exact, d, large).astype(np.int32)


def _group_bias(rel_bias, gi):
    window, dil = ATTN_GROUPS[gi]
    bkt = _t5_bucket(np.arange(window // dil + 1) * dil)
    onehot = jnp.asarray(bkt[:, None] == np.arange(N_BUCKETS)[None, :], F32)
    cols = rel_bias[:, gi * HEADS:(gi + 1) * HEADS].astype(F32)
    return jnp.dot(onehot, cols, precision=HIGHEST).T


def _prompt_bias_table(bias_j):
    blk = ATTN_BLOCK
    n = bias_j.shape[1] - 1
    period = 3 * blk
    row0 = jnp.concatenate([jnp.flip(bias_j, axis=1), jnp.full((HEADS, period - n - 1), NEG, F32)], axis=1)
    flat = jnp.tile(row0, (1, blk))[:, :blk * (period - 1)]
    tab = flat.reshape(HEADS, blk, period - 1)[:, :, :2 * blk]
    return tab.reshape(HEADS // HEADS_PER_SLAB, HEADS_PER_SLAB * blk, 2 * blk)


def _decode_bias_tables(bias_j):
    old = jnp.flip(bias_j[:, 1:], axis=1).T[:, :, None]
    new = bias_j[:, 0][:, None]
    return old, new


def _block_schedule(counts, na):
    te = EXPERT_TILE
    n_blocks = na // te + N_EXPERTS
    pcounts = (counts + te - 1) // te * te
    pends = jnp.cumsum(pcounts)
    offs = pends - pcounts
    blk_start = jnp.arange(n_blocks, dtype=jnp.int32) * te
    blk_exp = jnp.minimum(jnp.sum(pends[None, :] <= blk_start[:, None], axis=1), N_EXPERTS - 1).astype(jnp.int32)
    n_used = (pends[-1] // te).astype(jnp.int32).reshape(1)
    last_blk = jnp.where(counts > 0, pends // te - 1, -1).astype(jnp.int32)
    return offs, blk_exp, n_used, last_blk, n_blocks


def kernel(x_prompt, x_sample, cache_k_w128, cache_v_w128, cache_k_w512, cache_v_w512, cache_k_w2048, cache_v_w2048, state_pool, c_prompt, c_sample, ada_w, ada_b, norm1, norm2, w_in, q_gain, k_gain, rel_bias, pool_w, pool_scale, w_br_a, w_br_b, w_out, router_w, router_bias, exp_w_gate, exp_w_up, exp_w_down, sh_w_gate, sh_w_up, sh_w_down):
    depth = ada_w.shape[0]
    bsz, seq, d = x_prompt.shape
    db = x_sample.shape[0]
    assert x_sample.shape[1] == 1 and db <= SAMPLE_PAD and seq % (ATTN_BLOCK * ATTN_GROUPS[-1][1]) == 0
    k_caches = (cache_k_w128, cache_k_w512, cache_k_w2048)
    v_caches = (cache_v_w128, cache_v_w512, cache_v_w2048)
    for c, (w, dil) in zip(k_caches, ATTN_GROUPS):
        assert c.shape[2] == w == ATTN_STEPS * dil

    bias_js = [_group_bias(rel_bias, g) for g in range(N_GROUPS)]
    bias_tabs = jnp.stack([_prompt_bias_table(b) for b in bias_js])
    dec = [_decode_bias_tables(b) for b in bias_js]
    bdec = jnp.stack([t[0] for t in dec])
    bnew = jnp.stack([t[1] for t in dec])
    bd = jnp.asarray((np.arange(MXU_DIM)[:, None] // HEAD_DIM == np.arange(MXU_DIM)[None, :] // HEAD_DIM)
                     / HEAD_DIM, BF16)
    wcols = np.repeat(np.asarray(POOL_WINDOWS), POOL_GROUP)
    pw_dec = jnp.asarray(np.where(np.arange(POOL_STATE + 1)[:, None] >= POOL_STATE + 1 - wcols[None, :],
                                  1.0 / wcols[None, :], 0.0), F32)

    xp = x_prompt
    xs = jnp.pad(x_sample.reshape(db, d), ((0, SAMPLE_PAD - db), (0, 0))).reshape(1, SAMPLE_PAD, d)
    n_prompt = bsz * seq
    outs = {name: [] for name in ('pk', 'pv', 'sk', 'sv')}
    outs['pk'] = [[] for _ in range(N_GROUPS)]
    outs['pv'] = [[] for _ in range(N_GROUPS)]
    outs['sk'] = [[] for _ in range(N_GROUPS)]
    outs['sv'] = [[] for _ in range(N_GROUPS)]
    ppool, spool = [], []
    for l in range(depth):
        c_all = jnp.concatenate([c_prompt, c_sample], axis=0)
        mod = _ada_call(c_all, ada_w[l], ada_b[l])
        mod_p = [m.reshape(bsz, 1, d) for m in jnp.split(mod[:bsz], 6, axis=-1)]
        mod_s = [jnp.pad(m, ((0, SAMPLE_PAD - db), (0, 0))).reshape(1, SAMPLE_PAD, d)
                 for m in jnp.split(mod[bsz:], 6, axis=-1)]
        w_in_bf = w_in[l].astype(BF16)
        qg = (jnp.tile(q_gain[l], (1, HEADS)) * (HEAD_DIM ** -0.5)).reshape(1, -1)
        kg = jnp.tile(k_gain[l], (1, HEADS)).reshape(1, -1)
        g1 = norm1[l].reshape(1, d)
        g2 = norm2[l].reshape(1, d)
        wa, wb, wo = (w_br_a[l].astype(BF16), w_br_b[l].astype(BF16), w_out[l].astype(BF16))
        pw = pool_w[l].astype(BF16)
        ps = pool_scale[l].reshape(1, -1)

        q, k, v, u, gates = _inproj_call(xp, mod_p[0], mod_p[1], g1, w_in_bf, qg, kg, bd)
        oa_p = _attn_call(q, k, v, bias_tabs)
        for g, (window, dil) in enumerate(ATTN_GROUPS):
            keep = min(window, seq)
            cs = slice(g * GROUP_W, (g + 1) * GROUP_W)
            outs['pk'][g].append(k[:, seq - keep:, cs].reshape(bsz, keep, HEADS, HEAD_DIM))
            outs['pv'][g].append(v[:, seq - keep:, cs].reshape(bsz, keep, HEADS, HEAD_DIM))
        ppool.append(u[:, seq - POOL_STATE:])
        x1p, h2p = _merge_call(xp, gates, (oa_p, u), (mod_p[2], mod_p[3], mod_p[4]),
                               g2, wa, wb, wo, pw, ps, prompt=True)

        qs, ks, vs, us, gates_s = _inproj_call(xs, mod_s[0], mod_s[1], g1, w_in_bf, qg, kg, bd)
        heads3 = lambda t: t[0, :db].reshape(db, N_GROUPS * HEADS, HEAD_DIM)
        q3, k3, v3 = heads3(qs), heads3(ks), heads3(vs)
        kcl = [c[l] for c in k_caches]
        vcl = [c[l] for c in v_caches]
        oa_s, pooled_s = _decode_call(q3, k3, v3, us[0, :db], state_pool[l], kcl, vcl, bdec, bnew, pw_dec)
        padrows = lambda t: jnp.pad(t.reshape(db, GROUP_W), ((0, SAMPLE_PAD - db), (0, 0))).reshape(1, SAMPLE_PAD, GROUP_W)
        x1s, h2s = _merge_call(xs, gates_s, (padrows(oa_s), padrows(pooled_s)), (mod_s[2], mod_s[3], mod_s[4]),
                               g2, wa, wb, wo, pw, ps, prompt=False)
        for g in range(N_GROUPS):
            hs = slice(g * HEADS, (g + 1) * HEADS)
            outs['sk'][g].append(_roll_call(kcl[g], k3[:, None, hs]))
            outs['sv'][g].append(_roll_call(vcl[g], v3[:, None, hs]))
        spool.append(jnp.concatenate([state_pool[l][:, 1:], us[0, :db, None, :]], axis=1))

        h2 = jnp.concatenate([h2p.reshape(n_prompt * ROW_SLABS, LANES), h2s.reshape(SAMPLE_PAD * ROW_SLABS, LANES)], axis=0)
        nt = n_prompt + SAMPLE_PAD
        idx, wts, rank, counts = _router_call(h2, router_w[l], router_bias[l])
        offs, blk_exp, n_used, last_blk, n_blocks = _block_schedule(counts[0], nt * TOP_K)
        pos = _slot_call(idx, rank, offs.astype(F32).reshape(1, -1))[:, :TOP_K]
        xs_sorted = _dispatch_call(jnp.concatenate([last_blk, n_used]),
                                   pos.reshape(nt // ROW_TILE, 1, ROW_TILE * TOP_K), h2, n_blocks)
        y_sorted = _expert_call(blk_exp, n_used, xs_sorted, exp_w_gate[l], exp_w_up[l], exp_w_down[l])
        tc = COMBINE_TILE
        pos_tiles = pos.reshape(nt // tc, tc, TOP_K).transpose(0, 2, 1).reshape(nt // tc, 1, TOP_K * tc)
        sg, su, sd = sh_w_gate[l].astype(BF16), sh_w_up[l].astype(BF16), sh_w_down[l].astype(BF16)
        xp = _combine_call(pos_tiles, 0, y_sorted, wts, h2, x1p, mod_p[5], sg, su, sd)
        xs = _combine_call(pos_tiles, n_prompt // tc, y_sorted, wts, h2, x1s, mod_s[5], sg, su, sd)

    stack = lambda parts: jnp.stack(parts)
    res = [xp, xs[0, :db].reshape(db, 1, d)]
    for g in range(N_GROUPS):
        res += [stack(outs['pk'][g]), stack(outs['pv'][g])]
    res.append(stack(ppool))
    for g in range(N_GROUPS):
        res += [stack(outs['sk'][g]), stack(outs['sv'][g])]
    res.append(stack(spool))
    return tuple(res)
```

```python
import functools

import numpy as np
import jax
import jax.numpy as jnp
from jax import lax
from jax.experimental import pallas as pl
from jax.experimental.pallas import tpu as pltpu

F32 = jnp.float32
BF16 = jnp.bfloat16
HIGHEST = lax.Precision.HIGHEST

HEAD_DIM = 64
HEADS = 8
GROUP_W = HEADS * HEAD_DIM
ATTN_GROUPS = ((128, 1), (512, 4), (2048, 16))
N_GROUPS = len(ATTN_GROUPS)
ATTN_STEPS = 128
ATTN_BLOCK = 128
POOL_WINDOWS = (2, 4, 8, 16)
POOL_GROUP = 128
POOL_STATE = 15
N_BUCKETS = 32
REL_MAX_DIST = 2048
N_EXPERTS = 256
TOP_K = 8
N_EXPERT_GROUPS = 8
TOPK_GROUPS = 4
ROUTED_SCALE = 2.5
EPS = 1e-6
NEG = -1e30

LANES = 128
MXU_DIM = 256
VMEM_LIMIT = 56 * 1024 * 1024

INPROJ_TILE = 512
ROW_TILE = 256
EXPERT_TILE = 256
COMBINE_TILE = 128
SAMPLE_PAD = 256


def _sigmoid(x):
    return 1.0 / (1.0 + jnp.exp(-x))


def _params(sem):
    return pltpu.CompilerParams(dimension_semantics=sem, vmem_limit_bytes=VMEM_LIMIT)


ROW_SLABS = 8


def _load_rows(ref, n, first=0):
    return jnp.concatenate([ref[pl.ds(first * ROW_SLABS + s, n, stride=ROW_SLABS), :]
                            for s in range(ROW_SLABS)], axis=1)


def _store_rows(ref, val):
    n = val.shape[0]
    for s in range(ROW_SLABS):
        ref[pl.ds(s, n, stride=ROW_SLABS), :] = val[:, s * LANES:(s + 1) * LANES]


def _ada_body(c_ref, w_ref, b_ref, o_ref):
    c = c_ref[...]
    s = c * _sigmoid(c)
    o_ref[...] = jnp.dot(s, w_ref[...], precision=HIGHEST, preferred_element_type=F32) + b_ref[...]


def _ada_call(c, w, b):
    n, d = c.shape
    cols = w.shape[1]
    tn = 512
    return pl.pallas_call(
        _ada_body,
        grid=(cols // tn,),
        in_specs=[pl.BlockSpec((n, d), lambda j: (0, 0)),
                  pl.BlockSpec((d, tn), lambda j: (0, j)),
                  pl.BlockSpec((1, tn), lambda j: (0, j))],
        out_specs=pl.BlockSpec((n, tn), lambda j: (0, j)),
        out_shape=jax.ShapeDtypeStruct((n, cols), F32),
        compiler_params=_params(("arbitrary",)),
        name="adaln",
    )(c, w, b.reshape(1, cols))


def _inproj_body(x_ref, shift_ref, scale_ref, g_ref, w_ref, qg_ref, kg_ref, bd_ref,
                 q_ref, k_ref, v_ref, u_ref, gt_ref):
    x = x_ref[0]
    ms = jnp.mean(x * x, axis=-1, keepdims=True)
    h = x * lax.rsqrt(ms + EPS) * g_ref[...]
    h = h * (1.0 + scale_ref[0]) + shift_ref[0]
    hb = h.astype(BF16)
    qkv = N_GROUPS * GROUP_W
    n_chunks = w_ref.shape[1] // GROUP_W
    for c in range(n_chunks):
        z = jnp.dot(hb, w_ref[:, c * GROUP_W:(c + 1) * GROUP_W], preferred_element_type=F32)
        if c < 2 * N_GROUPS:
            zz = (z * z).astype(BF16)
            hms = jnp.concatenate(
                [jnp.dot(zz[:, s:s + MXU_DIM], bd_ref[...], preferred_element_type=F32)
                 for s in range(0, GROUP_W, MXU_DIM)], axis=1)
            g = c % N_GROUPS
            gain = (qg_ref if c < N_GROUPS else kg_ref)[:, g * GROUP_W:(g + 1) * GROUP_W]
            y = z * lax.rsqrt(hms + EPS) * gain
            if c < N_GROUPS:
                q_ref[0, :, g * GROUP_W:(g + 1) * GROUP_W] = y
            else:
                k_ref[0, :, g * GROUP_W:(g + 1) * GROUP_W] = y
        elif c < 3 * N_GROUPS:
            g = c - 2 * N_GROUPS
            v_ref[0, :, g * GROUP_W:(g + 1) * GROUP_W] = z
        elif c == 3 * N_GROUPS:
            u_ref[0] = z
        else:
            g = c - 3 * N_GROUPS - 1
            gt_ref[0, :, g * GROUP_W:(g + 1) * GROUP_W] = z.astype(BF16)
    del qkv


def _mod_spec(mod, tm, d):
    if mod.shape[1] == 1:
        return pl.BlockSpec((1, 1, d), lambda b, i: (b, 0, 0))
    return pl.BlockSpec((1, tm, d), lambda b, i: (b, i, 0))


def _inproj_call(x, shift, scale, g1, w_bf, qg, kg, bd):
    bsz, s, d = x.shape
    tm = min(INPROJ_TILE, s)
    qkv = N_GROUPS * GROUP_W
    n_gate = w_bf.shape[1] - 3 * qkv - GROUP_W
    const = lambda b, i: (0, 0)
    row = lambda width: pl.BlockSpec((1, tm, width), lambda b, i: (b, i, 0))
    return pl.pallas_call(
        _inproj_body,
        grid=(bsz, s // tm),
        in_specs=[row(d), _mod_spec(shift, tm, d), _mod_spec(scale, tm, d),
                  pl.BlockSpec((1, d), const),
                  pl.BlockSpec(w_bf.shape, const, pipeline_mode=pl.Buffered(1)),
                  pl.BlockSpec((1, qkv), const), pl.BlockSpec((1, qkv), const),
                  pl.BlockSpec((MXU_DIM, MXU_DIM), const)],
        out_specs=[row(qkv), row(qkv), row(qkv), row(GROUP_W), row(n_gate)],
        out_shape=[jax.ShapeDtypeStruct((bsz, s, qkv), F32),
                   jax.ShapeDtypeStruct((bsz, s, qkv), F32),
                   jax.ShapeDtypeStruct((bsz, s, qkv), F32),
                   jax.ShapeDtypeStruct((bsz, s, GROUP_W), F32),
                   jax.ShapeDtypeStruct((bsz, s, n_gate), BF16)],
        compiler_params=_params(("arbitrary", "arbitrary")),
        name="inproj",
    )(x, shift, scale, g1, w_bf, qg, kg, bd)


ATTN_ROWS = ATTN_BLOCK * ATTN_GROUPS[-1][1]
HEADS_PER_SLAB = LANES // HEAD_DIM


def _attn_body(*refs):
    ng = N_GROUPS
    qkv = refs[:3 * ng]
    bias_ref, o_ref = refs[3 * ng], refs[3 * ng + 1]
    scr = refs[3 * ng + 2:]
    first = pl.program_id(2) == 0
    blk = ATTN_BLOCK
    pb = ATTN_ROWS
    hs = HEADS_PER_SLAB
    lane = lax.broadcasted_iota(jnp.int32, (blk, LANES), 1)
    col = lax.broadcasted_iota(jnp.int32, (hs * blk, 2 * blk), 1)
    for g, (_, dil) in enumerate(ATTN_GROUPS):
        q_ref, k_ref, v_ref = qkv[3 * g:3 * g + 3]
        kext, vext, og, lg = scr[4 * g:4 * g + 4]
        span = blk * dil
        bias = bias_ref[g, 0]

        @pl.when(first)
        def _():
            kext[0:span, :] = jnp.zeros((span, LANES), F32)
            vext[0:span, :] = jnp.zeros((span, LANES), F32)
        kext[span:span + pb, :] = k_ref[0]
        vext[span:span + pb, :] = v_ref[0]

        rows = lambda start: pl.ds(start, blk, stride=dil) if dil > 1 else pl.ds(pl.multiple_of(start, blk), blk)
        per = 4

        def units(i, carry):
            bases, v2s, scores = [], [], []
            for j in range(per):
                u = per * i + j
                qb = u // dil
                base = qb * span + (u % dil)
                q = q_ref[0, rows(base), :]
                k2 = jnp.concatenate([kext[rows(base), :], kext[rows(base + span), :]], axis=0).astype(BF16)
                v2s.append(jnp.concatenate([vext[rows(base), :], vext[rows(base + span), :]], axis=0).astype(BF16))
                qs = jnp.concatenate([jnp.where(lane // HEAD_DIM == h, q, 0.0) for h in range(hs)],
                                     axis=0).astype(BF16)
                s = lax.dot_general(qs, k2, (((1,), (1,)), ((), ())), preferred_element_type=F32) + bias
                scores.append(jnp.where(jnp.logical_and(jnp.logical_and(first, qb == 0), col < blk), NEG, s))
                bases.append(base)
            s = jnp.concatenate(scores, axis=0)
            m = jnp.max(s, axis=-1, keepdims=True)
            p = jnp.exp(s - m)
            l = jnp.sum(p, axis=-1, keepdims=True)
            pb16 = p.astype(BF16)
            lse = m + jnp.log(l)
            for j in range(per):
                r0 = j * hs * blk
                o2 = jnp.dot(pb16[r0:r0 + hs * blk], v2s[j], preferred_element_type=F32) / l[r0:r0 + hs * blk]
                o = o2[0:blk]
                lw = jnp.broadcast_to(lse[r0:r0 + blk], (blk, LANES))
                for h in range(1, hs):
                    o = jnp.where(lane // HEAD_DIM == h, o2[h * blk:(h + 1) * blk], o)
                    lw = jnp.where(lane // HEAD_DIM == h, lse[r0 + h * blk:r0 + (h + 1) * blk], lw)
                og[rows(bases[j]), :] = o
                lg[rows(bases[j]), :] = lw
            return carry
        lax.fori_loop(0, pb // blk // per, units, 0)
        kext[0:span, :] = kext[pb:pb + span, :]
        vext[0:span, :] = vext[pb:pb + span, :]

    def merge(c, carry):
        rs = pl.ds(pl.multiple_of(c * blk, blk), blk)
        ls = [scr[4 * g + 3][rs, :] for g in range(ng)]
        mx = jnp.maximum(jnp.maximum(ls[0], ls[1]), ls[2])
        es = [jnp.exp(t - mx) for t in ls]
        den = es[0] + es[1] + es[2]
        acc = (es[0] / den) * scr[2][rs, :]
        for g in range(1, ng):
            acc = acc + (es[g] / den) * scr[4 * g + 2][rs, :]
        o_ref[0, rs, :] = acc.astype(o_ref.dtype)
        return carry
    lax.fori_loop(0, pb // blk, merge, 0)


def _attn_call(q, k, v, bias_tabs):
    bsz, s, qkv = q.shape
    pb = ATTN_ROWS
    n_slab = GROUP_W // LANES
    in_specs, args = [], []
    for g in range(N_GROUPS):
        spec = pl.BlockSpec((1, pb, LANES), lambda b, c, i, g=g: (b, i, g * n_slab + c))
        in_specs += [spec, spec, spec]
        args += [q, k, v]
    in_specs.append(pl.BlockSpec((N_GROUPS, 1) + bias_tabs.shape[2:], lambda b, c, i: (0, c, 0, 0)))
    scratch = []
    for _, dil in ATTN_GROUPS:
        ext = pltpu.VMEM((ATTN_BLOCK * dil + pb, LANES), F32)
        scratch += [ext, ext, pltpu.VMEM((pb, LANES), F32), pltpu.VMEM((pb, LANES), F32)]
    return pl.pallas_call(
        _attn_body,
        grid=(bsz, n_slab, s // pb),
        in_specs=in_specs,
        out_specs=pl.BlockSpec((1, pb, LANES), lambda b, c, i: (b, i, c)),
        out_shape=jax.ShapeDtypeStruct((bsz, s, GROUP_W), BF16),
        scratch_shapes=scratch,
        compiler_params=_params(("arbitrary", "arbitrary", "arbitrary")),
        name="attn",
    )(*args, bias_tabs)


def _decode_body(q_ref, kn_ref, vn_ref, u_ref, pool_ref, kc0, vc0, kc1, vc1, kc2, vc2,
                 bdec_ref, bnew_ref, pw_ref, oa_ref, pooled_ref):
    kcs = (kc0, kc1, kc2)
    vcs = (vc0, vc1, vc2)
    outs, lses = [], []
    for g in range(N_GROUPS):
        hs = slice(g * HEADS, (g + 1) * HEADS)
        q = q_ref[0, hs, :]
        kn = kn_ref[0, hs, :]
        vn = vn_ref[0, hs, :]
        kc = kcs[g][0]
        vc = vcs[g][0]
        s_old = jnp.sum(kc * q[None], axis=-1, keepdims=True) + bdec_ref[g]
        s_new = jnp.sum(kn * q, axis=-1, keepdims=True) + bnew_ref[g]
        m = jnp.maximum(jnp.max(s_old, axis=0), s_new)
        p_old = jnp.exp(s_old - m[None])
        p_new = jnp.exp(s_new - m)
        l = jnp.sum(p_old, axis=0) + p_new
        outs.append((jnp.sum(p_old * vc, axis=0) + p_new * vn) / l)
        lses.append(m + jnp.log(l))
    mx = jnp.maximum(jnp.maximum(lses[0], lses[1]), lses[2])
    es = [jnp.exp(t - mx) for t in lses]
    den = es[0] + es[1] + es[2]
    oa_ref[0] = (es[0] / den) * outs[0] + (es[1] / den) * outs[1] + (es[2] / den) * outs[2]
    u = u_ref[0]
    stored = jnp.sum(pool_ref[0] * pw_ref[0:POOL_STATE, :], axis=0, keepdims=True)
    pooled_ref[0] = stored + u * pw_ref[POOL_STATE:POOL_STATE + 1, :] - u


def _decode_call(q, kn, vn, u, pool_prev, k_caches, v_caches, bdec, bnew, pw):
    db = pool_prev.shape[0]
    per_b = lambda t: pl.BlockSpec((1,) + t.shape[1:], lambda b: (b,) + (0,) * (t.ndim - 1))
    const = lambda t: pl.BlockSpec(t.shape, lambda b: (0,) * t.ndim)
    cache_specs, cache_args = [], []
    for g, (_, dil) in enumerate(ATTN_GROUPS):
        for c in (k_caches[g], v_caches[g]):
            rows = c.shape[1]
            cache_args.append(c.reshape(db, rows // dil, dil, HEADS, HEAD_DIM))
            cache_specs.append(pl.BlockSpec((1, ATTN_STEPS, None, HEADS, HEAD_DIM), lambda b: (b, 0, 0, 0, 0)))
    u3 = u.reshape(db, 1, u.shape[-1])
    return pl.pallas_call(
        _decode_body,
        grid=(db,),
        in_specs=[per_b(q), per_b(kn), per_b(vn), per_b(u3), per_b(pool_prev)] + cache_specs +
                 [const(bdec), const(bnew), const(pw)],
        out_specs=[pl.BlockSpec((1, HEADS, HEAD_DIM), lambda b: (b, 0, 0)),
                   pl.BlockSpec((1, 1, GROUP_W), lambda b: (b, 0, 0))],
        out_shape=[jax.ShapeDtypeStruct((db, HEADS, HEAD_DIM), F32),
                   jax.ShapeDtypeStruct((db, 1, GROUP_W), F32)],
        compiler_params=_params(("arbitrary",)),
        name="decode_mix",
    )(q, kn, vn, u3, pool_prev, *cache_args, bdec, bnew, pw)


def _roll_body(c_ref, nxt_ref, new_ref, o_ref):
    ch = c_ref.shape[1]
    last = pl.program_id(1) == pl.num_programs(1) - 1
    o_ref[0, 0:ch - 1] = c_ref[0, 1:ch]
    o_ref[0, ch - 1:ch] = jnp.where(last, new_ref[0], nxt_ref[0])


def _roll_call(cache, new):
    db, rows, nh, e = cache.shape
    ch = min(rows, 1024)
    return pl.pallas_call(
        _roll_body,
        grid=(db, rows // ch),
        in_specs=[pl.BlockSpec((1, ch, nh, e), lambda b, i: (b, i, 0, 0)),
                  pl.BlockSpec((1, 1, nh, e), lambda b, i: (b, jnp.minimum((i + 1) * ch, rows - 1), 0, 0)),
                  pl.BlockSpec((1, 1, nh, e), lambda b, i: (b, 0, 0, 0))],
        out_specs=pl.BlockSpec((1, ch, nh, e), lambda b, i: (b, i, 0, 0)),
        out_shape=jax.ShapeDtypeStruct(cache.shape, cache.dtype),
        compiler_params=_params(("arbitrary", "arbitrary")),
        name="cache_roll",
    )(cache, cache, new)


def _merge_tail(x, oa, pooled, gates, gate_msa, shift2, scale2, g2, wa, wb, wo, pw_ref, ps,
                x1_ref, h2_ref):
    mixed = jnp.concatenate(
        [jnp.dot(pooled[:, j * POOL_GROUP:(j + 1) * POOL_GROUP].astype(BF16), pw_ref[j],
                 preferred_element_type=F32) for j in range(len(POOL_WINDOWS))], axis=1) * ps
    d = x.shape[-1]
    ga = gates[:, :d].astype(F32)
    gb = gates[:, d:].astype(F32)
    merged = (_sigmoid(ga) * jnp.dot(oa.astype(BF16), wa, preferred_element_type=F32) +
              _sigmoid(gb) * jnp.dot(mixed.astype(BF16), wb, preferred_element_type=F32))
    y = jnp.dot(merged.astype(BF16), wo, preferred_element_type=F32)
    x1 = x + gate_msa * y
    x1_ref[0] = x1
    ms = jnp.mean(x1 * x1, axis=-1, keepdims=True)
    h2 = x1 * lax.rsqrt(ms + EPS) * g2
    _store_rows(h2_ref.at[0], h2 * (1.0 + scale2) + shift2)


def _merge_prompt_body(x_ref, gt_ref, oa_ref, u_ref, uh_ref,
                       gm_ref, sh_ref, sc_ref, g2_ref, wa_ref, wb_ref, wo_ref, pw_ref, ps_ref,
                       x1_ref, h2_ref):
    i = pl.program_id(1)
    tm = x_ref.shape[1]
    oa = oa_ref[0]
    u = u_ref[0]
    halo = jnp.where(i == 0, 0.0, uh_ref[0])
    ext = jnp.concatenate([halo, u], axis=0)
    hw = halo.shape[0]
    acc = ext
    parts = []
    for j, w in enumerate(POOL_WINDOWS):
        acc = acc[:, POOL_GROUP * (1 if j else 0):]
        acc = acc + pltpu.roll(acc, w // 2, axis=0)
        parts.append(acc[hw:, :POOL_GROUP])
    tot = jnp.concatenate(parts, axis=1)
    lane = lax.broadcasted_iota(jnp.int32, (tm, GROUP_W), 1)
    wcol = jnp.left_shift(2, lane // POOL_GROUP)
    pos = i * tm + lax.broadcasted_iota(jnp.int32, (tm, GROUP_W), 0)
    cnt = jnp.minimum(pos + 1, wcol).astype(F32)
    pooled = tot / cnt - u
    _merge_tail(x_ref[0], oa, pooled, gt_ref[0], gm_ref[0], sh_ref[0], sc_ref[0], g2_ref[...],
                wa_ref[...], wb_ref[...], wo_ref[...], pw_ref, ps_ref[...], x1_ref, h2_ref)


def _merge_sample_body(x_ref, gt_ref, oa_ref, pooled_ref,
                       gm_ref, sh_ref, sc_ref, g2_ref, wa_ref, wb_ref, wo_ref, pw_ref, ps_ref,
                       x1_ref, h2_ref):
    _merge_tail(x_ref[0], oa_ref[0], pooled_ref[0], gt_ref[0], gm_ref[0], sh_ref[0], sc_ref[0], g2_ref[...],
                wa_ref[...], wb_ref[...], wo_ref[...], pw_ref, ps_ref[...], x1_ref, h2_ref)


def _merge_call(x, gates, mixer_inputs, mods, g2, wa, wb, wo, pw, ps, prompt):
    bsz, s, d = x.shape
    assert d == ROW_SLABS * LANES
    tm = min(ROW_TILE, s)
    row = lambda t: pl.BlockSpec((1, tm, t.shape[-1]), lambda b, i: (b, i, 0))
    const = lambda t: pl.BlockSpec(t.shape, lambda b, i: (0,) * t.ndim)
    gm, sh, sc = mods
    in_specs = [row(x), row(gates)]
    args = [x, gates]
    if prompt:
        oa, u = mixer_inputs
        hw = 16
        in_specs += [row(oa), row(u),
                     pl.BlockSpec((1, hw, u.shape[-1]), lambda b, i: (b, jnp.maximum(i * (tm // hw) - 1, 0), 0))]
        args += [oa, u, u]
        body = _merge_prompt_body
    else:
        in_specs += [row(t) for t in mixer_inputs]
        args += list(mixer_inputs)
        body = _merge_sample_body
    in_specs += [_mod_spec(m, tm, d) for m in (gm, sh, sc)]
    in_specs += [const(t) for t in (g2, wa, wb, wo, pw, ps)]
    args += [gm, sh, sc, g2, wa, wb, wo, pw, ps]
    return pl.pallas_call(
        body,
        grid=(bsz, s // tm),
        in_specs=in_specs,
        out_specs=[row(x), pl.BlockSpec((1, tm * ROW_SLABS, LANES), lambda b, i: (b, i, 0))],
        out_shape=[jax.ShapeDtypeStruct((bsz, s, d), F32),
                   jax.ShapeDtypeStruct((bsz, s * ROW_SLABS, LANES), F32)],
        compiler_params=_params(("arbitrary", "arbitrary")),
        name="merge_prompt" if prompt else "merge_sample",
    )(*args)


def _router_body(h_ref, rw_ref, rb_ref, tri_ref, idx_ref, wt_ref, rank_ref, cnt_ref, carry):
    tm = h_ref.shape[0] // ROW_SLABS

    @pl.when(pl.program_id(0) == 0)
    def _():
        carry[...] = jnp.zeros_like(carry)

    ne = rw_ref.shape[1]
    gsz = ne // N_EXPERT_GROUPS
    logits = jnp.dot(_load_rows(h_ref, tm), rw_ref[...], precision=HIGHEST, preferred_element_type=F32)
    scores = _sigmoid(logits)
    sel = scores + rb_ref[...]
    lane = lax.broadcasted_iota(jnp.int32, (tm, ne), 1)
    lanef = lane.astype(F32)
    grp = lane // gsz
    ninf = -jnp.inf
    far = float(ne)

    def first_max(x):
        m = jnp.max(x, axis=-1, keepdims=True)
        at = jnp.min(jnp.where(x == m, lanef, far), axis=-1, keepdims=True)
        return m, at

    gs = []
    for g in range(N_EXPERT_GROUPS):
        xg = jnp.where(grp == g, sel, ninf)
        m1, a1 = first_max(xg)
        m2 = jnp.max(jnp.where(lanef == a1, ninf, xg), axis=-1, keepdims=True)
        gs.append(m1 + m2)
    allowed = jnp.zeros((tm, ne), jnp.bool_)
    for g in range(N_EXPERT_GROUPS):
        ahead = jnp.zeros((tm, 1), F32)
        for o in range(N_EXPERT_GROUPS):
            if o == g:
                continue
            beats = (gs[o] > gs[g]) | ((gs[o] == gs[g]) & (o < g))
            ahead = ahead + beats.astype(F32)
        allowed = allowed | ((grp == g) & (ahead < TOPK_GROUPS))
    masked = jnp.where(allowed, sel, ninf)
    lane_o = lax.broadcasted_iota(jnp.int32, (tm, LANES), 1)
    idx_t = jnp.zeros((tm, LANES), F32)
    wt_t = jnp.zeros((tm, LANES), F32)
    wsum = jnp.zeros((tm, 1), F32)
    picked = jnp.zeros((tm, ne), F32)
    ats = []
    for k in range(TOP_K):
        _, at = first_max(masked)
        hit = lanef == at
        wk = jnp.sum(jnp.where(hit, scores, 0.0), axis=-1, keepdims=True)
        masked = jnp.where(hit, ninf, masked)
        picked = picked + hit.astype(F32)
        ats.append(at)
        idx_t = jnp.where(lane_o == k, at, idx_t)
        wt_t = jnp.where(lane_o == k, wk, wt_t)
        wsum = wsum + wk
    idx_ref[...] = idx_t.astype(jnp.int32)
    wt_ref[...] = wt_t / wsum * ROUTED_SCALE
    before = jnp.dot(tri_ref[...], picked.astype(BF16), preferred_element_type=F32) + carry[...]
    rank_t = jnp.zeros((tm, LANES), F32)
    for k in range(TOP_K):
        rk = jnp.sum(jnp.where(lanef == ats[k], before, 0.0), axis=-1, keepdims=True)
        rank_t = jnp.where(lane_o == k, rk, rank_t)
    rank_ref[...] = rank_t.astype(jnp.int32)
    carry[...] = carry[...] + jnp.sum(picked, axis=0, keepdims=True)
    cnt_ref[...] = carry[...].astype(jnp.int32)


def _router_call(h2, rw, rb):
    nt = h2.shape[0] // ROW_SLABS
    tm = ROW_TILE
    d, ne = rw.shape
    tri = jnp.asarray(np.tril(np.ones((tm, tm), np.float32), -1), BF16)
    tile = pl.BlockSpec((tm, LANES), lambda i: (i, 0))
    return pl.pallas_call(
        _router_body,
        grid=(nt // tm,),
        in_specs=[pl.BlockSpec((tm * ROW_SLABS, LANES), lambda i: (i, 0)),
                  pl.BlockSpec((d, ne), lambda i: (0, 0)),
                  pl.BlockSpec((1, ne), lambda i: (0, 0)),
                  pl.BlockSpec((tm, tm), lambda i: (0, 0))],
        out_specs=[tile, tile, tile, pl.BlockSpec((1, ne), lambda i: (0, 0))],
        out_shape=[jax.ShapeDtypeStruct((nt, LANES), jnp.int32),
                   jax.ShapeDtypeStruct((nt, LANES), F32),
                   jax.ShapeDtypeStruct((nt, LANES), jnp.int32),
                   jax.ShapeDtypeStruct((1, ne), jnp.int32)],
        scratch_shapes=[pltpu.VMEM((1, ne), F32)],
        compiler_params=_params(("arbitrary",)),
        name="router",
    )(h2, rw, rb.reshape(1, ne), tri)


def _slot_body(idx_ref, rank_ref, offs_ref, pos_ref):
    tm = idx_ref.shape[0]
    ne = offs_ref.shape[1]
    idx = idx_ref[...]
    rank = rank_ref[...]
    lane = lax.broadcasted_iota(jnp.int32, (tm, ne), 1)
    lane_o = lax.broadcasted_iota(jnp.int32, (tm, LANES), 1)
    offs = offs_ref[...]
    pos = jnp.zeros((tm, LANES), F32)
    for k in range(TOP_K):
        start = jnp.sum(jnp.where(lane == idx[:, k:k + 1], offs, 0.0), axis=-1, keepdims=True)
        pos = jnp.where(lane_o == k, start, pos)
    pos_ref[...] = pos.astype(jnp.int32) + rank


def _slot_call(idx, rank, offs):
    nt = idx.shape[0]
    tm = ROW_TILE
    ne = offs.shape[1]
    tile = pl.BlockSpec((tm, LANES), lambda i: (i, 0))
    return pl.pallas_call(
        _slot_body,
        grid=(nt // tm,),
        in_specs=[tile, tile, pl.BlockSpec((1, ne), lambda i: (0, 0))],
        out_specs=tile,
        out_shape=jax.ShapeDtypeStruct((nt, LANES), jnp.int32),
        compiler_params=_params(("arbitrary",)),
        name="slots",
    )(idx, rank, offs)


def _row_tile(ref, r):
    return ref.at[pl.ds(pl.multiple_of(r * ROW_SLABS, ROW_SLABS), ROW_SLABS), :]


def _dispatch_body(last_ref, pos_ref, h_ref, xs_hbm, zeros, zsem, sem):
    td = h_ref.shape[0] // ROW_SLABS
    te = zeros.shape[0] // ROW_SLABS

    @pl.when(pl.program_id(0) == 0)
    def _():
        zeros[...] = jnp.zeros_like(zeros)

        def fill_block(blk, wait):
            start = pl.multiple_of(blk * (te * ROW_SLABS), te * ROW_SLABS)
            cp = pltpu.make_async_copy(zeros, xs_hbm.at[pl.ds(start, te * ROW_SLABS), :], zsem)
            if wait:
                cp.wait()
            else:
                cp.start()

        def fill(e, wait):
            @pl.when(last_ref[e] >= 0)
            def _():
                fill_block(last_ref[e], wait)
        n_exp = last_ref.shape[0] - 1
        n_blocks = xs_hbm.shape[0] // (te * ROW_SLABS)
        for wait in (False, True):
            lax.fori_loop(0, n_exp, lambda e, c, wait=wait: (fill(e, wait), c)[1], 0)
            lax.fori_loop(last_ref[n_exp], n_blocks, lambda b, c, wait=wait: (fill_block(b, wait), c)[1], 0)

    def one(t, carry):
        for k in range(TOP_K):
            pltpu.make_async_copy(_row_tile(h_ref, t), _row_tile(xs_hbm, pos_ref[0, 0, t * TOP_K + k]),
                                  sem).start(priority=k % 2)
        return carry
    lax.fori_loop(0, td, one, 0, unroll=2)
    for k in range(TOP_K):
        pltpu.make_async_copy(h_ref, xs_hbm.at[pl.ds(0, td * ROW_SLABS), :], sem).wait()


def _dispatch_call(last_blk, pos_tok, h2, n_blocks):
    nt = h2.shape[0] // ROW_SLABS
    td = ROW_TILE
    te = EXPERT_TILE
    grid_spec = pltpu.PrefetchScalarGridSpec(
        num_scalar_prefetch=1,
        grid=(nt // td,),
        in_specs=[pl.BlockSpec((1, 1, td * TOP_K), lambda i, lb: (i, 0, 0), memory_space=pltpu.SMEM),
                  pl.BlockSpec((td * ROW_SLABS, LANES), lambda i, lb: (i, 0))],
        out_specs=pl.BlockSpec(memory_space=pl.ANY),
        scratch_shapes=[pltpu.VMEM((te * ROW_SLABS, LANES), F32), pltpu.SemaphoreType.DMA, pltpu.SemaphoreType.DMA],
    )
    return pl.pallas_call(
        _dispatch_body,
        grid_spec=grid_spec,
        out_shape=jax.ShapeDtypeStruct((n_blocks * te * ROW_SLABS, LANES), F32),
        compiler_params=_params(("arbitrary",)),
        name="dispatch",
    )(last_blk, pos_tok, h2)


def _expert_body(exp_ref, nu_ref, xs_ref, wg_ref, wu_ref, wd_ref, y_ref, wgb, wub, wdb):
    b = pl.program_id(0)
    live = b < nu_ref[0]
    new_expert = jnp.logical_or(b == 0, exp_ref[b] != exp_ref[jnp.maximum(b - 1, 0)])

    @pl.when(jnp.logical_and(live, new_expert))
    def _():
        wgb[...] = wg_ref[0].astype(BF16)
        wub[...] = wu_ref[0].astype(BF16)
        wdb[...] = wd_ref[0].astype(BF16)

    @pl.when(live)
    def _():
        te = xs_ref.shape[0] // ROW_SLABS
        x = _load_rows(xs_ref, te).astype(BF16)
        gt = jnp.dot(x, wgb[...], preferred_element_type=F32)
        up = jnp.dot(x, wub[...], preferred_element_type=F32)
        act = (gt * _sigmoid(gt)) * up
        _store_rows(y_ref, jnp.dot(act.astype(BF16), wdb[...], preferred_element_type=F32))

    @pl.when(jnp.logical_not(live))
    def _():
        y_ref[...] = jnp.zeros_like(y_ref)


def _expert_call(blk_exp, n_used, xs, wg, wu, wd):
    n_blocks = blk_exp.shape[0]
    te = EXPERT_TILE
    d, hid = wg.shape[-2:]
    used = lambda b, nu: jnp.minimum(b, nu[0] - 1)
    rows_in = pl.BlockSpec((te * ROW_SLABS, LANES), lambda b, e, nu: (used(b, nu), 0))
    wspec = lambda r, c: pl.BlockSpec((1, r, c), lambda b, e, nu: (e[used(b, nu)], 0, 0))
    grid_spec = pltpu.PrefetchScalarGridSpec(
        num_scalar_prefetch=2,
        grid=(n_blocks,),
        in_specs=[rows_in, wspec(d, hid), wspec(d, hid), wspec(hid, d)],
        out_specs=pl.BlockSpec((te * ROW_SLABS, LANES), lambda b, e, nu: (b, 0)),
        scratch_shapes=[pltpu.VMEM((d, hid), BF16), pltpu.VMEM((d, hid), BF16), pltpu.VMEM((hid, d), BF16)],
    )
    return pl.pallas_call(
        _expert_body,
        grid_spec=grid_spec,
        out_shape=jax.ShapeDtypeStruct(xs.shape, F32),
        compiler_params=_params(("arbitrary",)),
        name="expert_ffn",
    )(blk_exp, n_used, xs, wg, wu, wd)


def _gather_rows(idx_ref, src_hbm, dst, sem, n):
    def pair(i, carry):
        for p in range(2):
            j = 2 * i + p
            pltpu.make_async_copy(_row_tile(src_hbm, idx_ref[0, 0, j]), _row_tile(dst, j), sem).start(priority=p)
        return carry
    lax.fori_loop(0, n // 2, pair, 0, unroll=4)


def _combine_body(pos_ref, posn_ref, y_hbm, wt_ref, h_ref, x1_ref, gate_ref, sg_ref, su_ref, sd_ref,
                  out_ref, ybuf, sem):
    nb, ni = pl.num_programs(0), pl.num_programs(1)
    step = pl.program_id(0) * ni + pl.program_id(1)
    slot = step % 2
    tc = h_ref.shape[0] // ROW_SLABS
    rows = tc * TOP_K

    @pl.when(step == 0)
    def _():
        _gather_rows(pos_ref, y_hbm, ybuf.at[0], sem.at[0], rows)

    @pl.when(step + 1 < nb * ni)
    def _():
        _gather_rows(posn_ref, y_hbm, ybuf.at[1 - slot], sem.at[1 - slot], rows)

    pltpu.make_async_copy(y_hbm.at[pl.ds(0, rows * ROW_SLABS), :], ybuf.at[slot], sem.at[slot]).wait()
    wt = wt_ref[...]
    routed = jnp.zeros(out_ref.shape[1:], F32)
    for k in range(TOP_K):
        routed = routed + wt[:, k:k + 1] * _load_rows(ybuf.at[slot], tc, first=k * tc)
    hb = _load_rows(h_ref, tc).astype(BF16)
    gt = jnp.dot(hb, sg_ref[...], preferred_element_type=F32)
    up = jnp.dot(hb, su_ref[...], preferred_element_type=F32)
    shared = jnp.dot(((gt * _sigmoid(gt)) * up).astype(BF16), sd_ref[...], preferred_element_type=F32)
    out_ref[0] = x1_ref[0] + gate_ref[0] * (routed + shared)


def _combine_call(pos_tiles, tile_off, y_sorted, wts, h2, x1, gate, sg, su, sd):
    bsz, s, d = x1.shape
    tc = min(COMBINE_TILE, s)
    ni = s // tc
    n_tiles = pos_tiles.shape[0]
    lin = lambda b, i: tile_off + b * ni + i
    const = lambda t: pl.BlockSpec(t.shape, lambda b, i: (0,) * t.ndim)
    return pl.pallas_call(
        _combine_body,
        grid=(bsz, ni),
        in_specs=[
            pl.BlockSpec((1, 1, tc * TOP_K), lambda b, i: (lin(b, i), 0, 0), memory_space=pltpu.SMEM),
            pl.BlockSpec((1, 1, tc * TOP_K), lambda b, i: (jnp.minimum(lin(b, i) + 1, n_tiles - 1), 0, 0),
                         memory_space=pltpu.SMEM),
            pl.BlockSpec(memory_space=pl.ANY),
            pl.BlockSpec((tc, LANES), lambda b, i: (lin(b, i), 0)),
            pl.BlockSpec((tc * ROW_SLABS, LANES), lambda b, i: (lin(b, i), 0)),
            pl.BlockSpec((1, tc, d), lambda b, i: (b, i, 0)),
            _mod_spec(gate, tc, d),
            const(sg), const(su), const(sd),
        ],
        out_specs=pl.BlockSpec((1, tc, d), lambda b, i: (b, i, 0)),
        out_shape=jax.ShapeDtypeStruct((bsz, s, d), F32),
        scratch_shapes=[pltpu.VMEM((2, tc * TOP_K * ROW_SLABS, LANES), F32), pltpu.SemaphoreType.DMA((2,))],
        compiler_params=_params(("arbitrary", "arbitrary")),
        name="combine",
    )(pos_tiles, pos_tiles, y_sorted, wts, h2, x1, gate, sg, su, sd)


def _t5_bucket(dist):
    exact = N_BUCKETS // 2
    d = np.asarray(dist)
    large = exact + (np.log(np.maximum(d, 1) / exact) / np.log(REL_MAX_DIST / exact) * (N_BUCKETS - exact)).astype(np.int32)
    large = np.minimum(large, N_BUCKETS - 1)
    return np.where(d < exact, d, large).astype(np.int32)


def _group_bias(rel_bias, gi):
    window, dil = ATTN_GROUPS[gi]
    bkt = _t5_bucket(np.arange(window // dil + 1) * dil)
    onehot = jnp.asarray(bkt[:, None] == np.arange(N_BUCKETS)[None, :], F32)
    cols = rel_bias[:, gi * HEADS:(gi + 1) * HEADS].astype(F32)
    return jnp.dot(onehot, cols, precision=HIGHEST).T


def _prompt_bias_table(bias_j):
    blk = ATTN_BLOCK
    n = bias_j.shape[1] - 1
    period = 3 * blk
    row0 = jnp.concatenate([jnp.flip(bias_j, axis=1), jnp.full((HEADS, period - n - 1), NEG, F32)], axis=1)
    flat = jnp.tile(row0, (1, blk))[:, :blk * (period - 1)]
    tab = flat.reshape(HEADS, blk, period - 1)[:, :, :2 * blk]
    return tab.reshape(HEADS // HEADS_PER_SLAB, HEADS_PER_SLAB * blk, 2 * blk)


def _decode_bias_tables(bias_j):
    old = jnp.flip(bias_j[:, 1:], axis=1).T[:, :, None]
    new = bias_j[:, 0][:, None]
    return old, new


def _block_schedule(counts, na):
    te = EXPERT_TILE
    n_blocks = na // te + N_EXPERTS
    pcounts = (counts + te - 1) // te * te
    pends = jnp.cumsum(pcounts)
    offs = pends - pcounts
    blk_start = jnp.arange(n_blocks, dtype=jnp.int32) * te
    blk_exp = jnp.minimum(jnp.sum(pends[None, :] <= blk_start[:, None], axis=1), N_EXPERTS - 1).astype(jnp.int32)
    n_used = (pends[-1] // te).astype(jnp.int32).reshape(1)
    last_blk = jnp.where(counts > 0, pends // te - 1, -1).astype(jnp.int32)
    return offs, blk_exp, n_used, last_blk, n_blocks


def kernel(x_prompt, x_sample, cache_k_w128, cache_v_w128, cache_k_w512, cache_v_w512, cache_k_w2048, cache_v_w2048, state_pool, c_prompt, c_sample, ada_w, ada_b, norm1, norm2, w_in, q_gain, k_gain, rel_bias, pool_w, pool_scale, w_br_a, w_br_b, w_out, router_w, router_bias, exp_w_gate, exp_w_up, exp_w_down, sh_w_gate, sh_w_up, sh_w_down):
    depth = ada_w.shape[0]
    bsz, seq, d = x_prompt.shape
    db = x_sample.shape[0]
    assert x_sample.shape[1] == 1 and db <= SAMPLE_PAD and seq % (ATTN_BLOCK * ATTN_GROUPS[-1][1]) == 0
    k_caches = (cache_k_w128, cache_k_w512, cache_k_w2048)
    v_caches = (cache_v_w128, cache_v_w512, cache_v_w2048)
    for c, (w, dil) in zip(k_caches, ATTN_GROUPS):
        assert c.shape[2] == w == ATTN_STEPS * dil

    bias_js = [_group_bias(rel_bias, g) for g in range(N_GROUPS)]
    bias_tabs = jnp.stack([_prompt_bias_table(b) for b in bias_js])
    dec = [_decode_bias_tables(b) for b in bias_js]
    bdec = jnp.stack([t[0] for t in dec])
    bnew = jnp.stack([t[1] for t in dec])
    bd = jnp.asarray((np.arange(MXU_DIM)[:, None] // HEAD_DIM == np.arange(MXU_DIM)[None, :] // HEAD_DIM)
                     / HEAD_DIM, BF16)
    wcols = np.repeat(np.asarray(POOL_WINDOWS), POOL_GROUP)
    pw_dec = jnp.asarray(np.where(np.arange(POOL_STATE + 1)[:, None] >= POOL_STATE + 1 - wcols[None, :],
                                  1.0 / wcols[None, :], 0.0), F32)

    xp = x_prompt
    xs = jnp.pad(x_sample.reshape(db, d), ((0, SAMPLE_PAD - db), (0, 0))).reshape(1, SAMPLE_PAD, d)
    n_prompt = bsz * seq
    outs = {name: [] for name in ('pk', 'pv', 'sk', 'sv')}
    outs['pk'] = [[] for _ in range(N_GROUPS)]
    outs['pv'] = [[] for _ in range(N_GROUPS)]
    outs['sk'] = [[] for _ in range(N_GROUPS)]
    outs['sv'] = [[] for _ in range(N_GROUPS)]
    ppool, spool = [], []
    for l in range(depth):
        c_all = jnp.concatenate([c_prompt, c_sample], axis=0)
        mod = _ada_call(c_all, ada_w[l], ada_b[l])
        mod_p = [m.reshape(bsz, 1, d) for m in jnp.split(mod[:bsz], 6, axis=-1)]
        mod_s = [jnp.pad(m, ((0, SAMPLE_PAD - db), (0, 0))).reshape(1, SAMPLE_PAD, d)
                 for m in jnp.split(mod[bsz:], 6, axis=-1)]
        w_in_bf = w_in[l].astype(BF16)
        qg = (jnp.tile(q_gain[l], (1, HEADS)) * (HEAD_DIM ** -0.5)).reshape(1, -1)
        kg = jnp.tile(k_gain[l], (1, HEADS)).reshape(1, -1)
        g1 = norm1[l].reshape(1, d)
        g2 = norm2[l].reshape(1, d)
        wa, wb, wo = (w_br_a[l].astype(BF16), w_br_b[l].astype(BF16), w_out[l].astype(BF16))
        pw = pool_w[l].astype(BF16)
        ps = pool_scale[l].reshape(1, -1)

        q, k, v, u, gates = _inproj_call(xp, mod_p[0], mod_p[1], g1, w_in_bf, qg, kg, bd)
        oa_p = _attn_call(q, k, v, bias_tabs)
        for g, (window, dil) in enumerate(ATTN_GROUPS):
            keep = min(window, seq)
            cs = slice(g * GROUP_W, (g + 1) * GROUP_W)
            outs['pk'][g].append(k[:, seq - keep:, cs].reshape(bsz, keep, HEADS, HEAD_DIM))
            outs['pv'][g].append(v[:, seq - keep:, cs].reshape(bsz, keep, HEADS, HEAD_DIM))
        ppool.append(u[:, seq - POOL_STATE:])
        x1p, h2p = _merge_call(xp, gates, (oa_p, u), (mod_p[2], mod_p[3], mod_p[4]),
                               g2, wa, wb, wo, pw, ps, prompt=True)

        qs, ks, vs, us, gates_s = _inproj_call(xs, mod_s[0], mod_s[1], g1, w_in_bf, qg, kg, bd)
        heads3 = lambda t: t[0, :db].reshape(db, N_GROUPS * HEADS, HEAD_DIM)
        q3, k3, v3 = heads3(qs), heads3(ks), heads3(vs)
        kcl = [c[l] for c in k_caches]
        vcl = [c[l] for c in v_caches]
        oa_s, pooled_s = _decode_call(q3, k3, v3, us[0, :db], state_pool[l], kcl, vcl, bdec, bnew, pw_dec)
        padrows = lambda t: jnp.pad(t.reshape(db, GROUP_W), ((0, SAMPLE_PAD - db), (0, 0))).reshape(1, SAMPLE_PAD, GROUP_W)
        x1s, h2s = _merge_call(xs, gates_s, (padrows(oa_s), padrows(pooled_s)), (mod_s[2], mod_s[3], mod_s[4]),
                               g2, wa, wb, wo, pw, ps, prompt=False)
        for g in range(N_GROUPS):
            hs = slice(g * HEADS, (g + 1) * HEADS)
            outs['sk'][g].append(_roll_call(kcl[g], k3[:, None, hs]))
            outs['sv'][g].append(_roll_call(vcl[g], v3[:, None, hs]))
        spool.append(jnp.concatenate([state_pool[l][:, 1:], us[0, :db, None, :]], axis=1))

        h2 = jnp.concatenate([h2p.reshape(n_prompt * ROW_SLABS, LANES), h2s.reshape(SAMPLE_PAD * ROW_SLABS, LANES)], axis=0)
        nt = n_prompt + SAMPLE_PAD
        idx, wts, rank, counts = _router_call(h2, router_w[l], router_bias[l])
        offs, blk_exp, n_used, last_blk, n_blocks = _block_schedule(counts[0], nt * TOP_K)
        pos = _slot_call(idx, rank, offs.astype(F32).reshape(1, -1))[:, :TOP_K]
        xs_sorted = _dispatch_call(jnp.concatenate([last_blk, n_used]),
                                   pos.reshape(nt // ROW_TILE, 1, ROW_TILE * TOP_K), h2, n_blocks)
        y_sorted = _expert_call(blk_exp, n_used, xs_sorted, exp_w_gate[l], exp_w_up[l], exp_w_down[l])
        tc = COMBINE_TILE
        pos_tiles = pos.reshape(nt // tc, tc, TOP_K).transpose(0, 2, 1).reshape(nt // tc, 1, TOP_K * tc)
        sg, su, sd = sh_w_gate[l].astype(BF16), sh_w_up[l].astype(BF16), sh_w_down[l].astype(BF16)
        xp = _combine_call(pos_tiles, 0, y_sorted, wts, h2, x1p, mod_p[5], sg, su, sd)
        xs = _combine_call(pos_tiles, n_prompt // tc, y_sorted, wts, h2, x1s, mod_s[5], sg, su, sd)

    stack = lambda parts: jnp.stack(parts)
    res = [xp, xs[0, :db].reshape(db, 1, d)]
    for g in range(N_GROUPS):
        res += [stack(outs['pk'][g]), stack(outs['pv'][g])]
    res.append(stack(ppool))
    for g in range(N_GROUPS):
        res += [stack(outs['sk'][g]), stack(outs['sv'][g])]
    res.append(stack(spool))
    return tuple(res)
```

```python
import functools

import numpy as np
import jax
import jax.numpy as jnp
from jax import lax
from jax.experimental import pallas as pl
from jax.experimental.pallas import tpu as pltpu

F32 = jnp.float32
BF16 = jnp.bfloat16
HIGHEST = lax.Precision.HIGHEST

HEAD_DIM = 64
HEADS = 8
GROUP_W = HEADS * HEAD_DIM
ATTN_GROUPS = ((128, 1), (512, 4), (2048, 16))
N_GROUPS = len(ATTN_GROUPS)
ATTN_STEPS = 128
ATTN_BLOCK = 128
POOL_WINDOWS = (2, 4, 8, 16)
POOL_GROUP = 128
POOL_STATE = 15
N_BUCKETS = 32
REL_MAX_DIST = 2048
N_EXPERTS = 256
TOP_K = 8
N_EXPERT_GROUPS = 8
TOPK_GROUPS = 4
ROUTED_SCALE = 2.5
EPS = 1e-6
NEG = -1e30

LANES = 128
MXU_DIM = 256
VMEM_LIMIT = 56 * 1024 * 1024

INPROJ_TILE = 512
ROW_TILE = 256
EXPERT_TILE = 256
EXPERT_RING = 3
COMBINE_TILE = 128
SAMPLE_PAD = 256


def _sigmoid(x):
    return 1.0 / (1.0 + jnp.exp(-x))


def _params(sem):
    return pltpu.CompilerParams(dimension_semantics=sem, vmem_limit_bytes=VMEM_LIMIT)


ROW_SLABS = 8


def _load_rows(ref, n, first=0):
    return jnp.concatenate([ref[pl.ds(first * ROW_SLABS + s, n, stride=ROW_SLABS), :]
                            for s in range(ROW_SLABS)], axis=1)


def _store_rows(ref, val):
    n = val.shape[0]
    for s in range(ROW_SLABS):
        ref[pl.ds(s, n, stride=ROW_SLABS), :] = val[:, s * LANES:(s + 1) * LANES]


def _ada_body(c_ref, w_ref, b_ref, o_ref):
    c = c_ref[...]
    s = c * _sigmoid(c)
    o_ref[...] = jnp.dot(s, w_ref[...], precision=HIGHEST, preferred_element_type=F32) + b_ref[...]


def _ada_call(c, w, b):
    n, d = c.shape
    cols = w.shape[1]
    tn = 512
    return pl.pallas_call(
        _ada_body,
        grid=(cols // tn,),
        in_specs=[pl.BlockSpec((n, d), lambda j: (0, 0)),
                  pl.BlockSpec((d, tn), lambda j: (0, j)),
                  pl.BlockSpec((1, tn), lambda j: (0, j))],
        out_specs=pl.BlockSpec((n, tn), lambda j: (0, j)),
        out_shape=jax.ShapeDtypeStruct((n, cols), F32),
        compiler_params=_params(("arbitrary",)),
        name="adaln",
    )(c, w, b.reshape(1, cols))


def _inproj_body(x_ref, shift_ref, scale_ref, g_ref, w_ref, qg_ref, kg_ref, bd_ref,
                 q_ref, k_ref, v_ref, u_ref, gt_ref):
    x = x_ref[0]
    ms = jnp.mean(x * x, axis=-1, keepdims=True)
    h = x * lax.rsqrt(ms + EPS) * g_ref[...]
    h = h * (1.0 + scale_ref[0]) + shift_ref[0]
    hb = h.astype(BF16)
    qkv = N_GROUPS * GROUP_W
    n_chunks = w_ref.shape[1] // GROUP_W
    for c in range(n_chunks):
        z = jnp.dot(hb, w_ref[:, c * GROUP_W:(c + 1) * GROUP_W], preferred_element_type=F32)
        if c < 2 * N_GROUPS:
            zz = (z * z).astype(BF16)
            hms = jnp.concatenate(
                [jnp.dot(zz[:, s:s + MXU_DIM], bd_ref[...], preferred_element_type=F32)
                 for s in range(0, GROUP_W, MXU_DIM)], axis=1)
            g = c % N_GROUPS
            gain = (qg_ref if c < N_GROUPS else kg_ref)[:, g * GROUP_W:(g + 1) * GROUP_W]
            y = z * lax.rsqrt(hms + EPS) * gain
            if c < N_GROUPS:
                q_ref[0, :, g * GROUP_W:(g + 1) * GROUP_W] = y
            else:
                k_ref[0, :, g * GROUP_W:(g + 1) * GROUP_W] = y
        elif c < 3 * N_GROUPS:
            g = c - 2 * N_GROUPS
            v_ref[0, :, g * GROUP_W:(g + 1) * GROUP_W] = z
        elif c == 3 * N_GROUPS:
            u_ref[0] = z
        else:
            g = c - 3 * N_GROUPS - 1
            gt_ref[0, :, g * GROUP_W:(g + 1) * GROUP_W] = z.astype(BF16)
    del qkv


def _mod_spec(mod, tm, d):
    if mod.shape[1] == 1:
        return pl.BlockSpec((1, 1, d), lambda b, i: (b, 0, 0))
    return pl.BlockSpec((1, tm, d), lambda b, i: (b, i, 0))


def _inproj_call(x, shift, scale, g1, w_bf, qg, kg, bd):
    bsz, s, d = x.shape
    tm = min(INPROJ_TILE, s)
    qkv = N_GROUPS * GROUP_W
    n_gate = w_bf.shape[1] - 3 * qkv - GROUP_W
    const = lambda b, i: (0, 0)
    row = lambda width: pl.BlockSpec((1, tm, width), lambda b, i: (b, i, 0))
    return pl.pallas_call(
        _inproj_body,
        grid=(bsz, s // tm),
        in_specs=[row(d), _mod_spec(shift, tm, d), _mod_spec(scale, tm, d),
                  pl.BlockSpec((1, d), const),
                  pl.BlockSpec(w_bf.shape, const, pipeline_mode=pl.Buffered(1)),
                  pl.BlockSpec((1, qkv), const), pl.BlockSpec((1, qkv), const),
                  pl.BlockSpec((MXU_DIM, MXU_DIM), const)],
        out_specs=[row(qkv), row(qkv), row(qkv), row(GROUP_W), row(n_gate)],
        out_shape=[jax.ShapeDtypeStruct((bsz, s, qkv), F32),
                   jax.ShapeDtypeStruct((bsz, s, qkv), F32),
                   jax.ShapeDtypeStruct((bsz, s, qkv), F32),
                   jax.ShapeDtypeStruct((bsz, s, GROUP_W), F32),
                   jax.ShapeDtypeStruct((bsz, s, n_gate), BF16)],
        compiler_params=_params(("arbitrary", "arbitrary")),
        name="inproj",
    )(x, shift, scale, g1, w_bf, qg, kg, bd)


ATTN_ROWS = ATTN_BLOCK * ATTN_GROUPS[-1][1]
HEADS_PER_SLAB = LANES // HEAD_DIM


def _attn_body(*refs):
    ng = N_GROUPS
    qkv = refs[:3 * ng]
    bias_ref, o_ref = refs[3 * ng], refs[3 * ng + 1]
    scr = refs[3 * ng + 2:]
    first = pl.program_id(2) == 0
    blk = ATTN_BLOCK
    pb = ATTN_ROWS
    hs = HEADS_PER_SLAB
    lane = lax.broadcasted_iota(jnp.int32, (blk, LANES), 1)
    col = lax.broadcasted_iota(jnp.int32, (hs * blk, 2 * blk), 1)
    for g, (_, dil) in enumerate(ATTN_GROUPS):
        q_ref, k_ref, v_ref = qkv[3 * g:3 * g + 3]
        kext, vext, og, lg = scr[4 * g:4 * g + 4]
        span = blk * dil
        bias = bias_ref[g, 0]

        @pl.when(first)
        def _():
            kext[0:span, :] = jnp.zeros((span, LANES), F32)
            vext[0:span, :] = jnp.zeros((span, LANES), F32)
        kext[span:span + pb, :] = k_ref[0]
        vext[span:span + pb, :] = v_ref[0]

        rows = lambda start: pl.ds(start, blk, stride=dil) if dil > 1 else pl.ds(pl.multiple_of(start, blk), blk)
        per = 4

        def units(i, carry):
            bases, v2s, scores = [], [], []
            for j in range(per):
                u = per * i + j
                qb = u // dil
                base = qb * span + (u % dil)
                q = q_ref[0, rows(base), :]
                k2 = jnp.concatenate([kext[rows(base), :], kext[rows(base + span), :]], axis=0).astype(BF16)
                v2s.append(jnp.concatenate([vext[rows(base), :], vext[rows(base + span), :]], axis=0).astype(BF16))
                qs = jnp.concatenate([jnp.where(lane // HEAD_DIM == h, q, 0.0) for h in range(hs)],
                                     axis=0).astype(BF16)
                s = lax.dot_general(qs, k2, (((1,), (1,)), ((), ())), preferred_element_type=F32) + bias
                scores.append(jnp.where(jnp.logical_and(jnp.logical_and(first, qb == 0), col < blk), NEG, s))
                bases.append(base)
            s = jnp.concatenate(scores, axis=0)
            m = jnp.max(s, axis=-1, keepdims=True)
            p = jnp.exp(s - m)
            l = jnp.sum(p, axis=-1, keepdims=True)
            pb16 = p.astype(BF16)
            lse = m + jnp.log(l)
            for j in range(per):
                r0 = j * hs * blk
                o2 = jnp.dot(pb16[r0:r0 + hs * blk], v2s[j], preferred_element_type=F32) / l[r0:r0 + hs * blk]
                o = o2[0:blk]
                lw = jnp.broadcast_to(lse[r0:r0 + blk], (blk, LANES))
                for h in range(1, hs):
                    o = jnp.where(lane // HEAD_DIM == h, o2[h * blk:(h + 1) * blk], o)
                    lw = jnp.where(lane // HEAD_DIM == h, lse[r0 + h * blk:r0 + (h + 1) * blk], lw)
                og[rows(bases[j]), :] = o
                lg[rows(bases[j]), :] = lw
            return carry
        lax.fori_loop(0, pb // blk // per, units, 0)
        kext[0:span, :] = kext[pb:pb + span, :]
        vext[0:span, :] = vext[pb:pb + span, :]

    def merge(c, carry):
        rs = pl.ds(pl.multiple_of(c * blk, blk), blk)
        ls = [scr[4 * g + 3][rs, :] for g in range(ng)]
        mx = jnp.maximum(jnp.maximum(ls[0], ls[1]), ls[2])
        es = [jnp.exp(t - mx) for t in ls]
        den = es[0] + es[1] + es[2]
        acc = (es[0] / den) * scr[2][rs, :]
        for g in range(1, ng):
            acc = acc + (es[g] / den) * scr[4 * g + 2][rs, :]
        o_ref[0, rs, :] = acc.astype(o_ref.dtype)
        return carry
    lax.fori_loop(0, pb // blk, merge, 0)


def _attn_call(q, k, v, bias_tabs):
    bsz, s, qkv = q.shape
    pb = ATTN_ROWS
    n_slab = GROUP_W // LANES
    in_specs, args = [], []
    for g in range(N_GROUPS):
        spec = pl.BlockSpec((1, pb, LANES), lambda b, c, i, g=g: (b, i, g * n_slab + c))
        in_specs += [spec, spec, spec]
        args += [q, k, v]
    in_specs.append(pl.BlockSpec((N_GROUPS, 1) + bias_tabs.shape[2:], lambda b, c, i: (0, c, 0, 0)))
    scratch = []
    for _, dil in ATTN_GROUPS:
        ext = pltpu.VMEM((ATTN_BLOCK * dil + pb, LANES), F32)
        scratch += [ext, ext, pltpu.VMEM((pb, LANES), F32), pltpu.VMEM((pb, LANES), F32)]
    return pl.pallas_call(
        _attn_body,
        grid=(bsz, n_slab, s // pb),
        in_specs=in_specs,
        out_specs=pl.BlockSpec((1, pb, LANES), lambda b, c, i: (b, i, c)),
        out_shape=jax.ShapeDtypeStruct((bsz, s, GROUP_W), BF16),
        scratch_shapes=scratch,
        compiler_params=_params(("arbitrary", "arbitrary", "arbitrary")),
        name="attn",
    )(*args, bias_tabs)


def _decode_body(q_ref, kn_ref, vn_ref, u_ref, pool_ref, kc0, vc0, kc1, vc1, kc2, vc2,
                 bdec_ref, bnew_ref, pw_ref, oa_ref, pooled_ref):
    kcs = (kc0, kc1, kc2)
    vcs = (vc0, vc1, vc2)
    outs, lses = [], []
    for g in range(N_GROUPS):
        hs = slice(g * HEADS, (g + 1) * HEADS)
        q = q_ref[0, hs, :]
        kn = kn_ref[0, hs, :]
        vn = vn_ref[0, hs, :]
        kc = kcs[g][0]
        vc = vcs[g][0]
        s_old = jnp.sum(kc * q[None], axis=-1, keepdims=True) + bdec_ref[g]
        s_new = jnp.sum(kn * q, axis=-1, keepdims=True) + bnew_ref[g]
        m = jnp.maximum(jnp.max(s_old, axis=0), s_new)
        p_old = jnp.exp(s_old - m[None])
        p_new = jnp.exp(s_new - m)
        l = jnp.sum(p_old, axis=0) + p_new
        outs.append((jnp.sum(p_old * vc, axis=0) + p_new * vn) / l)
        lses.append(m + jnp.log(l))
    mx = jnp.maximum(jnp.maximum(lses[0], lses[1]), lses[2])
    es = [jnp.exp(t - mx) for t in lses]
    den = es[0] + es[1] + es[2]
    oa_ref[0] = (es[0] / den) * outs[0] + (es[1] / den) * outs[1] + (es[2] / den) * outs[2]
    u = u_ref[0]
    stored = jnp.sum(pool_ref[0] * pw_ref[0:POOL_STATE, :], axis=0, keepdims=True)
    pooled_ref[0] = stored + u * pw_ref[POOL_STATE:POOL_STATE + 1, :] - u


def _decode_call(q, kn, vn, u, pool_prev, k_caches, v_caches, bdec, bnew, pw):
    db = pool_prev.shape[0]
    per_b = lambda t: pl.BlockSpec((1,) + t.shape[1:], lambda b: (b,) + (0,) * (t.ndim - 1))
    const = lambda t: pl.BlockSpec(t.shape, lambda b: (0,) * t.ndim)
    cache_specs, cache_args = [], []
    for g, (_, dil) in enumerate(ATTN_GROUPS):
        for c in (k_caches[g], v_caches[g]):
            rows = c.shape[1]
            cache_args.append(c.reshape(db, rows // dil, dil, HEADS, HEAD_DIM))
            cache_specs.append(pl.BlockSpec((1, ATTN_STEPS, None, HEADS, HEAD_DIM), lambda b: (b, 0, 0, 0, 0)))
    u3 = u.reshape(db, 1, u.shape[-1])
    return pl.pallas_call(
        _decode_body,
        grid=(db,),
        in_specs=[per_b(q), per_b(kn), per_b(vn), per_b(u3), per_b(pool_prev)] + cache_specs +
                 [const(bdec), const(bnew), const(pw)],
        out_specs=[pl.BlockSpec((1, HEADS, HEAD_DIM), lambda b: (b, 0, 0)),
                   pl.BlockSpec((1, 1, GROUP_W), lambda b: (b, 0, 0))],
        out_shape=[jax.ShapeDtypeStruct((db, HEADS, HEAD_DIM), F32),
                   jax.ShapeDtypeStruct((db, 1, GROUP_W), F32)],
        compiler_params=_params(("arbitrary",)),
        name="decode_mix",
    )(q, kn, vn, u3, pool_prev, *cache_args, bdec, bnew, pw)


def _roll_body(c_ref, nxt_ref, new_ref, o_ref):
    ch = c_ref.shape[1]
    last = pl.program_id(1) == pl.num_programs(1) - 1
    o_ref[0, 0:ch - 1] = c_ref[0, 1:ch]
    o_ref[0, ch - 1:ch] = jnp.where(last, new_ref[0], nxt_ref[0])


def _roll_call(cache, new):
    db, rows, nh, e = cache.shape
    ch = min(rows, 1024)
    return pl.pallas_call(
        _roll_body,
        grid=(db, rows // ch),
        in_specs=[pl.BlockSpec((1, ch, nh, e), lambda b, i: (b, i, 0, 0)),
                  pl.BlockSpec((1, 1, nh, e), lambda b, i: (b, jnp.minimum((i + 1) * ch, rows - 1), 0, 0)),
                  pl.BlockSpec((1, 1, nh, e), lambda b, i: (b, 0, 0, 0))],
        out_specs=pl.BlockSpec((1, ch, nh, e), lambda b, i: (b, i, 0, 0)),
        out_shape=jax.ShapeDtypeStruct(cache.shape, cache.dtype),
        compiler_params=_params(("arbitrary", "arbitrary")),
        name="cache_roll",
    )(cache, cache, new)


def _merge_tail(x, oa, pooled, gates, gate_msa, shift2, scale2, g2, wa, wb, wo, pw_ref, ps,
                x1_ref, h2_ref):
    mixed = jnp.concatenate(
        [jnp.dot(pooled[:, j * POOL_GROUP:(j + 1) * POOL_GROUP].astype(BF16), pw_ref[j],
                 preferred_element_type=F32) for j in range(len(POOL_WINDOWS))], axis=1) * ps
    d = x.shape[-1]
    ga = gates[:, :d].astype(F32)
    gb = gates[:, d:].astype(F32)
    merged = (_sigmoid(ga) * jnp.dot(oa.astype(BF16), wa, preferred_element_type=F32) +
              _sigmoid(gb) * jnp.dot(mixed.astype(BF16), wb, preferred_element_type=F32))
    y = jnp.dot(merged.astype(BF16), wo, preferred_element_type=F32)
    x1 = x + gate_msa * y
    x1_ref[0] = x1
    ms = jnp.mean(x1 * x1, axis=-1, keepdims=True)
    h2 = x1 * lax.rsqrt(ms + EPS) * g2
    _store_rows(h2_ref.at[0], h2 * (1.0 + scale2) + shift2)


def _merge_prompt_body(x_ref, gt_ref, oa_ref, u_ref, uh_ref,
                       gm_ref, sh_ref, sc_ref, g2_ref, wa_ref, wb_ref, wo_ref, pw_ref, ps_ref,
                       x1_ref, h2_ref):
    i = pl.program_id(1)
    tm = x_ref.shape[1]
    oa = oa_ref[0]
    u = u_ref[0]
    halo = jnp.where(i == 0, 0.0, uh_ref[0])
    ext = jnp.concatenate([halo, u], axis=0)
    hw = halo.shape[0]
    acc = ext
    parts = []
    for j, w in enumerate(POOL_WINDOWS):
        acc = acc[:, POOL_GROUP * (1 if j else 0):]
        acc = acc + pltpu.roll(acc, w // 2, axis=0)
        parts.append(acc[hw:, :POOL_GROUP])
    tot = jnp.concatenate(parts, axis=1)
    lane = lax.broadcasted_iota(jnp.int32, (tm, GROUP_W), 1)
    wcol = jnp.left_shift(2, lane // POOL_GROUP)
    pos = i * tm + lax.broadcasted_iota(jnp.int32, (tm, GROUP_W), 0)
    cnt = jnp.minimum(pos + 1, wcol).astype(F32)
    pooled = tot / cnt - u
    _merge_tail(x_ref[0], oa, pooled, gt_ref[0], gm_ref[0], sh_ref[0], sc_ref[0], g2_ref[...],
                wa_ref[...], wb_ref[...], wo_ref[...], pw_ref, ps_ref[...], x1_ref, h2_ref)


def _merge_sample_body(x_ref, gt_ref, oa_ref, pooled_ref,
                       gm_ref, sh_ref, sc_ref, g2_ref, wa_ref, wb_ref, wo_ref, pw_ref, ps_ref,
                       x1_ref, h2_ref):
    _merge_tail(x_ref[0], oa_ref[0], pooled_ref[0], gt_ref[0], gm_ref[0], sh_ref[0], sc_ref[0], g2_ref[...],
                wa_ref[...], wb_ref[...], wo_ref[...], pw_ref, ps_ref[...], x1_ref, h2_ref)


def _merge_call(x, gates, mixer_inputs, mods, g2, wa, wb, wo, pw, ps, prompt):
    bsz, s, d = x.shape
    assert d == ROW_SLABS * LANES
    tm = min(ROW_TILE, s)
    row = lambda t: pl.BlockSpec((1, tm, t.shape[-1]), lambda b, i: (b, i, 0))
    const = lambda t: pl.BlockSpec(t.shape, lambda b, i: (0,) * t.ndim)
    gm, sh, sc = mods
    in_specs = [row(x), row(gates)]
    args = [x, gates]
    if prompt:
        oa, u = mixer_inputs
        hw = 16
        in_specs += [row(oa), row(u),
                     pl.BlockSpec((1, hw, u.shape[-1]), lambda b, i: (b, jnp.maximum(i * (tm // hw) - 1, 0), 0))]
        args += [oa, u, u]
        body = _merge_prompt_body
    else:
        in_specs += [row(t) for t in mixer_inputs]
        args += list(mixer_inputs)
        body = _merge_sample_body
    in_specs += [_mod_spec(m, tm, d) for m in (gm, sh, sc)]
    in_specs += [const(t) for t in (g2, wa, wb, wo, pw, ps)]
    args += [gm, sh, sc, g2, wa, wb, wo, pw, ps]
    return pl.pallas_call(
        body,
        grid=(bsz, s // tm),
        in_specs=in_specs,
        out_specs=[row(x), pl.BlockSpec((1, tm * ROW_SLABS, LANES), lambda b, i: (b, i, 0))],
        out_shape=[jax.ShapeDtypeStruct((bsz, s, d), F32),
                   jax.ShapeDtypeStruct((bsz, s * ROW_SLABS, LANES), F32)],
        compiler_params=_params(("arbitrary", "arbitrary")),
        name="merge_prompt" if prompt else "merge_sample",
    )(*args)


def _router_body(h_ref, rw_ref, rb_ref, tri_ref, idx_ref, wt_ref, rank_ref, cnt_ref, carry):
    tm = h_ref.shape[0] // ROW_SLABS

    @pl.when(pl.program_id(0) == 0)
    def _():
        carry[...] = jnp.zeros_like(carry)

    ne = rw_ref.shape[1]
    gsz = ne // N_EXPERT_GROUPS
    logits = jnp.dot(_load_rows(h_ref, tm), rw_ref[...], precision=HIGHEST, preferred_element_type=F32)
    scores = _sigmoid(logits)
    sel = scores + rb_ref[...]
    lane = lax.broadcasted_iota(jnp.int32, (tm, ne), 1)
    lanef = lane.astype(F32)
    grp = lane // gsz
    ninf = -jnp.inf
    far = float(ne)

    def first_max(x):
        m = jnp.max(x, axis=-1, keepdims=True)
        at = jnp.min(jnp.where(x == m, lanef, far), axis=-1, keepdims=True)
        return m, at

    gs = []
    for g in range(N_EXPERT_GROUPS):
        xg = jnp.where(grp == g, sel, ninf)
        m1, a1 = first_max(xg)
        m2 = jnp.max(jnp.where(lanef == a1, ninf, xg), axis=-1, keepdims=True)
        gs.append(m1 + m2)
    allowed = jnp.zeros((tm, ne), jnp.bool_)
    for g in range(N_EXPERT_GROUPS):
        ahead = jnp.zeros((tm, 1), F32)
        for o in range(N_EXPERT_GROUPS):
            if o == g:
                continue
            beats = (gs[o] > gs[g]) | ((gs[o] == gs[g]) & (o < g))
            ahead = ahead + beats.astype(F32)
        allowed = allowed | ((grp == g) & (ahead < TOPK_GROUPS))
    masked = jnp.where(allowed, sel, ninf)
    lane_o = lax.broadcasted_iota(jnp.int32, (tm, LANES), 1)
    idx_t = jnp.zeros((tm, LANES), F32)
    wt_t = jnp.zeros((tm, LANES), F32)
    wsum = jnp.zeros((tm, 1), F32)
    picked = jnp.zeros((tm, ne), F32)
    ats = []
    for k in range(TOP_K):
        _, at = first_max(masked)
        hit = lanef == at
        wk = jnp.sum(jnp.where(hit, scores, 0.0), axis=-1, keepdims=True)
        masked = jnp.where(hit, ninf, masked)
        picked = picked + hit.astype(F32)
        ats.append(at)
        idx_t = jnp.where(lane_o == k, at, idx_t)
        wt_t = jnp.where(lane_o == k, wk, wt_t)
        wsum = wsum + wk
    idx_ref[...] = idx_t.astype(jnp.int32)
    wt_ref[...] = wt_t / wsum * ROUTED_SCALE
    before = jnp.dot(tri_ref[...], picked.astype(BF16), preferred_element_type=F32) + carry[...]
    rank_t = jnp.zeros((tm, LANES), F32)
    for k in range(TOP_K):
        rk = jnp.sum(jnp.where(lanef == ats[k], before, 0.0), axis=-1, keepdims=True)
        rank_t = jnp.where(lane_o == k, rk, rank_t)
    rank_ref[...] = rank_t.astype(jnp.int32)
    carry[...] = carry[...] + jnp.sum(picked, axis=0, keepdims=True)
    cnt_ref[...] = carry[...].astype(jnp.int32)


def _router_call(h2, rw, rb):
    nt = h2.shape[0] // ROW_SLABS
    tm = ROW_TILE
    d, ne = rw.shape
    tri = jnp.asarray(np.tril(np.ones((tm, tm), np.float32), -1), BF16)
    tile = pl.BlockSpec((tm, LANES), lambda i: (i, 0))
    return pl.pallas_call(
        _router_body,
        grid=(nt // tm,),
        in_specs=[pl.BlockSpec((tm * ROW_SLABS, LANES), lambda i: (i, 0)),
                  pl.BlockSpec((d, ne), lambda i: (0, 0)),
                  pl.BlockSpec((1, ne), lambda i: (0, 0)),
                  pl.BlockSpec((tm, tm), lambda i: (0, 0))],
        out_specs=[tile, tile, tile, pl.BlockSpec((1, ne), lambda i: (0, 0))],
        out_shape=[jax.ShapeDtypeStruct((nt, LANES), jnp.int32),
                   jax.ShapeDtypeStruct((nt, LANES), F32),
                   jax.ShapeDtypeStruct((nt, LANES), jnp.int32),
                   jax.ShapeDtypeStruct((1, ne), jnp.int32)],
        scratch_shapes=[pltpu.VMEM((1, ne), F32)],
        compiler_params=_params(("arbitrary",)),
        name="router",
    )(h2, rw, rb.reshape(1, ne), tri)


def _slot_body(idx_ref, rank_ref, offs_ref, pos_ref):
    tm = idx_ref.shape[0]
    ne = offs_ref.shape[1]
    idx = idx_ref[...]
    rank = rank_ref[...]
    lane = lax.broadcasted_iota(jnp.int32, (tm, ne), 1)
    lane_o = lax.broadcasted_iota(jnp.int32, (tm, LANES), 1)
    offs = offs_ref[...]
    pos = jnp.zeros((tm, LANES), F32)
    for k in range(TOP_K):
        start = jnp.sum(jnp.where(lane == idx[:, k:k + 1], offs, 0.0), axis=-1, keepdims=True)
        pos = jnp.where(lane_o == k, start, pos)
    pos_ref[...] = pos.astype(jnp.int32) + rank


def _slot_call(idx, rank, offs):
    nt = idx.shape[0]
    tm = ROW_TILE
    ne = offs.shape[1]
    tile = pl.BlockSpec((tm, LANES), lambda i: (i, 0))
    return pl.pallas_call(
        _slot_body,
        grid=(nt // tm,),
        in_specs=[tile, tile, pl.BlockSpec((1, ne), lambda i: (0, 0))],
        out_specs=tile,
        out_shape=jax.ShapeDtypeStruct((nt, LANES), jnp.int32),
        compiler_params=_params(("arbitrary",)),
        name="slots",
    )(idx, rank, offs)


def _row_tile(ref, r):
    return ref.at[pl.ds(pl.multiple_of(r * ROW_SLABS, ROW_SLABS), ROW_SLABS), :]


def _dispatch_body(last_ref, pos_ref, h_ref, xs_hbm, zeros, zsem, sem):
    td = h_ref.shape[0] // ROW_SLABS
    te = zeros.shape[0] // ROW_SLABS

    @pl.when(pl.program_id(0) == 0)
    def _():
        zeros[...] = jnp.zeros_like(zeros)

        def fill_block(blk, wait):
            start = pl.multiple_of(blk * (te * ROW_SLABS), te * ROW_SLABS)
            cp = pltpu.make_async_copy(zeros, xs_hbm.at[pl.ds(start, te * ROW_SLABS), :], zsem)
            if wait:
                cp.wait()
            else:
                cp.start()

        def fill(e, wait):
            @pl.when(last_ref[e] >= 0)
            def _():
                fill_block(last_ref[e], wait)
        n_exp = last_ref.shape[0] - 1
        n_blocks = xs_hbm.shape[0] // (te * ROW_SLABS)
        for wait in (False, True):
            lax.fori_loop(0, n_exp, lambda e, c, wait=wait: (fill(e, wait), c)[1], 0)
            lax.fori_loop(last_ref[n_exp], n_blocks, lambda b, c, wait=wait: (fill_block(b, wait), c)[1], 0)

    def one(t, carry):
        for k in range(TOP_K):
            pltpu.make_async_copy(_row_tile(h_ref, t), _row_tile(xs_hbm, pos_ref[0, 0, t * TOP_K + k]),
                                  sem).start(priority=k % 2)
        return carry
    lax.fori_loop(0, td, one, 0, unroll=2)
    for k in range(TOP_K):
        pltpu.make_async_copy(h_ref, xs_hbm.at[pl.ds(0, td * ROW_SLABS), :], sem).wait()


def _dispatch_call(last_blk, pos_tok, h2, n_blocks):
    nt = h2.shape[0] // ROW_SLABS
    td = ROW_TILE
    te = EXPERT_TILE
    grid_spec = pltpu.PrefetchScalarGridSpec(
        num_scalar_prefetch=1,
        grid=(nt // td,),
        in_specs=[pl.BlockSpec((1, 1, td * TOP_K), lambda i, lb: (i, 0, 0), memory_space=pltpu.SMEM),
                  pl.BlockSpec((td * ROW_SLABS, LANES), lambda i, lb: (i, 0))],
        out_specs=pl.BlockSpec(memory_space=pl.ANY),
        scratch_shapes=[pltpu.VMEM((te * ROW_SLABS, LANES), F32), pltpu.SemaphoreType.DMA, pltpu.SemaphoreType.DMA],
    )
    return pl.pallas_call(
        _dispatch_body,
        grid_spec=grid_spec,
        out_shape=jax.ShapeDtypeStruct((n_blocks * te * ROW_SLABS, LANES), F32),
        compiler_params=_params(("arbitrary",)),
        name="dispatch",
    )(last_blk, pos_tok, h2)


def _expert_body(exp_ref, nu_ref, xs_hbm, wg_ref, wu_ref, wd_ref, y_ref, xbuf, xsem, wgb, wub, wdb):
    b = pl.program_id(0)
    n_used = nu_ref[0]
    live = b < n_used
    new_expert = jnp.logical_or(b == 0, exp_ref[b] != exp_ref[jnp.maximum(b - 1, 0)])
    ring, rows = xbuf.shape[0], xbuf.shape[1]

    def fetch(blk):
        slot = blk % ring
        return pltpu.make_async_copy(xs_hbm.at[pl.ds(pl.multiple_of(blk * rows, rows), rows), :],
                                     xbuf.at[slot], xsem.at[slot])

    @pl.when(b == 0)
    def _():
        for j in range(ring - 1):
            @pl.when(j < n_used)
            def _():
                fetch(j).start()

    @pl.when(b + ring - 1 < n_used)
    def _():
        fetch(b + ring - 1).start()

    @pl.when(jnp.logical_and(live, new_expert))
    def _():
        wgb[...] = wg_ref[0].astype(BF16)
        wub[...] = wu_ref[0].astype(BF16)
        wdb[...] = wd_ref[0].astype(BF16)

    @pl.when(live)
    def _():
        te = rows // ROW_SLABS
        fetch(b).wait()
        x = _load_rows(xbuf.at[b % ring], te).astype(BF16)
        gt = jnp.dot(x, wgb[...], preferred_element_type=F32)
        up = jnp.dot(x, wub[...], preferred_element_type=F32)
        act = (gt * _sigmoid(gt)) * up
        _store_rows(y_ref, jnp.dot(act.astype(BF16), wdb[...], preferred_element_type=F32))

    @pl.when(jnp.logical_not(live))
    def _():
        y_ref[...] = jnp.zeros_like(y_ref)


def _expert_call(blk_exp, n_used, xs, wg, wu, wd):
    n_blocks = blk_exp.shape[0]
    te = EXPERT_TILE
    d, hid = wg.shape[-2:]
    used = lambda b, nu: jnp.minimum(b, nu[0] - 1)
    wspec = lambda r, c: pl.BlockSpec((1, r, c), lambda b, e, nu: (e[used(b, nu)], 0, 0))
    grid_spec = pltpu.PrefetchScalarGridSpec(
        num_scalar_prefetch=2,
        grid=(n_blocks,),
        in_specs=[pl.BlockSpec(memory_space=pl.ANY), wspec(d, hid), wspec(d, hid), wspec(hid, d)],
        out_specs=pl.BlockSpec((te * ROW_SLABS, LANES), lambda b, e, nu: (b, 0)),
        scratch_shapes=[pltpu.VMEM((EXPERT_RING, te * ROW_SLABS, LANES), F32), pltpu.SemaphoreType.DMA((EXPERT_RING,)),
                        pltpu.VMEM((d, hid), BF16), pltpu.VMEM((d, hid), BF16), pltpu.VMEM((hid, d), BF16)],
    )
    return pl.pallas_call(
        _expert_body,
        grid_spec=grid_spec,
        out_shape=jax.ShapeDtypeStruct(xs.shape, F32),
        compiler_params=_params(("arbitrary",)),
        name="expert_ffn",
    )(blk_exp, n_used, xs, wg, wu, wd)


def _gather_rows(idx_ref, src_hbm, dst, sem, n):
    def pair(i, carry):
        for p in range(2):
            j = 2 * i + p
            pltpu.make_async_copy(_row_tile(src_hbm, idx_ref[0, 0, j]), _row_tile(dst, j), sem).start(priority=p)
        return carry
    lax.fori_loop(0, n // 2, pair, 0, unroll=4)


def _combine_body(pos_ref, posn_ref, y_hbm, wt_ref, h_ref, x1_ref, gate_ref, sg_ref, su_ref, sd_ref,
                  out_ref, ybuf, sem):
    nb, ni = pl.num_programs(0), pl.num_programs(1)
    step = pl.program_id(0) * ni + pl.program_id(1)
    slot = step % 2
    tc = h_ref.shape[0] // ROW_SLABS
    rows = tc * TOP_K

    @pl.when(step == 0)
    def _():
        _gather_rows(pos_ref, y_hbm, ybuf.at[0], sem.at[0], rows)

    @pl.when(step + 1 < nb * ni)
    def _():
        _gather_rows(posn_ref, y_hbm, ybuf.at[1 - slot], sem.at[1 - slot], rows)

    pltpu.make_async_copy(y_hbm.at[pl.ds(0, rows * ROW_SLABS), :], ybuf.at[slot], sem.at[slot]).wait()
    wt = wt_ref[...]
    routed = jnp.zeros(out_ref.shape[1:], F32)
    for k in range(TOP_K):
        routed = routed + wt[:, k:k + 1] * _load_rows(ybuf.at[slot], tc, first=k * tc)
    hb = _load_rows(h_ref, tc).astype(BF16)
    gt = jnp.dot(hb, sg_ref[...], preferred_element_type=F32)
    up = jnp.dot(hb, su_ref[...], preferred_element_type=F32)
    shared = jnp.dot(((gt * _sigmoid(gt)) * up).astype(BF16), sd_ref[...], preferred_element_type=F32)
    out_ref[0] = x1_ref[0] + gate_ref[0] * (routed + shared)


def _combine_call(pos_tiles, tile_off, y_sorted, wts, h2, x1, gate, sg, su, sd):
    bsz, s, d = x1.shape
    tc = min(COMBINE_TILE, s)
    ni = s // tc
    n_tiles = pos_tiles.shape[0]
    lin = lambda b, i: tile_off + b * ni + i
    const = lambda t: pl.BlockSpec(t.shape, lambda b, i: (0,) * t.ndim)
    return pl.pallas_call(
        _combine_body,
        grid=(bsz, ni),
        in_specs=[
            pl.BlockSpec((1, 1, tc * TOP_K), lambda b, i: (lin(b, i), 0, 0), memory_space=pltpu.SMEM),
            pl.BlockSpec((1, 1, tc * TOP_K), lambda b, i: (jnp.minimum(lin(b, i) + 1, n_tiles - 1), 0, 0),
                         memory_space=pltpu.SMEM),
            pl.BlockSpec(memory_space=pl.ANY),
            pl.BlockSpec((tc, LANES), lambda b, i: (lin(b, i), 0)),
            pl.BlockSpec((tc * ROW_SLABS, LANES), lambda b, i: (lin(b, i), 0)),
            pl.BlockSpec((1, tc, d), lambda b, i: (b, i, 0)),
            _mod_spec(gate, tc, d),
            const(sg), const(su), const(sd),
        ],
        out_specs=pl.BlockSpec((1, tc, d), lambda b, i: (b, i, 0)),
        out_shape=jax.ShapeDtypeStruct((bsz, s, d), F32),
        scratch_shapes=[pltpu.VMEM((2, tc * TOP_K * ROW_SLABS, LANES), F32), pltpu.SemaphoreType.DMA((2,))],
        compiler_params=_params(("arbitrary", "arbitrary")),
        name="combine",
    )(pos_tiles, pos_tiles, y_sorted, wts, h2, x1, gate, sg, su, sd)


def _t5_bucket(dist):
    exact = N_BUCKETS // 2
    d = np.asarray(dist)
    large = exact + (np.log(np.maximum(d, 1) / exact) / np.log(REL_MAX_DIST / exact) * (N_BUCKETS - exact)).astype(np.int32)
    large = np.minimum(large, N_BUCKETS - 1)
    return np.where(d < exact, d, large).astype(np.int32)


def _group_bias(rel_bias, gi):
    window, dil = ATTN_GROUPS[gi]
    bkt = _t5_bucket(np.arange(window // dil + 1) * dil)
    onehot = jnp.asarray(bkt[:, None] == np.arange(N_BUCKETS)[None, :], F32)
    cols = rel_bias[:, gi * HEADS:(gi + 1) * HEADS].astype(F32)
    return jnp.dot(onehot, cols, precision=HIGHEST).T


def _prompt_bias_table(bias_j):
    blk = ATTN_BLOCK
    n = bias_j.shape[1] - 1
    period = 3 * blk
    row0 = jnp.concatenate([jnp.flip(bias_j, axis=1), jnp.full((HEADS, period - n - 1), NEG, F32)], axis=1)
    flat = jnp.tile(row0, (1, blk))[:, :blk * (period - 1)]
    tab = flat.reshape(HEADS, blk, period - 1)[:, :, :2 * blk]
    return tab.reshape(HEADS // HEADS_PER_SLAB, HEADS_PER_SLAB * blk, 2 * blk)


def _decode_bias_tables(bias_j):
    old = jnp.flip(bias_j[:, 1:], axis=1).T[:, :, None]
    new = bias_j[:, 0][:, None]
    return old, new


def _block_schedule(counts, na):
    te = EXPERT_TILE
    n_blocks = na // te + N_EXPERTS
    pcounts = (counts + te - 1) // te * te
    pends = jnp.cumsum(pcounts)
    offs = pends - pcounts
    blk_start = jnp.arange(n_blocks, dtype=jnp.int32) * te
    blk_exp = jnp.minimum(jnp.sum(pends[None, :] <= blk_start[:, None], axis=1), N_EXPERTS - 1).astype(jnp.int32)
    n_used = (pends[-1] // te).astype(jnp.int32).reshape(1)
    last_blk = jnp.where(counts > 0, pends // te - 1, -1).astype(jnp.int32)
    return offs, blk_exp, n_used, last_blk, n_blocks


def kernel(x_prompt, x_sample, cache_k_w128, cache_v_w128, cache_k_w512, cache_v_w512, cache_k_w2048, cache_v_w2048, state_pool, c_prompt, c_sample, ada_w, ada_b, norm1, norm2, w_in, q_gain, k_gain, rel_bias, pool_w, pool_scale, w_br_a, w_br_b, w_out, router_w, router_bias, exp_w_gate, exp_w_up, exp_w_down, sh_w_gate, sh_w_up, sh_w_down):
    depth = ada_w.shape[0]
    bsz, seq, d = x_prompt.shape
    db = x_sample.shape[0]
    assert x_sample.shape[1] == 1 and db <= SAMPLE_PAD and seq % (ATTN_BLOCK * ATTN_GROUPS[-1][1]) == 0
    k_caches = (cache_k_w128, cache_k_w512, cache_k_w2048)
    v_caches = (cache_v_w128, cache_v_w512, cache_v_w2048)
    for c, (w, dil) in zip(k_caches, ATTN_GROUPS):
        assert c.shape[2] == w == ATTN_STEPS * dil

    bias_js = [_group_bias(rel_bias, g) for g in range(N_GROUPS)]
    bias_tabs = jnp.stack([_prompt_bias_table(b) for b in bias_js])
    dec = [_decode_bias_tables(b) for b in bias_js]
    bdec = jnp.stack([t[0] for t in dec])
    bnew = jnp.stack([t[1] for t in dec])
    bd = jnp.asarray((np.arange(MXU_DIM)[:, None] // HEAD_DIM == np.arange(MXU_DIM)[None, :] // HEAD_DIM)
                     / HEAD_DIM, BF16)
    wcols = np.repeat(np.asarray(POOL_WINDOWS), POOL_GROUP)
    pw_dec = jnp.asarray(np.where(np.arange(POOL_STATE + 1)[:, None] >= POOL_STATE + 1 - wcols[None, :],
                                  1.0 / wcols[None, :], 0.0), F32)

    xp = x_prompt
    xs = jnp.pad(x_sample.reshape(db, d), ((0, SAMPLE_PAD - db), (0, 0))).reshape(1, SAMPLE_PAD, d)
    n_prompt = bsz * seq
    outs = {name: [] for name in ('pk', 'pv', 'sk', 'sv')}
    outs['pk'] = [[] for _ in range(N_GROUPS)]
    outs['pv'] = [[] for _ in range(N_GROUPS)]
    outs['sk'] = [[] for _ in range(N_GROUPS)]
    outs['sv'] = [[] for _ in range(N_GROUPS)]
    ppool, spool = [], []
    for l in range(depth):
        c_all = jnp.concatenate([c_prompt, c_sample], axis=0)
        mod = _ada_call(c_all, ada_w[l], ada_b[l])
        mod_p = [m.reshape(bsz, 1, d) for m in jnp.split(mod[:bsz], 6, axis=-1)]
        mod_s = [jnp.pad(m, ((0, SAMPLE_PAD - db), (0, 0))).reshape(1, SAMPLE_PAD, d)
                 for m in jnp.split(mod[bsz:], 6, axis=-1)]
        w_in_bf = w_in[l].astype(BF16)
        qg = (jnp.tile(q_gain[l], (1, HEADS)) * (HEAD_DIM ** -0.5)).reshape(1, -1)
        kg = jnp.tile(k_gain[l], (1, HEADS)).reshape(1, -1)
        g1 = norm1[l].reshape(1, d)
        g2 = norm2[l].reshape(1, d)
        wa, wb, wo = (w_br_a[l].astype(BF16), w_br_b[l].astype(BF16), w_out[l].astype(BF16))
        pw = pool_w[l].astype(BF16)
        ps = pool_scale[l].reshape(1, -1)

        q, k, v, u, gates = _inproj_call(xp, mod_p[0], mod_p[1], g1, w_in_bf, qg, kg, bd)
        oa_p = _attn_call(q, k, v, bias_tabs)
        for g, (window, dil) in enumerate(ATTN_GROUPS):
            keep = min(window, seq)
            cs = slice(g * GROUP_W, (g + 1) * GROUP_W)
            outs['pk'][g].append(k[:, seq - keep:, cs].reshape(bsz, keep, HEADS, HEAD_DIM))
            outs['pv'][g].append(v[:, seq - keep:, cs].reshape(bsz, keep, HEADS, HEAD_DIM))
        ppool.append(u[:, seq - POOL_STATE:])
        x1p, h2p = _merge_call(xp, gates, (oa_p, u), (mod_p[2], mod_p[3], mod_p[4]),
                               g2, wa, wb, wo, pw, ps, prompt=True)

        qs, ks, vs, us, gates_s = _inproj_call(xs, mod_s[0], mod_s[1], g1, w_in_bf, qg, kg, bd)
        heads3 = lambda t: t[0, :db].reshape(db, N_GROUPS * HEADS, HEAD_DIM)
        q3, k3, v3 = heads3(qs), heads3(ks), heads3(vs)
        kcl = [c[l] for c in k_caches]
        vcl = [c[l] for c in v_caches]
        oa_s, pooled_s = _decode_call(q3, k3, v3, us[0, :db], state_pool[l], kcl, vcl, bdec, bnew, pw_dec)
        padrows = lambda t: jnp.pad(t.reshape(db, GROUP_W), ((0, SAMPLE_PAD - db), (0, 0))).reshape(1, SAMPLE_PAD, GROUP_W)
        x1s, h2s = _merge_call(xs, gates_s, (padrows(oa_s), padrows(pooled_s)), (mod_s[2], mod_s[3], mod_s[4]),
                               g2, wa, wb, wo, pw, ps, prompt=False)
        for g in range(N_GROUPS):
            hs = slice(g * HEADS, (g + 1) * HEADS)
            outs['sk'][g].append(_roll_call(kcl[g], k3[:, None, hs]))
            outs['sv'][g].append(_roll_call(vcl[g], v3[:, None, hs]))
        spool.append(jnp.concatenate([state_pool[l][:, 1:], us[0, :db, None, :]], axis=1))

        h2 = jnp.concatenate([h2p.reshape(n_prompt * ROW_SLABS, LANES), h2s.reshape(SAMPLE_PAD * ROW_SLABS, LANES)], axis=0)
        nt = n_prompt + SAMPLE_PAD
        idx, wts, rank, counts = _router_call(h2, router_w[l], router_bias[l])
        offs, blk_exp, n_used, last_blk, n_blocks = _block_schedule(counts[0], nt * TOP_K)
        pos = _slot_call(idx, rank, offs.astype(F32).reshape(1, -1))[:, :TOP_K]
        xs_sorted = _dispatch_call(jnp.concatenate([last_blk, n_used]),
                                   pos.reshape(nt // ROW_TILE, 1, ROW_TILE * TOP_K), h2, n_blocks)
        y_sorted = _expert_call(blk_exp, n_used, xs_sorted, exp_w_gate[l], exp_w_up[l], exp_w_down[l])
        tc = COMBINE_TILE
        pos_tiles = pos.reshape(nt // tc, tc, TOP_K).transpose(0, 2, 1).reshape(nt // tc, 1, TOP_K * tc)
        sg, su, sd = sh_w_gate[l].astype(BF16), sh_w_up[l].astype(BF16), sh_w_down[l].astype(BF16)
        xp = _combine_call(pos_tiles, 0, y_sorted, wts, h2, x1p, mod_p[5], sg, su, sd)
        xs = _combine_call(pos_tiles, n_prompt // tc, y_sorted, wts, h2, x1s, mod_s[5], sg, su, sd)

    stack = lambda parts: jnp.stack(parts)
    res = [xp, xs[0, :db].reshape(db, 1, d)]
    for g in range(N_GROUPS):
        res += [stack(outs['pk'][g]), stack(outs['pv'][g])]
    res.append(stack(ppool))
    for g in range(N_GROUPS):
        res += [stack(outs['sk'][g]), stack(outs['sv'][g])]
    res.append(stack(spool))
    return tuple(res)
```

```python
import functools

import numpy as np
import jax
import jax.numpy as jnp
from jax import lax
from jax.experimental import pallas as pl
from jax.experimental.pallas import tpu as pltpu

F32 = jnp.float32
BF16 = jnp.bfloat16
HIGHEST = lax.Precision.HIGHEST

HEAD_DIM = 64
HEADS = 8
GROUP_W = HEADS * HEAD_DIM
ATTN_GROUPS = ((128, 1), (512, 4), (2048, 16))
N_GROUPS = len(ATTN_GROUPS)
ATTN_STEPS = 128
ATTN_BLOCK = 128
POOL_WINDOWS = (2, 4, 8, 16)
POOL_GROUP = 128
POOL_STATE = 15
N_BUCKETS = 32
REL_MAX_DIST = 2048
N_EXPERTS = 256
TOP_K = 8
N_EXPERT_GROUPS = 8
TOPK_GROUPS = 4
ROUTED_SCALE = 2.5
EPS = 1e-6
NEG = -1e30

LANES = 128
MXU_DIM = 256
VMEM_LIMIT = 56 * 1024 * 1024

INPROJ_TILE = 512
ROW_TILE = 256
EXPERT_TILE = 512
FILL_ROWS = 256
EXPERT_RING = 3
COMBINE_TILE = 128
SAMPLE_PAD = 256


def _sigmoid(x):
    return 1.0 / (1.0 + jnp.exp(-x))


def _params(sem):
    return pltpu.CompilerParams(dimension_semantics=sem, vmem_limit_bytes=VMEM_LIMIT)


ROW_SLABS = 8


def _load_rows(ref, n, first=0):
    return jnp.concatenate([ref[pl.ds(first * ROW_SLABS + s, n, stride=ROW_SLABS), :]
                            for s in range(ROW_SLABS)], axis=1)


def _store_rows(ref, val):
    n = val.shape[0]
    for s in range(ROW_SLABS):
        ref[pl.ds(s, n, stride=ROW_SLABS), :] = val[:, s * LANES:(s + 1) * LANES]


def _ada_body(c_ref, w_ref, b_ref, o_ref):
    c = c_ref[...]
    s = c * _sigmoid(c)
    o_ref[...] = jnp.dot(s, w_ref[...], precision=HIGHEST, preferred_element_type=F32) + b_ref[...]


def _ada_call(c, w, b):
    n, d = c.shape
    cols = w.shape[1]
    tn = 512
    return pl.pallas_call(
        _ada_body,
        grid=(cols // tn,),
        in_specs=[pl.BlockSpec((n, d), lambda j: (0, 0)),
                  pl.BlockSpec((d, tn), lambda j: (0, j)),
                  pl.BlockSpec((1, tn), lambda j: (0, j))],
        out_specs=pl.BlockSpec((n, tn), lambda j: (0, j)),
        out_shape=jax.ShapeDtypeStruct((n, cols), F32),
        compiler_params=_params(("arbitrary",)),
        name="adaln",
    )(c, w, b.reshape(1, cols))


def _inproj_body(x_ref, shift_ref, scale_ref, g_ref, w_ref, qg_ref, kg_ref, bd_ref,
                 q_ref, k_ref, v_ref, u_ref, gt_ref):
    x = x_ref[0]
    ms = jnp.mean(x * x, axis=-1, keepdims=True)
    h = x * lax.rsqrt(ms + EPS) * g_ref[...]
    h = h * (1.0 + scale_ref[0]) + shift_ref[0]
    hb = h.astype(BF16)
    qkv = N_GROUPS * GROUP_W
    n_chunks = w_ref.shape[1] // GROUP_W
    for c in range(n_chunks):
        z = jnp.dot(hb, w_ref[:, c * GROUP_W:(c + 1) * GROUP_W], preferred_element_type=F32)
        if c < 2 * N_GROUPS:
            zz = (z * z).astype(BF16)
            hms = jnp.concatenate(
                [jnp.dot(zz[:, s:s + MXU_DIM], bd_ref[...], preferred_element_type=F32)
                 for s in range(0, GROUP_W, MXU_DIM)], axis=1)
            g = c % N_GROUPS
            gain = (qg_ref if c < N_GROUPS else kg_ref)[:, g * GROUP_W:(g + 1) * GROUP_W]
            y = z * lax.rsqrt(hms + EPS) * gain
            if c < N_GROUPS:
                q_ref[0, :, g * GROUP_W:(g + 1) * GROUP_W] = y
            else:
                k_ref[0, :, g * GROUP_W:(g + 1) * GROUP_W] = y
        elif c < 3 * N_GROUPS:
            g = c - 2 * N_GROUPS
            v_ref[0, :, g * GROUP_W:(g + 1) * GROUP_W] = z
        elif c == 3 * N_GROUPS:
            u_ref[0] = z
        else:
            g = c - 3 * N_GROUPS - 1
            gt_ref[0, :, g * GROUP_W:(g + 1) * GROUP_W] = z.astype(BF16)
    del qkv


def _mod_spec(mod, tm, d):
    if mod.shape[1] == 1:
        return pl.BlockSpec((1, 1, d), lambda b, i: (b, 0, 0))
    return pl.BlockSpec((1, tm, d), lambda b, i: (b, i, 0))


def _inproj_call(x, shift, scale, g1, w_bf, qg, kg, bd):
    bsz, s, d = x.shape
    tm = min(INPROJ_TILE, s)
    qkv = N_GROUPS * GROUP_W
    n_gate = w_bf.shape[1] - 3 * qkv - GROUP_W
    const = lambda b, i: (0, 0)
    row = lambda width: pl.BlockSpec((1, tm, width), lambda b, i: (b, i, 0))
    return pl.pallas_call(
        _inproj_body,
        grid=(bsz, s // tm),
        in_specs=[row(d), _mod_spec(shift, tm, d), _mod_spec(scale, tm, d),
                  pl.BlockSpec((1, d), const),
                  pl.BlockSpec(w_bf.shape, const, pipeline_mode=pl.Buffered(1)),
                  pl.BlockSpec((1, qkv), const), pl.BlockSpec((1, qkv), const),
                  pl.BlockSpec((MXU_DIM, MXU_DIM), const)],
        out_specs=[row(qkv), row(qkv), row(qkv), row(GROUP_W), row(n_gate)],
        out_shape=[jax.ShapeDtypeStruct((bsz, s, qkv), F32),
                   jax.ShapeDtypeStruct((bsz, s, qkv), F32),
                   jax.ShapeDtypeStruct((bsz, s, qkv), F32),
                   jax.ShapeDtypeStruct((bsz, s, GROUP_W), F32),
                   jax.ShapeDtypeStruct((bsz, s, n_gate), BF16)],
        compiler_params=_params(("arbitrary", "arbitrary")),
        name="inproj",
    )(x, shift, scale, g1, w_bf, qg, kg, bd)


ATTN_ROWS = ATTN_BLOCK * ATTN_GROUPS[-1][1]
HEADS_PER_SLAB = LANES // HEAD_DIM


def _attn_body(*refs):
    ng = N_GROUPS
    qkv = refs[:3 * ng]
    bias_ref, o_ref = refs[3 * ng], refs[3 * ng + 1]
    scr = refs[3 * ng + 2:]
    first = pl.program_id(2) == 0
    blk = ATTN_BLOCK
    pb = ATTN_ROWS
    hs = HEADS_PER_SLAB
    lane = lax.broadcasted_iota(jnp.int32, (blk, LANES), 1)
    col = lax.broadcasted_iota(jnp.int32, (hs * blk, 2 * blk), 1)
    for g, (_, dil) in enumerate(ATTN_GROUPS):
        q_ref, k_ref, v_ref = qkv[3 * g:3 * g + 3]
        kext, vext, og, lg = scr[4 * g:4 * g + 4]
        span = blk * dil
        bias = bias_ref[g, 0]

        @pl.when(first)
        def _():
            kext[0:span, :] = jnp.zeros((span, LANES), F32)
            vext[0:span, :] = jnp.zeros((span, LANES), F32)
        kext[span:span + pb, :] = k_ref[0]
        vext[span:span + pb, :] = v_ref[0]

        rows = lambda start: pl.ds(start, blk, stride=dil) if dil > 1 else pl.ds(pl.multiple_of(start, blk), blk)
        per = 4

        def units(i, carry):
            bases, v2s, scores = [], [], []
            for j in range(per):
                u = per * i + j
                qb = u // dil
                base = qb * span + (u % dil)
                q = q_ref[0, rows(base), :]
                k2 = jnp.concatenate([kext[rows(base), :], kext[rows(base + span), :]], axis=0).astype(BF16)
                v2s.append(jnp.concatenate([vext[rows(base), :], vext[rows(base + span), :]], axis=0).astype(BF16))
                qs = jnp.concatenate([jnp.where(lane // HEAD_DIM == h, q, 0.0) for h in range(hs)],
                                     axis=0).astype(BF16)
                s = lax.dot_general(qs, k2, (((1,), (1,)), ((), ())), preferred_element_type=F32) + bias
                scores.append(jnp.where(jnp.logical_and(jnp.logical_and(first, qb == 0), col < blk), NEG, s))
                bases.append(base)
            s = jnp.concatenate(scores, axis=0)
            m = jnp.max(s, axis=-1, keepdims=True)
            p = jnp.exp(s - m)
            l = jnp.sum(p, axis=-1, keepdims=True)
            pb16 = p.astype(BF16)
            lse = m + jnp.log(l)
            for j in range(per):
                r0 = j * hs * blk
                o2 = jnp.dot(pb16[r0:r0 + hs * blk], v2s[j], preferred_element_type=F32) / l[r0:r0 + hs * blk]
                o = o2[0:blk]
                lw = jnp.broadcast_to(lse[r0:r0 + blk], (blk, LANES))
                for h in range(1, hs):
                    o = jnp.where(lane // HEAD_DIM == h, o2[h * blk:(h + 1) * blk], o)
                    lw = jnp.where(lane // HEAD_DIM == h, lse[r0 + h * blk:r0 + (h + 1) * blk], lw)
                og[rows(bases[j]), :] = o
                lg[rows(bases[j]), :] = lw
            return carry
        lax.fori_loop(0, pb // blk // per, units, 0)
        kext[0:span, :] = kext[pb:pb + span, :]
        vext[0:span, :] = vext[pb:pb + span, :]

    def merge(c, carry):
        rs = pl.ds(pl.multiple_of(c * blk, blk), blk)
        ls = [scr[4 * g + 3][rs, :] for g in range(ng)]
        mx = jnp.maximum(jnp.maximum(ls[0], ls[1]), ls[2])
        es = [jnp.exp(t - mx) for t in ls]
        den = es[0] + es[1] + es[2]
        acc = (es[0] / den) * scr[2][rs, :]
        for g in range(1, ng):
            acc = acc + (es[g] / den) * scr[4 * g + 2][rs, :]
        o_ref[0, rs, :] = acc.astype(o_ref.dtype)
        return carry
    lax.fori_loop(0, pb // blk, merge, 0)


def _attn_call(q, k, v, bias_tabs):
    bsz, s, qkv = q.shape
    pb = ATTN_ROWS
    n_slab = GROUP_W // LANES
    in_specs, args = [], []
    for g in range(N_GROUPS):
        spec = pl.BlockSpec((1, pb, LANES), lambda b, c, i, g=g: (b, i, g * n_slab + c))
        in_specs += [spec, spec, spec]
        args += [q, k, v]
    in_specs.append(pl.BlockSpec((N_GROUPS, 1) + bias_tabs.shape[2:], lambda b, c, i: (0, c, 0, 0)))
    scratch = []
    for _, dil in ATTN_GROUPS:
        ext = pltpu.VMEM((ATTN_BLOCK * dil + pb, LANES), F32)
        scratch += [ext, ext, pltpu.VMEM((pb, LANES), F32), pltpu.VMEM((pb, LANES), F32)]
    return pl.pallas_call(
        _attn_body,
        grid=(bsz, n_slab, s // pb),
        in_specs=in_specs,
        out_specs=pl.BlockSpec((1, pb, LANES), lambda b, c, i: (b, i, c)),
        out_shape=jax.ShapeDtypeStruct((bsz, s, GROUP_W), BF16),
        scratch_shapes=scratch,
        compiler_params=_params(("arbitrary", "arbitrary", "arbitrary")),
        name="attn",
    )(*args, bias_tabs)


def _decode_body(q_ref, kn_ref, vn_ref, u_ref, pool_ref, kc0, vc0, kc1, vc1, kc2, vc2,
                 bdec_ref, bnew_ref, pw_ref, oa_ref, pooled_ref):
    kcs = (kc0, kc1, kc2)
    vcs = (vc0, vc1, vc2)
    outs, lses = [], []
    for g in range(N_GROUPS):
        hs = slice(g * HEADS, (g + 1) * HEADS)
        q = q_ref[0, hs, :]
        kn = kn_ref[0, hs, :]
        vn = vn_ref[0, hs, :]
        kc = kcs[g][0]
        vc = vcs[g][0]
        s_old = jnp.sum(kc * q[None], axis=-1, keepdims=True) + bdec_ref[g]
        s_new = jnp.sum(kn * q, axis=-1, keepdims=True) + bnew_ref[g]
        m = jnp.maximum(jnp.max(s_old, axis=0), s_new)
        p_old = jnp.exp(s_old - m[None])
        p_new = jnp.exp(s_new - m)
        l = jnp.sum(p_old, axis=0) + p_new
        outs.append((jnp.sum(p_old * vc, axis=0) + p_new * vn) / l)
        lses.append(m + jnp.log(l))
    mx = jnp.maximum(jnp.maximum(lses[0], lses[1]), lses[2])
    es = [jnp.exp(t - mx) for t in lses]
    den = es[0] + es[1] + es[2]
    oa_ref[0] = (es[0] / den) * outs[0] + (es[1] / den) * outs[1] + (es[2] / den) * outs[2]
    u = u_ref[0]
    stored = jnp.sum(pool_ref[0] * pw_ref[0:POOL_STATE, :], axis=0, keepdims=True)
    pooled_ref[0] = stored + u * pw_ref[POOL_STATE:POOL_STATE + 1, :] - u


def _decode_call(q, kn, vn, u, pool_prev, k_caches, v_caches, bdec, bnew, pw):
    db = pool_prev.shape[0]
    per_b = lambda t: pl.BlockSpec((1,) + t.shape[1:], lambda b: (b,) + (0,) * (t.ndim - 1))
    const = lambda t: pl.BlockSpec(t.shape, lambda b: (0,) * t.ndim)
    cache_specs, cache_args = [], []
    for g, (_, dil) in enumerate(ATTN_GROUPS):
        for c in (k_caches[g], v_caches[g]):
            rows = c.shape[1]
            cache_args.append(c.reshape(db, rows // dil, dil, HEADS, HEAD_DIM))
            cache_specs.append(pl.BlockSpec((1, ATTN_STEPS, None, HEADS, HEAD_DIM), lambda b: (b, 0, 0, 0, 0)))
    u3 = u.reshape(db, 1, u.shape[-1])
    return pl.pallas_call(
        _decode_body,
        grid=(db,),
        in_specs=[per_b(q), per_b(kn), per_b(vn), per_b(u3), per_b(pool_prev)] + cache_specs +
                 [const(bdec), const(bnew), const(pw)],
        out_specs=[pl.BlockSpec((1, HEADS, HEAD_DIM), lambda b: (b, 0, 0)),
                   pl.BlockSpec((1, 1, GROUP_W), lambda b: (b, 0, 0))],
        out_shape=[jax.ShapeDtypeStruct((db, HEADS, HEAD_DIM), F32),
                   jax.ShapeDtypeStruct((db, 1, GROUP_W), F32)],
        compiler_params=_params(("arbitrary",)),
        name="decode_mix",
    )(q, kn, vn, u3, pool_prev, *cache_args, bdec, bnew, pw)


def _roll_body(c_ref, nxt_ref, new_ref, o_ref):
    ch = c_ref.shape[1]
    last = pl.program_id(1) == pl.num_programs(1) - 1
    o_ref[0, 0:ch - 1] = c_ref[0, 1:ch]
    o_ref[0, ch - 1:ch] = jnp.where(last, new_ref[0], nxt_ref[0])


def _roll_call(cache, new):
    db, rows, nh, e = cache.shape
    ch = min(rows, 1024)
    return pl.pallas_call(
        _roll_body,
        grid=(db, rows // ch),
        in_specs=[pl.BlockSpec((1, ch, nh, e), lambda b, i: (b, i, 0, 0)),
                  pl.BlockSpec((1, 1, nh, e), lambda b, i: (b, jnp.minimum((i + 1) * ch, rows - 1), 0, 0)),
                  pl.BlockSpec((1, 1, nh, e), lambda b, i: (b, 0, 0, 0))],
        out_specs=pl.BlockSpec((1, ch, nh, e), lambda b, i: (b, i, 0, 0)),
        out_shape=jax.ShapeDtypeStruct(cache.shape, cache.dtype),
        compiler_params=_params(("arbitrary", "arbitrary")),
        name="cache_roll",
    )(cache, cache, new)


def _merge_tail(x, oa, pooled, gates, gate_msa, shift2, scale2, g2, wa, wb, wo, pw_ref, ps,
                x1_ref, h2_ref):
    mixed = jnp.concatenate(
        [jnp.dot(pooled[:, j * POOL_GROUP:(j + 1) * POOL_GROUP].astype(BF16), pw_ref[j],
                 preferred_element_type=F32) for j in range(len(POOL_WINDOWS))], axis=1) * ps
    d = x.shape[-1]
    ga = gates[:, :d].astype(F32)
    gb = gates[:, d:].astype(F32)
    merged = (_sigmoid(ga) * jnp.dot(oa.astype(BF16), wa, preferred_element_type=F32) +
              _sigmoid(gb) * jnp.dot(mixed.astype(BF16), wb, preferred_element_type=F32))
    y = jnp.dot(merged.astype(BF16), wo, preferred_element_type=F32)
    x1 = x + gate_msa * y
    x1_ref[0] = x1
    ms = jnp.mean(x1 * x1, axis=-1, keepdims=True)
    h2 = x1 * lax.rsqrt(ms + EPS) * g2
    _store_rows(h2_ref.at[0], h2 * (1.0 + scale2) + shift2)


def _merge_prompt_body(x_ref, gt_ref, oa_ref, u_ref, uh_ref,
                       gm_ref, sh_ref, sc_ref, g2_ref, wa_ref, wb_ref, wo_ref, pw_ref, ps_ref,
                       x1_ref, h2_ref):
    i = pl.program_id(1)
    tm = x_ref.shape[1]
    oa = oa_ref[0]
    u = u_ref[0]
    halo = jnp.where(i == 0, 0.0, uh_ref[0])
    ext = jnp.concatenate([halo, u], axis=0)
    hw = halo.shape[0]
    acc = ext
    parts = []
    for j, w in enumerate(POOL_WINDOWS):
        acc = acc[:, POOL_GROUP * (1 if j else 0):]
        acc = acc + pltpu.roll(acc, w // 2, axis=0)
        parts.append(acc[hw:, :POOL_GROUP])
    tot = jnp.concatenate(parts, axis=1)
    lane = lax.broadcasted_iota(jnp.int32, (tm, GROUP_W), 1)
    wcol = jnp.left_shift(2, lane // POOL_GROUP)
    pos = i * tm + lax.broadcasted_iota(jnp.int32, (tm, GROUP_W), 0)
    cnt = jnp.minimum(pos + 1, wcol).astype(F32)
    pooled = tot / cnt - u
    _merge_tail(x_ref[0], oa, pooled, gt_ref[0], gm_ref[0], sh_ref[0], sc_ref[0], g2_ref[...],
                wa_ref[...], wb_ref[...], wo_ref[...], pw_ref, ps_ref[...], x1_ref, h2_ref)


def _merge_sample_body(x_ref, gt_ref, oa_ref, pooled_ref,
                       gm_ref, sh_ref, sc_ref, g2_ref, wa_ref, wb_ref, wo_ref, pw_ref, ps_ref,
                       x1_ref, h2_ref):
    _merge_tail(x_ref[0], oa_ref[0], pooled_ref[0], gt_ref[0], gm_ref[0], sh_ref[0], sc_ref[0], g2_ref[...],
                wa_ref[...], wb_ref[...], wo_ref[...], pw_ref, ps_ref[...], x1_ref, h2_ref)


def _merge_call(x, gates, mixer_inputs, mods, g2, wa, wb, wo, pw, ps, prompt):
    bsz, s, d = x.shape
    assert d == ROW_SLABS * LANES
    tm = min(ROW_TILE, s)
    row = lambda t: pl.BlockSpec((1, tm, t.shape[-1]), lambda b, i: (b, i, 0))
    const = lambda t: pl.BlockSpec(t.shape, lambda b, i: (0,) * t.ndim)
    gm, sh, sc = mods
    in_specs = [row(x), row(gates)]
    args = [x, gates]
    if prompt:
        oa, u = mixer_inputs
        hw = 16
        in_specs += [row(oa), row(u),
                     pl.BlockSpec((1, hw, u.shape[-1]), lambda b, i: (b, jnp.maximum(i * (tm // hw) - 1, 0), 0))]
        args += [oa, u, u]
        body = _merge_prompt_body
    else:
        in_specs += [row(t) for t in mixer_inputs]
        args += list(mixer_inputs)
        body = _merge_sample_body
    in_specs += [_mod_spec(m, tm, d) for m in (gm, sh, sc)]
    in_specs += [const(t) for t in (g2, wa, wb, wo, pw, ps)]
    args += [gm, sh, sc, g2, wa, wb, wo, pw, ps]
    return pl.pallas_call(
        body,
        grid=(bsz, s // tm),
        in_specs=in_specs,
        out_specs=[row(x), pl.BlockSpec((1, tm * ROW_SLABS, LANES), lambda b, i: (b, i, 0))],
        out_shape=[jax.ShapeDtypeStruct((bsz, s, d), F32),
                   jax.ShapeDtypeStruct((bsz, s * ROW_SLABS, LANES), F32)],
        compiler_params=_params(("arbitrary", "arbitrary")),
        name="merge_prompt" if prompt else "merge_sample",
    )(*args)


def _router_body(h_ref, rw_ref, rb_ref, tri_ref, idx_ref, wt_ref, rank_ref, cnt_ref, carry):
    tm = h_ref.shape[0] // ROW_SLABS

    @pl.when(pl.program_id(0) == 0)
    def _():
        carry[...] = jnp.zeros_like(carry)

    ne = rw_ref.shape[1]
    gsz = ne // N_EXPERT_GROUPS
    logits = jnp.dot(_load_rows(h_ref, tm), rw_ref[...], precision=HIGHEST, preferred_element_type=F32)
    scores = _sigmoid(logits)
    sel = scores + rb_ref[...]
    lane = lax.broadcasted_iota(jnp.int32, (tm, ne), 1)
    lanef = lane.astype(F32)
    grp = lane // gsz
    ninf = -jnp.inf
    far = float(ne)

    def first_max(x):
        m = jnp.max(x, axis=-1, keepdims=True)
        at = jnp.min(jnp.where(x == m, lanef, far), axis=-1, keepdims=True)
        return m, at

    gs = []
    for g in range(N_EXPERT_GROUPS):
        xg = jnp.where(grp == g, sel, ninf)
        m1, a1 = first_max(xg)
        m2 = jnp.max(jnp.where(lanef == a1, ninf, xg), axis=-1, keepdims=True)
        gs.append(m1 + m2)
    allowed = jnp.zeros((tm, ne), jnp.bool_)
    for g in range(N_EXPERT_GROUPS):
        ahead = jnp.zeros((tm, 1), F32)
        for o in range(N_EXPERT_GROUPS):
            if o == g:
                continue
            beats = (gs[o] > gs[g]) | ((gs[o] == gs[g]) & (o < g))
            ahead = ahead + beats.astype(F32)
        allowed = allowed | ((grp == g) & (ahead < TOPK_GROUPS))
    masked = jnp.where(allowed, sel, ninf)
    lane_o = lax.broadcasted_iota(jnp.int32, (tm, LANES), 1)
    idx_t = jnp.zeros((tm, LANES), F32)
    wt_t = jnp.zeros((tm, LANES), F32)
    wsum = jnp.zeros((tm, 1), F32)
    picked = jnp.zeros((tm, ne), F32)
    ats = []
    for k in range(TOP_K):
        _, at = first_max(masked)
        hit = lanef == at
        wk = jnp.sum(jnp.where(hit, scores, 0.0), axis=-1, keepdims=True)
        masked = jnp.where(hit, ninf, masked)
        picked = picked + hit.astype(F32)
        ats.append(at)
        idx_t = jnp.where(lane_o == k, at, idx_t)
        wt_t = jnp.where(lane_o == k, wk, wt_t)
        wsum = wsum + wk
    idx_ref[...] = idx_t.astype(jnp.int32)
    wt_ref[...] = wt_t / wsum * ROUTED_SCALE
    before = jnp.dot(tri_ref[...], picked.astype(BF16), preferred_element_type=F32) + carry[...]
    rank_t = jnp.zeros((tm, LANES), F32)
    for k in range(TOP_K):
        rk = jnp.sum(jnp.where(lanef == ats[k], before, 0.0), axis=-1, keepdims=True)
        rank_t = jnp.where(lane_o == k, rk, rank_t)
    rank_ref[...] = rank_t.astype(jnp.int32)
    carry[...] = carry[...] + jnp.sum(picked, axis=0, keepdims=True)
    cnt_ref[...] = carry[...].astype(jnp.int32)


def _router_call(h2, rw, rb):
    nt = h2.shape[0] // ROW_SLABS
    tm = ROW_TILE
    d, ne = rw.shape
    tri = jnp.asarray(np.tril(np.ones((tm, tm), np.float32), -1), BF16)
    tile = pl.BlockSpec((tm, LANES), lambda i: (i, 0))
    return pl.pallas_call(
        _router_body,
        grid=(nt // tm,),
        in_specs=[pl.BlockSpec((tm * ROW_SLABS, LANES), lambda i: (i, 0)),
                  pl.BlockSpec((d, ne), lambda i: (0, 0)),
                  pl.BlockSpec((1, ne), lambda i: (0, 0)),
                  pl.BlockSpec((tm, tm), lambda i: (0, 0))],
        out_specs=[tile, tile, tile, pl.BlockSpec((1, ne), lambda i: (0, 0))],
        out_shape=[jax.ShapeDtypeStruct((nt, LANES), jnp.int32),
                   jax.ShapeDtypeStruct((nt, LANES), F32),
                   jax.ShapeDtypeStruct((nt, LANES), jnp.int32),
                   jax.ShapeDtypeStruct((1, ne), jnp.int32)],
        scratch_shapes=[pltpu.VMEM((1, ne), F32)],
        compiler_params=_params(("arbitrary",)),
        name="router",
    )(h2, rw, rb.reshape(1, ne), tri)


def _slot_body(idx_ref, rank_ref, offs_ref, pos_ref):
    tm = idx_ref.shape[0]
    ne = offs_ref.shape[1]
    idx = idx_ref[...]
    rank = rank_ref[...]
    lane = lax.broadcasted_iota(jnp.int32, (tm, ne), 1)
    lane_o = lax.broadcasted_iota(jnp.int32, (tm, LANES), 1)
    offs = offs_ref[...]
    pos = jnp.zeros((tm, LANES), F32)
    for k in range(TOP_K):
        start = jnp.sum(jnp.where(lane == idx[:, k:k + 1], offs, 0.0), axis=-1, keepdims=True)
        pos = jnp.where(lane_o == k, start, pos)
    pos_ref[...] = pos.astype(jnp.int32) + rank


def _slot_call(idx, rank, offs):
    nt = idx.shape[0]
    tm = ROW_TILE
    ne = offs.shape[1]
    tile = pl.BlockSpec((tm, LANES), lambda i: (i, 0))
    return pl.pallas_call(
        _slot_body,
        grid=(nt // tm,),
        in_specs=[tile, tile, pl.BlockSpec((1, ne), lambda i: (0, 0))],
        out_specs=tile,
        out_shape=jax.ShapeDtypeStruct((nt, LANES), jnp.int32),
        compiler_params=_params(("arbitrary",)),
        name="slots",
    )(idx, rank, offs)


def _row_tile(ref, r):
    return ref.at[pl.ds(pl.multiple_of(r * ROW_SLABS, ROW_SLABS), ROW_SLABS), :]


def _dispatch_body(fill_ref, pos_ref, h_ref, xs_hbm, zeros, zsem, sem):
    td = h_ref.shape[0] // ROW_SLABS
    chunk = zeros.shape[0]

    @pl.when(pl.program_id(0) == 0)
    def _():
        zeros[...] = jnp.zeros_like(zeros)

        def fill_chunk(c, wait):
            cp = pltpu.make_async_copy(zeros, xs_hbm.at[pl.ds(pl.multiple_of(c * chunk, chunk), chunk), :], zsem)
            if wait:
                cp.wait()
            else:
                cp.start()

        def fill(j, wait):
            @pl.when(fill_ref[j] >= 0)
            def _():
                fill_chunk(fill_ref[j], wait)
        n_ent = fill_ref.shape[0] - 1
        n_chunks = xs_hbm.shape[0] // chunk
        for wait in (False, True):
            lax.fori_loop(0, n_ent, lambda j, c, wait=wait: (fill(j, wait), c)[1], 0)
            lax.fori_loop(fill_ref[n_ent], n_chunks, lambda b, c, wait=wait: (fill_chunk(b, wait), c)[1], 0)

    def one(t, carry):
        for k in range(TOP_K):
            pltpu.make_async_copy(_row_tile(h_ref, t), _row_tile(xs_hbm, pos_ref[0, 0, t * TOP_K + k]),
                                  sem).start(priority=k % 2)
        return carry
    lax.fori_loop(0, td, one, 0, unroll=2)
    for k in range(TOP_K):
        pltpu.make_async_copy(h_ref, xs_hbm.at[pl.ds(0, td * ROW_SLABS), :], sem).wait()


def _dispatch_call(fill_chunks, pos_tok, h2, n_blocks):
    nt = h2.shape[0] // ROW_SLABS
    td = ROW_TILE
    te = EXPERT_TILE
    grid_spec = pltpu.PrefetchScalarGridSpec(
        num_scalar_prefetch=1,
        grid=(nt // td,),
        in_specs=[pl.BlockSpec((1, 1, td * TOP_K), lambda i, lb: (i, 0, 0), memory_space=pltpu.SMEM),
                  pl.BlockSpec((td * ROW_SLABS, LANES), lambda i, lb: (i, 0))],
        out_specs=pl.BlockSpec(memory_space=pl.ANY),
        scratch_shapes=[pltpu.VMEM((FILL_ROWS * ROW_SLABS, LANES), F32), pltpu.SemaphoreType.DMA,
                        pltpu.SemaphoreType.DMA],
    )
    return pl.pallas_call(
        _dispatch_body,
        grid_spec=grid_spec,
        out_shape=jax.ShapeDtypeStruct((n_blocks * te * ROW_SLABS, LANES), F32),
        compiler_params=_params(("arbitrary",)),
        name="dispatch",
    )(fill_chunks, pos_tok, h2)


def _expert_body(exp_ref, nu_ref, xs_hbm, wg_ref, wu_ref, wd_ref, y_ref, xbuf, xsem, wgb, wub, wdb):
    b = pl.program_id(0)
    n_used = nu_ref[0]
    live = b < n_used
    new_expert = jnp.logical_or(b == 0, exp_ref[b] != exp_ref[jnp.maximum(b - 1, 0)])
    ring, rows = xbuf.shape[0], xbuf.shape[1]

    def fetch(blk):
        slot = blk % ring
        return pltpu.make_async_copy(xs_hbm.at[pl.ds(pl.multiple_of(blk * rows, rows), rows), :],
                                     xbuf.at[slot], xsem.at[slot])

    @pl.when(b == 0)
    def _():
        for j in range(ring - 1):
            @pl.when(j < n_used)
            def _():
                fetch(j).start()

    @pl.when(b + ring - 1 < n_used)
    def _():
        fetch(b + ring - 1).start()

    @pl.when(jnp.logical_and(live, new_expert))
    def _():
        wgb[...] = wg_ref[0].astype(BF16)
        wub[...] = wu_ref[0].astype(BF16)
        wdb[...] = wd_ref[0].astype(BF16)

    @pl.when(live)
    def _():
        te = rows // ROW_SLABS
        fetch(b).wait()
        x = _load_rows(xbuf.at[b % ring], te).astype(BF16)
        gt = jnp.dot(x, wgb[...], preferred_element_type=F32)
        up = jnp.dot(x, wub[...], preferred_element_type=F32)
        act = (gt * _sigmoid(gt)) * up
        _store_rows(y_ref, jnp.dot(act.astype(BF16), wdb[...], preferred_element_type=F32))

    @pl.when(jnp.logical_not(live))
    def _():
        y_ref[...] = jnp.zeros_like(y_ref)


def _expert_call(blk_exp, n_used, xs, wg, wu, wd):
    n_blocks = blk_exp.shape[0]
    te = EXPERT_TILE
    d, hid = wg.shape[-2:]
    used = lambda b, nu: jnp.minimum(b, nu[0] - 1)
    wspec = lambda r, c: pl.BlockSpec((1, r, c), lambda b, e, nu: (e[used(b, nu)], 0, 0))
    grid_spec = pltpu.PrefetchScalarGridSpec(
        num_scalar_prefetch=2,
        grid=(n_blocks,),
        in_specs=[pl.BlockSpec(memory_space=pl.ANY), wspec(d, hid), wspec(d, hid), wspec(hid, d)],
        out_specs=pl.BlockSpec((te * ROW_SLABS, LANES), lambda b, e, nu: (b, 0)),
        scratch_shapes=[pltpu.VMEM((EXPERT_RING, te * ROW_SLABS, LANES), F32), pltpu.SemaphoreType.DMA((EXPERT_RING,)),
                        pltpu.VMEM((d, hid), BF16), pltpu.VMEM((d, hid), BF16), pltpu.VMEM((hid, d), BF16)],
    )
    return pl.pallas_call(
        _expert_body,
        grid_spec=grid_spec,
        out_shape=jax.ShapeDtypeStruct(xs.shape, F32),
        compiler_params=_params(("arbitrary",)),
        name="expert_ffn",
    )(blk_exp, n_used, xs, wg, wu, wd)


def _gather_rows(idx_ref, src_hbm, dst, sem, n):
    def pair(i, carry):
        for p in range(2):
            j = 2 * i + p
            pltpu.make_async_copy(_row_tile(src_hbm, idx_ref[0, 0, j]), _row_tile(dst, j), sem).start(priority=p)
        return carry
    lax.fori_loop(0, n // 2, pair, 0, unroll=4)


def _combine_body(pos_ref, posn_ref, y_hbm, wt_ref, h_ref, x1_ref, gate_ref, sg_ref, su_ref, sd_ref,
                  out_ref, ybuf, sem):
    nb, ni = pl.num_programs(0), pl.num_programs(1)
    step = pl.program_id(0) * ni + pl.program_id(1)
    slot = step % 2
    tc = h_ref.shape[0] // ROW_SLABS
    rows = tc * TOP_K

    @pl.when(step == 0)
    def _():
        _gather_rows(pos_ref, y_hbm, ybuf.at[0], sem.at[0], rows)

    @pl.when(step + 1 < nb * ni)
    def _():
        _gather_rows(posn_ref, y_hbm, ybuf.at[1 - slot], sem.at[1 - slot], rows)

    pltpu.make_async_copy(y_hbm.at[pl.ds(0, rows * ROW_SLABS), :], ybuf.at[slot], sem.at[slot]).wait()
    wt = wt_ref[...]
    routed = jnp.zeros(out_ref.shape[1:], F32)
    for k in range(TOP_K):
        routed = routed + wt[:, k:k + 1] * _load_rows(ybuf.at[slot], tc, first=k * tc)
    hb = _load_rows(h_ref, tc).astype(BF16)
    gt = jnp.dot(hb, sg_ref[...], preferred_element_type=F32)
    up = jnp.dot(hb, su_ref[...], preferred_element_type=F32)
    shared = jnp.dot(((gt * _sigmoid(gt)) * up).astype(BF16), sd_ref[...], preferred_element_type=F32)
    out_ref[0] = x1_ref[0] + gate_ref[0] * (routed + shared)


def _combine_call(pos_tiles, tile_off, y_sorted, wts, h2, x1, gate, sg, su, sd):
    bsz, s, d = x1.shape
    tc = min(COMBINE_TILE, s)
    ni = s // tc
    n_tiles = pos_tiles.shape[0]
    lin = lambda b, i: tile_off + b * ni + i
    const = lambda t: pl.BlockSpec(t.shape, lambda b, i: (0,) * t.ndim)
    return pl.pallas_call(
        _combine_body,
        grid=(bsz, ni),
        in_specs=[
            pl.BlockSpec((1, 1, tc * TOP_K), lambda b, i: (lin(b, i), 0, 0), memory_space=pltpu.SMEM),
            pl.BlockSpec((1, 1, tc * TOP_K), lambda b, i: (jnp.minimum(lin(b, i) + 1, n_tiles - 1), 0, 0),
                         memory_space=pltpu.SMEM),
            pl.BlockSpec(memory_space=pl.ANY),
            pl.BlockSpec((tc, LANES), lambda b, i: (lin(b, i), 0)),
            pl.BlockSpec((tc * ROW_SLABS, LANES), lambda b, i: (lin(b, i), 0)),
            pl.BlockSpec((1, tc, d), lambda b, i: (b, i, 0)),
            _mod_spec(gate, tc, d),
            const(sg), const(su), const(sd),
        ],
        out_specs=pl.BlockSpec((1, tc, d), lambda b, i: (b, i, 0)),
        out_shape=jax.ShapeDtypeStruct((bsz, s, d), F32),
        scratch_shapes=[pltpu.VMEM((2, tc * TOP_K * ROW_SLABS, LANES), F32), pltpu.SemaphoreType.DMA((2,))],
        compiler_params=_params(("arbitrary", "arbitrary")),
        name="combine",
    )(pos_tiles, pos_tiles, y_sorted, wts, h2, x1, gate, sg, su, sd)


def _t5_bucket(dist):
    exact = N_BUCKETS // 2
    d = np.asarray(dist)
    large = exact + (np.log(np.maximum(d, 1) / exact) / np.log(REL_MAX_DIST / exact) * (N_BUCKETS - exact)).astype(np.int32)
    large = np.minimum(large, N_BUCKETS - 1)
    return np.where(d < exact, d, large).astype(np.int32)


def _group_bias(rel_bias, gi):
    window, dil = ATTN_GROUPS[gi]
    bkt = _t5_bucket(np.arange(window // dil + 1) * dil)
    onehot = jnp.asarray(bkt[:, None] == np.arange(N_BUCKETS)[None, :], F32)
    cols = rel_bias[:, gi * HEADS:(gi + 1) * HEADS].astype(F32)
    return jnp.dot(onehot, cols, precision=HIGHEST).T


def _prompt_bias_table(bias_j):
    blk = ATTN_BLOCK
    n = bias_j.shape[1] - 1
    period = 3 * blk
    row0 = jnp.concatenate([jnp.flip(bias_j, axis=1), jnp.full((HEADS, period - n - 1), NEG, F32)], axis=1)
    flat = jnp.tile(row0, (1, blk))[:, :blk * (period - 1)]
    tab = flat.reshape(HEADS, blk, period - 1)[:, :, :2 * blk]
    return tab.reshape(HEADS // HEADS_PER_SLAB, HEADS_PER_SLAB * blk, 2 * blk)


def _decode_bias_tables(bias_j):
    old = jnp.flip(bias_j[:, 1:], axis=1).T[:, :, None]
    new = bias_j[:, 0][:, None]
    return old, new


def _block_schedule(counts, na):
    te = EXPERT_TILE
    n_blocks = na // te + N_EXPERTS
    pcounts = (counts + te - 1) // te * te
    pends = jnp.cumsum(pcounts)
    offs = pends - pcounts
    blk_start = jnp.arange(n_blocks, dtype=jnp.int32) * te
    blk_exp = jnp.minimum(jnp.sum(pends[None, :] <= blk_start[:, None], axis=1), N_EXPERTS - 1).astype(jnp.int32)
    n_used = (pends[-1] // te).astype(jnp.int32).reshape(1)
    fr = FILL_ROWS
    fill_start = offs + counts // fr * fr
    n_fill = (pends - fill_start) // fr
    first = fill_start // fr
    chunks = jnp.concatenate([jnp.where(n_fill > j, first + j, -1) for j in range(te // fr)]
                             + [n_used * (te // fr)]).astype(jnp.int32)
    return offs, blk_exp, n_used, chunks, n_blocks


def kernel(x_prompt, x_sample, cache_k_w128, cache_v_w128, cache_k_w512, cache_v_w512, cache_k_w2048, cache_v_w2048, state_pool, c_prompt, c_sample, ada_w, ada_b, norm1, norm2, w_in, q_gain, k_gain, rel_bias, pool_w, pool_scale, w_br_a, w_br_b, w_out, router_w, router_bias, exp_w_gate, exp_w_up, exp_w_down, sh_w_gate, sh_w_up, sh_w_down):
    depth = ada_w.shape[0]
    bsz, seq, d = x_prompt.shape
    db = x_sample.shape[0]
    assert x_sample.shape[1] == 1 and db <= SAMPLE_PAD and seq % (ATTN_BLOCK * ATTN_GROUPS[-1][1]) == 0
    k_caches = (cache_k_w128, cache_k_w512, cache_k_w2048)
    v_caches = (cache_v_w128, cache_v_w512, cache_v_w2048)
    for c, (w, dil) in zip(k_caches, ATTN_GROUPS):
        assert c.shape[2] == w == ATTN_STEPS * dil

    bias_js = [_group_bias(rel_bias, g) for g in range(N_GROUPS)]
    bias_tabs = jnp.stack([_prompt_bias_table(b) for b in bias_js])
    dec = [_decode_bias_tables(b) for b in bias_js]
    bdec = jnp.stack([t[0] for t in dec])
    bnew = jnp.stack([t[1] for t in dec])
    bd = jnp.asarray((np.arange(MXU_DIM)[:, None] // HEAD_DIM == np.arange(MXU_DIM)[None, :] // HEAD_DIM)
                     / HEAD_DIM, BF16)
    wcols = np.repeat(np.asarray(POOL_WINDOWS), POOL_GROUP)
    pw_dec = jnp.asarray(np.where(np.arange(POOL_STATE + 1)[:, None] >= POOL_STATE + 1 - wcols[None, :],
                                  1.0 / wcols[None, :], 0.0), F32)

    xp = x_prompt
    xs = jnp.pad(x_sample.reshape(db, d), ((0, SAMPLE_PAD - db), (0, 0))).reshape(1, SAMPLE_PAD, d)
    n_prompt = bsz * seq
    outs = {name: [] for name in ('pk', 'pv', 'sk', 'sv')}
    outs['pk'] = [[] for _ in range(N_GROUPS)]
    outs['pv'] = [[] for _ in range(N_GROUPS)]
    outs['sk'] = [[] for _ in range(N_GROUPS)]
    outs['sv'] = [[] for _ in range(N_GROUPS)]
    ppool, spool = [], []
    for l in range(depth):
        c_all = jnp.concatenate([c_prompt, c_sample], axis=0)
        mod = _ada_call(c_all, ada_w[l], ada_b[l])
        mod_p = [m.reshape(bsz, 1, d) for m in jnp.split(mod[:bsz], 6, axis=-1)]
        mod_s = [jnp.pad(m, ((0, SAMPLE_PAD - db), (0, 0))).reshape(1, SAMPLE_PAD, d)
                 for m in jnp.split(mod[bsz:], 6, axis=-1)]
        w_in_bf = w_in[l].astype(BF16)
        qg = (jnp.tile(q_gain[l], (1, HEADS)) * (HEAD_DIM ** -0.5)).reshape(1, -1)
        kg = jnp.tile(k_gain[l], (1, HEADS)).reshape(1, -1)
        g1 = norm1[l].reshape(1, d)
        g2 = norm2[l].reshape(1, d)
        wa, wb, wo = (w_br_a[l].astype(BF16), w_br_b[l].astype(BF16), w_out[l].astype(BF16))
        pw = pool_w[l].astype(BF16)
        ps = pool_scale[l].reshape(1, -1)

        q, k, v, u, gates = _inproj_call(xp, mod_p[0], mod_p[1], g1, w_in_bf, qg, kg, bd)
        oa_p = _attn_call(q, k, v, bias_tabs)
        for g, (window, dil) in enumerate(ATTN_GROUPS):
            keep = min(window, seq)
            cs = slice(g * GROUP_W, (g + 1) * GROUP_W)
            outs['pk'][g].append(k[:, seq - keep:, cs].reshape(bsz, keep, HEADS, HEAD_DIM))
            outs['pv'][g].append(v[:, seq - keep:, cs].reshape(bsz, keep, HEADS, HEAD_DIM))
        ppool.append(u[:, seq - POOL_STATE:])
        x1p, h2p = _merge_call(xp, gates, (oa_p, u), (mod_p[2], mod_p[3], mod_p[4]),
                               g2, wa, wb, wo, pw, ps, prompt=True)

        qs, ks, vs, us, gates_s = _inproj_call(xs, mod_s[0], mod_s[1], g1, w_in_bf, qg, kg, bd)
        heads3 = lambda t: t[0, :db].reshape(db, N_GROUPS * HEADS, HEAD_DIM)
        q3, k3, v3 = heads3(qs), heads3(ks), heads3(vs)
        kcl = [c[l] for c in k_caches]
        vcl = [c[l] for c in v_caches]
        oa_s, pooled_s = _decode_call(q3, k3, v3, us[0, :db], state_pool[l], kcl, vcl, bdec, bnew, pw_dec)
        padrows = lambda t: jnp.pad(t.reshape(db, GROUP_W), ((0, SAMPLE_PAD - db), (0, 0))).reshape(1, SAMPLE_PAD, GROUP_W)
        x1s, h2s = _merge_call(xs, gates_s, (padrows(oa_s), padrows(pooled_s)), (mod_s[2], mod_s[3], mod_s[4]),
                               g2, wa, wb, wo, pw, ps, prompt=False)
        for g in range(N_GROUPS):
            hs = slice(g * HEADS, (g + 1) * HEADS)
            outs['sk'][g].append(_roll_call(kcl[g], k3[:, None, hs]))
            outs['sv'][g].append(_roll_call(vcl[g], v3[:, None, hs]))
        spool.append(jnp.concatenate([state_pool[l][:, 1:], us[0, :db, None, :]], axis=1))

        h2 = jnp.concatenate([h2p.reshape(n_prompt * ROW_SLABS, LANES), h2s.reshape(SAMPLE_PAD * ROW_SLABS, LANES)], axis=0)
        nt = n_prompt + SAMPLE_PAD
        idx, wts, rank, counts = _router_call(h2, router_w[l], router_bias[l])
        offs, blk_exp, n_used, fill_chunks, n_blocks = _block_schedule(counts[0], nt * TOP_K)
        pos = _slot_call(idx, rank, offs.astype(F32).reshape(1, -1))[:, :TOP_K]
        xs_sorted = _dispatch_call(fill_chunks, pos.reshape(nt // ROW_TILE, 1, ROW_TILE * TOP_K), h2, n_blocks)
        y_sorted = _expert_call(blk_exp, n_used, xs_sorted, exp_w_gate[l], exp_w_up[l], exp_w_down[l])
        tc = COMBINE_TILE
        pos_tiles = pos.reshape(nt // tc, tc, TOP_K).transpose(0, 2, 1).reshape(nt // tc, 1, TOP_K * tc)
        sg, su, sd = sh_w_gate[l].astype(BF16), sh_w_up[l].astype(BF16), sh_w_down[l].astype(BF16)
        xp = _combine_call(pos_tiles, 0, y_sorted, wts, h2, x1p, mod_p[5], sg, su, sd)
        xs = _combine_call(pos_tiles, n_prompt // tc, y_sorted, wts, h2, x1s, mod_s[5], sg, su, sd)

    stack = lambda parts: jnp.stack(parts)
    res = [xp, xs[0, :db].reshape(db, 1, d)]
    for g in range(N_GROUPS):
        res += [stack(outs['pk'][g]), stack(outs['pv'][g])]
    res.append(stack(ppool))
    for g in range(N_GROUPS):
        res += [stack(outs['sk'][g]), stack(outs['sv'][g])]
    res.append(stack(spool))
    return tuple(res)
```

```python
import functools

import numpy as np
import jax
import jax.numpy as jnp
from jax import lax
from jax.experimental import pallas as pl
from jax.experimental.pallas import tpu as pltpu

F32 = jnp.float32
BF16 = jnp.bfloat16
HIGHEST = lax.Precision.HIGHEST

HEAD_DIM = 64
HEADS = 8
GROUP_W = HEADS * HEAD_DIM
ATTN_GROUPS = ((128, 1), (512, 4), (2048, 16))
N_GROUPS = len(ATTN_GROUPS)
ATTN_STEPS = 128
ATTN_BLOCK = 128
POOL_WINDOWS = (2, 4, 8, 16)
POOL_GROUP = 128
POOL_STATE = 15
N_BUCKETS = 32
REL_MAX_DIST = 2048
N_EXPERTS = 256
TOP_K = 8
N_EXPERT_GROUPS = 8
TOPK_GROUPS = 4
ROUTED_SCALE = 2.5
EPS = 1e-6
NEG = -1e30

LANES = 128
MXU_DIM = 256
VMEM_LIMIT = 56 * 1024 * 1024

INPROJ_TILE = 512
ROW_TILE = 256
EXPERT_TILE = 512
FILL_ROWS = 256
EXPERT_RING = 3
COMBINE_TILE = 128
SAMPLE_PAD = 256


def _sigmoid(x):
    return 1.0 / (1.0 + jnp.exp(-x))


def _params(sem):
    return pltpu.CompilerParams(dimension_semantics=sem, vmem_limit_bytes=VMEM_LIMIT)


ROW_SLABS = 8


def _load_rows(ref, n, first=0):
    return jnp.concatenate([ref[pl.ds(first * ROW_SLABS + s, n, stride=ROW_SLABS), :]
                            for s in range(ROW_SLABS)], axis=1)


def _store_rows(ref, val):
    n = val.shape[0]
    for s in range(ROW_SLABS):
        ref[pl.ds(s, n, stride=ROW_SLABS), :] = val[:, s * LANES:(s + 1) * LANES]


def _ada_body(c_ref, w_ref, b_ref, o_ref):
    c = c_ref[...]
    s = c * _sigmoid(c)
    o_ref[...] = jnp.dot(s, w_ref[...], precision=HIGHEST, preferred_element_type=F32) + b_ref[...]


def _ada_call(c, w, b):
    n, d = c.shape
    cols = w.shape[1]
    tn = 512
    return pl.pallas_call(
        _ada_body,
        grid=(cols // tn,),
        in_specs=[pl.BlockSpec((n, d), lambda j: (0, 0)),
                  pl.BlockSpec((d, tn), lambda j: (0, j)),
                  pl.BlockSpec((1, tn), lambda j: (0, j))],
        out_specs=pl.BlockSpec((n, tn), lambda j: (0, j)),
        out_shape=jax.ShapeDtypeStruct((n, cols), F32),
        compiler_params=_params(("arbitrary",)),
        name="adaln",
    )(c, w, b.reshape(1, cols))


def _inproj_body(x_ref, shift_ref, scale_ref, g_ref, w_ref, qg_ref, kg_ref, bd_ref,
                 q_ref, k_ref, v_ref, u_ref, gt_ref):
    x = x_ref[0]
    ms = jnp.mean(x * x, axis=-1, keepdims=True)
    h = x * lax.rsqrt(ms + EPS) * g_ref[...]
    h = h * (1.0 + scale_ref[0]) + shift_ref[0]
    hb = h.astype(BF16)
    qkv = N_GROUPS * GROUP_W
    n_chunks = w_ref.shape[1] // GROUP_W
    for c in range(n_chunks):
        z = jnp.dot(hb, w_ref[:, c * GROUP_W:(c + 1) * GROUP_W], preferred_element_type=F32)
        if c < 2 * N_GROUPS:
            zz = (z * z).astype(BF16)
            hms = jnp.concatenate(
                [jnp.dot(zz[:, s:s + MXU_DIM], bd_ref[...], preferred_element_type=F32)
                 for s in range(0, GROUP_W, MXU_DIM)], axis=1)
            g = c % N_GROUPS
            gain = (qg_ref if c < N_GROUPS else kg_ref)[:, g * GROUP_W:(g + 1) * GROUP_W]
            y = z * lax.rsqrt(hms + EPS) * gain
            if c < N_GROUPS:
                q_ref[0, :, g * GROUP_W:(g + 1) * GROUP_W] = y
            else:
                k_ref[0, :, g * GROUP_W:(g + 1) * GROUP_W] = y
        elif c < 3 * N_GROUPS:
            g = c - 2 * N_GROUPS
            v_ref[0, :, g * GROUP_W:(g + 1) * GROUP_W] = z
        elif c == 3 * N_GROUPS:
            u_ref[0] = z
        else:
            g = c - 3 * N_GROUPS - 1
            gt_ref[0, :, g * GROUP_W:(g + 1) * GROUP_W] = z.astype(BF16)
    del qkv


def _mod_spec(mod, tm, d):
    if mod.shape[1] == 1:
        return pl.BlockSpec((1, 1, d), lambda b, i: (b, 0, 0))
    return pl.BlockSpec((1, tm, d), lambda b, i: (b, i, 0))


def _inproj_call(x, shift, scale, g1, w_bf, qg, kg, bd):
    bsz, s, d = x.shape
    tm = min(INPROJ_TILE, s)
    qkv = N_GROUPS * GROUP_W
    n_gate = w_bf.shape[1] - 3 * qkv - GROUP_W
    const = lambda b, i: (0, 0)
    row = lambda width: pl.BlockSpec((1, tm, width), lambda b, i: (b, i, 0))
    return pl.pallas_call(
        _inproj_body,
        grid=(bsz, s // tm),
        in_specs=[row(d), _mod_spec(shift, tm, d), _mod_spec(scale, tm, d),
                  pl.BlockSpec((1, d), const),
                  pl.BlockSpec(w_bf.shape, const, pipeline_mode=pl.Buffered(1)),
                  pl.BlockSpec((1, qkv), const), pl.BlockSpec((1, qkv), const),
                  pl.BlockSpec((MXU_DIM, MXU_DIM), const)],
        out_specs=[row(qkv), row(qkv), row(qkv), row(GROUP_W), row(n_gate)],
        out_shape=[jax.ShapeDtypeStruct((bsz, s, qkv), F32),
                   jax.ShapeDtypeStruct((bsz, s, qkv), F32),
                   jax.ShapeDtypeStruct((bsz, s, qkv), F32),
                   jax.ShapeDtypeStruct((bsz, s, GROUP_W), F32),
                   jax.ShapeDtypeStruct((bsz, s, n_gate), BF16)],
        compiler_params=_params(("arbitrary", "arbitrary")),
        name="inproj",
    )(x, shift, scale, g1, w_bf, qg, kg, bd)


ATTN_ROWS = ATTN_BLOCK * ATTN_GROUPS[-1][1]
HEADS_PER_SLAB = LANES // HEAD_DIM


def _attn_body(*refs):
    ng = N_GROUPS
    qkv = refs[:3 * ng]
    bias_ref, o_ref = refs[3 * ng], refs[3 * ng + 1]
    scr = refs[3 * ng + 2:]
    first = pl.program_id(2) == 0
    blk = ATTN_BLOCK
    pb = ATTN_ROWS
    hs = HEADS_PER_SLAB
    lane = lax.broadcasted_iota(jnp.int32, (blk, LANES), 1)
    col = lax.broadcasted_iota(jnp.int32, (hs * blk, 2 * blk), 1)
    for g, (_, dil) in enumerate(ATTN_GROUPS):
        q_ref, k_ref, v_ref = qkv[3 * g:3 * g + 3]
        kext, vext, og, lg = scr[4 * g:4 * g + 4]
        span = blk * dil
        bias = bias_ref[g, 0]

        @pl.when(first)
        def _():
            kext[0:span, :] = jnp.zeros((span, LANES), F32)
            vext[0:span, :] = jnp.zeros((span, LANES), F32)
        kext[span:span + pb, :] = k_ref[0]
        vext[span:span + pb, :] = v_ref[0]

        rows = lambda start: pl.ds(start, blk, stride=dil) if dil > 1 else pl.ds(pl.multiple_of(start, blk), blk)
        per = 4

        def units(i, carry):
            bases, v2s, scores = [], [], []
            for j in range(per):
                u = per * i + j
                qb = u // dil
                base = qb * span + (u % dil)
                q = q_ref[0, rows(base), :]
                k2 = jnp.concatenate([kext[rows(base), :], kext[rows(base + span), :]], axis=0).astype(BF16)
                v2s.append(jnp.concatenate([vext[rows(base), :], vext[rows(base + span), :]], axis=0).astype(BF16))
                qs = jnp.concatenate([jnp.where(lane // HEAD_DIM == h, q, 0.0) for h in range(hs)],
                                     axis=0).astype(BF16)
                s = lax.dot_general(qs, k2, (((1,), (1,)), ((), ())), preferred_element_type=F32) + bias
                scores.append(jnp.where(jnp.logical_and(jnp.logical_and(first, qb == 0), col < blk), NEG, s))
                bases.append(base)
            s = jnp.concatenate(scores, axis=0)
            m = jnp.max(s, axis=-1, keepdims=True)
            p = jnp.exp(s - m)
            l = jnp.sum(p, axis=-1, keepdims=True)
            pb16 = p.astype(BF16)
            lse = m + jnp.log(l)
            for j in range(per):
                r0 = j * hs * blk
                o2 = jnp.dot(pb16[r0:r0 + hs * blk], v2s[j], preferred_element_type=F32) / l[r0:r0 + hs * blk]
                o = o2[0:blk]
                lw = jnp.broadcast_to(lse[r0:r0 + blk], (blk, LANES))
                for h in range(1, hs):
                    o = jnp.where(lane // HEAD_DIM == h, o2[h * blk:(h + 1) * blk], o)
                    lw = jnp.where(lane // HEAD_DIM == h, lse[r0 + h * blk:r0 + (h + 1) * blk], lw)
                og[rows(bases[j]), :] = o
                lg[rows(bases[j]), :] = lw
            return carry
        lax.fori_loop(0, pb // blk // per, units, 0)
        kext[0:span, :] = kext[pb:pb + span, :]
        vext[0:span, :] = vext[pb:pb + span, :]

    def merge(c, carry):
        rs = pl.ds(pl.multiple_of(c * blk, blk), blk)
        ls = [scr[4 * g + 3][rs, :] for g in range(ng)]
        mx = jnp.maximum(jnp.maximum(ls[0], ls[1]), ls[2])
        es = [jnp.exp(t - mx) for t in ls]
        den = es[0] + es[1] + es[2]
        acc = (es[0] / den) * scr[2][rs, :]
        for g in range(1, ng):
            acc = acc + (es[g] / den) * scr[4 * g + 2][rs, :]
        o_ref[0, rs, :] = acc.astype(o_ref.dtype)
        return carry
    lax.fori_loop(0, pb // blk, merge, 0)


def _attn_call(q, k, v, bias_tabs):
    bsz, s, qkv = q.shape
    pb = ATTN_ROWS
    n_slab = GROUP_W // LANES
    in_specs, args = [], []
    for g in range(N_GROUPS):
        spec = pl.BlockSpec((1, pb, LANES), lambda b, c, i, g=g: (b, i, g * n_slab + c))
        in_specs += [spec, spec, spec]
        args += [q, k, v]
    in_specs.append(pl.BlockSpec((N_GROUPS, 1) + bias_tabs.shape[2:], lambda b, c, i: (0, c, 0, 0)))
    scratch = []
    for _, dil in ATTN_GROUPS:
        ext = pltpu.VMEM((ATTN_BLOCK * dil + pb, LANES), F32)
        scratch += [ext, ext, pltpu.VMEM((pb, LANES), F32), pltpu.VMEM((pb, LANES), F32)]
    return pl.pallas_call(
        _attn_body,
        grid=(bsz, n_slab, s // pb),
        in_specs=in_specs,
        out_specs=pl.BlockSpec((1, pb, LANES), lambda b, c, i: (b, i, c)),
        out_shape=jax.ShapeDtypeStruct((bsz, s, GROUP_W), BF16),
        scratch_shapes=scratch,
        compiler_params=_params(("arbitrary", "arbitrary", "arbitrary")),
        name="attn",
    )(*args, bias_tabs)


def _tail_body(x_ref, o_ref):
    o_ref[0] = pltpu.einshape("t(he)->the", x_ref[0], h=HEADS)


def _tail_call(x, g, keep):
    bsz, s, _ = x.shape
    tm = min(ROW_TILE, keep)
    first = (s - keep) // tm
    return pl.pallas_call(
        _tail_body,
        grid=(bsz, keep // tm),
        in_specs=[pl.BlockSpec((1, tm, GROUP_W), lambda b, i: (b, first + i, g))],
        out_specs=pl.BlockSpec((1, tm, HEADS, HEAD_DIM), lambda b, i: (b, i, 0, 0)),
        out_shape=jax.ShapeDtypeStruct((bsz, keep, HEADS, HEAD_DIM), x.dtype),
        compiler_params=_params(("arbitrary", "arbitrary")),
        name="kv_tail",
    )(x)


def _decode_body(q_ref, kn_ref, vn_ref, u_ref, pool_ref, kc0, vc0, kc1, vc1, kc2, vc2,
                 bdec_ref, bnew_ref, pw_ref, oa_ref, pooled_ref):
    kcs = (kc0, kc1, kc2)
    vcs = (vc0, vc1, vc2)
    outs, lses = [], []
    for g in range(N_GROUPS):
        hs = slice(g * HEADS, (g + 1) * HEADS)
        q = q_ref[0, hs, :]
        kn = kn_ref[0, hs, :]
        vn = vn_ref[0, hs, :]
        kc = kcs[g][0]
        vc = vcs[g][0]
        s_old = jnp.sum(kc * q[None], axis=-1, keepdims=True) + bdec_ref[g]
        s_new = jnp.sum(kn * q, axis=-1, keepdims=True) + bnew_ref[g]
        m = jnp.maximum(jnp.max(s_old, axis=0), s_new)
        p_old = jnp.exp(s_old - m[None])
        p_new = jnp.exp(s_new - m)
        l = jnp.sum(p_old, axis=0) + p_new
        outs.append((jnp.sum(p_old * vc, axis=0) + p_new * vn) / l)
        lses.append(m + jnp.log(l))
    mx = jnp.maximum(jnp.maximum(lses[0], lses[1]), lses[2])
    es = [jnp.exp(t - mx) for t in lses]
    den = es[0] + es[1] + es[2]
    oa_ref[0] = (es[0] / den) * outs[0] + (es[1] / den) * outs[1] + (es[2] / den) * outs[2]
    u = u_ref[0]
    stored = jnp.sum(pool_ref[0] * pw_ref[0:POOL_STATE, :], axis=0, keepdims=True)
    pooled_ref[0] = stored + u * pw_ref[POOL_STATE:POOL_STATE + 1, :] - u


def _decode_call(q, kn, vn, u, pool_prev, k_caches, v_caches, bdec, bnew, pw):
    db = pool_prev.shape[0]
    per_b = lambda t: pl.BlockSpec((1,) + t.shape[1:], lambda b: (b,) + (0,) * (t.ndim - 1))
    const = lambda t: pl.BlockSpec(t.shape, lambda b: (0,) * t.ndim)
    cache_specs, cache_args = [], []
    for g, (_, dil) in enumerate(ATTN_GROUPS):
        for c in (k_caches[g], v_caches[g]):
            rows = c.shape[1]
            cache_args.append(c.reshape(db, rows // dil, dil, HEADS, HEAD_DIM))
            cache_specs.append(pl.BlockSpec((1, ATTN_STEPS, None, HEADS, HEAD_DIM), lambda b: (b, 0, 0, 0, 0)))
    u3 = u.reshape(db, 1, u.shape[-1])
    return pl.pallas_call(
        _decode_body,
        grid=(db,),
        in_specs=[per_b(q), per_b(kn), per_b(vn), per_b(u3), per_b(pool_prev)] + cache_specs +
                 [const(bdec), const(bnew), const(pw)],
        out_specs=[pl.BlockSpec((1, HEADS, HEAD_DIM), lambda b: (b, 0, 0)),
                   pl.BlockSpec((1, 1, GROUP_W), lambda b: (b, 0, 0))],
        out_shape=[jax.ShapeDtypeStruct((db, HEADS, HEAD_DIM), F32),
                   jax.ShapeDtypeStruct((db, 1, GROUP_W), F32)],
        compiler_params=_params(("arbitrary",)),
        name="decode_mix",
    )(q, kn, vn, u3, pool_prev, *cache_args, bdec, bnew, pw)


def _roll_body(c_ref, nxt_ref, new_ref, o_ref):
    ch = c_ref.shape[1]
    last = pl.program_id(1) == pl.num_programs(1) - 1
    o_ref[0, 0:ch - 1] = c_ref[0, 1:ch]
    o_ref[0, ch - 1:ch] = jnp.where(last, new_ref[0], nxt_ref[0])


def _roll_call(cache, new):
    db, rows, nh, e = cache.shape
    ch = min(rows, 1024)
    return pl.pallas_call(
        _roll_body,
        grid=(db, rows // ch),
        in_specs=[pl.BlockSpec((1, ch, nh, e), lambda b, i: (b, i, 0, 0)),
                  pl.BlockSpec((1, 1, nh, e), lambda b, i: (b, jnp.minimum((i + 1) * ch, rows - 1), 0, 0)),
                  pl.BlockSpec((1, 1, nh, e), lambda b, i: (b, 0, 0, 0))],
        out_specs=pl.BlockSpec((1, ch, nh, e), lambda b, i: (b, i, 0, 0)),
        out_shape=jax.ShapeDtypeStruct(cache.shape, cache.dtype),
        compiler_params=_params(("arbitrary", "arbitrary")),
        name="cache_roll",
    )(cache, cache, new)


def _merge_tail(x, oa, pooled, gates, gate_msa, shift2, scale2, g2, wa, wb, wo, pw_ref, ps,
                x1_ref, h2_ref):
    mixed = jnp.concatenate(
        [jnp.dot(pooled[:, j * POOL_GROUP:(j + 1) * POOL_GROUP].astype(BF16), pw_ref[j],
                 preferred_element_type=F32) for j in range(len(POOL_WINDOWS))], axis=1) * ps
    d = x.shape[-1]
    ga = gates[:, :d].astype(F32)
    gb = gates[:, d:].astype(F32)
    merged = (_sigmoid(ga) * jnp.dot(oa.astype(BF16), wa, preferred_element_type=F32) +
              _sigmoid(gb) * jnp.dot(mixed.astype(BF16), wb, preferred_element_type=F32))
    y = jnp.dot(merged.astype(BF16), wo, preferred_element_type=F32)
    x1 = x + gate_msa * y
    x1_ref[0] = x1
    ms = jnp.mean(x1 * x1, axis=-1, keepdims=True)
    h2 = x1 * lax.rsqrt(ms + EPS) * g2
    _store_rows(h2_ref.at[0], h2 * (1.0 + scale2) + shift2)


def _merge_prompt_body(x_ref, gt_ref, oa_ref, u_ref, uh_ref,
                       gm_ref, sh_ref, sc_ref, g2_ref, wa_ref, wb_ref, wo_ref, pw_ref, ps_ref,
                       x1_ref, h2_ref):
    i = pl.program_id(1)
    tm = x_ref.shape[1]
    oa = oa_ref[0]
    u = u_ref[0]
    halo = jnp.where(i == 0, 0.0, uh_ref[0])
    ext = jnp.concatenate([halo, u], axis=0)
    hw = halo.shape[0]
    acc = ext
    parts = []
    for j, w in enumerate(POOL_WINDOWS):
        acc = acc[:, POOL_GROUP * (1 if j else 0):]
        acc = acc + pltpu.roll(acc, w // 2, axis=0)
        parts.append(acc[hw:, :POOL_GROUP])
    tot = jnp.concatenate(parts, axis=1)
    lane = lax.broadcasted_iota(jnp.int32, (tm, GROUP_W), 1)
    wcol = jnp.left_shift(2, lane // POOL_GROUP)
    pos = i * tm + lax.broadcasted_iota(jnp.int32, (tm, GROUP_W), 0)
    cnt = jnp.minimum(pos + 1, wcol).astype(F32)
    pooled = tot / cnt - u
    _merge_tail(x_ref[0], oa, pooled, gt_ref[0], gm_ref[0], sh_ref[0], sc_ref[0], g2_ref[...],
                wa_ref[...], wb_ref[...], wo_ref[...], pw_ref, ps_ref[...], x1_ref, h2_ref)


def _merge_sample_body(x_ref, gt_ref, oa_ref, pooled_ref,
                       gm_ref, sh_ref, sc_ref, g2_ref, wa_ref, wb_ref, wo_ref, pw_ref, ps_ref,
                       x1_ref, h2_ref):
    _merge_tail(x_ref[0], oa_ref[0], pooled_ref[0], gt_ref[0], gm_ref[0], sh_ref[0], sc_ref[0], g2_ref[...],
                wa_ref[...], wb_ref[...], wo_ref[...], pw_ref, ps_ref[...], x1_ref, h2_ref)


def _merge_call(x, gates, mixer_inputs, mods, g2, wa, wb, wo, pw, ps, prompt):
    bsz, s, d = x.shape
    assert d == ROW_SLABS * LANES
    tm = min(ROW_TILE, s)
    row = lambda t: pl.BlockSpec((1, tm, t.shape[-1]), lambda b, i: (b, i, 0))
    const = lambda t: pl.BlockSpec(t.shape, lambda b, i: (0,) * t.ndim)
    gm, sh, sc = mods
    in_specs = [row(x), row(gates)]
    args = [x, gates]
    if prompt:
        oa, u = mixer_inputs
        hw = 16
        in_specs += [row(oa), row(u),
                     pl.BlockSpec((1, hw, u.shape[-1]), lambda b, i: (b, jnp.maximum(i * (tm // hw) - 1, 0), 0))]
        args += [oa, u, u]
        body = _merge_prompt_body
    else:
        in_specs += [row(t) for t in mixer_inputs]
        args += list(mixer_inputs)
        body = _merge_sample_body
    in_specs += [_mod_spec(m, tm, d) for m in (gm, sh, sc)]
    in_specs += [const(t) for t in (g2, wa, wb, wo, pw, ps)]
    args += [gm, sh, sc, g2, wa, wb, wo, pw, ps]
    return pl.pallas_call(
        body,
        grid=(bsz, s // tm),
        in_specs=in_specs,
        out_specs=[row(x), pl.BlockSpec((1, tm * ROW_SLABS, LANES), lambda b, i: (b, i, 0))],
        out_shape=[jax.ShapeDtypeStruct((bsz, s, d), F32),
                   jax.ShapeDtypeStruct((bsz, s * ROW_SLABS, LANES), F32)],
        compiler_params=_params(("arbitrary", "arbitrary")),
        name="merge_prompt" if prompt else "merge_sample",
    )(*args)


def _router_body(h_ref, rw_ref, rb_ref, tri_ref, idx_ref, wt_ref, rank_ref, cnt_ref, carry):
    tm = h_ref.shape[0] // ROW_SLABS

    @pl.when(pl.program_id(0) == 0)
    def _():
        carry[...] = jnp.zeros_like(carry)

    ne = rw_ref.shape[1]
    gsz = ne // N_EXPERT_GROUPS
    logits = jnp.dot(_load_rows(h_ref, tm), rw_ref[...], precision=HIGHEST, preferred_element_type=F32)
    scores = _sigmoid(logits)
    sel = scores + rb_ref[...]
    lane = lax.broadcasted_iota(jnp.int32, (tm, ne), 1)
    lanef = lane.astype(F32)
    grp = lane // gsz
    ninf = -jnp.inf
    far = float(ne)

    def first_max(x):
        m = jnp.max(x, axis=-1, keepdims=True)
        at = jnp.min(jnp.where(x == m, lanef, far), axis=-1, keepdims=True)
        return m, at

    gs = []
    for g in range(N_EXPERT_GROUPS):
        xg = jnp.where(grp == g, sel, ninf)
        m1, a1 = first_max(xg)
        m2 = jnp.max(jnp.where(lanef == a1, ninf, xg), axis=-1, keepdims=True)
        gs.append(m1 + m2)
    allowed = jnp.zeros((tm, ne), jnp.bool_)
    for g in range(N_EXPERT_GROUPS):
        ahead = jnp.zeros((tm, 1), F32)
        for o in range(N_EXPERT_GROUPS):
            if o == g:
                continue
            beats = (gs[o] > gs[g]) | ((gs[o] == gs[g]) & (o < g))
            ahead = ahead + beats.astype(F32)
        allowed = allowed | ((grp == g) & (ahead < TOPK_GROUPS))
    masked = jnp.where(allowed, sel, ninf)
    lane_o = lax.broadcasted_iota(jnp.int32, (tm, LANES), 1)
    idx_t = jnp.zeros((tm, LANES), F32)
    wt_t = jnp.zeros((tm, LANES), F32)
    wsum = jnp.zeros((tm, 1), F32)
    picked = jnp.zeros((tm, ne), F32)
    ats = []
    for k in range(TOP_K):
        _, at = first_max(masked)
        hit = lanef == at
        wk = jnp.sum(jnp.where(hit, scores, 0.0), axis=-1, keepdims=True)
        masked = jnp.where(hit, ninf, masked)
        picked = picked + hit.astype(F32)
        ats.append(at)
        idx_t = jnp.where(lane_o == k, at, idx_t)
        wt_t = jnp.where(lane_o == k, wk, wt_t)
        wsum = wsum + wk
    idx_ref[...] = idx_t.astype(jnp.int32)
    wt_ref[...] = wt_t / wsum * ROUTED_SCALE
    before = jnp.dot(tri_ref[...], picked.astype(BF16), preferred_element_type=F32) + carry[...]
    rank_t = jnp.zeros((tm, LANES), F32)
    for k in range(TOP_K):
        rk = jnp.sum(jnp.where(lanef == ats[k], before, 0.0), axis=-1, keepdims=True)
        rank_t = jnp.where(lane_o == k, rk, rank_t)
    rank_ref[...] = rank_t.astype(jnp.int32)
    carry[...] = carry[...] + jnp.sum(picked, axis=0, keepdims=True)
    cnt_ref[...] = carry[...].astype(jnp.int32)


def _router_call(h2, rw, rb):
    nt = h2.shape[0] // ROW_SLABS
    tm = ROW_TILE
    d, ne = rw.shape
    tri = jnp.asarray(np.tril(np.ones((tm, tm), np.float32), -1), BF16)
    tile = pl.BlockSpec((tm, LANES), lambda i: (i, 0))
    return pl.pallas_call(
        _router_body,
        grid=(nt // tm,),
        in_specs=[pl.BlockSpec((tm * ROW_SLABS, LANES), lambda i: (i, 0)),
                  pl.BlockSpec((d, ne), lambda i: (0, 0)),
                  pl.BlockSpec((1, ne), lambda i: (0, 0)),
                  pl.BlockSpec((tm, tm), lambda i: (0, 0))],
        out_specs=[tile, tile, tile, pl.BlockSpec((1, ne), lambda i: (0, 0))],
        out_shape=[jax.ShapeDtypeStruct((nt, LANES), jnp.int32),
                   jax.ShapeDtypeStruct((nt, LANES), F32),
                   jax.ShapeDtypeStruct((nt, LANES), jnp.int32),
                   jax.ShapeDtypeStruct((1, ne), jnp.int32)],
        scratch_shapes=[pltpu.VMEM((1, ne), F32)],
        compiler_params=_params(("arbitrary",)),
        name="router",
    )(h2, rw, rb.reshape(1, ne), tri)


def _slot_body(idx_ref, rank_ref, offs_ref, pos_ref):
    tm = idx_ref.shape[0]
    ne = offs_ref.shape[1]
    idx = idx_ref[...]
    rank = rank_ref[...]
    lane = lax.broadcasted_iota(jnp.int32, (tm, ne), 1)
    lane_o = lax.broadcasted_iota(jnp.int32, (tm, LANES), 1)
    offs = offs_ref[...]
    pos = jnp.zeros((tm, LANES), F32)
    for k in range(TOP_K):
        start = jnp.sum(jnp.where(lane == idx[:, k:k + 1], offs, 0.0), axis=-1, keepdims=True)
        pos = jnp.where(lane_o == k, start, pos)
    pos_ref[...] = pos.astype(jnp.int32) + rank


def _slot_call(idx, rank, offs):
    nt = idx.shape[0]
    tm = ROW_TILE
    ne = offs.shape[1]
    tile = pl.BlockSpec((tm, LANES), lambda i: (i, 0))
    return pl.pallas_call(
        _slot_body,
        grid=(nt // tm,),
        in_specs=[tile, tile, pl.BlockSpec((1, ne), lambda i: (0, 0))],
        out_specs=tile,
        out_shape=jax.ShapeDtypeStruct((nt, LANES), jnp.int32),
        compiler_params=_params(("arbitrary",)),
        name="slots",
    )(idx, rank, offs)


def _row_tile(ref, r):
    return ref.at[pl.ds(pl.multiple_of(r * ROW_SLABS, ROW_SLABS), ROW_SLABS), :]


def _dispatch_body(fill_ref, pos_ref, h_ref, xs_hbm, zeros, zsem, sem):
    td = h_ref.shape[0] // ROW_SLABS
    chunk = zeros.shape[0]

    @pl.when(pl.program_id(0) == 0)
    def _():
        zeros[...] = jnp.zeros_like(zeros)

        def fill_chunk(c, wait):
            cp = pltpu.make_async_copy(zeros, xs_hbm.at[pl.ds(pl.multiple_of(c * chunk, chunk), chunk), :], zsem)
            if wait:
                cp.wait()
            else:
                cp.start()

        def fill(j, wait):
            @pl.when(fill_ref[j] >= 0)
            def _():
                fill_chunk(fill_ref[j], wait)
        n_ent = fill_ref.shape[0] - 1
        n_chunks = xs_hbm.shape[0] // chunk
        for wait in (False, True):
            lax.fori_loop(0, n_ent, lambda j, c, wait=wait: (fill(j, wait), c)[1], 0)
            lax.fori_loop(fill_ref[n_ent], n_chunks, lambda b, c, wait=wait: (fill_chunk(b, wait), c)[1], 0)

    def one(t, carry):
        for k in range(TOP_K):
            pltpu.make_async_copy(_row_tile(h_ref, t), _row_tile(xs_hbm, pos_ref[0, 0, t * TOP_K + k]),
                                  sem).start(priority=k % 2)
        return carry
    lax.fori_loop(0, td, one, 0, unroll=2)
    for k in range(TOP_K):
        pltpu.make_async_copy(h_ref, xs_hbm.at[pl.ds(0, td * ROW_SLABS), :], sem).wait()


def _dispatch_call(fill_chunks, pos_tok, h2, n_blocks):
    nt = h2.shape[0] // ROW_SLABS
    td = ROW_TILE
    te = EXPERT_TILE
    grid_spec = pltpu.PrefetchScalarGridSpec(
        num_scalar_prefetch=1,
        grid=(nt // td,),
        in_specs=[pl.BlockSpec((1, 1, td * TOP_K), lambda i, lb: (i, 0, 0), memory_space=pltpu.SMEM),
                  pl.BlockSpec((td * ROW_SLABS, LANES), lambda i, lb: (i, 0))],
        out_specs=pl.BlockSpec(memory_space=pl.ANY),
        scratch_shapes=[pltpu.VMEM((FILL_ROWS * ROW_SLABS, LANES), F32), pltpu.SemaphoreType.DMA,
                        pltpu.SemaphoreType.DMA],
    )
    return pl.pallas_call(
        _dispatch_body,
        grid_spec=grid_spec,
        out_shape=jax.ShapeDtypeStruct((n_blocks * te * ROW_SLABS, LANES), F32),
        compiler_params=_params(("arbitrary",)),
        name="dispatch",
    )(fill_chunks, pos_tok, h2)


def _expert_body(exp_ref, nu_ref, xs_hbm, wg_ref, wu_ref, wd_ref, y_ref, xbuf, xsem, wgb, wub, wdb):
    b = pl.program_id(0)
    n_used = nu_ref[0]
    live = b < n_used
    new_expert = jnp.logical_or(b == 0, exp_ref[b] != exp_ref[jnp.maximum(b - 1, 0)])
    ring, rows = xbuf.shape[0], xbuf.shape[1]

    def fetch(blk):
        slot = blk % ring
        return pltpu.make_async_copy(xs_hbm.at[pl.ds(pl.multiple_of(blk * rows, rows), rows), :],
                                     xbuf.at[slot], xsem.at[slot])

    @pl.when(b == 0)
    def _():
        for j in range(ring - 1):
            @pl.when(j < n_used)
            def _():
                fetch(j).start()

    @pl.when(b + ring - 1 < n_used)
    def _():
        fetch(b + ring - 1).start()

    @pl.when(jnp.logical_and(live, new_expert))
    def _():
        wgb[...] = wg_ref[0].astype(BF16)
        wub[...] = wu_ref[0].astype(BF16)
        wdb[...] = wd_ref[0].astype(BF16)

    @pl.when(live)
    def _():
        te = rows // ROW_SLABS
        fetch(b).wait()
        x = _load_rows(xbuf.at[b % ring], te).astype(BF16)
        gt = jnp.dot(x, wgb[...], preferred_element_type=F32)
        up = jnp.dot(x, wub[...], preferred_element_type=F32)
        act = (gt * _sigmoid(gt)) * up
        _store_rows(y_ref, jnp.dot(act.astype(BF16), wdb[...], preferred_element_type=F32))

    @pl.when(jnp.logical_not(live))
    def _():
        y_ref[...] = jnp.zeros_like(y_ref)


def _expert_call(blk_exp, n_used, xs, wg, wu, wd):
    n_blocks = blk_exp.shape[0]
    te = EXPERT_TILE
    d, hid = wg.shape[-2:]
    used = lambda b, nu: jnp.minimum(b, nu[0] - 1)
    wspec = lambda r, c: pl.BlockSpec((1, r, c), lambda b, e, nu: (e[used(b, nu)], 0, 0))
    grid_spec = pltpu.PrefetchScalarGridSpec(
        num_scalar_prefetch=2,
        grid=(n_blocks,),
        in_specs=[pl.BlockSpec(memory_space=pl.ANY), wspec(d, hid), wspec(d, hid), wspec(hid, d)],
        out_specs=pl.BlockSpec((te * ROW_SLABS, LANES), lambda b, e, nu: (b, 0)),
        scratch_shapes=[pltpu.VMEM((EXPERT_RING, te * ROW_SLABS, LANES), F32), pltpu.SemaphoreType.DMA((EXPERT_RING,)),
                        pltpu.VMEM((d, hid), BF16), pltpu.VMEM((d, hid), BF16), pltpu.VMEM((hid, d), BF16)],
    )
    return pl.pallas_call(
        _expert_body,
        grid_spec=grid_spec,
        out_shape=jax.ShapeDtypeStruct(xs.shape, F32),
        compiler_params=_params(("arbitrary",)),
        name="expert_ffn",
    )(blk_exp, n_used, xs, wg, wu, wd)


def _gather_rows(idx_ref, src_hbm, dst, sem, n):
    def pair(i, carry):
        for p in range(2):
            j = 2 * i + p
            pltpu.make_async_copy(_row_tile(src_hbm, idx_ref[0, 0, j]), _row_tile(dst, j), sem).start(priority=p)
        return carry
    lax.fori_loop(0, n // 2, pair, 0, unroll=4)


def _combine_body(pos_ref, posn_ref, y_hbm, wt_ref, h_ref, x1_ref, gate_ref, sg_ref, su_ref, sd_ref,
                  out_ref, ybuf, sem):
    nb, ni = pl.num_programs(0), pl.num_programs(1)
    step = pl.program_id(0) * ni + pl.program_id(1)
    slot = step % 2
    tc = h_ref.shape[0] // ROW_SLABS
    rows = tc * TOP_K

    @pl.when(step == 0)
    def _():
        _gather_rows(pos_ref, y_hbm, ybuf.at[0], sem.at[0], rows)

    @pl.when(step + 1 < nb * ni)
    def _():
        _gather_rows(posn_ref, y_hbm, ybuf.at[1 - slot], sem.at[1 - slot], rows)

    pltpu.make_async_copy(y_hbm.at[pl.ds(0, rows * ROW_SLABS), :], ybuf.at[slot], sem.at[slot]).wait()
    wt = wt_ref[...]
    routed = jnp.zeros(out_ref.shape[1:], F32)
    for k in range(TOP_K):
        routed = routed + wt[:, k:k + 1] * _load_rows(ybuf.at[slot], tc, first=k * tc)
    hb = _load_rows(h_ref, tc).astype(BF16)
    gt = jnp.dot(hb, sg_ref[...], preferred_element_type=F32)
    up = jnp.dot(hb, su_ref[...], preferred_element_type=F32)
    shared = jnp.dot(((gt * _sigmoid(gt)) * up).astype(BF16), sd_ref[...], preferred_element_type=F32)
    out_ref[0] = x1_ref[0] + gate_ref[0] * (routed + shared)


def _combine_call(pos_tiles, tile_off, y_sorted, wts, h2, x1, gate, sg, su, sd):
    bsz, s, d = x1.shape
    tc = min(COMBINE_TILE, s)
    ni = s // tc
    n_tiles = pos_tiles.shape[0]
    lin = lambda b, i: tile_off + b * ni + i
    const = lambda t: pl.BlockSpec(t.shape, lambda b, i: (0,) * t.ndim)
    return pl.pallas_call(
        _combine_body,
        grid=(bsz, ni),
        in_specs=[
            pl.BlockSpec((1, 1, tc * TOP_K), lambda b, i: (lin(b, i), 0, 0), memory_space=pltpu.SMEM),
            pl.BlockSpec((1, 1, tc * TOP_K), lambda b, i: (jnp.minimum(lin(b, i) + 1, n_tiles - 1), 0, 0),
                         memory_space=pltpu.SMEM),
            pl.BlockSpec(memory_space=pl.ANY),
            pl.BlockSpec((tc, LANES), lambda b, i: (lin(b, i), 0)),
            pl.BlockSpec((tc * ROW_SLABS, LANES), lambda b, i: (lin(b, i), 0)),
            pl.BlockSpec((1, tc, d), lambda b, i: (b, i, 0)),
            _mod_spec(gate, tc, d),
            const(sg), const(su), const(sd),
        ],
        out_specs=pl.BlockSpec((1, tc, d), lambda b, i: (b, i, 0)),
        out_shape=jax.ShapeDtypeStruct((bsz, s, d), F32),
        scratch_shapes=[pltpu.VMEM((2, tc * TOP_K * ROW_SLABS, LANES), F32), pltpu.SemaphoreType.DMA((2,))],
        compiler_params=_params(("arbitrary", "arbitrary")),
        name="combine",
    )(pos_tiles, pos_tiles, y_sorted, wts, h2, x1, gate, sg, su, sd)


def _t5_bucket(dist):
    exact = N_BUCKETS // 2
    d = np.asarray(dist)
    large = exact + (np.log(np.maximum(d, 1) / exact) / np.log(REL_MAX_DIST / exact) * (N_BUCKETS - exact)).astype(np.int32)
    large = np.minimum(large, N_BUCKETS - 1)
    return np.where(d < exact, d, large).astype(np.int32)


def _group_bias(rel_bias, gi):
    window, dil = ATTN_GROUPS[gi]
    bkt = _t5_bucket(np.arange(window // dil + 1) * dil)
    onehot = jnp.asarray(bkt[:, None] == np.arange(N_BUCKETS)[None, :], F32)
    cols = rel_bias[:, gi * HEADS:(gi + 1) * HEADS].astype(F32)
    return jnp.dot(onehot, cols, precision=HIGHEST).T


def _prompt_bias_table(bias_j):
    blk = ATTN_BLOCK
    n = bias_j.shape[1] - 1
    period = 3 * blk
    row0 = jnp.concatenate([jnp.flip(bias_j, axis=1), jnp.full((HEADS, period - n - 1), NEG, F32)], axis=1)
    flat = jnp.tile(row0, (1, blk))[:, :blk * (period - 1)]
    tab = flat.reshape(HEADS, blk, period - 1)[:, :, :2 * blk]
    return tab.reshape(HEADS // HEADS_PER_SLAB, HEADS_PER_SLAB * blk, 2 * blk)


def _decode_bias_tables(bias_j):
    old = jnp.flip(bias_j[:, 1:], axis=1).T[:, :, None]
    new = bias_j[:, 0][:, None]
    return old, new


def _block_schedule(counts, na):
    te = EXPERT_TILE
    n_blocks = na // te + N_EXPERTS
    pcounts = (counts + te - 1) // te * te
    pends = jnp.cumsum(pcounts)
    offs = pends - pcounts
    blk_start = jnp.arange(n_blocks, dtype=jnp.int32) * te
    blk_exp = jnp.minimum(jnp.sum(pends[None, :] <= blk_start[:, None], axis=1), N_EXPERTS - 1).astype(jnp.int32)
    n_used = (pends[-1] // te).astype(jnp.int32).reshape(1)
    fr = FILL_ROWS
    fill_start = offs + counts // fr * fr
    n_fill = (pends - fill_start) // fr
    first = fill_start // fr
    chunks = jnp.concatenate([jnp.where(n_fill > j, first + j, -1) for j in range(te // fr)]
                             + [n_used * (te // fr)]).astype(jnp.int32)
    return offs, blk_exp, n_used, chunks, n_blocks


def kernel(x_prompt, x_sample, cache_k_w128, cache_v_w128, cache_k_w512, cache_v_w512, cache_k_w2048, cache_v_w2048, state_pool, c_prompt, c_sample, ada_w, ada_b, norm1, norm2, w_in, q_gain, k_gain, rel_bias, pool_w, pool_scale, w_br_a, w_br_b, w_out, router_w, router_bias, exp_w_gate, exp_w_up, exp_w_down, sh_w_gate, sh_w_up, sh_w_down):
    depth = ada_w.shape[0]
    bsz, seq, d = x_prompt.shape
    db = x_sample.shape[0]
    assert x_sample.shape[1] == 1 and db <= SAMPLE_PAD and seq % (ATTN_BLOCK * ATTN_GROUPS[-1][1]) == 0
    k_caches = (cache_k_w128, cache_k_w512, cache_k_w2048)
    v_caches = (cache_v_w128, cache_v_w512, cache_v_w2048)
    for c, (w, dil) in zip(k_caches, ATTN_GROUPS):
        assert c.shape[2] == w == ATTN_STEPS * dil

    bias_js = [_group_bias(rel_bias, g) for g in range(N_GROUPS)]
    bias_tabs = jnp.stack([_prompt_bias_table(b) for b in bias_js])
    dec = [_decode_bias_tables(b) for b in bias_js]
    bdec = jnp.stack([t[0] for t in dec])
    bnew = jnp.stack([t[1] for t in dec])
    bd = jnp.asarray((np.arange(MXU_DIM)[:, None] // HEAD_DIM == np.arange(MXU_DIM)[None, :] // HEAD_DIM)
                     / HEAD_DIM, BF16)
    wcols = np.repeat(np.asarray(POOL_WINDOWS), POOL_GROUP)
    pw_dec = jnp.asarray(np.where(np.arange(POOL_STATE + 1)[:, None] >= POOL_STATE + 1 - wcols[None, :],
                                  1.0 / wcols[None, :], 0.0), F32)

    xp = x_prompt
    xs = jnp.pad(x_sample.reshape(db, d), ((0, SAMPLE_PAD - db), (0, 0))).reshape(1, SAMPLE_PAD, d)
    n_prompt = bsz * seq
    outs = {name: [] for name in ('pk', 'pv', 'sk', 'sv')}
    outs['pk'] = [[] for _ in range(N_GROUPS)]
    outs['pv'] = [[] for _ in range(N_GROUPS)]
    outs['sk'] = [[] for _ in range(N_GROUPS)]
    outs['sv'] = [[] for _ in range(N_GROUPS)]
    ppool, spool = [], []
    for l in range(depth):
        c_all = jnp.concatenate([c_prompt, c_sample], axis=0)
        mod = _ada_call(c_all, ada_w[l], ada_b[l])
        mod_p = [m.reshape(bsz, 1, d) for m in jnp.split(mod[:bsz], 6, axis=-1)]
        mod_s = [jnp.pad(m, ((0, SAMPLE_PAD - db), (0, 0))).reshape(1, SAMPLE_PAD, d)
                 for m in jnp.split(mod[bsz:], 6, axis=-1)]
        w_in_bf = w_in[l].astype(BF16)
        qg = (jnp.tile(q_gain[l], (1, HEADS)) * (HEAD_DIM ** -0.5)).reshape(1, -1)
        kg = jnp.tile(k_gain[l], (1, HEADS)).reshape(1, -1)
        g1 = norm1[l].reshape(1, d)
        g2 = norm2[l].reshape(1, d)
        wa, wb, wo = (w_br_a[l].astype(BF16), w_br_b[l].astype(BF16), w_out[l].astype(BF16))
        pw = pool_w[l].astype(BF16)
        ps = pool_scale[l].reshape(1, -1)

        q, k, v, u, gates = _inproj_call(xp, mod_p[0], mod_p[1], g1, w_in_bf, qg, kg, bd)
        oa_p = _attn_call(q, k, v, bias_tabs)
        for g, (window, dil) in enumerate(ATTN_GROUPS):
            keep = min(window, seq)
            cs = slice(g * GROUP_W, (g + 1) * GROUP_W)
            outs['pk'][g].append(_tail_call(k, g, keep))
            outs['pv'][g].append(_tail_call(v, g, keep))
        ppool.append(u[:, seq - POOL_STATE:])
        x1p, h2p = _merge_call(xp, gates, (oa_p, u), (mod_p[2], mod_p[3], mod_p[4]),
                               g2, wa, wb, wo, pw, ps, prompt=True)

        qs, ks, vs, us, gates_s = _inproj_call(xs, mod_s[0], mod_s[1], g1, w_in_bf, qg, kg, bd)
        heads3 = lambda t: t[0, :db].reshape(db, N_GROUPS * HEADS, HEAD_DIM)
        q3, k3, v3 = heads3(qs), heads3(ks), heads3(vs)
        kcl = [c[l] for c in k_caches]
        vcl = [c[l] for c in v_caches]
        oa_s, pooled_s = _decode_call(q3, k3, v3, us[0, :db], state_pool[l], kcl, vcl, bdec, bnew, pw_dec)
        padrows = lambda t: jnp.pad(t.reshape(db, GROUP_W), ((0, SAMPLE_PAD - db), (0, 0))).reshape(1, SAMPLE_PAD, GROUP_W)
        x1s, h2s = _merge_call(xs, gates_s, (padrows(oa_s), padrows(pooled_s)), (mod_s[2], mod_s[3], mod_s[4]),
                               g2, wa, wb, wo, pw, ps, prompt=False)
        for g in range(N_GROUPS):
            hs = slice(g * HEADS, (g + 1) * HEADS)
            outs['sk'][g].append(_roll_call(kcl[g], k3[:, None, hs]))
            outs['sv'][g].append(_roll_call(vcl[g], v3[:, None, hs]))
        spool.append(jnp.concatenate([state_pool[l][:, 1:], us[0, :db, None, :]], axis=1))

        h2 = jnp.concatenate([h2p.reshape(n_prompt * ROW_SLABS, LANES), h2s.reshape(SAMPLE_PAD * ROW_SLABS, LANES)], axis=0)
        nt = n_prompt + SAMPLE_PAD
        idx, wts, rank, counts = _router_call(h2, router_w[l], router_bias[l])
        offs, blk_exp, n_used, fill_chunks, n_blocks = _block_schedule(counts[0], nt * TOP_K)
        pos = _slot_call(idx, rank, offs.astype(F32).reshape(1, -1))[:, :TOP_K]
        xs_sorted = _dispatch_call(fill_chunks, pos.reshape(nt // ROW_TILE, 1, ROW_TILE * TOP_K), h2, n_blocks)
        y_sorted = _expert_call(blk_exp, n_used, xs_sorted, exp_w_gate[l], exp_w_up[l], exp_w_down[l])
        tc = COMBINE_TILE
        pos_tiles = pos.reshape(nt // tc, tc, TOP_K).transpose(0, 2, 1).reshape(nt // tc, 1, TOP_K * tc)
        sg, su, sd = sh_w_gate[l].astype(BF16), sh_w_up[l].astype(BF16), sh_w_down[l].astype(BF16)
        xp = _combine_call(pos_tiles, 0, y_sorted, wts, h2, x1p, mod_p[5], sg, su, sd)
        xs = _combine_call(pos_tiles, n_prompt // tc, y_sorted, wts, h2, x1s, mod_s[5], sg, su, sd)

    stack = lambda parts: jnp.stack(parts)
    res = [xp, xs[0, :db].reshape(db, 1, d)]
    for g in range(N_GROUPS):
        res += [stack(outs['pk'][g]), stack(outs['pv'][g])]
    res.append(stack(ppool))
    for g in range(N_GROUPS):
        res += [stack(outs['sk'][g]), stack(outs['sv'][g])]
    res.append(stack(spool))
    return tuple(res)
```

```python
import functools

import numpy as np
import jax
import jax.numpy as jnp
from jax import lax
from jax.experimental import pallas as pl
from jax.experimental.pallas import tpu as pltpu

F32 = jnp.float32
BF16 = jnp.bfloat16
HIGHEST = lax.Precision.HIGHEST

HEAD_DIM = 64
HEADS = 8
GROUP_W = HEADS * HEAD_DIM
ATTN_GROUPS = ((128, 1), (512, 4), (2048, 16))
N_GROUPS = len(ATTN_GROUPS)
ATTN_STEPS = 128
ATTN_BLOCK = 128
POOL_WINDOWS = (2, 4, 8, 16)
POOL_GROUP = 128
POOL_STATE = 15
N_BUCKETS = 32
REL_MAX_DIST = 2048
N_EXPERTS = 256
TOP_K = 8
N_EXPERT_GROUPS = 8
TOPK_GROUPS = 4
ROUTED_SCALE = 2.5
EPS = 1e-6
NEG = -1e30

LANES = 128
MXU_DIM = 256
VMEM_LIMIT = 56 * 1024 * 1024

INPROJ_TILE = 512
ROW_TILE = 256
EXPERT_TILE = 512
FILL_ROWS = 256
EXPERT_RING = 3
COMBINE_TILE = 128
SAMPLE_PAD = 256


def _sigmoid(x):
    return 1.0 / (1.0 + jnp.exp(-x))


def _params(sem):
    return pltpu.CompilerParams(dimension_semantics=sem, vmem_limit_bytes=VMEM_LIMIT)


ROW_SLABS = 8


def _load_rows(ref, n, first=0):
    return jnp.concatenate([ref[pl.ds(first * ROW_SLABS + s, n, stride=ROW_SLABS), :]
                            for s in range(ROW_SLABS)], axis=1)


def _store_rows(ref, val):
    n = val.shape[0]
    for s in range(ROW_SLABS):
        ref[pl.ds(s, n, stride=ROW_SLABS), :] = val[:, s * LANES:(s + 1) * LANES]


def _ada_body(c_ref, w_ref, b_ref, o_ref):
    c = c_ref[...]
    s = c * _sigmoid(c)
    o_ref[...] = jnp.dot(s, w_ref[...], precision=HIGHEST, preferred_element_type=F32) + b_ref[...]


def _ada_call(c, w, b):
    n, d = c.shape
    cols = w.shape[1]
    tn = 512
    return pl.pallas_call(
        _ada_body,
        grid=(cols // tn,),
        in_specs=[pl.BlockSpec((n, d), lambda j: (0, 0)),
                  pl.BlockSpec((d, tn), lambda j: (0, j)),
                  pl.BlockSpec((1, tn), lambda j: (0, j))],
        out_specs=pl.BlockSpec((n, tn), lambda j: (0, j)),
        out_shape=jax.ShapeDtypeStruct((n, cols), F32),
        compiler_params=_params(("arbitrary",)),
        name="adaln",
    )(c, w, b.reshape(1, cols))


def _inproj_body(x_ref, shift_ref, scale_ref, g_ref, w_ref, qg_ref, kg_ref, bd_ref,
                 q_ref, k_ref, v_ref, u_ref, gt_ref):
    x = x_ref[0]
    ms = jnp.mean(x * x, axis=-1, keepdims=True)
    h = x * lax.rsqrt(ms + EPS) * g_ref[...]
    h = h * (1.0 + scale_ref[0]) + shift_ref[0]
    hb = h.astype(BF16)
    qkv = N_GROUPS * GROUP_W
    n_chunks = w_ref.shape[1] // GROUP_W
    for c in range(n_chunks):
        z = jnp.dot(hb, w_ref[:, c * GROUP_W:(c + 1) * GROUP_W], preferred_element_type=F32)
        if c < 2 * N_GROUPS:
            zz = (z * z).astype(BF16)
            hms = jnp.concatenate(
                [jnp.dot(zz[:, s:s + MXU_DIM], bd_ref[...], preferred_element_type=F32)
                 for s in range(0, GROUP_W, MXU_DIM)], axis=1)
            g = c % N_GROUPS
            gain = (qg_ref if c < N_GROUPS else kg_ref)[:, g * GROUP_W:(g + 1) * GROUP_W]
            y = z * lax.rsqrt(hms + EPS) * gain
            if c < N_GROUPS:
                q_ref[0, :, g * GROUP_W:(g + 1) * GROUP_W] = y
            else:
                k_ref[0, :, g * GROUP_W:(g + 1) * GROUP_W] = y
        elif c < 3 * N_GROUPS:
            g = c - 2 * N_GROUPS
            v_ref[0, :, g * GROUP_W:(g + 1) * GROUP_W] = z
        elif c == 3 * N_GROUPS:
            u_ref[0] = z
        else:
            g = c - 3 * N_GROUPS - 1
            gt_ref[0, :, g * GROUP_W:(g + 1) * GROUP_W] = z.astype(BF16)
    del qkv


def _mod_spec(mod, tm, d):
    if mod.shape[1] == 1:
        return pl.BlockSpec((1, 1, d), lambda b, i: (b, 0, 0))
    return pl.BlockSpec((1, tm, d), lambda b, i: (b, i, 0))


def _inproj_call(x, shift, scale, g1, w_bf, qg, kg, bd):
    bsz, s, d = x.shape
    tm = min(INPROJ_TILE, s)
    qkv = N_GROUPS * GROUP_W
    n_gate = w_bf.shape[1] - 3 * qkv - GROUP_W
    const = lambda b, i: (0, 0)
    row = lambda width: pl.BlockSpec((1, tm, width), lambda b, i: (b, i, 0))
    return pl.pallas_call(
        _inproj_body,
        grid=(bsz, s // tm),
        in_specs=[row(d), _mod_spec(shift, tm, d), _mod_spec(scale, tm, d),
                  pl.BlockSpec((1, d), const),
                  pl.BlockSpec(w_bf.shape, const, pipeline_mode=pl.Buffered(1)),
                  pl.BlockSpec((1, qkv), const), pl.BlockSpec((1, qkv), const),
                  pl.BlockSpec((MXU_DIM, MXU_DIM), const)],
        out_specs=[row(qkv), row(qkv), row(qkv), row(GROUP_W), row(n_gate)],
        out_shape=[jax.ShapeDtypeStruct((bsz, s, qkv), F32),
                   jax.ShapeDtypeStruct((bsz, s, qkv), F32),
                   jax.ShapeDtypeStruct((bsz, s, qkv), F32),
                   jax.ShapeDtypeStruct((bsz, s, GROUP_W), F32),
                   jax.ShapeDtypeStruct((bsz, s, n_gate), BF16)],
        compiler_params=_params(("arbitrary", "arbitrary")),
        name="inproj",
    )(x, shift, scale, g1, w_bf, qg, kg, bd)


ATTN_ROWS = ATTN_BLOCK * ATTN_GROUPS[-1][1]
HEADS_PER_SLAB = LANES // HEAD_DIM


def _attn_body(*refs):
    ng = N_GROUPS
    qkv = refs[:3 * ng]
    bias_ref, o_ref = refs[3 * ng], refs[3 * ng + 1]
    scr = refs[3 * ng + 2:]
    first = pl.program_id(2) == 0
    blk = ATTN_BLOCK
    pb = ATTN_ROWS
    hs = HEADS_PER_SLAB
    lane = lax.broadcasted_iota(jnp.int32, (blk, LANES), 1)
    col = lax.broadcasted_iota(jnp.int32, (hs * blk, 2 * blk), 1)
    for g, (_, dil) in enumerate(ATTN_GROUPS):
        q_ref, k_ref, v_ref = qkv[3 * g:3 * g + 3]
        kext, vext, og, lg = scr[4 * g:4 * g + 4]
        span = blk * dil
        bias = bias_ref[g, 0]

        @pl.when(first)
        def _():
            kext[0:span, :] = jnp.zeros((span, LANES), F32)
            vext[0:span, :] = jnp.zeros((span, LANES), F32)
        kext[span:span + pb, :] = k_ref[0]
        vext[span:span + pb, :] = v_ref[0]

        rows = lambda start: pl.ds(start, blk, stride=dil) if dil > 1 else pl.ds(pl.multiple_of(start, blk), blk)
        per = 4

        def units(i, carry):
            bases, v2s, scores = [], [], []
            for j in range(per):
                u = per * i + j
                qb = u // dil
                base = qb * span + (u % dil)
                q = q_ref[0, rows(base), :]
                k2 = jnp.concatenate([kext[rows(base), :], kext[rows(base + span), :]], axis=0).astype(BF16)
                v2s.append(jnp.concatenate([vext[rows(base), :], vext[rows(base + span), :]], axis=0).astype(BF16))
                qs = jnp.concatenate([jnp.where(lane // HEAD_DIM == h, q, 0.0) for h in range(hs)],
                                     axis=0).astype(BF16)
                s = lax.dot_general(qs, k2, (((1,), (1,)), ((), ())), preferred_element_type=F32) + bias
                scores.append(jnp.where(jnp.logical_and(jnp.logical_and(first, qb == 0), col < blk), NEG, s))
                bases.append(base)
            s = jnp.concatenate(scores, axis=0)
            m = jnp.max(s, axis=-1, keepdims=True)
            p = jnp.exp(s - m)
            l = jnp.sum(p, axis=-1, keepdims=True)
            pb16 = p.astype(BF16)
            lse = m + jnp.log(l)
            for j in range(per):
                r0 = j * hs * blk
                o2 = jnp.dot(pb16[r0:r0 + hs * blk], v2s[j], preferred_element_type=F32) / l[r0:r0 + hs * blk]
                o = o2[0:blk]
                lw = jnp.broadcast_to(lse[r0:r0 + blk], (blk, LANES))
                for h in range(1, hs):
                    o = jnp.where(lane // HEAD_DIM == h, o2[h * blk:(h + 1) * blk], o)
                    lw = jnp.where(lane // HEAD_DIM == h, lse[r0 + h * blk:r0 + (h + 1) * blk], lw)
                og[rows(bases[j]), :] = o
                lg[rows(bases[j]), :] = lw
            return carry
        lax.fori_loop(0, pb // blk // per, units, 0)
        kext[0:span, :] = kext[pb:pb + span, :]
        vext[0:span, :] = vext[pb:pb + span, :]

    def merge(c, carry):
        rs = pl.ds(pl.multiple_of(c * blk, blk), blk)
        ls = [scr[4 * g + 3][rs, :] for g in range(ng)]
        mx = jnp.maximum(jnp.maximum(ls[0], ls[1]), ls[2])
        es = [jnp.exp(t - mx) for t in ls]
        den = es[0] + es[1] + es[2]
        acc = (es[0] / den) * scr[2][rs, :]
        for g in range(1, ng):
            acc = acc + (es[g] / den) * scr[4 * g + 2][rs, :]
        o_ref[0, rs, :] = acc.astype(o_ref.dtype)
        return carry
    lax.fori_loop(0, pb // blk, merge, 0)


def _attn_call(q, k, v, bias_tabs):
    bsz, s, qkv = q.shape
    pb = ATTN_ROWS
    n_slab = GROUP_W // LANES
    in_specs, args = [], []
    for g in range(N_GROUPS):
        spec = pl.BlockSpec((1, pb, LANES), lambda b, c, i, g=g: (b, i, g * n_slab + c))
        in_specs += [spec, spec, spec]
        args += [q, k, v]
    in_specs.append(pl.BlockSpec((N_GROUPS, 1) + bias_tabs.shape[2:], lambda b, c, i: (0, c, 0, 0)))
    scratch = []
    for _, dil in ATTN_GROUPS:
        ext = pltpu.VMEM((ATTN_BLOCK * dil + pb, LANES), F32)
        scratch += [ext, ext, pltpu.VMEM((pb, LANES), F32), pltpu.VMEM((pb, LANES), F32)]
    return pl.pallas_call(
        _attn_body,
        grid=(bsz, n_slab, s // pb),
        in_specs=in_specs,
        out_specs=pl.BlockSpec((1, pb, LANES), lambda b, c, i: (b, i, c)),
        out_shape=jax.ShapeDtypeStruct((bsz, s, GROUP_W), BF16),
        scratch_shapes=scratch,
        compiler_params=_params(("arbitrary", "arbitrary", "arbitrary")),
        name="attn",
    )(*args, bias_tabs)


def _decode_body(q_ref, kn_ref, vn_ref, u_ref, pool_ref, kc0, vc0, kc1, vc1, kc2, vc2,
                 bdec_ref, bnew_ref, pw_ref, oa_ref, pooled_ref):
    kcs = (kc0, kc1, kc2)
    vcs = (vc0, vc1, vc2)
    outs, lses = [], []
    for g in range(N_GROUPS):
        hs = slice(g * HEADS, (g + 1) * HEADS)
        q = q_ref[0, hs, :]
        kn = kn_ref[0, hs, :]
        vn = vn_ref[0, hs, :]
        kc = kcs[g][0]
        vc = vcs[g][0]
        s_old = jnp.sum(kc * q[None], axis=-1, keepdims=True) + bdec_ref[g]
        s_new = jnp.sum(kn * q, axis=-1, keepdims=True) + bnew_ref[g]
        m = jnp.maximum(jnp.max(s_old, axis=0), s_new)
        p_old = jnp.exp(s_old - m[None])
        p_new = jnp.exp(s_new - m)
        l = jnp.sum(p_old, axis=0) + p_new
        outs.append((jnp.sum(p_old * vc, axis=0) + p_new * vn) / l)
        lses.append(m + jnp.log(l))
    mx = jnp.maximum(jnp.maximum(lses[0], lses[1]), lses[2])
    es = [jnp.exp(t - mx) for t in lses]
    den = es[0] + es[1] + es[2]
    oa_ref[0] = (es[0] / den) * outs[0] + (es[1] / den) * outs[1] + (es[2] / den) * outs[2]
    u = u_ref[0]
    stored = jnp.sum(pool_ref[0] * pw_ref[0:POOL_STATE, :], axis=0, keepdims=True)
    pooled_ref[0] = stored + u * pw_ref[POOL_STATE:POOL_STATE + 1, :] - u


def _decode_call(q, kn, vn, u, pool_prev, k_caches, v_caches, bdec, bnew, pw):
    db = pool_prev.shape[0]
    per_b = lambda t: pl.BlockSpec((1,) + t.shape[1:], lambda b: (b,) + (0,) * (t.ndim - 1))
    const = lambda t: pl.BlockSpec(t.shape, lambda b: (0,) * t.ndim)
    cache_specs, cache_args = [], []
    for g, (_, dil) in enumerate(ATTN_GROUPS):
        for c in (k_caches[g], v_caches[g]):
            rows = c.shape[1]
            cache_args.append(c.reshape(db, rows // dil, dil, HEADS, HEAD_DIM)[:, :, 0])
            cache_specs.append(pl.BlockSpec((1, ATTN_STEPS, HEADS, HEAD_DIM), lambda b: (b, 0, 0, 0)))
    u3 = u.reshape(db, 1, u.shape[-1])
    return pl.pallas_call(
        _decode_body,
        grid=(db,),
        in_specs=[per_b(q), per_b(kn), per_b(vn), per_b(u3), per_b(pool_prev)] + cache_specs +
                 [const(bdec), const(bnew), const(pw)],
        out_specs=[pl.BlockSpec((1, HEADS, HEAD_DIM), lambda b: (b, 0, 0)),
                   pl.BlockSpec((1, 1, GROUP_W), lambda b: (b, 0, 0))],
        out_shape=[jax.ShapeDtypeStruct((db, HEADS, HEAD_DIM), F32),
                   jax.ShapeDtypeStruct((db, 1, GROUP_W), F32)],
        compiler_params=_params(("arbitrary",)),
        name="decode_mix",
    )(q, kn, vn, u3, pool_prev, *cache_args, bdec, bnew, pw)


def _roll_body(c_ref, nxt_ref, new_ref, o_ref):
    ch = c_ref.shape[1]
    last = pl.program_id(1) == pl.num_programs(1) - 1
    o_ref[0, 0:ch - 1] = c_ref[0, 1:ch]
    o_ref[0, ch - 1:ch] = jnp.where(last, new_ref[0], nxt_ref[0])


def _roll_call(cache, new):
    db, rows, nh, e = cache.shape
    ch = min(rows, 1024)
    return pl.pallas_call(
        _roll_body,
        grid=(db, rows // ch),
        in_specs=[pl.BlockSpec((1, ch, nh, e), lambda b, i: (b, i, 0, 0)),
                  pl.BlockSpec((1, 1, nh, e), lambda b, i: (b, jnp.minimum((i + 1) * ch, rows - 1), 0, 0)),
                  pl.BlockSpec((1, 1, nh, e), lambda b, i: (b, 0, 0, 0))],
        out_specs=pl.BlockSpec((1, ch, nh, e), lambda b, i: (b, i, 0, 0)),
        out_shape=jax.ShapeDtypeStruct(cache.shape, cache.dtype),
        compiler_params=_params(("arbitrary", "arbitrary")),
        name="cache_roll",
    )(cache, cache, new)


def _merge_tail(x, oa, pooled, gates, gate_msa, shift2, scale2, g2, wa, wb, wo, pw_ref, ps,
                x1_ref, h2_ref):
    mixed = jnp.concatenate(
        [jnp.dot(pooled[:, j * POOL_GROUP:(j + 1) * POOL_GROUP].astype(BF16), pw_ref[j],
                 preferred_element_type=F32) for j in range(len(POOL_WINDOWS))], axis=1) * ps
    d = x.shape[-1]
    ga = gates[:, :d].astype(F32)
    gb = gates[:, d:].astype(F32)
    merged = (_sigmoid(ga) * jnp.dot(oa.astype(BF16), wa, preferred_element_type=F32) +
              _sigmoid(gb) * jnp.dot(mixed.astype(BF16), wb, preferred_element_type=F32))
    y = jnp.dot(merged.astype(BF16), wo, preferred_element_type=F32)
    x1 = x + gate_msa * y
    x1_ref[0] = x1
    ms = jnp.mean(x1 * x1, axis=-1, keepdims=True)
    h2 = x1 * lax.rsqrt(ms + EPS) * g2
    _store_rows(h2_ref.at[0], h2 * (1.0 + scale2) + shift2)


def _merge_prompt_body(x_ref, gt_ref, oa_ref, u_ref, uh_ref,
                       gm_ref, sh_ref, sc_ref, g2_ref, wa_ref, wb_ref, wo_ref, pw_ref, ps_ref,
                       x1_ref, h2_ref):
    i = pl.program_id(1)
    tm = x_ref.shape[1]
    oa = oa_ref[0]
    u = u_ref[0]
    halo = jnp.where(i == 0, 0.0, uh_ref[0])
    ext = jnp.concatenate([halo, u], axis=0)
    hw = halo.shape[0]
    acc = ext
    parts = []
    for j, w in enumerate(POOL_WINDOWS):
        acc = acc[:, POOL_GROUP * (1 if j else 0):]
        acc = acc + pltpu.roll(acc, w // 2, axis=0)
        parts.append(acc[hw:, :POOL_GROUP])
    tot = jnp.concatenate(parts, axis=1)
    lane = lax.broadcasted_iota(jnp.int32, (tm, GROUP_W), 1)
    wcol = jnp.left_shift(2, lane // POOL_GROUP)
    pos = i * tm + lax.broadcasted_iota(jnp.int32, (tm, GROUP_W), 0)
    cnt = jnp.minimum(pos + 1, wcol).astype(F32)
    pooled = tot / cnt - u
    _merge_tail(x_ref[0], oa, pooled, gt_ref[0], gm_ref[0], sh_ref[0], sc_ref[0], g2_ref[...],
                wa_ref[...], wb_ref[...], wo_ref[...], pw_ref, ps_ref[...], x1_ref, h2_ref)


def _merge_sample_body(x_ref, gt_ref, oa_ref, pooled_ref,
                       gm_ref, sh_ref, sc_ref, g2_ref, wa_ref, wb_ref, wo_ref, pw_ref, ps_ref,
                       x1_ref, h2_ref):
    _merge_tail(x_ref[0], oa_ref[0], pooled_ref[0], gt_ref[0], gm_ref[0], sh_ref[0], sc_ref[0], g2_ref[...],
                wa_ref[...], wb_ref[...], wo_ref[...], pw_ref, ps_ref[...], x1_ref, h2_ref)


def _merge_call(x, gates, mixer_inputs, mods, g2, wa, wb, wo, pw, ps, prompt):
    bsz, s, d = x.shape
    assert d == ROW_SLABS * LANES
    tm = min(ROW_TILE, s)
    row = lambda t: pl.BlockSpec((1, tm, t.shape[-1]), lambda b, i: (b, i, 0))
    const = lambda t: pl.BlockSpec(t.shape, lambda b, i: (0,) * t.ndim)
    gm, sh, sc = mods
    in_specs = [row(x), row(gates)]
    args = [x, gates]
    if prompt:
        oa, u = mixer_inputs
        hw = 16
        in_specs += [row(oa), row(u),
                     pl.BlockSpec((1, hw, u.shape[-1]), lambda b, i: (b, jnp.maximum(i * (tm // hw) - 1, 0), 0))]
        args += [oa, u, u]
        body = _merge_prompt_body
    else:
        in_specs += [row(t) for t in mixer_inputs]
        args += list(mixer_inputs)
        body = _merge_sample_body
    in_specs += [_mod_spec(m, tm, d) for m in (gm, sh, sc)]
    in_specs += [const(t) for t in (g2, wa, wb, wo, pw, ps)]
    args += [gm, sh, sc, g2, wa, wb, wo, pw, ps]
    return pl.pallas_call(
        body,
        grid=(bsz, s // tm),
        in_specs=in_specs,
        out_specs=[row(x), pl.BlockSpec((1, tm * ROW_SLABS, LANES), lambda b, i: (b, i, 0))],
        out_shape=[jax.ShapeDtypeStruct((bsz, s, d), F32),
                   jax.ShapeDtypeStruct((bsz, s * ROW_SLABS, LANES), F32)],
        compiler_params=_params(("arbitrary", "arbitrary")),
        name="merge_prompt" if prompt else "merge_sample",
    )(*args)


def _router_body(h_ref, rw_ref, rb_ref, tri_ref, idx_ref, wt_ref, rank_ref, cnt_ref, carry):
    tm = h_ref.shape[0] // ROW_SLABS

    @pl.when(pl.program_id(0) == 0)
    def _():
        carry[...] = jnp.zeros_like(carry)

    ne = rw_ref.shape[1]
    gsz = ne // N_EXPERT_GROUPS
    logits = jnp.dot(_load_rows(h_ref, tm), rw_ref[...], precision=HIGHEST, preferred_element_type=F32)
    scores = _sigmoid(logits)
    sel = scores + rb_ref[...]
    lane = lax.broadcasted_iota(jnp.int32, (tm, ne), 1)
    lanef = lane.astype(F32)
    grp = lane // gsz
    ninf = -jnp.inf
    far = float(ne)

    def first_max(x):
        m = jnp.max(x, axis=-1, keepdims=True)
        at = jnp.min(jnp.where(x == m, lanef, far), axis=-1, keepdims=True)
        return m, at

    gs = []
    for g in range(N_EXPERT_GROUPS):
        xg = jnp.where(grp == g, sel, ninf)
        m1, a1 = first_max(xg)
        m2 = jnp.max(jnp.where(lanef == a1, ninf, xg), axis=-1, keepdims=True)
        gs.append(m1 + m2)
    allowed = jnp.zeros((tm, ne), jnp.bool_)
    for g in range(N_EXPERT_GROUPS):
        ahead = jnp.zeros((tm, 1), F32)
        for o in range(N_EXPERT_GROUPS):
            if o == g:
                continue
            beats = (gs[o] > gs[g]) | ((gs[o] == gs[g]) & (o < g))
            ahead = ahead + beats.astype(F32)
        allowed = allowed | ((grp == g) & (ahead < TOPK_GROUPS))
    masked = jnp.where(allowed, sel, ninf)
    lane_o = lax.broadcasted_iota(jnp.int32, (tm, LANES), 1)
    idx_t = jnp.zeros((tm, LANES), F32)
    wt_t = jnp.zeros((tm, LANES), F32)
    wsum = jnp.zeros((tm, 1), F32)
    picked = jnp.zeros((tm, ne), F32)
    ats = []
    for k in range(TOP_K):
        _, at = first_max(masked)
        hit = lanef == at
        wk = jnp.sum(jnp.where(hit, scores, 0.0), axis=-1, keepdims=True)
        masked = jnp.where(hit, ninf, masked)
        picked = picked + hit.astype(F32)
        ats.append(at)
        idx_t = jnp.where(lane_o == k, at, idx_t)
        wt_t = jnp.where(lane_o == k, wk, wt_t)
        wsum = wsum + wk
    idx_ref[...] = idx_t.astype(jnp.int32)
    wt_ref[...] = wt_t / wsum * ROUTED_SCALE
    before = jnp.dot(tri_ref[...], picked.astype(BF16), preferred_element_type=F32) + carry[...]
    rank_t = jnp.zeros((tm, LANES), F32)
    for k in range(TOP_K):
        rk = jnp.sum(jnp.where(lanef == ats[k], before, 0.0), axis=-1, keepdims=True)
        rank_t = jnp.where(lane_o == k, rk, rank_t)
    rank_ref[...] = rank_t.astype(jnp.int32)
    carry[...] = carry[...] + jnp.sum(picked, axis=0, keepdims=True)
    cnt_ref[...] = carry[...].astype(jnp.int32)


def _router_call(h2, rw, rb):
    nt = h2.shape[0] // ROW_SLABS
    tm = ROW_TILE
    d, ne = rw.shape
    tri = jnp.asarray(np.tril(np.ones((tm, tm), np.float32), -1), BF16)
    tile = pl.BlockSpec((tm, LANES), lambda i: (i, 0))
    return pl.pallas_call(
        _router_body,
        grid=(nt // tm,),
        in_specs=[pl.BlockSpec((tm * ROW_SLABS, LANES), lambda i: (i, 0)),
                  pl.BlockSpec((d, ne), lambda i: (0, 0)),
                  pl.BlockSpec((1, ne), lambda i: (0, 0)),
                  pl.BlockSpec((tm, tm), lambda i: (0, 0))],
        out_specs=[tile, tile, tile, pl.BlockSpec((1, ne), lambda i: (0, 0))],
        out_shape=[jax.ShapeDtypeStruct((nt, LANES), jnp.int32),
                   jax.ShapeDtypeStruct((nt, LANES), F32),
                   jax.ShapeDtypeStruct((nt, LANES), jnp.int32),
                   jax.ShapeDtypeStruct((1, ne), jnp.int32)],
        scratch_shapes=[pltpu.VMEM((1, ne), F32)],
        compiler_params=_params(("arbitrary",)),
        name="router",
    )(h2, rw, rb.reshape(1, ne), tri)


def _slot_body(idx_ref, rank_ref, offs_ref, pos_ref):
    tm = idx_ref.shape[0]
    ne = offs_ref.shape[1]
    idx = idx_ref[...]
    rank = rank_ref[...]
    lane = lax.broadcasted_iota(jnp.int32, (tm, ne), 1)
    lane_o = lax.broadcasted_iota(jnp.int32, (tm, LANES), 1)
    offs = offs_ref[...]
    pos = jnp.zeros((tm, LANES), F32)
    for k in range(TOP_K):
        start = jnp.sum(jnp.where(lane == idx[:, k:k + 1], offs, 0.0), axis=-1, keepdims=True)
        pos = jnp.where(lane_o == k, start, pos)
    pos_ref[...] = pos.astype(jnp.int32) + rank


def _slot_call(idx, rank, offs):
    nt = idx.shape[0]
    tm = ROW_TILE
    ne = offs.shape[1]
    tile = pl.BlockSpec((tm, LANES), lambda i: (i, 0))
    return pl.pallas_call(
        _slot_body,
        grid=(nt // tm,),
        in_specs=[tile, tile, pl.BlockSpec((1, ne), lambda i: (0, 0))],
        out_specs=tile,
        out_shape=jax.ShapeDtypeStruct((nt, LANES), jnp.int32),
        compiler_params=_params(("arbitrary",)),
        name="slots",
    )(idx, rank, offs)


def _row_tile(ref, r):
    return ref.at[pl.ds(pl.multiple_of(r * ROW_SLABS, ROW_SLABS), ROW_SLABS), :]


def _dispatch_body(fill_ref, pos_ref, h_ref, xs_hbm, zeros, zsem, sem):
    td = h_ref.shape[0] // ROW_SLABS
    chunk = zeros.shape[0]

    @pl.when(pl.program_id(0) == 0)
    def _():
        zeros[...] = jnp.zeros_like(zeros)

        def fill_chunk(c, wait):
            cp = pltpu.make_async_copy(zeros, xs_hbm.at[pl.ds(pl.multiple_of(c * chunk, chunk), chunk), :], zsem)
            if wait:
                cp.wait()
            else:
                cp.start()

        def fill(j, wait):
            @pl.when(fill_ref[j] >= 0)
            def _():
                fill_chunk(fill_ref[j], wait)
        n_ent = fill_ref.shape[0] - 1
        n_chunks = xs_hbm.shape[0] // chunk
        for wait in (False, True):
            lax.fori_loop(0, n_ent, lambda j, c, wait=wait: (fill(j, wait), c)[1], 0)
            lax.fori_loop(fill_ref[n_ent], n_chunks, lambda b, c, wait=wait: (fill_chunk(b, wait), c)[1], 0)

    def one(t, carry):
        for k in range(TOP_K):
            pltpu.make_async_copy(_row_tile(h_ref, t), _row_tile(xs_hbm, pos_ref[0, 0, t * TOP_K + k]),
                                  sem).start(priority=k % 2)
        return carry
    lax.fori_loop(0, td, one, 0, unroll=2)
    for k in range(TOP_K):
        pltpu.make_async_copy(h_ref, xs_hbm.at[pl.ds(0, td * ROW_SLABS), :], sem).wait()


def _dispatch_call(fill_chunks, pos_tok, h2, n_blocks):
    nt = h2.shape[0] // ROW_SLABS
    td = ROW_TILE
    te = EXPERT_TILE
    grid_spec = pltpu.PrefetchScalarGridSpec(
        num_scalar_prefetch=1,
        grid=(nt // td,),
        in_specs=[pl.BlockSpec((1, 1, td * TOP_K), lambda i, lb: (i, 0, 0), memory_space=pltpu.SMEM),
                  pl.BlockSpec((td * ROW_SLABS, LANES), lambda i, lb: (i, 0))],
        out_specs=pl.BlockSpec(memory_space=pl.ANY),
        scratch_shapes=[pltpu.VMEM((FILL_ROWS * ROW_SLABS, LANES), F32), pltpu.SemaphoreType.DMA,
                        pltpu.SemaphoreType.DMA],
    )
    return pl.pallas_call(
        _dispatch_body,
        grid_spec=grid_spec,
        out_shape=jax.ShapeDtypeStruct((n_blocks * te * ROW_SLABS, LANES), F32),
        compiler_params=_params(("arbitrary",)),
        name="dispatch",
    )(fill_chunks, pos_tok, h2)


def _expert_body(exp_ref, nu_ref, xs_hbm, wg_ref, wu_ref, wd_ref, y_ref, xbuf, xsem, wgb, wub, wdb):
    b = pl.program_id(0)
    n_used = nu_ref[0]
    live = b < n_used
    new_expert = jnp.logical_or(b == 0, exp_ref[b] != exp_ref[jnp.maximum(b - 1, 0)])
    ring, rows = xbuf.shape[0], xbuf.shape[1]

    def fetch(blk):
        slot = blk % ring
        return pltpu.make_async_copy(xs_hbm.at[pl.ds(pl.multiple_of(blk * rows, rows), rows), :],
                                     xbuf.at[slot], xsem.at[slot])

    @pl.when(b == 0)
    def _():
        for j in range(ring - 1):
            @pl.when(j < n_used)
            def _():
                fetch(j).start()

    @pl.when(b + ring - 1 < n_used)
    def _():
        fetch(b + ring - 1).start()

    @pl.when(jnp.logical_and(live, new_expert))
    def _():
        wgb[...] = wg_ref[0].astype(BF16)
        wub[...] = wu_ref[0].astype(BF16)
        wdb[...] = wd_ref[0].astype(BF16)

    @pl.when(live)
    def _():
        te = rows // ROW_SLABS
        fetch(b).wait()
        x = _load_rows(xbuf.at[b % ring], te).astype(BF16)
        gt = jnp.dot(x, wgb[...], preferred_element_type=F32)
        up = jnp.dot(x, wub[...], preferred_element_type=F32)
        act = (gt * _sigmoid(gt)) * up
        _store_rows(y_ref, jnp.dot(act.astype(BF16), wdb[...], preferred_element_type=F32))

    @pl.when(jnp.logical_not(live))
    def _():
        y_ref[...] = jnp.zeros_like(y_ref)


def _expert_call(blk_exp, n_used, xs, wg, wu, wd):
    n_blocks = blk_exp.shape[0]
    te = EXPERT_TILE
    d, hid = wg.shape[-2:]
    used = lambda b, nu: jnp.minimum(b, nu[0] - 1)
    wspec = lambda r, c: pl.BlockSpec((1, r, c), lambda b, e, nu: (e[used(b, nu)], 0, 0))
    grid_spec = pltpu.PrefetchScalarGridSpec(
        num_scalar_prefetch=2,
        grid=(n_blocks,),
        in_specs=[pl.BlockSpec(memory_space=pl.ANY), wspec(d, hid), wspec(d, hid), wspec(hid, d)],
        out_specs=pl.BlockSpec((te * ROW_SLABS, LANES), lambda b, e, nu: (b, 0)),
        scratch_shapes=[pltpu.VMEM((EXPERT_RING, te * ROW_SLABS, LANES), F32), pltpu.SemaphoreType.DMA((EXPERT_RING,)),
                        pltpu.VMEM((d, hid), BF16), pltpu.VMEM((d, hid), BF16), pltpu.VMEM((hid, d), BF16)],
    )
    return pl.pallas_call(
        _expert_body,
        grid_spec=grid_spec,
        out_shape=jax.ShapeDtypeStruct(xs.shape, F32),
        compiler_params=_params(("arbitrary",)),
        name="expert_ffn",
    )(blk_exp, n_used, xs, wg, wu, wd)


def _gather_rows(idx_ref, src_hbm, dst, sem, n):
    def pair(i, carry):
        for p in range(2):
            j = 2 * i + p
            pltpu.make_async_copy(_row_tile(src_hbm, idx_ref[0, 0, j]), _row_tile(dst, j), sem).start(priority=p)
        return carry
    lax.fori_loop(0, n // 2, pair, 0, unroll=4)


def _combine_body(pos_ref, posn_ref, y_hbm, wt_ref, h_ref, x1_ref, gate_ref, sg_ref, su_ref, sd_ref,
                  out_ref, ybuf, sem):
    nb, ni = pl.num_programs(0), pl.num_programs(1)
    step = pl.program_id(0) * ni + pl.program_id(1)
    slot = step % 2
    tc = h_ref.shape[0] // ROW_SLABS
    rows = tc * TOP_K

    @pl.when(step == 0)
    def _():
        _gather_rows(pos_ref, y_hbm, ybuf.at[0], sem.at[0], rows)

    @pl.when(step + 1 < nb * ni)
    def _():
        _gather_rows(posn_ref, y_hbm, ybuf.at[1 - slot], sem.at[1 - slot], rows)

    pltpu.make_async_copy(y_hbm.at[pl.ds(0, rows * ROW_SLABS), :], ybuf.at[slot], sem.at[slot]).wait()
    wt = wt_ref[...]
    routed = jnp.zeros(out_ref.shape[1:], F32)
    for k in range(TOP_K):
        routed = routed + wt[:, k:k + 1] * _load_rows(ybuf.at[slot], tc, first=k * tc)
    hb = _load_rows(h_ref, tc).astype(BF16)
    gt = jnp.dot(hb, sg_ref[...], preferred_element_type=F32)
    up = jnp.dot(hb, su_ref[...], preferred_element_type=F32)
    shared = jnp.dot(((gt * _sigmoid(gt)) * up).astype(BF16), sd_ref[...], preferred_element_type=F32)
    out_ref[0] = x1_ref[0] + gate_ref[0] * (routed + shared)


def _combine_call(pos_tiles, tile_off, y_sorted, wts, h2, x1, gate, sg, su, sd):
    bsz, s, d = x1.shape
    tc = min(COMBINE_TILE, s)
    ni = s // tc
    n_tiles = pos_tiles.shape[0]
    lin = lambda b, i: tile_off + b * ni + i
    const = lambda t: pl.BlockSpec(t.shape, lambda b, i: (0,) * t.ndim)
    return pl.pallas_call(
        _combine_body,
        grid=(bsz, ni),
        in_specs=[
            pl.BlockSpec((1, 1, tc * TOP_K), lambda b, i: (lin(b, i), 0, 0), memory_space=pltpu.SMEM),
            pl.BlockSpec((1, 1, tc * TOP_K), lambda b, i: (jnp.minimum(lin(b, i) + 1, n_tiles - 1), 0, 0),
                         memory_space=pltpu.SMEM),
            pl.BlockSpec(memory_space=pl.ANY),
            pl.BlockSpec((tc, LANES), lambda b, i: (lin(b, i), 0)),
            pl.BlockSpec((tc * ROW_SLABS, LANES), lambda b, i: (lin(b, i), 0)),
            pl.BlockSpec((1, tc, d), lambda b, i: (b, i, 0)),
            _mod_spec(gate, tc, d),
            const(sg), const(su), const(sd),
        ],
        out_specs=pl.BlockSpec((1, tc, d), lambda b, i: (b, i, 0)),
        out_shape=jax.ShapeDtypeStruct((bsz, s, d), F32),
        scratch_shapes=[pltpu.VMEM((2, tc * TOP_K * ROW_SLABS, LANES), F32), pltpu.SemaphoreType.DMA((2,))],
        compiler_params=_params(("arbitrary", "arbitrary")),
        name="combine",
    )(pos_tiles, pos_tiles, y_sorted, wts, h2, x1, gate, sg, su, sd)


def _t5_bucket(dist):
    exact = N_BUCKETS // 2
    d = np.asarray(dist)
    large = exact + (np.log(np.maximum(d, 1) / exact) / np.log(REL_MAX_DIST / exact) * (N_BUCKETS - exact)).astype(np.int32)
    large = np.minimum(large, N_BUCKETS - 1)
    return np.where(d < exact, d, large).astype(np.int32)


def _group_bias(rel_bias, gi):
    window, dil = ATTN_GROUPS[gi]
    bkt = _t5_bucket(np.arange(window // dil + 1) * dil)
    onehot = jnp.asarray(bkt[:, None] == np.arange(N_BUCKETS)[None, :], F32)
    cols = rel_bias[:, gi * HEADS:(gi + 1) * HEADS].astype(F32)
    return jnp.dot(onehot, cols, precision=HIGHEST).T


def _prompt_bias_table(bias_j):
    blk = ATTN_BLOCK
    n = bias_j.shape[1] - 1
    period = 3 * blk
    row0 = jnp.concatenate([jnp.flip(bias_j, axis=1), jnp.full((HEADS, period - n - 1), NEG, F32)], axis=1)
    flat = jnp.tile(row0, (1, blk))[:, :blk * (period - 1)]
    tab = flat.reshape(HEADS, blk, period - 1)[:, :, :2 * blk]
    return tab.reshape(HEADS // HEADS_PER_SLAB, HEADS_PER_SLAB * blk, 2 * blk)


def _decode_bias_tables(bias_j):
    old = jnp.flip(bias_j[:, 1:], axis=1).T[:, :, None]
    new = bias_j[:, 0][:, None]
    return old, new


def _block_schedule(counts, na):
    te = EXPERT_TILE
    n_blocks = na // te + N_EXPERTS
    pcounts = (counts + te - 1) // te * te
    pends = jnp.cumsum(pcounts)
    offs = pends - pcounts
    blk_start = jnp.arange(n_blocks, dtype=jnp.int32) * te
    blk_exp = jnp.minimum(jnp.sum(pends[None, :] <= blk_start[:, None], axis=1), N_EXPERTS - 1).astype(jnp.int32)
    n_used = (pends[-1] // te).astype(jnp.int32).reshape(1)
    fr = FILL_ROWS
    fill_start = offs + counts // fr * fr
    n_fill = (pends - fill_start) // fr
    first = fill_start // fr
    chunks = jnp.concatenate([jnp.where(n_fill > j, first + j, -1) for j in range(te // fr)]
                             + [n_used * (te // fr)]).astype(jnp.int32)
    return offs, blk_exp, n_used, chunks, n_blocks


def kernel(x_prompt, x_sample, cache_k_w128, cache_v_w128, cache_k_w512, cache_v_w512, cache_k_w2048, cache_v_w2048, state_pool, c_prompt, c_sample, ada_w, ada_b, norm1, norm2, w_in, q_gain, k_gain, rel_bias, pool_w, pool_scale, w_br_a, w_br_b, w_out, router_w, router_bias, exp_w_gate, exp_w_up, exp_w_down, sh_w_gate, sh_w_up, sh_w_down):
    depth = ada_w.shape[0]
    bsz, seq, d = x_prompt.shape
    db = x_sample.shape[0]
    assert x_sample.shape[1] == 1 and db <= SAMPLE_PAD and seq % (ATTN_BLOCK * ATTN_GROUPS[-1][1]) == 0
    k_caches = (cache_k_w128, cache_k_w512, cache_k_w2048)
    v_caches = (cache_v_w128, cache_v_w512, cache_v_w2048)
    for c, (w, dil) in zip(k_caches, ATTN_GROUPS):
        assert c.shape[2] == w == ATTN_STEPS * dil

    bias_js = [_group_bias(rel_bias, g) for g in range(N_GROUPS)]
    bias_tabs = jnp.stack([_prompt_bias_table(b) for b in bias_js])
    dec = [_decode_bias_tables(b) for b in bias_js]
    bdec = jnp.stack([t[0] for t in dec])
    bnew = jnp.stack([t[1] for t in dec])
    bd = jnp.asarray((np.arange(MXU_DIM)[:, None] // HEAD_DIM == np.arange(MXU_DIM)[None, :] // HEAD_DIM)
                     / HEAD_DIM, BF16)
    wcols = np.repeat(np.asarray(POOL_WINDOWS), POOL_GROUP)
    pw_dec = jnp.asarray(np.where(np.arange(POOL_STATE + 1)[:, None] >= POOL_STATE + 1 - wcols[None, :],
                                  1.0 / wcols[None, :], 0.0), F32)

    xp = x_prompt
    xs = jnp.pad(x_sample.reshape(db, d), ((0, SAMPLE_PAD - db), (0, 0))).reshape(1, SAMPLE_PAD, d)
    n_prompt = bsz * seq
    outs = {name: [] for name in ('pk', 'pv', 'sk', 'sv')}
    outs['pk'] = [[] for _ in range(N_GROUPS)]
    outs['pv'] = [[] for _ in range(N_GROUPS)]
    outs['sk'] = [[] for _ in range(N_GROUPS)]
    outs['sv'] = [[] for _ in range(N_GROUPS)]
    ppool, spool = [], []
    for l in range(depth):
        c_all = jnp.concatenate([c_prompt, c_sample], axis=0)
        mod = _ada_call(c_all, ada_w[l], ada_b[l])
        mod_p = [m.reshape(bsz, 1, d) for m in jnp.split(mod[:bsz], 6, axis=-1)]
        mod_s = [jnp.pad(m, ((0, SAMPLE_PAD - db), (0, 0))).reshape(1, SAMPLE_PAD, d)
                 for m in jnp.split(mod[bsz:], 6, axis=-1)]
        w_in_bf = w_in[l].astype(BF16)
        qg = (jnp.tile(q_gain[l], (1, HEADS)) * (HEAD_DIM ** -0.5)).reshape(1, -1)
        kg = jnp.tile(k_gain[l], (1, HEADS)).reshape(1, -1)
        g1 = norm1[l].reshape(1, d)
        g2 = norm2[l].reshape(1, d)
        wa, wb, wo = (w_br_a[l].astype(BF16), w_br_b[l].astype(BF16), w_out[l].astype(BF16))
        pw = pool_w[l].astype(BF16)
        ps = pool_scale[l].reshape(1, -1)

        q, k, v, u, gates = _inproj_call(xp, mod_p[0], mod_p[1], g1, w_in_bf, qg, kg, bd)
        oa_p = _attn_call(q, k, v, bias_tabs)
        for g, (window, dil) in enumerate(ATTN_GROUPS):
            keep = min(window, seq)
            cs = slice(g * GROUP_W, (g + 1) * GROUP_W)
            outs['pk'][g].append(k[:, seq - keep:, cs].reshape(bsz, keep, HEADS, HEAD_DIM))
            outs['pv'][g].append(v[:, seq - keep:, cs].reshape(bsz, keep, HEADS, HEAD_DIM))
        ppool.append(u[:, seq - POOL_STATE:])
        x1p, h2p = _merge_call(xp, gates, (oa_p, u), (mod_p[2], mod_p[3], mod_p[4]),
                               g2, wa, wb, wo, pw, ps, prompt=True)

        qs, ks, vs, us, gates_s = _inproj_call(xs, mod_s[0], mod_s[1], g1, w_in_bf, qg, kg, bd)
        heads3 = lambda t: t[0, :db].reshape(db, N_GROUPS * HEADS, HEAD_DIM)
        q3, k3, v3 = heads3(qs), heads3(ks), heads3(vs)
        kcl = [c[l] for c in k_caches]
        vcl = [c[l] for c in v_caches]
        oa_s, pooled_s = _decode_call(q3, k3, v3, us[0, :db], state_pool[l], kcl, vcl, bdec, bnew, pw_dec)
        padrows = lambda t: jnp.pad(t.reshape(db, GROUP_W), ((0, SAMPLE_PAD - db), (0, 0))).reshape(1, SAMPLE_PAD, GROUP_W)
        x1s, h2s = _merge_call(xs, gates_s, (padrows(oa_s), padrows(pooled_s)), (mod_s[2], mod_s[3], mod_s[4]),
                               g2, wa, wb, wo, pw, ps, prompt=False)
        for g in range(N_GROUPS):
            hs = slice(g * HEADS, (g + 1) * HEADS)
            outs['sk'][g].append(jnp.concatenate([kcl[g][:, 1:], k3[:, None, hs]], axis=1))
            outs['sv'][g].append(jnp.concatenate([vcl[g][:, 1:], v3[:, None, hs]], axis=1))
        spool.append(jnp.concatenate([state_pool[l][:, 1:], us[0, :db, None, :]], axis=1))

        h2 = jnp.concatenate([h2p.reshape(n_prompt * ROW_SLABS, LANES), h2s.reshape(SAMPLE_PAD * ROW_SLABS, LANES)], axis=0)
        nt = n_prompt + SAMPLE_PAD
        idx, wts, rank, counts = _router_call(h2, router_w[l], router_bias[l])
        offs, blk_exp, n_used, fill_chunks, n_blocks = _block_schedule(counts[0], nt * TOP_K)
        pos = _slot_call(idx, rank, offs.astype(F32).reshape(1, -1))[:, :TOP_K]
        xs_sorted = _dispatch_call(fill_chunks, pos.reshape(nt // ROW_TILE, 1, ROW_TILE * TOP_K), h2, n_blocks)
        y_sorted = _expert_call(blk_exp, n_used, xs_sorted, exp_w_gate[l], exp_w_up[l], exp_w_down[l])
        tc = COMBINE_TILE
        pos_tiles = pos.reshape(nt // tc, tc, TOP_K).transpose(0, 2, 1).reshape(nt // tc, 1, TOP_K * tc)
        sg, su, sd = sh_w_gate[l].astype(BF16), sh_w_up[l].astype(BF16), sh_w_down[l].astype(BF16)
        xp = _combine_call(pos_tiles, 0, y_sorted, wts, h2, x1p, mod_p[5], sg, su, sd)
        xs = _combine_call(pos_tiles, n_prompt // tc, y_sorted, wts, h2, x1s, mod_s[5], sg, su, sd)

    stack = lambda parts: jnp.stack(parts)
    res = [xp, xs[0, :db].reshape(db, 1, d)]
    for g in range(N_GROUPS):
        res += [stack(outs['pk'][g]), stack(outs['pv'][g])]
    res.append(stack(ppool))
    for g in range(N_GROUPS):
        res += [stack(outs['sk'][g]), stack(outs['sv'][g])]
    res.append(stack(spool))
    return tuple(res)
```

```python
import functools

import numpy as np
import jax
import jax.numpy as jnp
from jax import lax
from jax.experimental import pallas as pl
from jax.experimental.pallas import tpu as pltpu

F32 = jnp.float32
BF16 = jnp.bfloat16
HIGHEST = lax.Precision.HIGHEST

HEAD_DIM = 64
HEADS = 8
GROUP_W = HEADS * HEAD_DIM
ATTN_GROUPS = ((128, 1), (512, 4), (2048, 16))
N_GROUPS = len(ATTN_GROUPS)
ATTN_STEPS = 128
ATTN_BLOCK = 128
POOL_WINDOWS = (2, 4, 8, 16)
POOL_GROUP = 128
POOL_STATE = 15
N_BUCKETS = 32
REL_MAX_DIST = 2048
N_EXPERTS = 256
TOP_K = 8
N_EXPERT_GROUPS = 8
TOPK_GROUPS = 4
ROUTED_SCALE = 2.5
EPS = 1e-6
NEG = -1e30

LANES = 128
MXU_DIM = 256
VMEM_LIMIT = 56 * 1024 * 1024

INPROJ_TILE = 512
ROW_TILE = 256
EXPERT_TILE = 512
FILL_ROWS = 256
EXPERT_RING = 3
COMBINE_TILE = 128
SAMPLE_PAD = 256


def _sigmoid(x):
    return 1.0 / (1.0 + jnp.exp(-x))


def _params(sem):
    return pltpu.CompilerParams(dimension_semantics=sem, vmem_limit_bytes=VMEM_LIMIT)


ROW_SLABS = 8


def _load_rows(ref, n, first=0):
    return jnp.concatenate([ref[pl.ds(first * ROW_SLABS + s, n, stride=ROW_SLABS), :]
                            for s in range(ROW_SLABS)], axis=1)


def _store_rows(ref, val):
    n = val.shape[0]
    for s in range(ROW_SLABS):
        ref[pl.ds(s, n, stride=ROW_SLABS), :] = val[:, s * LANES:(s + 1) * LANES]


def _ada_body(c_ref, w_ref, b_ref, o_ref):
    c = c_ref[...]
    s = c * _sigmoid(c)
    o_ref[...] = jnp.dot(s, w_ref[...], precision=HIGHEST, preferred_element_type=F32) + b_ref[...]


def _ada_call(c, w, b):
    n, d = c.shape
    cols = w.shape[1]
    tn = 512
    return pl.pallas_call(
        _ada_body,
        grid=(cols // tn,),
        in_specs=[pl.BlockSpec((n, d), lambda j: (0, 0)),
                  pl.BlockSpec((d, tn), lambda j: (0, j)),
                  pl.BlockSpec((1, tn), lambda j: (0, j))],
        out_specs=pl.BlockSpec((n, tn), lambda j: (0, j)),
        out_shape=jax.ShapeDtypeStruct((n, cols), F32),
        compiler_params=_params(("arbitrary",)),
        name="adaln",
    )(c, w, b.reshape(1, cols))


def _inproj_body(x_ref, shift_ref, scale_ref, g_ref, w_ref, qg_ref, kg_ref, bd_ref,
                 q_ref, k_ref, v_ref, u_ref, gt_ref):
    x = x_ref[0]
    ms = jnp.mean(x * x, axis=-1, keepdims=True)
    h = x * lax.rsqrt(ms + EPS) * g_ref[...]
    h = h * (1.0 + scale_ref[0]) + shift_ref[0]
    hb = h.astype(BF16)
    qkv = N_GROUPS * GROUP_W
    n_chunks = w_ref.shape[1] // GROUP_W
    for c in range(n_chunks):
        z = jnp.dot(hb, w_ref[:, c * GROUP_W:(c + 1) * GROUP_W], preferred_element_type=F32)
        if c < 2 * N_GROUPS:
            zz = (z * z).astype(BF16)
            hms = jnp.concatenate(
                [jnp.dot(zz[:, s:s + MXU_DIM], bd_ref[...], preferred_element_type=F32)
                 for s in range(0, GROUP_W, MXU_DIM)], axis=1)
            g = c % N_GROUPS
            gain = (qg_ref if c < N_GROUPS else kg_ref)[:, g * GROUP_W:(g + 1) * GROUP_W]
            y = z * lax.rsqrt(hms + EPS) * gain
            if c < N_GROUPS:
                q_ref[0, :, g * GROUP_W:(g + 1) * GROUP_W] = y
            else:
                k_ref[0, :, g * GROUP_W:(g + 1) * GROUP_W] = y
        elif c < 3 * N_GROUPS:
            g = c - 2 * N_GROUPS
            v_ref[0, :, g * GROUP_W:(g + 1) * GROUP_W] = z
        elif c == 3 * N_GROUPS:
            u_ref[0] = z
        else:
            g = c - 3 * N_GROUPS - 1
            gt_ref[0, :, g * GROUP_W:(g + 1) * GROUP_W] = z.astype(BF16)
    del qkv


def _mod_spec(mod, tm, d):
    if mod.shape[1] == 1:
        return pl.BlockSpec((1, 1, d), lambda b, i: (b, 0, 0))
    return pl.BlockSpec((1, tm, d), lambda b, i: (b, i, 0))


def _inproj_call(x, shift, scale, g1, w_bf, qg, kg, bd):
    bsz, s, d = x.shape
    tm = min(INPROJ_TILE, s)
    qkv = N_GROUPS * GROUP_W
    n_gate = w_bf.shape[1] - 3 * qkv - GROUP_W
    const = lambda b, i: (0, 0)
    row = lambda width: pl.BlockSpec((1, tm, width), lambda b, i: (b, i, 0))
    return pl.pallas_call(
        _inproj_body,
        grid=(bsz, s // tm),
        in_specs=[row(d), _mod_spec(shift, tm, d), _mod_spec(scale, tm, d),
                  pl.BlockSpec((1, d), const),
                  pl.BlockSpec(w_bf.shape, const, pipeline_mode=pl.Buffered(1)),
                  pl.BlockSpec((1, qkv), const), pl.BlockSpec((1, qkv), const),
                  pl.BlockSpec((MXU_DIM, MXU_DIM), const)],
        out_specs=[row(qkv), row(qkv), row(qkv), row(GROUP_W), row(n_gate)],
        out_shape=[jax.ShapeDtypeStruct((bsz, s, qkv), F32),
                   jax.ShapeDtypeStruct((bsz, s, qkv), F32),
                   jax.ShapeDtypeStruct((bsz, s, qkv), F32),
                   jax.ShapeDtypeStruct((bsz, s, GROUP_W), F32),
                   jax.ShapeDtypeStruct((bsz, s, n_gate), BF16)],
        compiler_params=_params(("arbitrary", "arbitrary")),
        name="inproj",
    )(x, shift, scale, g1, w_bf, qg, kg, bd)


ATTN_ROWS = ATTN_BLOCK * ATTN_GROUPS[-1][1]
HEADS_PER_SLAB = LANES // HEAD_DIM


def _attn_body(*refs):
    ng = N_GROUPS
    qkv = refs[:3 * ng]
    bias_ref, o_ref = refs[3 * ng], refs[3 * ng + 1]
    scr = refs[3 * ng + 2:]
    first = pl.program_id(2) == 0
    blk = ATTN_BLOCK
    pb = ATTN_ROWS
    hs = HEADS_PER_SLAB
    lane = lax.broadcasted_iota(jnp.int32, (blk, LANES), 1)
    col = lax.broadcasted_iota(jnp.int32, (hs * blk, 2 * blk), 1)
    for g, (_, dil) in enumerate(ATTN_GROUPS):
        q_ref, k_ref, v_ref = qkv[3 * g:3 * g + 3]
        kext, vext, og, lg = scr[4 * g:4 * g + 4]
        span = blk * dil
        bias = bias_ref[g, 0]

        @pl.when(first)
        def _():
            kext[0:span, :] = jnp.zeros((span, LANES), F32)
            vext[0:span, :] = jnp.zeros((span, LANES), F32)
        kext[span:span + pb, :] = k_ref[0]
        vext[span:span + pb, :] = v_ref[0]

        rows = lambda start: pl.ds(start, blk, stride=dil) if dil > 1 else pl.ds(pl.multiple_of(start, blk), blk)
        per = 4

        def units(i, carry):
            bases, v2s, scores = [], [], []
            for j in range(per):
                u = per * i + j
                qb = u // dil
                base = qb * span + (u % dil)
                q = q_ref[0, rows(base), :]
                k2 = jnp.concatenate([kext[rows(base), :], kext[rows(base + span), :]], axis=0).astype(BF16)
                v2s.append(jnp.concatenate([vext[rows(base), :], vext[rows(base + span), :]], axis=0).astype(BF16))
                qs = jnp.concatenate([jnp.where(lane // HEAD_DIM == h, q, 0.0) for h in range(hs)],
                                     axis=0).astype(BF16)
                s = lax.dot_general(qs, k2, (((1,), (1,)), ((), ())), preferred_element_type=F32) + bias
                scores.append(jnp.where(jnp.logical_and(jnp.logical_and(first, qb == 0), col < blk), NEG, s))
                bases.append(base)
            s = jnp.concatenate(scores, axis=0)
            m = jnp.max(s, axis=-1, keepdims=True)
            p = jnp.exp(s - m)
            l = jnp.sum(p, axis=-1, keepdims=True)
            pb16 = p.astype(BF16)
            lse = m + jnp.log(l)
            for j in range(per):
                r0 = j * hs * blk
                o2 = jnp.dot(pb16[r0:r0 + hs * blk], v2s[j], preferred_element_type=F32) / l[r0:r0 + hs * blk]
                o = o2[0:blk]
                lw = jnp.broadcast_to(lse[r0:r0 + blk], (blk, LANES))
                for h in range(1, hs):
                    o = jnp.where(lane // HEAD_DIM == h, o2[h * blk:(h + 1) * blk], o)
                    lw = jnp.where(lane // HEAD_DIM == h, lse[r0 + h * blk:r0 + (h + 1) * blk], lw)
                og[rows(bases[j]), :] = o
                lg[rows(bases[j]), :] = lw
            return carry
        lax.fori_loop(0, pb // blk // per, units, 0)
        kext[0:span, :] = kext[pb:pb + span, :]
        vext[0:span, :] = vext[pb:pb + span, :]

    def merge(c, carry):
        rs = pl.ds(pl.multiple_of(c * blk, blk), blk)
        ls = [scr[4 * g + 3][rs, :] for g in range(ng)]
        mx = jnp.maximum(jnp.maximum(ls[0], ls[1]), ls[2])
        es = [jnp.exp(t - mx) for t in ls]
        den = es[0] + es[1] + es[2]
        acc = (es[0] / den) * scr[2][rs, :]
        for g in range(1, ng):
            acc = acc + (es[g] / den) * scr[4 * g + 2][rs, :]
        o_ref[0, rs, :] = acc.astype(o_ref.dtype)
        return carry
    lax.fori_loop(0, pb // blk, merge, 0)


def _attn_call(q, k, v, bias_tabs):
    bsz, s, qkv = q.shape
    pb = ATTN_ROWS
    n_slab = GROUP_W // LANES
    in_specs, args = [], []
    for g in range(N_GROUPS):
        spec = pl.BlockSpec((1, pb, LANES), lambda b, c, i, g=g: (b, i, g * n_slab + c))
        in_specs += [spec, spec, spec]
        args += [q, k, v]
    in_specs.append(pl.BlockSpec((N_GROUPS, 1) + bias_tabs.shape[2:], lambda b, c, i: (0, c, 0, 0)))
    scratch = []
    for _, dil in ATTN_GROUPS:
        ext = pltpu.VMEM((ATTN_BLOCK * dil + pb, LANES), F32)
        scratch += [ext, ext, pltpu.VMEM((pb, LANES), F32), pltpu.VMEM((pb, LANES), F32)]
    return pl.pallas_call(
        _attn_body,
        grid=(bsz, n_slab, s // pb),
        in_specs=in_specs,
        out_specs=pl.BlockSpec((1, pb, LANES), lambda b, c, i: (b, i, c)),
        out_shape=jax.ShapeDtypeStruct((bsz, s, GROUP_W), BF16),
        scratch_shapes=scratch,
        compiler_params=_params(("arbitrary", "arbitrary", "arbitrary")),
        name="attn",
    )(*args, bias_tabs)


def _decode_body(q_ref, kn_ref, vn_ref, u_ref, pool_ref, kc0, vc0, kc1, vc1, kc2, vc2,
                 bdec_ref, bnew_ref, pw_ref, oa_ref, pooled_ref):
    kcs = (kc0, kc1, kc2)
    vcs = (vc0, vc1, vc2)
    outs, lses = [], []
    for g in range(N_GROUPS):
        hs = slice(g * HEADS, (g + 1) * HEADS)
        q = q_ref[0, hs, :]
        kn = kn_ref[0, hs, :]
        vn = vn_ref[0, hs, :]
        kc = kcs[g][0]
        vc = vcs[g][0]
        s_old = jnp.sum(kc * q[None], axis=-1, keepdims=True) + bdec_ref[g]
        s_new = jnp.sum(kn * q, axis=-1, keepdims=True) + bnew_ref[g]
        m = jnp.maximum(jnp.max(s_old, axis=0), s_new)
        p_old = jnp.exp(s_old - m[None])
        p_new = jnp.exp(s_new - m)
        l = jnp.sum(p_old, axis=0) + p_new
        outs.append((jnp.sum(p_old * vc, axis=0) + p_new * vn) / l)
        lses.append(m + jnp.log(l))
    mx = jnp.maximum(jnp.maximum(lses[0], lses[1]), lses[2])
    es = [jnp.exp(t - mx) for t in lses]
    den = es[0] + es[1] + es[2]
    oa_ref[0] = (es[0] / den) * outs[0] + (es[1] / den) * outs[1] + (es[2] / den) * outs[2]
    u = u_ref[0]
    stored = jnp.sum(pool_ref[0] * pw_ref[0:POOL_STATE, :], axis=0, keepdims=True)
    pooled_ref[0] = stored + u * pw_ref[POOL_STATE:POOL_STATE + 1, :] - u


def _decode_call(q, kn, vn, u, pool_prev, k_caches, v_caches, bdec, bnew, pw):
    db = pool_prev.shape[0]
    per_b = lambda t: pl.BlockSpec((1,) + t.shape[1:], lambda b: (b,) + (0,) * (t.ndim - 1))
    const = lambda t: pl.BlockSpec(t.shape, lambda b: (0,) * t.ndim)
    cache_specs, cache_args = [], []
    for g, (_, dil) in enumerate(ATTN_GROUPS):
        for c in (k_caches[g], v_caches[g]):
            rows = c.shape[1]
            cache_args.append(c.reshape(db, rows // dil, dil, HEADS, HEAD_DIM)[:, :, 0])
            cache_specs.append(pl.BlockSpec((1, ATTN_STEPS, HEADS, HEAD_DIM), lambda b: (b, 0, 0, 0)))
    u3 = u.reshape(db, 1, u.shape[-1])
    return pl.pallas_call(
        _decode_body,
        grid=(db,),
        in_specs=[per_b(q), per_b(kn), per_b(vn), per_b(u3), per_b(pool_prev)] + cache_specs +
                 [const(bdec), const(bnew), const(pw)],
        out_specs=[pl.BlockSpec((1, HEADS, HEAD_DIM), lambda b: (b, 0, 0)),
                   pl.BlockSpec((1, 1, GROUP_W), lambda b: (b, 0, 0))],
        out_shape=[jax.ShapeDtypeStruct((db, HEADS, HEAD_DIM), F32),
                   jax.ShapeDtypeStruct((db, 1, GROUP_W), F32)],
        compiler_params=_params(("arbitrary",)),
        name="decode_mix",
    )(q, kn, vn, u3, pool_prev, *cache_args, bdec, bnew, pw)


def _merge_tail(x, oa, pooled, gates, gate_msa, shift2, scale2, g2, wa, wb, wo, pw_ref, ps,
                x1_ref, h2_ref):
    mixed = jnp.concatenate(
        [jnp.dot(pooled[:, j * POOL_GROUP:(j + 1) * POOL_GROUP].astype(BF16), pw_ref[j],
                 preferred_element_type=F32) for j in range(len(POOL_WINDOWS))], axis=1) * ps
    d = x.shape[-1]
    ga = gates[:, :d].astype(F32)
    gb = gates[:, d:].astype(F32)
    merged = (_sigmoid(ga) * jnp.dot(oa.astype(BF16), wa, preferred_element_type=F32) +
              _sigmoid(gb) * jnp.dot(mixed.astype(BF16), wb, preferred_element_type=F32))
    y = jnp.dot(merged.astype(BF16), wo, preferred_element_type=F32)
    x1 = x + gate_msa * y
    x1_ref[0] = x1
    ms = jnp.mean(x1 * x1, axis=-1, keepdims=True)
    h2 = x1 * lax.rsqrt(ms + EPS) * g2
    _store_rows(h2_ref.at[0], h2 * (1.0 + scale2) + shift2)


def _merge_prompt_body(x_ref, gt_ref, oa_ref, u_ref, uh_ref,
                       gm_ref, sh_ref, sc_ref, g2_ref, wa_ref, wb_ref, wo_ref, pw_ref, ps_ref,
                       x1_ref, h2_ref):
    i = pl.program_id(1)
    tm = x_ref.shape[1]
    oa = oa_ref[0]
    u = u_ref[0]
    halo = jnp.where(i == 0, 0.0, uh_ref[0])
    ext = jnp.concatenate([halo, u], axis=0)
    hw = halo.shape[0]
    acc = ext
    parts = []
    for j, w in enumerate(POOL_WINDOWS):
        acc = acc[:, POOL_GROUP * (1 if j else 0):]
        acc = acc + pltpu.roll(acc, w // 2, axis=0)
        parts.append(acc[hw:, :POOL_GROUP])
    tot = jnp.concatenate(parts, axis=1)
    lane = lax.broadcasted_iota(jnp.int32, (tm, GROUP_W), 1)
    wcol = jnp.left_shift(2, lane // POOL_GROUP)
    pos = i * tm + lax.broadcasted_iota(jnp.int32, (tm, GROUP_W), 0)
    cnt = jnp.minimum(pos + 1, wcol).astype(F32)
    pooled = tot / cnt - u
    _merge_tail(x_ref[0], oa, pooled, gt_ref[0], gm_ref[0], sh_ref[0], sc_ref[0], g2_ref[...],
                wa_ref[...], wb_ref[...], wo_ref[...], pw_ref, ps_ref[...], x1_ref, h2_ref)


def _merge_sample_body(x_ref, gt_ref, oa_ref, pooled_ref,
                       gm_ref, sh_ref, sc_ref, g2_ref, wa_ref, wb_ref, wo_ref, pw_ref, ps_ref,
                       x1_ref, h2_ref):
    _merge_tail(x_ref[0], oa_ref[0], pooled_ref[0], gt_ref[0], gm_ref[0], sh_ref[0], sc_ref[0], g2_ref[...],
                wa_ref[...], wb_ref[...], wo_ref[...], pw_ref, ps_ref[...], x1_ref, h2_ref)


def _merge_call(x, gates, mixer_inputs, mods, g2, wa, wb, wo, pw, ps, prompt):
    bsz, s, d = x.shape
    assert d == ROW_SLABS * LANES
    tm = min(ROW_TILE, s)
    row = lambda t: pl.BlockSpec((1, tm, t.shape[-1]), lambda b, i: (b, i, 0))
    const = lambda t: pl.BlockSpec(t.shape, lambda b, i: (0,) * t.ndim)
    gm, sh, sc = mods
    in_specs = [row(x), row(gates)]
    args = [x, gates]
    if prompt:
        oa, u = mixer_inputs
        hw = 16
        in_specs += [row(oa), row(u),
                     pl.BlockSpec((1, hw, u.shape[-1]), lambda b, i: (b, jnp.maximum(i * (tm // hw) - 1, 0), 0))]
        args += [oa, u, u]
        body = _merge_prompt_body
    else:
        in_specs += [row(t) for t in mixer_inputs]
        args += list(mixer_inputs)
        body = _merge_sample_body
    in_specs += [_mod_spec(m, tm, d) for m in (gm, sh, sc)]
    in_specs += [const(t) for t in (g2, wa, wb, wo, pw, ps)]
    args += [gm, sh, sc, g2, wa, wb, wo, pw, ps]
    return pl.pallas_call(
        body,
        grid=(bsz, s // tm),
        in_specs=in_specs,
        out_specs=[row(x), pl.BlockSpec((1, tm * ROW_SLABS, LANES), lambda b, i: (b, i, 0))],
        out_shape=[jax.ShapeDtypeStruct((bsz, s, d), F32),
                   jax.ShapeDtypeStruct((bsz, s * ROW_SLABS, LANES), F32)],
        compiler_params=_params(("arbitrary", "arbitrary")),
        name="merge_prompt" if prompt else "merge_sample",
    )(*args)


def _router_body(h_ref, rw_ref, rb_ref, tri_ref, idx_ref, wt_ref, rank_ref, cnt_ref, carry):
    tm = h_ref.shape[0] // ROW_SLABS

    @pl.when(pl.program_id(0) == 0)
    def _():
        carry[...] = jnp.zeros_like(carry)

    ne = rw_ref.shape[1]
    gsz = ne // N_EXPERT_GROUPS
    logits = jnp.dot(_load_rows(h_ref, tm), rw_ref[...], precision=HIGHEST, preferred_element_type=F32)
    scores = _sigmoid(logits)
    sel = scores + rb_ref[...]
    lane = lax.broadcasted_iota(jnp.int32, (tm, ne), 1)
    lanef = lane.astype(F32)
    grp = lane // gsz
    ninf = -jnp.inf
    far = float(ne)

    def first_max(x):
        m = jnp.max(x, axis=-1, keepdims=True)
        at = jnp.min(jnp.where(x == m, lanef, far), axis=-1, keepdims=True)
        return m, at

    gs = []
    for g in range(N_EXPERT_GROUPS):
        xg = jnp.where(grp == g, sel, ninf)
        m1, a1 = first_max(xg)
        m2 = jnp.max(jnp.where(lanef == a1, ninf, xg), axis=-1, keepdims=True)
        gs.append(m1 + m2)
    allowed = jnp.zeros((tm, ne), jnp.bool_)
    for g in range(N_EXPERT_GROUPS):
        ahead = jnp.zeros((tm, 1), F32)
        for o in range(N_EXPERT_GROUPS):
            if o == g:
                continue
            beats = (gs[o] > gs[g]) | ((gs[o] == gs[g]) & (o < g))
            ahead = ahead + beats.astype(F32)
        allowed = allowed | ((grp == g) & (ahead < TOPK_GROUPS))
    masked = jnp.where(allowed, sel, ninf)
    lane_o = lax.broadcasted_iota(jnp.int32, (tm, LANES), 1)
    idx_t = jnp.zeros((tm, LANES), F32)
    wt_t = jnp.zeros((tm, LANES), F32)
    wsum = jnp.zeros((tm, 1), F32)
    picked = jnp.zeros((tm, ne), F32)
    ats = []
    for k in range(TOP_K):
        _, at = first_max(masked)
        hit = lanef == at
        wk = jnp.sum(jnp.where(hit, scores, 0.0), axis=-1, keepdims=True)
        masked = jnp.where(hit, ninf, masked)
        picked = picked + hit.astype(F32)
        ats.append(at)
        idx_t = jnp.where(lane_o == k, at, idx_t)
        wt_t = jnp.where(lane_o == k, wk, wt_t)
        wsum = wsum + wk
    idx_ref[...] = idx_t.astype(jnp.int32)
    wt_ref[...] = wt_t / wsum * ROUTED_SCALE
    before = jnp.dot(tri_ref[...], picked.astype(BF16), preferred_element_type=F32) + carry[...]
    rank_t = jnp.zeros((tm, LANES), F32)
    for k in range(TOP_K):
        rk = jnp.sum(jnp.where(lanef == ats[k], before, 0.0), axis=-1, keepdims=True)
        rank_t = jnp.where(lane_o == k, rk, rank_t)
    rank_ref[...] = rank_t.astype(jnp.int32)
    carry[...] = carry[...] + jnp.sum(picked, axis=0, keepdims=True)
    cnt_ref[...] = carry[...].astype(jnp.int32)


def _router_call(h2, rw, rb):
    nt = h2.shape[0] // ROW_SLABS
    tm = ROW_TILE
    d, ne = rw.shape
    tri = jnp.asarray(np.tril(np.ones((tm, tm), np.float32), -1), BF16)
    tile = pl.BlockSpec((tm, LANES), lambda i: (i, 0))
    return pl.pallas_call(
        _router_body,
        grid=(nt // tm,),
        in_specs=[pl.BlockSpec((tm * ROW_SLABS, LANES), lambda i: (i, 0)),
                  pl.BlockSpec((d, ne), lambda i: (0, 0)),
                  pl.BlockSpec((1, ne), lambda i: (0, 0)),
                  pl.BlockSpec((tm, tm), lambda i: (0, 0))],
        out_specs=[tile, tile, tile, pl.BlockSpec((1, ne), lambda i: (0, 0))],
        out_shape=[jax.ShapeDtypeStruct((nt, LANES), jnp.int32),
                   jax.ShapeDtypeStruct((nt, LANES), F32),
                   jax.ShapeDtypeStruct((nt, LANES), jnp.int32),
                   jax.ShapeDtypeStruct((1, ne), jnp.int32)],
        scratch_shapes=[pltpu.VMEM((1, ne), F32)],
        compiler_params=_params(("arbitrary",)),
        name="router",
    )(h2, rw, rb.reshape(1, ne), tri)


def _slot_body(idx_ref, rank_ref, offs_ref, pos_ref):
    tm = idx_ref.shape[0]
    ne = offs_ref.shape[1]
    idx = idx_ref[...]
    rank = rank_ref[...]
    lane = lax.broadcasted_iota(jnp.int32, (tm, ne), 1)
    lane_o = lax.broadcasted_iota(jnp.int32, (tm, LANES), 1)
    offs = offs_ref[...]
    pos = jnp.zeros((tm, LANES), F32)
    for k in range(TOP_K):
        start = jnp.sum(jnp.where(lane == idx[:, k:k + 1], offs, 0.0), axis=-1, keepdims=True)
        pos = jnp.where(lane_o == k, start, pos)
    pos_ref[...] = pos.astype(jnp.int32) + rank


def _slot_call(idx, rank, offs):
    nt = idx.shape[0]
    tm = ROW_TILE
    ne = offs.shape[1]
    tile = pl.BlockSpec((tm, LANES), lambda i: (i, 0))
    return pl.pallas_call(
        _slot_body,
        grid=(nt // tm,),
        in_specs=[tile, tile, pl.BlockSpec((1, ne), lambda i: (0, 0))],
        out_specs=tile,
        out_shape=jax.ShapeDtypeStruct((nt, LANES), jnp.int32),
        compiler_params=_params(("arbitrary",)),
        name="slots",
    )(idx, rank, offs)


def _row_tile(ref, r):
    return ref.at[pl.ds(pl.multiple_of(r * ROW_SLABS, ROW_SLABS), ROW_SLABS), :]


def _dispatch_body(fill_ref, pos_ref, h_ref, xs_hbm, zeros, zsem, tsem, sem):
    td = h_ref.shape[0] // ROW_SLABS
    chunk = zeros.shape[0]
    n_ent = fill_ref.shape[0] - 1
    n_chunks = xs_hbm.shape[0] // chunk

    def fill_chunk(c, sem_, wait):
        cp = pltpu.make_async_copy(zeros, xs_hbm.at[pl.ds(pl.multiple_of(c * chunk, chunk), chunk), :], sem_)
        if wait:
            cp.wait()
        else:
            cp.start()

    def fill_tail(wait):
        lax.fori_loop(fill_ref[n_ent], n_chunks, lambda b, c: (fill_chunk(b, tsem, wait), c)[1], 0)

    @pl.when(pl.program_id(0) == 0)
    def _():
        zeros[...] = jnp.zeros_like(zeros)

        def fill(j, wait):
            @pl.when(fill_ref[j] >= 0)
            def _():
                fill_chunk(fill_ref[j], zsem, wait)
        fill_tail(False)
        for wait in (False, True):
            lax.fori_loop(0, n_ent, lambda j, c, wait=wait: (fill(j, wait), c)[1], 0)

    @pl.when(pl.program_id(0) == pl.num_programs(0) - 1)
    def _():
        fill_tail(True)

    def one(t, carry):
        for k in range(TOP_K):
            pltpu.make_async_copy(_row_tile(h_ref, t), _row_tile(xs_hbm, pos_ref[0, 0, t * TOP_K + k]),
                                  sem).start(priority=k % 2)
        return carry
    lax.fori_loop(0, td, one, 0, unroll=2)
    for k in range(TOP_K):
        pltpu.make_async_copy(h_ref, xs_hbm.at[pl.ds(0, td * ROW_SLABS), :], sem).wait()


def _dispatch_call(fill_chunks, pos_tok, h2, n_blocks):
    nt = h2.shape[0] // ROW_SLABS
    td = ROW_TILE
    te = EXPERT_TILE
    grid_spec = pltpu.PrefetchScalarGridSpec(
        num_scalar_prefetch=1,
        grid=(nt // td,),
        in_specs=[pl.BlockSpec((1, 1, td * TOP_K), lambda i, lb: (i, 0, 0), memory_space=pltpu.SMEM),
                  pl.BlockSpec((td * ROW_SLABS, LANES), lambda i, lb: (i, 0))],
        out_specs=pl.BlockSpec(memory_space=pl.ANY),
        scratch_shapes=[pltpu.VMEM((FILL_ROWS * ROW_SLABS, LANES), F32), pltpu.SemaphoreType.DMA,
                        pltpu.SemaphoreType.DMA, pltpu.SemaphoreType.DMA],
    )
    return pl.pallas_call(
        _dispatch_body,
        grid_spec=grid_spec,
        out_shape=jax.ShapeDtypeStruct((n_blocks * te * ROW_SLABS, LANES), F32),
        compiler_params=_params(("arbitrary",)),
        name="dispatch",
    )(fill_chunks, pos_tok, h2)


def _expert_body(exp_ref, nu_ref, xs_hbm, wg_ref, wu_ref, wd_ref, y_ref, xbuf, xsem, wgb, wub, wdb):
    b = pl.program_id(0)
    n_used = nu_ref[0]
    live = b < n_used
    new_expert = jnp.logical_or(b == 0, exp_ref[b] != exp_ref[jnp.maximum(b - 1, 0)])
    ring, rows = xbuf.shape[0], xbuf.shape[1]

    def fetch(blk):
        slot = blk % ring
        return pltpu.make_async_copy(xs_hbm.at[pl.ds(pl.multiple_of(blk * rows, rows), rows), :],
                                     xbuf.at[slot], xsem.at[slot])

    @pl.when(b == 0)
    def _():
        for j in range(ring - 1):
            @pl.when(j < n_used)
            def _():
                fetch(j).start()

    @pl.when(b + ring - 1 < n_used)
    def _():
        fetch(b + ring - 1).start()

    @pl.when(jnp.logical_and(live, new_expert))
    def _():
        wgb[...] = wg_ref[0].astype(BF16)
        wub[...] = wu_ref[0].astype(BF16)
        wdb[...] = wd_ref[0].astype(BF16)

    @pl.when(live)
    def _():
        te = rows // ROW_SLABS
        fetch(b).wait()
        x = _load_rows(xbuf.at[b % ring], te).astype(BF16)
        gt = jnp.dot(x, wgb[...], preferred_element_type=F32)
        up = jnp.dot(x, wub[...], preferred_element_type=F32)
        act = (gt * _sigmoid(gt)) * up
        _store_rows(y_ref, jnp.dot(act.astype(BF16), wdb[...], preferred_element_type=F32))

    @pl.when(jnp.logical_not(live))
    def _():
        y_ref[...] = jnp.zeros_like(y_ref)


def _expert_call(blk_exp, n_used, xs, wg, wu, wd):
    n_blocks = blk_exp.shape[0]
    te = EXPERT_TILE
    d, hid = wg.shape[-2:]
    used = lambda b, nu: jnp.minimum(b, nu[0] - 1)
    wspec = lambda r, c: pl.BlockSpec((1, r, c), lambda b, e, nu: (e[used(b, nu)], 0, 0))
    grid_spec = pltpu.PrefetchScalarGridSpec(
        num_scalar_prefetch=2,
        grid=(n_blocks,),
        in_specs=[pl.BlockSpec(memory_space=pl.ANY), wspec(d, hid), wspec(d, hid), wspec(hid, d)],
        out_specs=pl.BlockSpec((te * ROW_SLABS, LANES), lambda b, e, nu: (b, 0)),
        scratch_shapes=[pltpu.VMEM((EXPERT_RING, te * ROW_SLABS, LANES), F32), pltpu.SemaphoreType.DMA((EXPERT_RING,)),
                        pltpu.VMEM((d, hid), BF16), pltpu.VMEM((d, hid), BF16), pltpu.VMEM((hid, d), BF16)],
    )
    return pl.pallas_call(
        _expert_body,
        grid_spec=grid_spec,
        out_shape=jax.ShapeDtypeStruct(xs.shape, F32),
        compiler_params=_params(("arbitrary",)),
        name="expert_ffn",
    )(blk_exp, n_used, xs, wg, wu, wd)


def _gather_rows(idx_ref, src_hbm, dst, sem, n):
    def pair(i, carry):
        for p in range(2):
            j = 2 * i + p
            pltpu.make_async_copy(_row_tile(src_hbm, idx_ref[0, 0, j]), _row_tile(dst, j), sem).start(priority=p)
        return carry
    lax.fori_loop(0, n // 2, pair, 0, unroll=4)


def _combine_body(pos_ref, posn_ref, y_hbm, wt_ref, h_ref, x1_ref, gate_ref, sg_ref, su_ref, sd_ref,
                  out_ref, ybuf, sem):
    nb, ni = pl.num_programs(0), pl.num_programs(1)
    step = pl.program_id(0) * ni + pl.program_id(1)
    slot = step % 2
    tc = h_ref.shape[0] // ROW_SLABS
    rows = tc * TOP_K

    @pl.when(step == 0)
    def _():
        _gather_rows(pos_ref, y_hbm, ybuf.at[0], sem.at[0], rows)

    @pl.when(step + 1 < nb * ni)
    def _():
        _gather_rows(posn_ref, y_hbm, ybuf.at[1 - slot], sem.at[1 - slot], rows)

    pltpu.make_async_copy(y_hbm.at[pl.ds(0, rows * ROW_SLABS), :], ybuf.at[slot], sem.at[slot]).wait()
    wt = wt_ref[...]
    routed = jnp.zeros(out_ref.shape[1:], F32)
    for k in range(TOP_K):
        routed = routed + wt[:, k:k + 1] * _load_rows(ybuf.at[slot], tc, first=k * tc)
    hb = _load_rows(h_ref, tc).astype(BF16)
    gt = jnp.dot(hb, sg_ref[...], preferred_element_type=F32)
    up = jnp.dot(hb, su_ref[...], preferred_element_type=F32)
    shared = jnp.dot(((gt * _sigmoid(gt)) * up).astype(BF16), sd_ref[...], preferred_element_type=F32)
    out_ref[0] = x1_ref[0] + gate_ref[0] * (routed + shared)


def _combine_call(pos_tiles, tile_off, y_sorted, wts, h2, x1, gate, sg, su, sd):
    bsz, s, d = x1.shape
    tc = min(COMBINE_TILE, s)
    ni = s // tc
    n_tiles = pos_tiles.shape[0]
    lin = lambda b, i: tile_off + b * ni + i
    const = lambda t: pl.BlockSpec(t.shape, lambda b, i: (0,) * t.ndim)
    return pl.pallas_call(
        _combine_body,
        grid=(bsz, ni),
        in_specs=[
            pl.BlockSpec((1, 1, tc * TOP_K), lambda b, i: (lin(b, i), 0, 0), memory_space=pltpu.SMEM),
            pl.BlockSpec((1, 1, tc * TOP_K), lambda b, i: (jnp.minimum(lin(b, i) + 1, n_tiles - 1), 0, 0),
                         memory_space=pltpu.SMEM),
            pl.BlockSpec(memory_space=pl.ANY),
            pl.BlockSpec((tc, LANES), lambda b, i: (lin(b, i), 0)),
            pl.BlockSpec((tc * ROW_SLABS, LANES), lambda b, i: (lin(b, i), 0)),
            pl.BlockSpec((1, tc, d), lambda b, i: (b, i, 0)),
            _mod_spec(gate, tc, d),
            const(sg), const(su), const(sd),
        ],
        out_specs=pl.BlockSpec((1, tc, d), lambda b, i: (b, i, 0)),
        out_shape=jax.ShapeDtypeStruct((bsz, s, d), F32),
        scratch_shapes=[pltpu.VMEM((2, tc * TOP_K * ROW_SLABS, LANES), F32), pltpu.SemaphoreType.DMA((2,))],
        compiler_params=_params(("arbitrary", "arbitrary")),
        name="combine",
    )(pos_tiles, pos_tiles, y_sorted, wts, h2, x1, gate, sg, su, sd)


def _t5_bucket(dist):
    exact = N_BUCKETS // 2
    d = np.asarray(dist)
    large = exact + (np.log(np.maximum(d, 1) / exact) / np.log(REL_MAX_DIST / exact) * (N_BUCKETS - exact)).astype(np.int32)
    large = np.minimum(large, N_BUCKETS - 1)
    return np.where(d < exact, d, large).astype(np.int32)


def _group_bias(rel_bias, gi):
    window, dil = ATTN_GROUPS[gi]
    bkt = _t5_bucket(np.arange(window // dil + 1) * dil)
    onehot = jnp.asarray(bkt[:, None] == np.arange(N_BUCKETS)[None, :], F32)
    cols = rel_bias[:, gi * HEADS:(gi + 1) * HEADS].astype(F32)
    return jnp.dot(onehot, cols, precision=HIGHEST).T


def _prompt_bias_table(bias_j):
    blk = ATTN_BLOCK
    n = bias_j.shape[1] - 1
    period = 3 * blk
    row0 = jnp.concatenate([jnp.flip(bias_j, axis=1), jnp.full((HEADS, period - n - 1), NEG, F32)], axis=1)
    flat = jnp.tile(row0, (1, blk))[:, :blk * (period - 1)]
    tab = flat.reshape(HEADS, blk, period - 1)[:, :, :2 * blk]
    return tab.reshape(HEADS // HEADS_PER_SLAB, HEADS_PER_SLAB * blk, 2 * blk)


def _decode_bias_tables(bias_j):
    old = jnp.flip(bias_j[:, 1:], axis=1).T[:, :, None]
    new = bias_j[:, 0][:, None]
    return old, new


def _block_schedule(counts, na):
    te = EXPERT_TILE
    n_blocks = na // te + N_EXPERTS
    pcounts = (counts + te - 1) // te * te
    pends = jnp.cumsum(pcounts)
    offs = pends - pcounts
    blk_start = jnp.arange(n_blocks, dtype=jnp.int32) * te
    blk_exp = jnp.minimum(jnp.sum(pends[None, :] <= blk_start[:, None], axis=1), N_EXPERTS - 1).astype(jnp.int32)
    n_used = (pends[-1] // te).astype(jnp.int32).reshape(1)
    fr = FILL_ROWS
    fill_start = offs + counts // fr * fr
    n_fill = (pends - fill_start) // fr
    first = fill_start // fr
    chunks = jnp.concatenate([jnp.where(n_fill > j, first + j, -1) for j in range(te // fr)]
                             + [n_used * (te // fr)]).astype(jnp.int32)
    return offs, blk_exp, n_used, chunks, n_blocks


def kernel(x_prompt, x_sample, cache_k_w128, cache_v_w128, cache_k_w512, cache_v_w512, cache_k_w2048, cache_v_w2048, state_pool, c_prompt, c_sample, ada_w, ada_b, norm1, norm2, w_in, q_gain, k_gain, rel_bias, pool_w, pool_scale, w_br_a, w_br_b, w_out, router_w, router_bias, exp_w_gate, exp_w_up, exp_w_down, sh_w_gate, sh_w_up, sh_w_down):
    depth = ada_w.shape[0]
    bsz, seq, d = x_prompt.shape
    db = x_sample.shape[0]
    assert x_sample.shape[1] == 1 and db <= SAMPLE_PAD and seq % (ATTN_BLOCK * ATTN_GROUPS[-1][1]) == 0
    k_caches = (cache_k_w128, cache_k_w512, cache_k_w2048)
    v_caches = (cache_v_w128, cache_v_w512, cache_v_w2048)
    for c, (w, dil) in zip(k_caches, ATTN_GROUPS):
        assert c.shape[2] == w == ATTN_STEPS * dil

    bias_js = [_group_bias(rel_bias, g) for g in range(N_GROUPS)]
    bias_tabs = jnp.stack([_prompt_bias_table(b) for b in bias_js])
    dec = [_decode_bias_tables(b) for b in bias_js]
    bdec = jnp.stack([t[0] for t in dec])
    bnew = jnp.stack([t[1] for t in dec])
    bd = jnp.asarray((np.arange(MXU_DIM)[:, None] // HEAD_DIM == np.arange(MXU_DIM)[None, :] // HEAD_DIM)
                     / HEAD_DIM, BF16)
    wcols = np.repeat(np.asarray(POOL_WINDOWS), POOL_GROUP)
    pw_dec = jnp.asarray(np.where(np.arange(POOL_STATE + 1)[:, None] >= POOL_STATE + 1 - wcols[None, :],
                                  1.0 / wcols[None, :], 0.0), F32)

    xp = x_prompt
    xs = jnp.pad(x_sample.reshape(db, d), ((0, SAMPLE_PAD - db), (0, 0))).reshape(1, SAMPLE_PAD, d)
    n_prompt = bsz * seq
    outs = {name: [] for name in ('pk', 'pv', 'sk', 'sv')}
    outs['pk'] = [[] for _ in range(N_GROUPS)]
    outs['pv'] = [[] for _ in range(N_GROUPS)]
    outs['sk'] = [[] for _ in range(N_GROUPS)]
    outs['sv'] = [[] for _ in range(N_GROUPS)]
    ppool, spool = [], []
    for l in range(depth):
        c_all = jnp.concatenate([c_prompt, c_sample], axis=0)
        mod = _ada_call(c_all, ada_w[l], ada_b[l])
        mod_p = [m.reshape(bsz, 1, d) for m in jnp.split(mod[:bsz], 6, axis=-1)]
        mod_s = [jnp.pad(m, ((0, SAMPLE_PAD - db), (0, 0))).reshape(1, SAMPLE_PAD, d)
                 for m in jnp.split(mod[bsz:], 6, axis=-1)]
        w_in_bf = w_in[l].astype(BF16)
        qg = (jnp.tile(q_gain[l], (1, HEADS)) * (HEAD_DIM ** -0.5)).reshape(1, -1)
        kg = jnp.tile(k_gain[l], (1, HEADS)).reshape(1, -1)
        g1 = norm1[l].reshape(1, d)
        g2 = norm2[l].reshape(1, d)
        wa, wb, wo = (w_br_a[l].astype(BF16), w_br_b[l].astype(BF16), w_out[l].astype(BF16))
        pw = pool_w[l].astype(BF16)
        ps = pool_scale[l].reshape(1, -1)

        q, k, v, u, gates = _inproj_call(xp, mod_p[0], mod_p[1], g1, w_in_bf, qg, kg, bd)
        oa_p = _attn_call(q, k, v, bias_tabs)
        for g, (window, dil) in enumerate(ATTN_GROUPS):
            keep = min(window, seq)
            cs = slice(g * GROUP_W, (g + 1) * GROUP_W)
            outs['pk'][g].append(k[:, seq - keep:, cs].reshape(bsz, keep, HEADS, HEAD_DIM))
            outs['pv'][g].append(v[:, seq - keep:, cs].reshape(bsz, keep, HEADS, HEAD_DIM))
        ppool.append(u[:, seq - POOL_STATE:])
        x1p, h2p = _merge_call(xp, gates, (oa_p, u), (mod_p[2], mod_p[3], mod_p[4]),
                               g2, wa, wb, wo, pw, ps, prompt=True)

        qs, ks, vs, us, gates_s = _inproj_call(xs, mod_s[0], mod_s[1], g1, w_in_bf, qg, kg, bd)
        heads3 = lambda t: t[0, :db].reshape(db, N_GROUPS * HEADS, HEAD_DIM)
        q3, k3, v3 = heads3(qs), heads3(ks), heads3(vs)
        kcl = [c[l] for c in k_caches]
        vcl = [c[l] for c in v_caches]
        oa_s, pooled_s = _decode_call(q3, k3, v3, us[0, :db], state_pool[l], kcl, vcl, bdec, bnew, pw_dec)
        padrows = lambda t: jnp.pad(t.reshape(db, GROUP_W), ((0, SAMPLE_PAD - db), (0, 0))).reshape(1, SAMPLE_PAD, GROUP_W)
        x1s, h2s = _merge_call(xs, gates_s, (padrows(oa_s), padrows(pooled_s)), (mod_s[2], mod_s[3], mod_s[4]),
                               g2, wa, wb, wo, pw, ps, prompt=False)
        for g in range(N_GROUPS):
            hs = slice(g * HEADS, (g + 1) * HEADS)
            outs['sk'][g].append(jnp.concatenate([kcl[g][:, 1:], k3[:, None, hs]], axis=1))
            outs['sv'][g].append(jnp.concatenate([vcl[g][:, 1:], v3[:, None, hs]], axis=1))
        spool.append(jnp.concatenate([state_pool[l][:, 1:], us[0, :db, None, :]], axis=1))

        h2 = jnp.concatenate([h2p.reshape(n_prompt * ROW_SLABS, LANES), h2s.reshape(SAMPLE_PAD * ROW_SLABS, LANES)], axis=0)
        nt = n_prompt + SAMPLE_PAD
        idx, wts, rank, counts = _router_call(h2, router_w[l], router_bias[l])
        offs, blk_exp, n_used, fill_chunks, n_blocks = _block_schedule(counts[0], nt * TOP_K)
        pos = _slot_call(idx, rank, offs.astype(F32).reshape(1, -1))[:, :TOP_K]
        xs_sorted = _dispatch_call(fill_chunks, pos.reshape(nt // ROW_TILE, 1, ROW_TILE * TOP_K), h2, n_blocks)
        y_sorted = _expert_call(blk_exp, n_used, xs_sorted, exp_w_gate[l], exp_w_up[l], exp_w_down[l])
        tc = COMBINE_TILE
        pos_tiles = pos.reshape(nt // tc, tc, TOP_K).transpose(0, 2, 1).reshape(nt // tc, 1, TOP_K * tc)
        sg, su, sd = sh_w_gate[l].astype(BF16), sh_w_up[l].astype(BF16), sh_w_down[l].astype(BF16)
        xp = _combine_call(pos_tiles, 0, y_sorted, wts, h2, x1p, mod_p[5], sg, su, sd)
        xs = _combine_call(pos_tiles, n_prompt // tc, y_sorted, wts, h2, x1s, mod_s[5], sg, su, sd)

    stack = lambda parts: jnp.stack(parts)
    res = [xp, xs[0, :db].reshape(db, 1, d)]
    for g in range(N_GROUPS):
        res += [stack(outs['pk'][g]), stack(outs['pv'][g])]
    res.append(stack(ppool))
    for g in range(N_GROUPS):
        res += [stack(outs['sk'][g]), stack(outs['sv'][g])]
    res.append(stack(spool))
    return tuple(res)
```

```python
import numpy as np
import jax
import jax.numpy as jnp
from jax import lax
from jax.experimental import pallas as pl
from jax.experimental.pallas import tpu as pltpu

F32 = jnp.float32
BF16 = jnp.bfloat16
HIGHEST = lax.Precision.HIGHEST

HEAD_DIM = 64
HEADS = 8
GROUP_W = HEADS * HEAD_DIM
ATTN_GROUPS = ((128, 1), (512, 4), (2048, 16))
N_GROUPS = len(ATTN_GROUPS)
ATTN_STEPS = 128
ATTN_BLOCK = 128
POOL_WINDOWS = (2, 4, 8, 16)
POOL_GROUP = 128
POOL_STATE = 15
N_BUCKETS = 32
REL_MAX_DIST = 2048
N_EXPERTS = 256
TOP_K = 8
N_EXPERT_GROUPS = 8
TOPK_GROUPS = 4
ROUTED_SCALE = 2.5
EPS = 1e-6
NEG = -1e30

LANES = 128
MXU_DIM = 256
VMEM_LIMIT = 56 * 1024 * 1024

INPROJ_TILE = 512
ROW_TILE = 256
EXPERT_TILE = 512
FILL_ROWS = 256
EXPERT_RING = 3
COMBINE_TILE = 128
SAMPLE_PAD = 256


def _sigmoid(x):
    return 1.0 / (1.0 + jnp.exp(-x))


def _params(sem):
    return pltpu.CompilerParams(dimension_semantics=sem, vmem_limit_bytes=VMEM_LIMIT)


ROW_SLABS = 8


def _load_rows(ref, n, first=0):
    return jnp.concatenate([ref[pl.ds(first * ROW_SLABS + s, n, stride=ROW_SLABS), :]
                            for s in range(ROW_SLABS)], axis=1)


def _store_rows(ref, val):
    n = val.shape[0]
    for s in range(ROW_SLABS):
        ref[pl.ds(s, n, stride=ROW_SLABS), :] = val[:, s * LANES:(s + 1) * LANES]


def _ada_body(c_ref, w_ref, b_ref, o_ref):
    c = c_ref[...]
    s = c * _sigmoid(c)
    o_ref[...] = jnp.dot(s, w_ref[...], precision=HIGHEST, preferred_element_type=F32) + b_ref[...]


def _ada_call(c, w, b):
    n, d = c.shape
    cols = w.shape[1]
    tn = 512
    return pl.pallas_call(
        _ada_body,
        grid=(cols // tn,),
        in_specs=[pl.BlockSpec((n, d), lambda j: (0, 0)),
                  pl.BlockSpec((d, tn), lambda j: (0, j)),
                  pl.BlockSpec((1, tn), lambda j: (0, j))],
        out_specs=pl.BlockSpec((n, tn), lambda j: (0, j)),
        out_shape=jax.ShapeDtypeStruct((n, cols), F32),
        compiler_params=_params(("arbitrary",)),
        name="adaln",
    )(c, w, b.reshape(1, cols))


def _inproj_body(x_ref, shift_ref, scale_ref, g_ref, w_ref, qg_ref, kg_ref, bd_ref,
                 q_ref, k_ref, v_ref, u_ref, gt_ref):
    x = x_ref[0]
    ms = jnp.mean(x * x, axis=-1, keepdims=True)
    h = x * lax.rsqrt(ms + EPS) * g_ref[...]
    h = h * (1.0 + scale_ref[0]) + shift_ref[0]
    hb = h.astype(BF16)
    n_chunks = w_ref.shape[1] // GROUP_W
    for c in range(n_chunks):
        z = jnp.dot(hb, w_ref[:, c * GROUP_W:(c + 1) * GROUP_W], preferred_element_type=F32)
        if c < 2 * N_GROUPS:
            zz = (z * z).astype(BF16)
            hms = jnp.concatenate(
                [jnp.dot(zz[:, s:s + MXU_DIM], bd_ref[...], preferred_element_type=F32)
                 for s in range(0, GROUP_W, MXU_DIM)], axis=1)
            g = c % N_GROUPS
            gain = (qg_ref if c < N_GROUPS else kg_ref)[:, g * GROUP_W:(g + 1) * GROUP_W]
            y = z * lax.rsqrt(hms + EPS) * gain
            if c < N_GROUPS:
                q_ref[0, :, g * GROUP_W:(g + 1) * GROUP_W] = y
            else:
                k_ref[0, :, g * GROUP_W:(g + 1) * GROUP_W] = y
        elif c < 3 * N_GROUPS:
            g = c - 2 * N_GROUPS
            v_ref[0, :, g * GROUP_W:(g + 1) * GROUP_W] = z
        elif c == 3 * N_GROUPS:
            u_ref[0] = z
        else:
            g = c - 3 * N_GROUPS - 1
            gt_ref[0, :, g * GROUP_W:(g + 1) * GROUP_W] = z.astype(BF16)


def _mod_spec(mod, tm, d):
    if mod.shape[1] == 1:
        return pl.BlockSpec((1, 1, d), lambda b, i: (b, 0, 0))
    return pl.BlockSpec((1, tm, d), lambda b, i: (b, i, 0))


def _inproj_call(x, shift, scale, g1, w_bf, qg, kg, bd):
    bsz, s, d = x.shape
    tm = min(INPROJ_TILE, s)
    qkv = N_GROUPS * GROUP_W
    n_gate = w_bf.shape[1] - 3 * qkv - GROUP_W
    const = lambda b, i: (0, 0)
    row = lambda width: pl.BlockSpec((1, tm, width), lambda b, i: (b, i, 0))
    return pl.pallas_call(
        _inproj_body,
        grid=(bsz, s // tm),
        in_specs=[row(d), _mod_spec(shift, tm, d), _mod_spec(scale, tm, d),
                  pl.BlockSpec((1, d), const),
                  pl.BlockSpec(w_bf.shape, const, pipeline_mode=pl.Buffered(1)),
                  pl.BlockSpec((1, qkv), const), pl.BlockSpec((1, qkv), const),
                  pl.BlockSpec((MXU_DIM, MXU_DIM), const)],
        out_specs=[row(qkv), row(qkv), row(qkv), row(GROUP_W), row(n_gate)],
        out_shape=[jax.ShapeDtypeStruct((bsz, s, qkv), F32),
                   jax.ShapeDtypeStruct((bsz, s, qkv), F32),
                   jax.ShapeDtypeStruct((bsz, s, qkv), F32),
                   jax.ShapeDtypeStruct((bsz, s, GROUP_W), F32),
                   jax.ShapeDtypeStruct((bsz, s, n_gate), BF16)],
        compiler_params=_params(("arbitrary", "arbitrary")),
        name="inproj",
    )(x, shift, scale, g1, w_bf, qg, kg, bd)


ATTN_ROWS = ATTN_BLOCK * ATTN_GROUPS[-1][1]
HEADS_PER_SLAB = LANES // HEAD_DIM


def _attn_body(*refs):
    ng = N_GROUPS
    qkv = refs[:3 * ng]
    bias_ref, o_ref = refs[3 * ng], refs[3 * ng + 1]
    scr = refs[3 * ng + 2:]
    first = pl.program_id(2) == 0
    blk = ATTN_BLOCK
    pb = ATTN_ROWS
    hs = HEADS_PER_SLAB
    lane = lax.broadcasted_iota(jnp.int32, (blk, LANES), 1)
    col = lax.broadcasted_iota(jnp.int32, (hs * blk, 2 * blk), 1)
    for g, (_, dil) in enumerate(ATTN_GROUPS):
        q_ref, k_ref, v_ref = qkv[3 * g:3 * g + 3]
        kext, vext, og, lg = scr[4 * g:4 * g + 4]
        span = blk * dil
        bias = bias_ref[g, 0]

        @pl.when(first)
        def _():
            kext[0:span, :] = jnp.zeros((span, LANES), F32)
            vext[0:span, :] = jnp.zeros((span, LANES), F32)
        kext[span:span + pb, :] = k_ref[0]
        vext[span:span + pb, :] = v_ref[0]

        rows = lambda start: pl.ds(start, blk, stride=dil) if dil > 1 else pl.ds(pl.multiple_of(start, blk), blk)
        per = 4

        def units(i, carry):
            bases, v2s, scores = [], [], []
            for j in range(per):
                u = per * i + j
                qb = u // dil
                base = qb * span + (u % dil)
                q = q_ref[0, rows(base), :]
                k2 = jnp.concatenate([kext[rows(base), :], kext[rows(base + span), :]], axis=0).astype(BF16)
                v2s.append(jnp.concatenate([vext[rows(base), :], vext[rows(base + span), :]], axis=0).astype(BF16))
                qs = jnp.concatenate([jnp.where(lane // HEAD_DIM == h, q, 0.0) for h in range(hs)],
                                     axis=0).astype(BF16)
                s = lax.dot_general(qs, k2, (((1,), (1,)), ((), ())), preferred_element_type=F32) + bias
                scores.append(jnp.where(jnp.logical_and(jnp.logical_and(first, qb == 0), col < blk), NEG, s))
                bases.append(base)
            s = jnp.concatenate(scores, axis=0)
            m = jnp.max(s, axis=-1, keepdims=True)
            p = jnp.exp(s - m)
            l = jnp.sum(p, axis=-1, keepdims=True)
            pb16 = p.astype(BF16)
            lse = m + jnp.log(l)
            for j in range(per):
                r0 = j * hs * blk
                o2 = jnp.dot(pb16[r0:r0 + hs * blk], v2s[j], preferred_element_type=F32) / l[r0:r0 + hs * blk]
                o = o2[0:blk]
                lw = jnp.broadcast_to(lse[r0:r0 + blk], (blk, LANES))
                for h in range(1, hs):
                    o = jnp.where(lane // HEAD_DIM == h, o2[h * blk:(h + 1) * blk], o)
                    lw = jnp.where(lane // HEAD_DIM == h, lse[r0 + h * blk:r0 + (h + 1) * blk], lw)
                og[rows(bases[j]), :] = o
                lg[rows(bases[j]), :] = lw
            return carry
        lax.fori_loop(0, pb // blk // per, units, 0)
        kext[0:span, :] = kext[pb:pb + span, :]
        vext[0:span, :] = vext[pb:pb + span, :]

    def merge(c, carry):
        rs = pl.ds(pl.multiple_of(c * blk, blk), blk)
        ls = [scr[4 * g + 3][rs, :] for g in range(ng)]
        mx = jnp.maximum(jnp.maximum(ls[0], ls[1]), ls[2])
        es = [jnp.exp(t - mx) for t in ls]
        den = es[0] + es[1] + es[2]
        acc = (es[0] / den) * scr[2][rs, :]
        for g in range(1, ng):
            acc = acc + (es[g] / den) * scr[4 * g + 2][rs, :]
        o_ref[0, rs, :] = acc.astype(o_ref.dtype)
        return carry
    lax.fori_loop(0, pb // blk, merge, 0)


def _attn_call(q, k, v, bias_tabs):
    bsz, s, qkv = q.shape
    pb = ATTN_ROWS
    n_slab = GROUP_W // LANES
    in_specs, args = [], []
    for g in range(N_GROUPS):
        spec = pl.BlockSpec((1, pb, LANES), lambda b, c, i, g=g: (b, i, g * n_slab + c))
        in_specs += [spec, spec, spec]
        args += [q, k, v]
    in_specs.append(pl.BlockSpec((N_GROUPS, 1) + bias_tabs.shape[2:], lambda b, c, i: (0, c, 0, 0)))
    scratch = []
    for _, dil in ATTN_GROUPS:
        ext = pltpu.VMEM((ATTN_BLOCK * dil + pb, LANES), F32)
        scratch += [ext, ext, pltpu.VMEM((pb, LANES), F32), pltpu.VMEM((pb, LANES), F32)]
    return pl.pallas_call(
        _attn_body,
        grid=(bsz, n_slab, s // pb),
        in_specs=in_specs,
        out_specs=pl.BlockSpec((1, pb, LANES), lambda b, c, i: (b, i, c)),
        out_shape=jax.ShapeDtypeStruct((bsz, s, GROUP_W), BF16),
        scratch_shapes=scratch,
        compiler_params=_params(("arbitrary", "arbitrary", "arbitrary")),
        name="attn",
    )(*args, bias_tabs)


def _decode_body(q_ref, kn_ref, vn_ref, u_ref, pool_ref, kc0, vc0, kc1, vc1, kc2, vc2,
                 bdec_ref, bnew_ref, pw_ref, oa_ref, pooled_ref):
    kcs = (kc0, kc1, kc2)
    vcs = (vc0, vc1, vc2)
    outs, lses = [], []
    for g in range(N_GROUPS):
        hs = slice(g * HEADS, (g + 1) * HEADS)
        q = q_ref[0, hs, :]
        kn = kn_ref[0, hs, :]
        vn = vn_ref[0, hs, :]
        kc = kcs[g][0]
        vc = vcs[g][0]
        s_old = jnp.sum(kc * q[None], axis=-1, keepdims=True) + bdec_ref[g]
        s_new = jnp.sum(kn * q, axis=-1, keepdims=True) + bnew_ref[g]
        m = jnp.maximum(jnp.max(s_old, axis=0), s_new)
        p_old = jnp.exp(s_old - m[None])
        p_new = jnp.exp(s_new - m)
        l = jnp.sum(p_old, axis=0) + p_new
        outs.append((jnp.sum(p_old * vc, axis=0) + p_new * vn) / l)
        lses.append(m + jnp.log(l))
    mx = jnp.maximum(jnp.maximum(lses[0], lses[1]), lses[2])
    es = [jnp.exp(t - mx) for t in lses]
    den = es[0] + es[1] + es[2]
    oa_ref[0] = (es[0] / den) * outs[0] + (es[1] / den) * outs[1] + (es[2] / den) * outs[2]
    u = u_ref[0]
    stored = jnp.sum(pool_ref[0] * pw_ref[0:POOL_STATE, :], axis=0, keepdims=True)
    pooled_ref[0] = stored + u * pw_ref[POOL_STATE:POOL_STATE + 1, :] - u


def _decode_call(q, kn, vn, u, pool_prev, k_caches, v_caches, bdec, bnew, pw):
    db = pool_prev.shape[0]
    per_b = lambda t: pl.BlockSpec((1,) + t.shape[1:], lambda b: (b,) + (0,) * (t.ndim - 1))
    const = lambda t: pl.BlockSpec(t.shape, lambda b: (0,) * t.ndim)
    cache_specs, cache_args = [], []
    for g, (_, dil) in enumerate(ATTN_GROUPS):
        for c in (k_caches[g], v_caches[g]):
            rows = c.shape[1]
            cache_args.append(c.reshape(db, rows // dil, dil, HEADS, HEAD_DIM)[:, :, 0])
            cache_specs.append(pl.BlockSpec((1, ATTN_STEPS, HEADS, HEAD_DIM), lambda b: (b, 0, 0, 0)))
    u3 = u.reshape(db, 1, u.shape[-1])
    return pl.pallas_call(
        _decode_body,
        grid=(db,),
        in_specs=[per_b(q), per_b(kn), per_b(vn), per_b(u3), per_b(pool_prev)] + cache_specs +
                 [const(bdec), const(bnew), const(pw)],
        out_specs=[pl.BlockSpec((1, HEADS, HEAD_DIM), lambda b: (b, 0, 0)),
                   pl.BlockSpec((1, 1, GROUP_W), lambda b: (b, 0, 0))],
        out_shape=[jax.ShapeDtypeStruct((db, HEADS, HEAD_DIM), F32),
                   jax.ShapeDtypeStruct((db, 1, GROUP_W), F32)],
        compiler_params=_params(("arbitrary",)),
        name="decode_mix",
    )(q, kn, vn, u3, pool_prev, *cache_args, bdec, bnew, pw)


def _merge_tail(x, oa, pooled, gates, gate_msa, shift2, scale2, g2, wa, wb, wo, pw_ref, ps,
                x1_ref, h2_ref):
    mixed = jnp.concatenate(
        [jnp.dot(pooled[:, j * POOL_GROUP:(j + 1) * POOL_GROUP].astype(BF16), pw_ref[j],
                 preferred_element_type=F32) for j in range(len(POOL_WINDOWS))], axis=1) * ps
    d = x.shape[-1]
    ga = gates[:, :d].astype(F32)
    gb = gates[:, d:].astype(F32)
    merged = (_sigmoid(ga) * jnp.dot(oa.astype(BF16), wa, preferred_element_type=F32) +
              _sigmoid(gb) * jnp.dot(mixed.astype(BF16), wb, preferred_element_type=F32))
    y = jnp.dot(merged.astype(BF16), wo, preferred_element_type=F32)
    x1 = x + gate_msa * y
    x1_ref[0] = x1
    ms = jnp.mean(x1 * x1, axis=-1, keepdims=True)
    h2 = x1 * lax.rsqrt(ms + EPS) * g2
    _store_rows(h2_ref.at[0], h2 * (1.0 + scale2) + shift2)


def _merge_prompt_body(x_ref, gt_ref, oa_ref, u_ref, uh_ref,
                       gm_ref, sh_ref, sc_ref, g2_ref, wa_ref, wb_ref, wo_ref, pw_ref, ps_ref,
                       x1_ref, h2_ref):
    i = pl.program_id(1)
    tm = x_ref.shape[1]
    oa = oa_ref[0]
    u = u_ref[0]
    halo = jnp.where(i == 0, 0.0, uh_ref[0])
    ext = jnp.concatenate([halo, u], axis=0)
    hw = halo.shape[0]
    acc = ext
    parts = []
    for j, w in enumerate(POOL_WINDOWS):
        acc = acc[:, POOL_GROUP * (1 if j else 0):]
        acc = acc + pltpu.roll(acc, w // 2, axis=0)
        parts.append(acc[hw:, :POOL_GROUP])
    tot = jnp.concatenate(parts, axis=1)
    lane = lax.broadcasted_iota(jnp.int32, (tm, GROUP_W), 1)
    wcol = jnp.left_shift(2, lane // POOL_GROUP)
    pos = i * tm + lax.broadcasted_iota(jnp.int32, (tm, GROUP_W), 0)
    cnt = jnp.minimum(pos + 1, wcol).astype(F32)
    pooled = tot / cnt - u
    _merge_tail(x_ref[0], oa, pooled, gt_ref[0], gm_ref[0], sh_ref[0], sc_ref[0], g2_ref[...],
                wa_ref[...], wb_ref[...], wo_ref[...], pw_ref, ps_ref[...], x1_ref, h2_ref)


def _merge_sample_body(x_ref, gt_ref, oa_ref, pooled_ref,
                       gm_ref, sh_ref, sc_ref, g2_ref, wa_ref, wb_ref, wo_ref, pw_ref, ps_ref,
                       x1_ref, h2_ref):
    _merge_tail(x_ref[0], oa_ref[0], pooled_ref[0], gt_ref[0], gm_ref[0], sh_ref[0], sc_ref[0], g2_ref[...],
                wa_ref[...], wb_ref[...], wo_ref[...], pw_ref, ps_ref[...], x1_ref, h2_ref)


def _merge_call(x, gates, mixer_inputs, mods, g2, wa, wb, wo, pw, ps, prompt):
    bsz, s, d = x.shape
    assert d == ROW_SLABS * LANES
    tm = min(ROW_TILE, s)
    row = lambda t: pl.BlockSpec((1, tm, t.shape[-1]), lambda b, i: (b, i, 0))
    const = lambda t: pl.BlockSpec(t.shape, lambda b, i: (0,) * t.ndim)
    gm, sh, sc = mods
    in_specs = [row(x), row(gates)]
    args = [x, gates]
    if prompt:
        oa, u = mixer_inputs
        hw = 16
        in_specs += [row(oa), row(u),
                     pl.BlockSpec((1, hw, u.shape[-1]), lambda b, i: (b, jnp.maximum(i * (tm // hw) - 1, 0), 0))]
        args += [oa, u, u]
        body = _merge_prompt_body
    else:
        in_specs += [row(t) for t in mixer_inputs]
        args += list(mixer_inputs)
        body = _merge_sample_body
    in_specs += [_mod_spec(m, tm, d) for m in (gm, sh, sc)]
    in_specs += [const(t) for t in (g2, wa, wb, wo, pw, ps)]
    args += [gm, sh, sc, g2, wa, wb, wo, pw, ps]
    return pl.pallas_call(
        body,
        grid=(bsz, s // tm),
        in_specs=in_specs,
        out_specs=[row(x), pl.BlockSpec((1, tm * ROW_SLABS, LANES), lambda b, i: (b, i, 0))],
        out_shape=[jax.ShapeDtypeStruct((bsz, s, d), F32),
                   jax.ShapeDtypeStruct((bsz, s * ROW_SLABS, LANES), F32)],
        compiler_params=_params(("arbitrary", "arbitrary")),
        name="merge_prompt" if prompt else "merge_sample",
    )(*args)


def _router_body(h_ref, rw_ref, rb_ref, tri_ref, idx_ref, wt_ref, rank_ref, cnt_ref, carry):
    tm = h_ref.shape[0] // ROW_SLABS

    @pl.when(pl.program_id(0) == 0)
    def _():
        carry[...] = jnp.zeros_like(carry)

    ne = rw_ref.shape[1]
    gsz = ne // N_EXPERT_GROUPS
    logits = jnp.dot(_load_rows(h_ref, tm), rw_ref[...], precision=HIGHEST, preferred_element_type=F32)
    scores = _sigmoid(logits)
    sel = scores + rb_ref[...]
    lane = lax.broadcasted_iota(jnp.int32, (tm, ne), 1)
    lanef = lane.astype(F32)
    grp = lane // gsz
    ninf = -jnp.inf
    far = float(ne)

    def first_max(x):
        m = jnp.max(x, axis=-1, keepdims=True)
        at = jnp.min(jnp.where(x == m, lanef, far), axis=-1, keepdims=True)
        return m, at

    gs = []
    for g in range(N_EXPERT_GROUPS):
        xg = jnp.where(grp == g, sel, ninf)
        m1, a1 = first_max(xg)
        m2 = jnp.max(jnp.where(lanef == a1, ninf, xg), axis=-1, keepdims=True)
        gs.append(m1 + m2)
    allowed = jnp.zeros((tm, ne), jnp.bool_)
    for g in range(N_EXPERT_GROUPS):
        ahead = jnp.zeros((tm, 1), F32)
        for o in range(N_EXPERT_GROUPS):
            if o == g:
                continue
            beats = (gs[o] > gs[g]) | ((gs[o] == gs[g]) & (o < g))
            ahead = ahead + beats.astype(F32)
        allowed = allowed | ((grp == g) & (ahead < TOPK_GROUPS))
    masked = jnp.where(allowed, sel, ninf)
    lane_o = lax.broadcasted_iota(jnp.int32, (tm, LANES), 1)
    idx_t = jnp.zeros((tm, LANES), F32)
    wt_t = jnp.zeros((tm, LANES), F32)
    wsum = jnp.zeros((tm, 1), F32)
    picked = jnp.zeros((tm, ne), F32)
    ats = []
    for k in range(TOP_K):
        _, at = first_max(masked)
        hit = lanef == at
        wk = jnp.sum(jnp.where(hit, scores, 0.0), axis=-1, keepdims=True)
        masked = jnp.where(hit, ninf, masked)
        picked = picked + hit.astype(F32)
        ats.append(at)
        idx_t = jnp.where(lane_o == k, at, idx_t)
        wt_t = jnp.where(lane_o == k, wk, wt_t)
        wsum = wsum + wk
    idx_ref[...] = idx_t.astype(jnp.int32)
    wt_ref[...] = wt_t / wsum * ROUTED_SCALE
    before = jnp.dot(tri_ref[...], picked.astype(BF16), preferred_element_type=F32) + carry[...]
    rank_t = jnp.zeros((tm, LANES), F32)
    for k in range(TOP_K):
        rk = jnp.sum(jnp.where(lanef == ats[k], before, 0.0), axis=-1, keepdims=True)
        rank_t = jnp.where(lane_o == k, rk, rank_t)
    rank_ref[...] = rank_t.astype(jnp.int32)
    carry[...] = carry[...] + jnp.sum(picked, axis=0, keepdims=True)
    cnt_ref[...] = carry[...].astype(jnp.int32)


def _router_call(h2, rw, rb):
    nt = h2.shape[0] // ROW_SLABS
    tm = ROW_TILE
    d, ne = rw.shape
    tri = jnp.asarray(np.tril(np.ones((tm, tm), np.float32), -1), BF16)
    tile = pl.BlockSpec((tm, LANES), lambda i: (i, 0))
    return pl.pallas_call(
        _router_body,
        grid=(nt // tm,),
        in_specs=[pl.BlockSpec((tm * ROW_SLABS, LANES), lambda i: (i, 0)),
                  pl.BlockSpec((d, ne), lambda i: (0, 0)),
                  pl.BlockSpec((1, ne), lambda i: (0, 0)),
                  pl.BlockSpec((tm, tm), lambda i: (0, 0))],
        out_specs=[tile, tile, tile, pl.BlockSpec((1, ne), lambda i: (0, 0))],
        out_shape=[jax.ShapeDtypeStruct((nt, LANES), jnp.int32),
                   jax.ShapeDtypeStruct((nt, LANES), F32),
                   jax.ShapeDtypeStruct((nt, LANES), jnp.int32),
                   jax.ShapeDtypeStruct((1, ne), jnp.int32)],
        scratch_shapes=[pltpu.VMEM((1, ne), F32)],
        compiler_params=_params(("arbitrary",)),
        name="router",
    )(h2, rw, rb.reshape(1, ne), tri)


def _slot_body(idx_ref, rank_ref, offs_ref, pos_ref):
    tm = idx_ref.shape[0]
    ne = offs_ref.shape[1]
    idx = idx_ref[...]
    rank = rank_ref[...]
    lane = lax.broadcasted_iota(jnp.int32, (tm, ne), 1)
    lane_o = lax.broadcasted_iota(jnp.int32, (tm, LANES), 1)
    offs = offs_ref[...]
    pos = jnp.zeros((tm, LANES), F32)
    for k in range(TOP_K):
        start = jnp.sum(jnp.where(lane == idx[:, k:k + 1], offs, 0.0), axis=-1, keepdims=True)
        pos = jnp.where(lane_o == k, start, pos)
    pos_ref[...] = pos.astype(jnp.int32) + rank


def _slot_call(idx, rank, offs):
    nt = idx.shape[0]
    tm = ROW_TILE
    ne = offs.shape[1]
    tile = pl.BlockSpec((tm, LANES), lambda i: (i, 0))
    return pl.pallas_call(
        _slot_body,
        grid=(nt // tm,),
        in_specs=[tile, tile, pl.BlockSpec((1, ne), lambda i: (0, 0))],
        out_specs=tile,
        out_shape=jax.ShapeDtypeStruct((nt, LANES), jnp.int32),
        compiler_params=_params(("arbitrary",)),
        name="slots",
    )(idx, rank, offs)


def _row_tile(ref, r):
    return ref.at[pl.ds(pl.multiple_of(r * ROW_SLABS, ROW_SLABS), ROW_SLABS), :]


def _dispatch_body(fill_ref, pos_ref, h_ref, xs_hbm, zeros, zsem, tsem, sem):
    td = h_ref.shape[0] // ROW_SLABS
    chunk = zeros.shape[0]
    n_ent = fill_ref.shape[0] - 1
    n_chunks = xs_hbm.shape[0] // chunk

    def fill_chunk(c, sem_, wait):
        cp = pltpu.make_async_copy(zeros, xs_hbm.at[pl.ds(pl.multiple_of(c * chunk, chunk), chunk), :], sem_)
        if wait:
            cp.wait()
        else:
            cp.start()

    def fill_tail(wait):
        lax.fori_loop(fill_ref[n_ent], n_chunks, lambda b, c: (fill_chunk(b, tsem, wait), c)[1], 0)

    @pl.when(pl.program_id(0) == 0)
    def _():
        zeros[...] = jnp.zeros_like(zeros)

        def fill(j, wait):
            @pl.when(fill_ref[j] >= 0)
            def _():
                fill_chunk(fill_ref[j], zsem, wait)
        fill_tail(False)
        for wait in (False, True):
            lax.fori_loop(0, n_ent, lambda j, c, wait=wait: (fill(j, wait), c)[1], 0)

    @pl.when(pl.program_id(0) == pl.num_programs(0) - 1)
    def _():
        fill_tail(True)

    def one(t, carry):
        for k in range(TOP_K):
            pltpu.make_async_copy(_row_tile(h_ref, t), _row_tile(xs_hbm, pos_ref[0, 0, t * TOP_K + k]),
                                  sem).start(priority=k % 2)
        return carry
    lax.fori_loop(0, td, one, 0, unroll=2)
    for k in range(TOP_K):
        pltpu.make_async_copy(h_ref, xs_hbm.at[pl.ds(0, td * ROW_SLABS), :], sem).wait()


def _dispatch_call(fill_chunks, pos_tok, h2, n_blocks):
    nt = h2.shape[0] // ROW_SLABS
    td = ROW_TILE
    te = EXPERT_TILE
    grid_spec = pltpu.PrefetchScalarGridSpec(
        num_scalar_prefetch=1,
        grid=(nt // td,),
        in_specs=[pl.BlockSpec((1, 1, td * TOP_K), lambda i, lb: (i, 0, 0), memory_space=pltpu.SMEM),
                  pl.BlockSpec((td * ROW_SLABS, LANES), lambda i, lb: (i, 0))],
        out_specs=pl.BlockSpec(memory_space=pl.ANY),
        scratch_shapes=[pltpu.VMEM((FILL_ROWS * ROW_SLABS, LANES), F32), pltpu.SemaphoreType.DMA,
                        pltpu.SemaphoreType.DMA, pltpu.SemaphoreType.DMA],
    )
    return pl.pallas_call(
        _dispatch_body,
        grid_spec=grid_spec,
        out_shape=jax.ShapeDtypeStruct((n_blocks * te * ROW_SLABS, LANES), F32),
        compiler_params=_params(("arbitrary",)),
        name="dispatch",
    )(fill_chunks, pos_tok, h2)


def _expert_body(exp_ref, nu_ref, xs_hbm, wg_ref, wu_ref, wd_ref, y_ref, xbuf, xsem, wgb, wub, wdb):
    b = pl.program_id(0)
    n_used = nu_ref[0]
    live = b < n_used
    new_expert = jnp.logical_or(b == 0, exp_ref[b] != exp_ref[jnp.maximum(b - 1, 0)])
    ring, rows = xbuf.shape[0], xbuf.shape[1]

    def fetch(blk):
        slot = blk % ring
        return pltpu.make_async_copy(xs_hbm.at[pl.ds(pl.multiple_of(blk * rows, rows), rows), :],
                                     xbuf.at[slot], xsem.at[slot])

    @pl.when(b == 0)
    def _():
        for j in range(ring - 1):
            @pl.when(j < n_used)
            def _():
                fetch(j).start()

    @pl.when(b + ring - 1 < n_used)
    def _():
        fetch(b + ring - 1).start()

    @pl.when(jnp.logical_and(live, new_expert))
    def _():
        wgb[...] = wg_ref[0].astype(BF16)
        wub[...] = wu_ref[0].astype(BF16)
        wdb[...] = wd_ref[0].astype(BF16)

    @pl.when(live)
    def _():
        te = rows // ROW_SLABS
        fetch(b).wait()
        x = _load_rows(xbuf.at[b % ring], te).astype(BF16)
        gt = jnp.dot(x, wgb[...], preferred_element_type=F32)
        up = jnp.dot(x, wub[...], preferred_element_type=F32)
        act = (gt * _sigmoid(gt)) * up
        _store_rows(y_ref, jnp.dot(act.astype(BF16), wdb[...], preferred_element_type=F32))

    @pl.when(jnp.logical_not(live))
    def _():
        y_ref[...] = jnp.zeros_like(y_ref)


def _expert_call(blk_exp, n_used, xs, wg, wu, wd):
    n_blocks = blk_exp.shape[0]
    te = EXPERT_TILE
    d, hid = wg.shape[-2:]
    used = lambda b, nu: jnp.minimum(b, nu[0] - 1)
    wspec = lambda r, c: pl.BlockSpec((1, r, c), lambda b, e, nu: (e[used(b, nu)], 0, 0))
    grid_spec = pltpu.PrefetchScalarGridSpec(
        num_scalar_prefetch=2,
        grid=(n_blocks,),
        in_specs=[pl.BlockSpec(memory_space=pl.ANY), wspec(d, hid), wspec(d, hid), wspec(hid, d)],
        out_specs=pl.BlockSpec((te * ROW_SLABS, LANES), lambda b, e, nu: (b, 0)),
        scratch_shapes=[pltpu.VMEM((EXPERT_RING, te * ROW_SLABS, LANES), F32), pltpu.SemaphoreType.DMA((EXPERT_RING,)),
                        pltpu.VMEM((d, hid), BF16), pltpu.VMEM((d, hid), BF16), pltpu.VMEM((hid, d), BF16)],
    )
    return pl.pallas_call(
        _expert_body,
        grid_spec=grid_spec,
        out_shape=jax.ShapeDtypeStruct(xs.shape, F32),
        compiler_params=_params(("arbitrary",)),
        name="expert_ffn",
    )(blk_exp, n_used, xs, wg, wu, wd)


def _gather_rows(idx_ref, src_hbm, dst, sem, n):
    def pair(i, carry):
        for p in range(2):
            j = 2 * i + p
            pltpu.make_async_copy(_row_tile(src_hbm, idx_ref[0, 0, j]), _row_tile(dst, j), sem).start(priority=p)
        return carry
    lax.fori_loop(0, n // 2, pair, 0, unroll=4)


def _combine_body(pos_ref, posn_ref, y_hbm, wt_ref, h_ref, x1_ref, gate_ref, sg_ref, su_ref, sd_ref,
                  out_ref, ybuf, sem):
    nb, ni = pl.num_programs(0), pl.num_programs(1)
    step = pl.program_id(0) * ni + pl.program_id(1)
    slot = step % 2
    tc = h_ref.shape[0] // ROW_SLABS
    rows = tc * TOP_K

    @pl.when(step == 0)
    def _():
        _gather_rows(pos_ref, y_hbm, ybuf.at[0], sem.at[0], rows)

    @pl.when(step + 1 < nb * ni)
    def _():
        _gather_rows(posn_ref, y_hbm, ybuf.at[1 - slot], sem.at[1 - slot], rows)

    pltpu.make_async_copy(y_hbm.at[pl.ds(0, rows * ROW_SLABS), :], ybuf.at[slot], sem.at[slot]).wait()
    wt = wt_ref[...]
    routed = jnp.zeros(out_ref.shape[1:], F32)
    for k in range(TOP_K):
        routed = routed + wt[:, k:k + 1] * _load_rows(ybuf.at[slot], tc, first=k * tc)
    hb = _load_rows(h_ref, tc).astype(BF16)
    gt = jnp.dot(hb, sg_ref[...], preferred_element_type=F32)
    up = jnp.dot(hb, su_ref[...], preferred_element_type=F32)
    shared = jnp.dot(((gt * _sigmoid(gt)) * up).astype(BF16), sd_ref[...], preferred_element_type=F32)
    out_ref[0] = x1_ref[0] + gate_ref[0] * (routed + shared)


def _combine_call(pos_tiles, tile_off, y_sorted, wts, h2, x1, gate, sg, su, sd):
    bsz, s, d = x1.shape
    tc = min(COMBINE_TILE, s)
    ni = s // tc
    n_tiles = pos_tiles.shape[0]
    lin = lambda b, i: tile_off + b * ni + i
    const = lambda t: pl.BlockSpec(t.shape, lambda b, i: (0,) * t.ndim)
    return pl.pallas_call(
        _combine_body,
        grid=(bsz, ni),
        in_specs=[
            pl.BlockSpec((1, 1, tc * TOP_K), lambda b, i: (lin(b, i), 0, 0), memory_space=pltpu.SMEM),
            pl.BlockSpec((1, 1, tc * TOP_K), lambda b, i: (jnp.minimum(lin(b, i) + 1, n_tiles - 1), 0, 0),
                         memory_space=pltpu.SMEM),
            pl.BlockSpec(memory_space=pl.ANY),
            pl.BlockSpec((tc, LANES), lambda b, i: (lin(b, i), 0)),
            pl.BlockSpec((tc * ROW_SLABS, LANES), lambda b, i: (lin(b, i), 0)),
            pl.BlockSpec((1, tc, d), lambda b, i: (b, i, 0)),
            _mod_spec(gate, tc, d),
            const(sg), const(su), const(sd),
        ],
        out_specs=pl.BlockSpec((1, tc, d), lambda b, i: (b, i, 0)),
        out_shape=jax.ShapeDtypeStruct((bsz, s, d), F32),
        scratch_shapes=[pltpu.VMEM((2, tc * TOP_K * ROW_SLABS, LANES), F32), pltpu.SemaphoreType.DMA((2,))],
        compiler_params=_params(("arbitrary", "arbitrary")),
        name="combine",
    )(pos_tiles, pos_tiles, y_sorted, wts, h2, x1, gate, sg, su, sd)


def _t5_bucket(dist):
    exact = N_BUCKETS // 2
    d = np.asarray(dist)
    large = exact + (np.log(np.maximum(d, 1) / exact) / np.log(REL_MAX_DIST / exact) * (N_BUCKETS - exact)).astype(np.int32)
    large = np.minimum(large, N_BUCKETS - 1)
    return np.where(d < exact, d, large).astype(np.int32)


def _group_bias(rel_bias, gi):
    window, dil = ATTN_GROUPS[gi]
    bkt = _t5_bucket(np.arange(window // dil + 1) * dil)
    onehot = jnp.asarray(bkt[:, None] == np.arange(N_BUCKETS)[None, :], F32)
    cols = rel_bias[:, gi * HEADS:(gi + 1) * HEADS].astype(F32)
    return jnp.dot(onehot, cols, precision=HIGHEST).T


def _prompt_bias_table(bias_j):
    blk = ATTN_BLOCK
    n = bias_j.shape[1] - 1
    period = 3 * blk
    row0 = jnp.concatenate([jnp.flip(bias_j, axis=1), jnp.full((HEADS, period - n - 1), NEG, F32)], axis=1)
    flat = jnp.tile(row0, (1, blk))[:, :blk * (period - 1)]
    tab = flat.reshape(HEADS, blk, period - 1)[:, :, :2 * blk]
    return tab.reshape(HEADS // HEADS_PER_SLAB, HEADS_PER_SLAB * blk, 2 * blk)


def _decode_bias_tables(bias_j):
    old = jnp.flip(bias_j[:, 1:], axis=1).T[:, :, None]
    new = bias_j[:, 0][:, None]
    return old, new


def _block_schedule(counts, na):
    te = EXPERT_TILE
    n_blocks = na // te + N_EXPERTS
    pcounts = (counts + te - 1) // te * te
    pends = jnp.cumsum(pcounts)
    offs = pends - pcounts
    blk_start = jnp.arange(n_blocks, dtype=jnp.int32) * te
    blk_exp = jnp.minimum(jnp.sum(pends[None, :] <= blk_start[:, None], axis=1), N_EXPERTS - 1).astype(jnp.int32)
    n_used = (pends[-1] // te).astype(jnp.int32).reshape(1)
    fr = FILL_ROWS
    fill_start = offs + counts // fr * fr
    n_fill = (pends - fill_start) // fr
    first = fill_start // fr
    chunks = jnp.concatenate([jnp.where(n_fill > j, first + j, -1) for j in range(te // fr)]
                             + [n_used * (te // fr)]).astype(jnp.int32)
    return offs, blk_exp, n_used, chunks, n_blocks


def kernel(x_prompt, x_sample, cache_k_w128, cache_v_w128, cache_k_w512, cache_v_w512, cache_k_w2048, cache_v_w2048, state_pool, c_prompt, c_sample, ada_w, ada_b, norm1, norm2, w_in, q_gain, k_gain, rel_bias, pool_w, pool_scale, w_br_a, w_br_b, w_out, router_w, router_bias, exp_w_gate, exp_w_up, exp_w_down, sh_w_gate, sh_w_up, sh_w_down):
    depth = ada_w.shape[0]
    bsz, seq, d = x_prompt.shape
    db = x_sample.shape[0]
    assert x_sample.shape[1] == 1 and db <= SAMPLE_PAD and seq % (ATTN_BLOCK * ATTN_GROUPS[-1][1]) == 0
    k_caches = (cache_k_w128, cache_k_w512, cache_k_w2048)
    v_caches = (cache_v_w128, cache_v_w512, cache_v_w2048)
    for c, (w, dil) in zip(k_caches, ATTN_GROUPS):
        assert c.shape[2] == w == ATTN_STEPS * dil

    bias_js = [_group_bias(rel_bias, g) for g in range(N_GROUPS)]
    bias_tabs = jnp.stack([_prompt_bias_table(b) for b in bias_js])
    dec = [_decode_bias_tables(b) for b in bias_js]
    bdec = jnp.stack([t[0] for t in dec])
    bnew = jnp.stack([t[1] for t in dec])
    bd = jnp.asarray((np.arange(MXU_DIM)[:, None] // HEAD_DIM == np.arange(MXU_DIM)[None, :] // HEAD_DIM)
                     / HEAD_DIM, BF16)
    wcols = np.repeat(np.asarray(POOL_WINDOWS), POOL_GROUP)
    pw_dec = jnp.asarray(np.where(np.arange(POOL_STATE + 1)[:, None] >= POOL_STATE + 1 - wcols[None, :],
                                  1.0 / wcols[None, :], 0.0), F32)

    xp = x_prompt
    xs = jnp.pad(x_sample.reshape(db, d), ((0, SAMPLE_PAD - db), (0, 0))).reshape(1, SAMPLE_PAD, d)
    n_prompt = bsz * seq
    outs = {name: [] for name in ('pk', 'pv', 'sk', 'sv')}
    outs['pk'] = [[] for _ in range(N_GROUPS)]
    outs['pv'] = [[] for _ in range(N_GROUPS)]
    outs['sk'] = [[] for _ in range(N_GROUPS)]
    outs['sv'] = [[] for _ in range(N_GROUPS)]
    ppool, spool = [], []
    for l in range(depth):
        c_all = jnp.concatenate([c_prompt, c_sample], axis=0)
        mod = _ada_call(c_all, ada_w[l], ada_b[l])
        mod_p = [m.reshape(bsz, 1, d) for m in jnp.split(mod[:bsz], 6, axis=-1)]
        mod_s = [jnp.pad(m, ((0, SAMPLE_PAD - db), (0, 0))).reshape(1, SAMPLE_PAD, d)
                 for m in jnp.split(mod[bsz:], 6, axis=-1)]
        w_in_bf = w_in[l].astype(BF16)
        qg = (jnp.tile(q_gain[l], (1, HEADS)) * (HEAD_DIM ** -0.5)).reshape(1, -1)
        kg = jnp.tile(k_gain[l], (1, HEADS)).reshape(1, -1)
        g1 = norm1[l].reshape(1, d)
        g2 = norm2[l].reshape(1, d)
        wa, wb, wo = (w_br_a[l].astype(BF16), w_br_b[l].astype(BF16), w_out[l].astype(BF16))
        pw = pool_w[l].astype(BF16)
        ps = pool_scale[l].reshape(1, -1)

        q, k, v, u, gates = _inproj_call(xp, mod_p[0], mod_p[1], g1, w_in_bf, qg, kg, bd)
        oa_p = _attn_call(q, k, v, bias_tabs)
        for g, (window, dil) in enumerate(ATTN_GROUPS):
            keep = min(window, seq)
            cs = slice(g * GROUP_W, (g + 1) * GROUP_W)
            outs['pk'][g].append(k[:, seq - keep:, cs].reshape(bsz, keep, HEADS, HEAD_DIM))
            outs['pv'][g].append(v[:, seq - keep:, cs].reshape(bsz, keep, HEADS, HEAD_DIM))
        ppool.append(u[:, seq - POOL_STATE:])
        x1p, h2p = _merge_call(xp, gates, (oa_p, u), (mod_p[2], mod_p[3], mod_p[4]),
                               g2, wa, wb, wo, pw, ps, prompt=True)

        qs, ks, vs, us, gates_s = _inproj_call(xs, mod_s[0], mod_s[1], g1, w_in_bf, qg, kg, bd)
        heads3 = lambda t: t[0, :db].reshape(db, N_GROUPS * HEADS, HEAD_DIM)
        q3, k3, v3 = heads3(qs), heads3(ks), heads3(vs)
        kcl = [c[l] for c in k_caches]
        vcl = [c[l] for c in v_caches]
        oa_s, pooled_s = _decode_call(q3, k3, v3, us[0, :db], state_pool[l], kcl, vcl, bdec, bnew, pw_dec)
        padrows = lambda t: jnp.pad(t.reshape(db, GROUP_W), ((0, SAMPLE_PAD - db), (0, 0))).reshape(1, SAMPLE_PAD, GROUP_W)
        x1s, h2s = _merge_call(xs, gates_s, (padrows(oa_s), padrows(pooled_s)), (mod_s[2], mod_s[3], mod_s[4]),
                               g2, wa, wb, wo, pw, ps, prompt=False)
        for g in range(N_GROUPS):
            hs = slice(g * HEADS, (g + 1) * HEADS)
            outs['sk'][g].append(jnp.concatenate([kcl[g][:, 1:], k3[:, None, hs]], axis=1))
            outs['sv'][g].append(jnp.concatenate([vcl[g][:, 1:], v3[:, None, hs]], axis=1))
        spool.append(jnp.concatenate([state_pool[l][:, 1:], us[0, :db, None, :]], axis=1))

        h2 = jnp.concatenate([h2p.reshape(n_prompt * ROW_SLABS, LANES), h2s.reshape(SAMPLE_PAD * ROW_SLABS, LANES)], axis=0)
        nt = n_prompt + SAMPLE_PAD
        idx, wts, rank, counts = _router_call(h2, router_w[l], router_bias[l])
        offs, blk_exp, n_used, fill_chunks, n_blocks = _block_schedule(counts[0], nt * TOP_K)
        pos = _slot_call(idx, rank, offs.astype(F32).reshape(1, -1))[:, :TOP_K]
        xs_sorted = _dispatch_call(fill_chunks, pos.reshape(nt // ROW_TILE, 1, ROW_TILE * TOP_K), h2, n_blocks)
        y_sorted = _expert_call(blk_exp, n_used, xs_sorted, exp_w_gate[l], exp_w_up[l], exp_w_down[l])
        tc = COMBINE_TILE
        pos_tiles = pos.reshape(nt // tc, tc, TOP_K).transpose(0, 2, 1).reshape(nt // tc, 1, TOP_K * tc)
        sg, su, sd = sh_w_gate[l].astype(BF16), sh_w_up[l].astype(BF16), sh_w_down[l].astype(BF16)
        xp = _combine_call(pos_tiles, 0, y_sorted, wts, h2, x1p, mod_p[5], sg, su, sd)
        xs = _combine_call(pos_tiles, n_prompt // tc, y_sorted, wts, h2, x1s, mod_s[5], sg, su, sd)

    stack = lambda parts: jnp.stack(parts)
    res = [xp, xs[0, :db].reshape(db, 1, d)]
    for g in range(N_GROUPS):
        res += [stack(outs['pk'][g]), stack(outs['pv'][g])]
    res.append(stack(ppool))
    for g in range(N_GROUPS):
        res += [stack(outs['sk'][g]), stack(outs['sv'][g])]
    res.append(stack(spool))
    return tuple(res)
```

```python
import numpy as np
import jax
import jax.numpy as jnp
from jax import lax
from jax.experimental import pallas as pl
from jax.experimental.pallas import tpu as pltpu

F32 = jnp.float32
BF16 = jnp.bfloat16
HIGHEST = lax.Precision.HIGHEST

HEAD_DIM = 64
HEADS = 8
GROUP_W = HEADS * HEAD_DIM
ATTN_GROUPS = ((128, 1), (512, 4), (2048, 16))
N_GROUPS = len(ATTN_GROUPS)
ATTN_STEPS = 128
ATTN_BLOCK = 128
POOL_WINDOWS = (2, 4, 8, 16)
POOL_GROUP = 128
POOL_STATE = 15
N_BUCKETS = 32
REL_MAX_DIST = 2048
N_EXPERTS = 256
TOP_K = 8
N_EXPERT_GROUPS = 8
TOPK_GROUPS = 4
ROUTED_SCALE = 2.5
EPS = 1e-6
NEG = -1e30

LANES = 128
MXU_DIM = 256
VMEM_LIMIT = 56 * 1024 * 1024

INPROJ_TILE = 512
ROW_TILE = 256
EXPERT_TILE = 512
FILL_ROWS = 256
EXPERT_RING = 3
COMBINE_TILE = 256
SAMPLE_PAD = 256


def _sigmoid(x):
    return 1.0 / (1.0 + jnp.exp(-x))


def _params(sem):
    return pltpu.CompilerParams(dimension_semantics=sem, vmem_limit_bytes=VMEM_LIMIT)


ROW_SLABS = 8


def _load_rows(ref, n, first=0):
    return jnp.concatenate([ref[pl.ds(first * ROW_SLABS + s, n, stride=ROW_SLABS), :]
                            for s in range(ROW_SLABS)], axis=1)


def _store_rows(ref, val):
    n = val.shape[0]
    for s in range(ROW_SLABS):
        ref[pl.ds(s, n, stride=ROW_SLABS), :] = val[:, s * LANES:(s + 1) * LANES]


def _ada_body(c_ref, w_ref, b_ref, o_ref):
    c = c_ref[...]
    s = c * _sigmoid(c)
    o_ref[...] = jnp.dot(s, w_ref[...], precision=HIGHEST, preferred_element_type=F32) + b_ref[...]


def _ada_call(c, w, b):
    n, d = c.shape
    cols = w.shape[1]
    tn = 512
    return pl.pallas_call(
        _ada_body,
        grid=(cols // tn,),
        in_specs=[pl.BlockSpec((n, d), lambda j: (0, 0)),
                  pl.BlockSpec((d, tn), lambda j: (0, j)),
                  pl.BlockSpec((1, tn), lambda j: (0, j))],
        out_specs=pl.BlockSpec((n, tn), lambda j: (0, j)),
        out_shape=jax.ShapeDtypeStruct((n, cols), F32),
        compiler_params=_params(("arbitrary",)),
        name="adaln",
    )(c, w, b.reshape(1, cols))


def _inproj_body(x_ref, shift_ref, scale_ref, g_ref, w_ref, qg_ref, kg_ref, bd_ref,
                 q_ref, k_ref, v_ref, u_ref, gt_ref):
    x = x_ref[0]
    ms = jnp.mean(x * x, axis=-1, keepdims=True)
    h = x * lax.rsqrt(ms + EPS) * g_ref[...]
    h = h * (1.0 + scale_ref[0]) + shift_ref[0]
    hb = h.astype(BF16)
    n_chunks = w_ref.shape[1] // GROUP_W
    for c in range(n_chunks):
        z = jnp.dot(hb, w_ref[:, c * GROUP_W:(c + 1) * GROUP_W], preferred_element_type=F32)
        if c < 2 * N_GROUPS:
            zz = (z * z).astype(BF16)
            hms = jnp.concatenate(
                [jnp.dot(zz[:, s:s + MXU_DIM], bd_ref[...], preferred_element_type=F32)
                 for s in range(0, GROUP_W, MXU_DIM)], axis=1)
            g = c % N_GROUPS
            gain = (qg_ref if c < N_GROUPS else kg_ref)[:, g * GROUP_W:(g + 1) * GROUP_W]
            y = z * lax.rsqrt(hms + EPS) * gain
            if c < N_GROUPS:
                q_ref[0, :, g * GROUP_W:(g + 1) * GROUP_W] = y
            else:
                k_ref[0, :, g * GROUP_W:(g + 1) * GROUP_W] = y
        elif c < 3 * N_GROUPS:
            g = c - 2 * N_GROUPS
            v_ref[0, :, g * GROUP_W:(g + 1) * GROUP_W] = z
        elif c == 3 * N_GROUPS:
            u_ref[0] = z
        else:
            g = c - 3 * N_GROUPS - 1
            gt_ref[0, :, g * GROUP_W:(g + 1) * GROUP_W] = z.astype(BF16)


def _mod_spec(mod, tm, d):
    if mod.shape[1] == 1:
        return pl.BlockSpec((1, 1, d), lambda b, i: (b, 0, 0))
    return pl.BlockSpec((1, tm, d), lambda b, i: (b, i, 0))


def _inproj_call(x, shift, scale, g1, w_bf, qg, kg, bd):
    bsz, s, d = x.shape
    tm = min(INPROJ_TILE, s)
    qkv = N_GROUPS * GROUP_W
    n_gate = w_bf.shape[1] - 3 * qkv - GROUP_W
    const = lambda b, i: (0, 0)
    row = lambda width: pl.BlockSpec((1, tm, width), lambda b, i: (b, i, 0))
    return pl.pallas_call(
        _inproj_body,
        grid=(bsz, s // tm),
        in_specs=[row(d), _mod_spec(shift, tm, d), _mod_spec(scale, tm, d),
                  pl.BlockSpec((1, d), const),
                  pl.BlockSpec(w_bf.shape, const, pipeline_mode=pl.Buffered(1)),
                  pl.BlockSpec((1, qkv), const), pl.BlockSpec((1, qkv), const),
                  pl.BlockSpec((MXU_DIM, MXU_DIM), const)],
        out_specs=[row(qkv), row(qkv), row(qkv), row(GROUP_W), row(n_gate)],
        out_shape=[jax.ShapeDtypeStruct((bsz, s, qkv), F32),
                   jax.ShapeDtypeStruct((bsz, s, qkv), F32),
                   jax.ShapeDtypeStruct((bsz, s, qkv), F32),
                   jax.ShapeDtypeStruct((bsz, s, GROUP_W), F32),
                   jax.ShapeDtypeStruct((bsz, s, n_gate), BF16)],
        compiler_params=_params(("arbitrary", "arbitrary")),
        name="inproj",
    )(x, shift, scale, g1, w_bf, qg, kg, bd)


ATTN_ROWS = ATTN_BLOCK * ATTN_GROUPS[-1][1]
HEADS_PER_SLAB = LANES // HEAD_DIM


def _attn_body(*refs):
    ng = N_GROUPS
    qkv = refs[:3 * ng]
    bias_ref, o_ref = refs[3 * ng], refs[3 * ng + 1]
    scr = refs[3 * ng + 2:]
    first = pl.program_id(2) == 0
    blk = ATTN_BLOCK
    pb = ATTN_ROWS
    hs = HEADS_PER_SLAB
    lane = lax.broadcasted_iota(jnp.int32, (blk, LANES), 1)
    col = lax.broadcasted_iota(jnp.int32, (hs * blk, 2 * blk), 1)
    for g, (_, dil) in enumerate(ATTN_GROUPS):
        q_ref, k_ref, v_ref = qkv[3 * g:3 * g + 3]
        kext, vext, og, lg = scr[4 * g:4 * g + 4]
        span = blk * dil
        bias = bias_ref[g, 0]

        @pl.when(first)
        def _():
            kext[0:span, :] = jnp.zeros((span, LANES), F32)
            vext[0:span, :] = jnp.zeros((span, LANES), F32)
        kext[span:span + pb, :] = k_ref[0]
        vext[span:span + pb, :] = v_ref[0]

        rows = lambda start: pl.ds(start, blk, stride=dil) if dil > 1 else pl.ds(pl.multiple_of(start, blk), blk)
        per = 4

        def units(i, carry):
            bases, v2s, scores = [], [], []
            for j in range(per):
                u = per * i + j
                qb = u // dil
                base = qb * span + (u % dil)
                q = q_ref[0, rows(base), :]
                k2 = jnp.concatenate([kext[rows(base), :], kext[rows(base + span), :]], axis=0).astype(BF16)
                v2s.append(jnp.concatenate([vext[rows(base), :], vext[rows(base + span), :]], axis=0).astype(BF16))
                qs = jnp.concatenate([jnp.where(lane // HEAD_DIM == h, q, 0.0) for h in range(hs)],
                                     axis=0).astype(BF16)
                s = lax.dot_general(qs, k2, (((1,), (1,)), ((), ())), preferred_element_type=F32) + bias
                scores.append(jnp.where(jnp.logical_and(jnp.logical_and(first, qb == 0), col < blk), NEG, s))
                bases.append(base)
            s = jnp.concatenate(scores, axis=0)
            m = jnp.max(s, axis=-1, keepdims=True)
            p = jnp.exp(s - m)
            l = jnp.sum(p, axis=-1, keepdims=True)
            pb16 = p.astype(BF16)
            lse = m + jnp.log(l)
            for j in range(per):
                r0 = j * hs * blk
                o2 = jnp.dot(pb16[r0:r0 + hs * blk], v2s[j], preferred_element_type=F32) / l[r0:r0 + hs * blk]
                o = o2[0:blk]
                lw = jnp.broadcast_to(lse[r0:r0 + blk], (blk, LANES))
                for h in range(1, hs):
                    o = jnp.where(lane // HEAD_DIM == h, o2[h * blk:(h + 1) * blk], o)
                    lw = jnp.where(lane // HEAD_DIM == h, lse[r0 + h * blk:r0 + (h + 1) * blk], lw)
                og[rows(bases[j]), :] = o
                lg[rows(bases[j]), :] = lw
            return carry
        lax.fori_loop(0, pb // blk // per, units, 0)
        kext[0:span, :] = kext[pb:pb + span, :]
        vext[0:span, :] = vext[pb:pb + span, :]

    def merge(c, carry):
        rs = pl.ds(pl.multiple_of(c * blk, blk), blk)
        ls = [scr[4 * g + 3][rs, :] for g in range(ng)]
        mx = jnp.maximum(jnp.maximum(ls[0], ls[1]), ls[2])
        es = [jnp.exp(t - mx) for t in ls]
        den = es[0] + es[1] + es[2]
        acc = (es[0] / den) * scr[2][rs, :]
        for g in range(1, ng):
            acc = acc + (es[g] / den) * scr[4 * g + 2][rs, :]
        o_ref[0, rs, :] = acc.astype(o_ref.dtype)
        return carry
    lax.fori_loop(0, pb // blk, merge, 0)


def _attn_call(q, k, v, bias_tabs):
    bsz, s, qkv = q.shape
    pb = ATTN_ROWS
    n_slab = GROUP_W // LANES
    in_specs, args = [], []
    for g in range(N_GROUPS):
        spec = pl.BlockSpec((1, pb, LANES), lambda b, c, i, g=g: (b, i, g * n_slab + c))
        in_specs += [spec, spec, spec]
        args += [q, k, v]
    in_specs.append(pl.BlockSpec((N_GROUPS, 1) + bias_tabs.shape[2:], lambda b, c, i: (0, c, 0, 0)))
    scratch = []
    for _, dil in ATTN_GROUPS:
        ext = pltpu.VMEM((ATTN_BLOCK * dil + pb, LANES), F32)
        scratch += [ext, ext, pltpu.VMEM((pb, LANES), F32), pltpu.VMEM((pb, LANES), F32)]
    return pl.pallas_call(
        _attn_body,
        grid=(bsz, n_slab, s // pb),
        in_specs=in_specs,
        out_specs=pl.BlockSpec((1, pb, LANES), lambda b, c, i: (b, i, c)),
        out_shape=jax.ShapeDtypeStruct((bsz, s, GROUP_W), BF16),
        scratch_shapes=scratch,
        compiler_params=_params(("arbitrary", "arbitrary", "arbitrary")),
        name="attn",
    )(*args, bias_tabs)


def _decode_body(q_ref, kn_ref, vn_ref, u_ref, pool_ref, kc0, vc0, kc1, vc1, kc2, vc2,
                 bdec_ref, bnew_ref, pw_ref, oa_ref, pooled_ref):
    kcs = (kc0, kc1, kc2)
    vcs = (vc0, vc1, vc2)
    outs, lses = [], []
    for g in range(N_GROUPS):
        hs = slice(g * HEADS, (g + 1) * HEADS)
        q = q_ref[0, hs, :]
        kn = kn_ref[0, hs, :]
        vn = vn_ref[0, hs, :]
        kc = kcs[g][0]
        vc = vcs[g][0]
        s_old = jnp.sum(kc * q[None], axis=-1, keepdims=True) + bdec_ref[g]
        s_new = jnp.sum(kn * q, axis=-1, keepdims=True) + bnew_ref[g]
        m = jnp.maximum(jnp.max(s_old, axis=0), s_new)
        p_old = jnp.exp(s_old - m[None])
        p_new = jnp.exp(s_new - m)
        l = jnp.sum(p_old, axis=0) + p_new
        outs.append((jnp.sum(p_old * vc, axis=0) + p_new * vn) / l)
        lses.append(m + jnp.log(l))
    mx = jnp.maximum(jnp.maximum(lses[0], lses[1]), lses[2])
    es = [jnp.exp(t - mx) for t in lses]
    den = es[0] + es[1] + es[2]
    oa_ref[0] = (es[0] / den) * outs[0] + (es[1] / den) * outs[1] + (es[2] / den) * outs[2]
    u = u_ref[0]
    stored = jnp.sum(pool_ref[0] * pw_ref[0:POOL_STATE, :], axis=0, keepdims=True)
    pooled_ref[0] = stored + u * pw_ref[POOL_STATE:POOL_STATE + 1, :] - u


def _decode_call(q, kn, vn, u, pool_prev, k_caches, v_caches, bdec, bnew, pw):
    db = pool_prev.shape[0]
    per_b = lambda t: pl.BlockSpec((1,) + t.shape[1:], lambda b: (b,) + (0,) * (t.ndim - 1))
    const = lambda t: pl.BlockSpec(t.shape, lambda b: (0,) * t.ndim)
    cache_specs, cache_args = [], []
    for g, (_, dil) in enumerate(ATTN_GROUPS):
        for c in (k_caches[g], v_caches[g]):
            rows = c.shape[1]
            cache_args.append(c.reshape(db, rows // dil, dil, HEADS, HEAD_DIM)[:, :, 0])
            cache_specs.append(pl.BlockSpec((1, ATTN_STEPS, HEADS, HEAD_DIM), lambda b: (b, 0, 0, 0)))
    u3 = u.reshape(db, 1, u.shape[-1])
    return pl.pallas_call(
        _decode_body,
        grid=(db,),
        in_specs=[per_b(q), per_b(kn), per_b(vn), per_b(u3), per_b(pool_prev)] + cache_specs +
                 [const(bdec), const(bnew), const(pw)],
        out_specs=[pl.BlockSpec((1, HEADS, HEAD_DIM), lambda b: (b, 0, 0)),
                   pl.BlockSpec((1, 1, GROUP_W), lambda b: (b, 0, 0))],
        out_shape=[jax.ShapeDtypeStruct((db, HEADS, HEAD_DIM), F32),
                   jax.ShapeDtypeStruct((db, 1, GROUP_W), F32)],
        compiler_params=_params(("arbitrary",)),
        name="decode_mix",
    )(q, kn, vn, u3, pool_prev, *cache_args, bdec, bnew, pw)


def _merge_tail(x, oa, pooled, gates, gate_msa, shift2, scale2, g2, wa, wb, wo, pw_ref, ps,
                x1_ref, h2_ref):
    mixed = jnp.concatenate(
        [jnp.dot(pooled[:, j * POOL_GROUP:(j + 1) * POOL_GROUP].astype(BF16), pw_ref[j],
                 preferred_element_type=F32) for j in range(len(POOL_WINDOWS))], axis=1) * ps
    d = x.shape[-1]
    ga = gates[:, :d].astype(F32)
    gb = gates[:, d:].astype(F32)
    merged = (_sigmoid(ga) * jnp.dot(oa.astype(BF16), wa, preferred_element_type=F32) +
              _sigmoid(gb) * jnp.dot(mixed.astype(BF16), wb, preferred_element_type=F32))
    y = jnp.dot(merged.astype(BF16), wo, preferred_element_type=F32)
    x1 = x + gate_msa * y
    x1_ref[0] = x1
    ms = jnp.mean(x1 * x1, axis=-1, keepdims=True)
    h2 = x1 * lax.rsqrt(ms + EPS) * g2
    _store_rows(h2_ref.at[0], h2 * (1.0 + scale2) + shift2)


def _merge_prompt_body(x_ref, gt_ref, oa_ref, u_ref, uh_ref,
                       gm_ref, sh_ref, sc_ref, g2_ref, wa_ref, wb_ref, wo_ref, pw_ref, ps_ref,
                       x1_ref, h2_ref):
    i = pl.program_id(1)
    tm = x_ref.shape[1]
    oa = oa_ref[0]
    u = u_ref[0]
    halo = jnp.where(i == 0, 0.0, uh_ref[0])
    ext = jnp.concatenate([halo, u], axis=0)
    hw = halo.shape[0]
    acc = ext
    parts = []
    for j, w in enumerate(POOL_WINDOWS):
        acc = acc[:, POOL_GROUP * (1 if j else 0):]
        acc = acc + pltpu.roll(acc, w // 2, axis=0)
        parts.append(acc[hw:, :POOL_GROUP])
    tot = jnp.concatenate(parts, axis=1)
    lane = lax.broadcasted_iota(jnp.int32, (tm, GROUP_W), 1)
    wcol = jnp.left_shift(2, lane // POOL_GROUP)
    pos = i * tm + lax.broadcasted_iota(jnp.int32, (tm, GROUP_W), 0)
    cnt = jnp.minimum(pos + 1, wcol).astype(F32)
    pooled = tot / cnt - u
    _merge_tail(x_ref[0], oa, pooled, gt_ref[0], gm_ref[0], sh_ref[0], sc_ref[0], g2_ref[...],
                wa_ref[...], wb_ref[...], wo_ref[...], pw_ref, ps_ref[...], x1_ref, h2_ref)


def _merge_sample_body(x_ref, gt_ref, oa_ref, pooled_ref,
                       gm_ref, sh_ref, sc_ref, g2_ref, wa_ref, wb_ref, wo_ref, pw_ref, ps_ref,
                       x1_ref, h2_ref):
    _merge_tail(x_ref[0], oa_ref[0], pooled_ref[0], gt_ref[0], gm_ref[0], sh_ref[0], sc_ref[0], g2_ref[...],
                wa_ref[...], wb_ref[...], wo_ref[...], pw_ref, ps_ref[...], x1_ref, h2_ref)


def _merge_call(x, gates, mixer_inputs, mods, g2, wa, wb, wo, pw, ps, prompt):
    bsz, s, d = x.shape
    assert d == ROW_SLABS * LANES
    tm = min(ROW_TILE, s)
    row = lambda t: pl.BlockSpec((1, tm, t.shape[-1]), lambda b, i: (b, i, 0))
    const = lambda t: pl.BlockSpec(t.shape, lambda b, i: (0,) * t.ndim)
    gm, sh, sc = mods
    in_specs = [row(x), row(gates)]
    args = [x, gates]
    if prompt:
        oa, u = mixer_inputs
        hw = 16
        in_specs += [row(oa), row(u),
                     pl.BlockSpec((1, hw, u.shape[-1]), lambda b, i: (b, jnp.maximum(i * (tm // hw) - 1, 0), 0))]
        args += [oa, u, u]
        body = _merge_prompt_body
    else:
        in_specs += [row(t) for t in mixer_inputs]
        args += list(mixer_inputs)
        body = _merge_sample_body
    in_specs += [_mod_spec(m, tm, d) for m in (gm, sh, sc)]
    in_specs += [const(t) for t in (g2, wa, wb, wo, pw, ps)]
    args += [gm, sh, sc, g2, wa, wb, wo, pw, ps]
    return pl.pallas_call(
        body,
        grid=(bsz, s // tm),
        in_specs=in_specs,
        out_specs=[row(x), pl.BlockSpec((1, tm * ROW_SLABS, LANES), lambda b, i: (b, i, 0))],
        out_shape=[jax.ShapeDtypeStruct((bsz, s, d), F32),
                   jax.ShapeDtypeStruct((bsz, s * ROW_SLABS, LANES), F32)],
        compiler_params=_params(("arbitrary", "arbitrary")),
        name="merge_prompt" if prompt else "merge_sample",
    )(*args)


def _router_body(h_ref, rw_ref, rb_ref, tri_ref, idx_ref, wt_ref, rank_ref, cnt_ref, carry):
    tm = h_ref.shape[0] // ROW_SLABS

    @pl.when(pl.program_id(0) == 0)
    def _():
        carry[...] = jnp.zeros_like(carry)

    ne = rw_ref.shape[1]
    gsz = ne // N_EXPERT_GROUPS
    logits = jnp.dot(_load_rows(h_ref, tm), rw_ref[...], precision=HIGHEST, preferred_element_type=F32)
    scores = _sigmoid(logits)
    sel = scores + rb_ref[...]
    lane = lax.broadcasted_iota(jnp.int32, (tm, ne), 1)
    lanef = lane.astype(F32)
    grp = lane // gsz
    ninf = -jnp.inf
    far = float(ne)

    def first_max(x):
        m = jnp.max(x, axis=-1, keepdims=True)
        at = jnp.min(jnp.where(x == m, lanef, far), axis=-1, keepdims=True)
        return m, at

    gs = []
    for g in range(N_EXPERT_GROUPS):
        xg = jnp.where(grp == g, sel, ninf)
        m1, a1 = first_max(xg)
        m2 = jnp.max(jnp.where(lanef == a1, ninf, xg), axis=-1, keepdims=True)
        gs.append(m1 + m2)
    allowed = jnp.zeros((tm, ne), jnp.bool_)
    for g in range(N_EXPERT_GROUPS):
        ahead = jnp.zeros((tm, 1), F32)
        for o in range(N_EXPERT_GROUPS):
            if o == g:
                continue
            beats = (gs[o] > gs[g]) | ((gs[o] == gs[g]) & (o < g))
            ahead = ahead + beats.astype(F32)
        allowed = allowed | ((grp == g) & (ahead < TOPK_GROUPS))
    masked = jnp.where(allowed, sel, ninf)
    lane_o = lax.broadcasted_iota(jnp.int32, (tm, LANES), 1)
    idx_t = jnp.zeros((tm, LANES), F32)
    wt_t = jnp.zeros((tm, LANES), F32)
    wsum = jnp.zeros((tm, 1), F32)
    picked = jnp.zeros((tm, ne), F32)
    ats = []
    for k in range(TOP_K):
        _, at = first_max(masked)
        hit = lanef == at
        wk = jnp.sum(jnp.where(hit, scores, 0.0), axis=-1, keepdims=True)
        masked = jnp.where(hit, ninf, masked)
        picked = picked + hit.astype(F32)
        ats.append(at)
        idx_t = jnp.where(lane_o == k, at, idx_t)
        wt_t = jnp.where(lane_o == k, wk, wt_t)
        wsum = wsum + wk
    idx_ref[...] = idx_t.astype(jnp.int32)
    wt_ref[...] = wt_t / wsum * ROUTED_SCALE
    before = jnp.dot(tri_ref[...], picked.astype(BF16), preferred_element_type=F32) + carry[...]
    rank_t = jnp.zeros((tm, LANES), F32)
    for k in range(TOP_K):
        rk = jnp.sum(jnp.where(lanef == ats[k], before, 0.0), axis=-1, keepdims=True)
        rank_t = jnp.where(lane_o == k, rk, rank_t)
    rank_ref[...] = rank_t.astype(jnp.int32)
    carry[...] = carry[...] + jnp.sum(picked, axis=0, keepdims=True)
    cnt_ref[...] = carry[...].astype(jnp.int32)


def _router_call(h2, rw, rb):
    nt = h2.shape[0] // ROW_SLABS
    tm = ROW_TILE
    d, ne = rw.shape
    tri = jnp.asarray(np.tril(np.ones((tm, tm), np.float32), -1), BF16)
    tile = pl.BlockSpec((tm, LANES), lambda i: (i, 0))
    return pl.pallas_call(
        _router_body,
        grid=(nt // tm,),
        in_specs=[pl.BlockSpec((tm * ROW_SLABS, LANES), lambda i: (i, 0)),
                  pl.BlockSpec((d, ne), lambda i: (0, 0)),
                  pl.BlockSpec((1, ne), lambda i: (0, 0)),
                  pl.BlockSpec((tm, tm), lambda i: (0, 0))],
        out_specs=[tile, tile, tile, pl.BlockSpec((1, ne), lambda i: (0, 0))],
        out_shape=[jax.ShapeDtypeStruct((nt, LANES), jnp.int32),
                   jax.ShapeDtypeStruct((nt, LANES), F32),
                   jax.ShapeDtypeStruct((nt, LANES), jnp.int32),
                   jax.ShapeDtypeStruct((1, ne), jnp.int32)],
        scratch_shapes=[pltpu.VMEM((1, ne), F32)],
        compiler_params=_params(("arbitrary",)),
        name="router",
    )(h2, rw, rb.reshape(1, ne), tri)


def _slot_body(idx_ref, rank_ref, offs_ref, pos_ref):
    tm = idx_ref.shape[0]
    ne = offs_ref.shape[1]
    idx = idx_ref[...]
    rank = rank_ref[...]
    lane = lax.broadcasted_iota(jnp.int32, (tm, ne), 1)
    lane_o = lax.broadcasted_iota(jnp.int32, (tm, LANES), 1)
    offs = offs_ref[...]
    pos = jnp.zeros((tm, LANES), F32)
    for k in range(TOP_K):
        start = jnp.sum(jnp.where(lane == idx[:, k:k + 1], offs, 0.0), axis=-1, keepdims=True)
        pos = jnp.where(lane_o == k, start, pos)
    pos_ref[...] = pos.astype(jnp.int32) + rank


def _slot_call(idx, rank, offs):
    nt = idx.shape[0]
    tm = ROW_TILE
    ne = offs.shape[1]
    tile = pl.BlockSpec((tm, LANES), lambda i: (i, 0))
    return pl.pallas_call(
        _slot_body,
        grid=(nt // tm,),
        in_specs=[tile, tile, pl.BlockSpec((1, ne), lambda i: (0, 0))],
        out_specs=tile,
        out_shape=jax.ShapeDtypeStruct((nt, LANES), jnp.int32),
        compiler_params=_params(("arbitrary",)),
        name="slots",
    )(idx, rank, offs)


def _row_tile(ref, r):
    return ref.at[pl.ds(pl.multiple_of(r * ROW_SLABS, ROW_SLABS), ROW_SLABS), :]


def _dispatch_body(fill_ref, pos_ref, h_ref, xs_hbm, zeros, zsem, tsem, sem):
    td = h_ref.shape[0] // ROW_SLABS
    chunk = zeros.shape[0]
    n_ent = fill_ref.shape[0] - 1
    n_chunks = xs_hbm.shape[0] // chunk

    def fill_chunk(c, sem_, wait):
        cp = pltpu.make_async_copy(zeros, xs_hbm.at[pl.ds(pl.multiple_of(c * chunk, chunk), chunk), :], sem_)
        if wait:
            cp.wait()
        else:
            cp.start()

    def fill_tail(wait):
        lax.fori_loop(fill_ref[n_ent], n_chunks, lambda b, c: (fill_chunk(b, tsem, wait), c)[1], 0)

    @pl.when(pl.program_id(0) == 0)
    def _():
        zeros[...] = jnp.zeros_like(zeros)

        def fill(j, wait):
            @pl.when(fill_ref[j] >= 0)
            def _():
                fill_chunk(fill_ref[j], zsem, wait)
        fill_tail(False)
        for wait in (False, True):
            lax.fori_loop(0, n_ent, lambda j, c, wait=wait: (fill(j, wait), c)[1], 0)

    @pl.when(pl.program_id(0) == pl.num_programs(0) - 1)
    def _():
        fill_tail(True)

    def one(t, carry):
        for k in range(TOP_K):
            pltpu.make_async_copy(_row_tile(h_ref, t), _row_tile(xs_hbm, pos_ref[0, 0, t * TOP_K + k]),
                                  sem).start(priority=k % 2)
        return carry
    lax.fori_loop(0, td, one, 0, unroll=2)
    for k in range(TOP_K):
        pltpu.make_async_copy(h_ref, xs_hbm.at[pl.ds(0, td * ROW_SLABS), :], sem).wait()


def _dispatch_call(fill_chunks, pos_tok, h2, n_blocks):
    nt = h2.shape[0] // ROW_SLABS
    td = ROW_TILE
    te = EXPERT_TILE
    grid_spec = pltpu.PrefetchScalarGridSpec(
        num_scalar_prefetch=1,
        grid=(nt // td,),
        in_specs=[pl.BlockSpec((1, 1, td * TOP_K), lambda i, lb: (i, 0, 0), memory_space=pltpu.SMEM),
                  pl.BlockSpec((td * ROW_SLABS, LANES), lambda i, lb: (i, 0))],
        out_specs=pl.BlockSpec(memory_space=pl.ANY),
        scratch_shapes=[pltpu.VMEM((FILL_ROWS * ROW_SLABS, LANES), F32), pltpu.SemaphoreType.DMA,
                        pltpu.SemaphoreType.DMA, pltpu.SemaphoreType.DMA],
    )
    return pl.pallas_call(
        _dispatch_body,
        grid_spec=grid_spec,
        out_shape=jax.ShapeDtypeStruct((n_blocks * te * ROW_SLABS, LANES), F32),
        compiler_params=_params(("arbitrary",)),
        name="dispatch",
    )(fill_chunks, pos_tok, h2)


def _expert_body(exp_ref, nu_ref, xs_hbm, wg_ref, wu_ref, wd_ref, y_ref, xbuf, xsem, wgb, wub, wdb):
    b = pl.program_id(0)
    n_used = nu_ref[0]
    live = b < n_used
    new_expert = jnp.logical_or(b == 0, exp_ref[b] != exp_ref[jnp.maximum(b - 1, 0)])
    ring, rows = xbuf.shape[0], xbuf.shape[1]

    def fetch(blk):
        slot = blk % ring
        return pltpu.make_async_copy(xs_hbm.at[pl.ds(pl.multiple_of(blk * rows, rows), rows), :],
                                     xbuf.at[slot], xsem.at[slot])

    @pl.when(b == 0)
    def _():
        for j in range(ring - 1):
            @pl.when(j < n_used)
            def _():
                fetch(j).start()

    @pl.when(b + ring - 1 < n_used)
    def _():
        fetch(b + ring - 1).start()

    @pl.when(jnp.logical_and(live, new_expert))
    def _():
        wgb[...] = wg_ref[0].astype(BF16)
        wub[...] = wu_ref[0].astype(BF16)
        wdb[...] = wd_ref[0].astype(BF16)

    @pl.when(live)
    def _():
        te = rows // ROW_SLABS
        fetch(b).wait()
        x = _load_rows(xbuf.at[b % ring], te).astype(BF16)
        gt = jnp.dot(x, wgb[...], preferred_element_type=F32)
        up = jnp.dot(x, wub[...], preferred_element_type=F32)
        act = (gt * _sigmoid(gt)) * up
        _store_rows(y_ref, jnp.dot(act.astype(BF16), wdb[...], preferred_element_type=F32))

    @pl.when(jnp.logical_not(live))
    def _():
        y_ref[...] = jnp.zeros_like(y_ref)


def _expert_call(blk_exp, n_used, xs, wg, wu, wd):
    n_blocks = blk_exp.shape[0]
    te = EXPERT_TILE
    d, hid = wg.shape[-2:]
    used = lambda b, nu: jnp.minimum(b, nu[0] - 1)
    wspec = lambda r, c: pl.BlockSpec((1, r, c), lambda b, e, nu: (e[used(b, nu)], 0, 0))
    grid_spec = pltpu.PrefetchScalarGridSpec(
        num_scalar_prefetch=2,
        grid=(n_blocks,),
        in_specs=[pl.BlockSpec(memory_space=pl.ANY), wspec(d, hid), wspec(d, hid), wspec(hid, d)],
        out_specs=pl.BlockSpec((te * ROW_SLABS, LANES), lambda b, e, nu: (b, 0)),
        scratch_shapes=[pltpu.VMEM((EXPERT_RING, te * ROW_SLABS, LANES), F32), pltpu.SemaphoreType.DMA((EXPERT_RING,)),
                        pltpu.VMEM((d, hid), BF16), pltpu.VMEM((d, hid), BF16), pltpu.VMEM((hid, d), BF16)],
    )
    return pl.pallas_call(
        _expert_body,
        grid_spec=grid_spec,
        out_shape=jax.ShapeDtypeStruct(xs.shape, F32),
        compiler_params=_params(("arbitrary",)),
        name="expert_ffn",
    )(blk_exp, n_used, xs, wg, wu, wd)


def _gather_rows(idx_ref, src_hbm, dst, sem, n):
    def pair(i, carry):
        for p in range(2):
            j = 2 * i + p
            pltpu.make_async_copy(_row_tile(src_hbm, idx_ref[0, 0, j]), _row_tile(dst, j), sem).start(priority=p)
        return carry
    lax.fori_loop(0, n // 2, pair, 0, unroll=4)


def _combine_body(pos_ref, posn_ref, y_hbm, wt_ref, h_ref, x1_ref, gate_ref, sg_ref, su_ref, sd_ref,
                  out_ref, ybuf, sem):
    nb, ni = pl.num_programs(0), pl.num_programs(1)
    step = pl.program_id(0) * ni + pl.program_id(1)
    slot = step % 2
    tc = h_ref.shape[0] // ROW_SLABS
    rows = tc * TOP_K

    @pl.when(step == 0)
    def _():
        _gather_rows(pos_ref, y_hbm, ybuf.at[0], sem.at[0], rows)

    @pl.when(step + 1 < nb * ni)
    def _():
        _gather_rows(posn_ref, y_hbm, ybuf.at[1 - slot], sem.at[1 - slot], rows)

    pltpu.make_async_copy(y_hbm.at[pl.ds(0, rows * ROW_SLABS), :], ybuf.at[slot], sem.at[slot]).wait()
    wt = wt_ref[...]
    routed = jnp.zeros(out_ref.shape[1:], F32)
    for k in range(TOP_K):
        routed = routed + wt[:, k:k + 1] * _load_rows(ybuf.at[slot], tc, first=k * tc)
    hb = _load_rows(h_ref, tc).astype(BF16)
    gt = jnp.dot(hb, sg_ref[...], preferred_element_type=F32)
    up = jnp.dot(hb, su_ref[...], preferred_element_type=F32)
    shared = jnp.dot(((gt * _sigmoid(gt)) * up).astype(BF16), sd_ref[...], preferred_element_type=F32)
    out_ref[0] = x1_ref[0] + gate_ref[0] * (routed + shared)


def _combine_call(pos_tiles, tile_off, y_sorted, wts, h2, x1, gate, sg, su, sd):
    bsz, s, d = x1.shape
    tc = min(COMBINE_TILE, s)
    ni = s // tc
    n_tiles = pos_tiles.shape[0]
    lin = lambda b, i: tile_off + b * ni + i
    const = lambda t: pl.BlockSpec(t.shape, lambda b, i: (0,) * t.ndim)
    return pl.pallas_call(
        _combine_body,
        grid=(bsz, ni),
        in_specs=[
            pl.BlockSpec((1, 1, tc * TOP_K), lambda b, i: (lin(b, i), 0, 0), memory_space=pltpu.SMEM),
            pl.BlockSpec((1, 1, tc * TOP_K), lambda b, i: (jnp.minimum(lin(b, i) + 1, n_tiles - 1), 0, 0),
                         memory_space=pltpu.SMEM),
            pl.BlockSpec(memory_space=pl.ANY),
            pl.BlockSpec((tc, LANES), lambda b, i: (lin(b, i), 0)),
            pl.BlockSpec((tc * ROW_SLABS, LANES), lambda b, i: (lin(b, i), 0)),
            pl.BlockSpec((1, tc, d), lambda b, i: (b, i, 0)),
            _mod_spec(gate, tc, d),
            const(sg), const(su), const(sd),
        ],
        out_specs=pl.BlockSpec((1, tc, d), lambda b, i: (b, i, 0)),
        out_shape=jax.ShapeDtypeStruct((bsz, s, d), F32),
        scratch_shapes=[pltpu.VMEM((2, tc * TOP_K * ROW_SLABS, LANES), F32), pltpu.SemaphoreType.DMA((2,))],
        compiler_params=_params(("arbitrary", "arbitrary")),
        name="combine",
    )(pos_tiles, pos_tiles, y_sorted, wts, h2, x1, gate, sg, su, sd)


def _t5_bucket(dist):
    exact = N_BUCKETS // 2
    d = np.asarray(dist)
    large = exact + (np.log(np.maximum(d, 1) / exact) / np.log(REL_MAX_DIST / exact) * (N_BUCKETS - exact)).astype(np.int32)
    large = np.minimum(large, N_BUCKETS - 1)
    return np.where(d < exact, d, large).astype(np.int32)


def _group_bias(rel_bias, gi):
    window, dil = ATTN_GROUPS[gi]
    bkt = _t5_bucket(np.arange(window // dil + 1) * dil)
    onehot = jnp.asarray(bkt[:, None] == np.arange(N_BUCKETS)[None, :], F32)
    cols = rel_bias[:, gi * HEADS:(gi + 1) * HEADS].astype(F32)
    return jnp.dot(onehot, cols, precision=HIGHEST).T


def _prompt_bias_table(bias_j):
    blk = ATTN_BLOCK
    n = bias_j.shape[1] - 1
    period = 3 * blk
    row0 = jnp.concatenate([jnp.flip(bias_j, axis=1), jnp.full((HEADS, period - n - 1), NEG, F32)], axis=1)
    flat = jnp.tile(row0, (1, blk))[:, :blk * (period - 1)]
    tab = flat.reshape(HEADS, blk, period - 1)[:, :, :2 * blk]
    return tab.reshape(HEADS // HEADS_PER_SLAB, HEADS_PER_SLAB * blk, 2 * blk)


def _decode_bias_tables(bias_j):
    old = jnp.flip(bias_j[:, 1:], axis=1).T[:, :, None]
    new = bias_j[:, 0][:, None]
    return old, new


def _block_schedule(counts, na):
    te = EXPERT_TILE
    n_blocks = na // te + N_EXPERTS
    pcounts = (counts + te - 1) // te * te
    pends = jnp.cumsum(pcounts)
    offs = pends - pcounts
    blk_start = jnp.arange(n_blocks, dtype=jnp.int32) * te
    blk_exp = jnp.minimum(jnp.sum(pends[None, :] <= blk_start[:, None], axis=1), N_EXPERTS - 1).astype(jnp.int32)
    n_used = (pends[-1] // te).astype(jnp.int32).reshape(1)
    fr = FILL_ROWS
    fill_start = offs + counts // fr * fr
    n_fill = (pends - fill_start) // fr
    first = fill_start // fr
    chunks = jnp.concatenate([jnp.where(n_fill > j, first + j, -1) for j in range(te // fr)]
                             + [n_used * (te // fr)]).astype(jnp.int32)
    return offs, blk_exp, n_used, chunks, n_blocks


def kernel(x_prompt, x_sample, cache_k_w128, cache_v_w128, cache_k_w512, cache_v_w512, cache_k_w2048, cache_v_w2048, state_pool, c_prompt, c_sample, ada_w, ada_b, norm1, norm2, w_in, q_gain, k_gain, rel_bias, pool_w, pool_scale, w_br_a, w_br_b, w_out, router_w, router_bias, exp_w_gate, exp_w_up, exp_w_down, sh_w_gate, sh_w_up, sh_w_down):
    depth = ada_w.shape[0]
    bsz, seq, d = x_prompt.shape
    db = x_sample.shape[0]
    assert x_sample.shape[1] == 1 and db <= SAMPLE_PAD and seq % (ATTN_BLOCK * ATTN_GROUPS[-1][1]) == 0
    k_caches = (cache_k_w128, cache_k_w512, cache_k_w2048)
    v_caches = (cache_v_w128, cache_v_w512, cache_v_w2048)
    for c, (w, dil) in zip(k_caches, ATTN_GROUPS):
        assert c.shape[2] == w == ATTN_STEPS * dil

    bias_js = [_group_bias(rel_bias, g) for g in range(N_GROUPS)]
    bias_tabs = jnp.stack([_prompt_bias_table(b) for b in bias_js])
    dec = [_decode_bias_tables(b) for b in bias_js]
    bdec = jnp.stack([t[0] for t in dec])
    bnew = jnp.stack([t[1] for t in dec])
    bd = jnp.asarray((np.arange(MXU_DIM)[:, None] // HEAD_DIM == np.arange(MXU_DIM)[None, :] // HEAD_DIM)
                     / HEAD_DIM, BF16)
    wcols = np.repeat(np.asarray(POOL_WINDOWS), POOL_GROUP)
    pw_dec = jnp.asarray(np.where(np.arange(POOL_STATE + 1)[:, None] >= POOL_STATE + 1 - wcols[None, :],
                                  1.0 / wcols[None, :], 0.0), F32)

    xp = x_prompt
    xs = jnp.pad(x_sample.reshape(db, d), ((0, SAMPLE_PAD - db), (0, 0))).reshape(1, SAMPLE_PAD, d)
    n_prompt = bsz * seq
    outs = {name: [] for name in ('pk', 'pv', 'sk', 'sv')}
    outs['pk'] = [[] for _ in range(N_GROUPS)]
    outs['pv'] = [[] for _ in range(N_GROUPS)]
    outs['sk'] = [[] for _ in range(N_GROUPS)]
    outs['sv'] = [[] for _ in range(N_GROUPS)]
    ppool, spool = [], []
    for l in range(depth):
        c_all = jnp.concatenate([c_prompt, c_sample], axis=0)
        mod = _ada_call(c_all, ada_w[l], ada_b[l])
        mod_p = [m.reshape(bsz, 1, d) for m in jnp.split(mod[:bsz], 6, axis=-1)]
        mod_s = [jnp.pad(m, ((0, SAMPLE_PAD - db), (0, 0))).reshape(1, SAMPLE_PAD, d)
                 for m in jnp.split(mod[bsz:], 6, axis=-1)]
        w_in_bf = w_in[l].astype(BF16)
        qg = (jnp.tile(q_gain[l], (1, HEADS)) * (HEAD_DIM ** -0.5)).reshape(1, -1)
        kg = jnp.tile(k_gain[l], (1, HEADS)).reshape(1, -1)
        g1 = norm1[l].reshape(1, d)
        g2 = norm2[l].reshape(1, d)
        wa, wb, wo = (w_br_a[l].astype(BF16), w_br_b[l].astype(BF16), w_out[l].astype(BF16))
        pw = pool_w[l].astype(BF16)
        ps = pool_scale[l].reshape(1, -1)

        q, k, v, u, gates = _inproj_call(xp, mod_p[0], mod_p[1], g1, w_in_bf, qg, kg, bd)
        oa_p = _attn_call(q, k, v, bias_tabs)
        for g, (window, dil) in enumerate(ATTN_GROUPS):
            keep = min(window, seq)
            cs = slice(g * GROUP_W, (g + 1) * GROUP_W)
            outs['pk'][g].append(k[:, seq - keep:, cs].reshape(bsz, keep, HEADS, HEAD_DIM))
            outs['pv'][g].append(v[:, seq - keep:, cs].reshape(bsz, keep, HEADS, HEAD_DIM))
        ppool.append(u[:, seq - POOL_STATE:])
        x1p, h2p = _merge_call(xp, gates, (oa_p, u), (mod_p[2], mod_p[3], mod_p[4]),
                               g2, wa, wb, wo, pw, ps, prompt=True)

        qs, ks, vs, us, gates_s = _inproj_call(xs, mod_s[0], mod_s[1], g1, w_in_bf, qg, kg, bd)
        heads3 = lambda t: t[0, :db].reshape(db, N_GROUPS * HEADS, HEAD_DIM)
        q3, k3, v3 = heads3(qs), heads3(ks), heads3(vs)
        kcl = [c[l] for c in k_caches]
        vcl = [c[l] for c in v_caches]
        oa_s, pooled_s = _decode_call(q3, k3, v3, us[0, :db], state_pool[l], kcl, vcl, bdec, bnew, pw_dec)
        padrows = lambda t: jnp.pad(t.reshape(db, GROUP_W), ((0, SAMPLE_PAD - db), (0, 0))).reshape(1, SAMPLE_PAD, GROUP_W)
        x1s, h2s = _merge_call(xs, gates_s, (padrows(oa_s), padrows(pooled_s)), (mod_s[2], mod_s[3], mod_s[4]),
                               g2, wa, wb, wo, pw, ps, prompt=False)
        for g in range(N_GROUPS):
            hs = slice(g * HEADS, (g + 1) * HEADS)
            outs['sk'][g].append(jnp.concatenate([kcl[g][:, 1:], k3[:, None, hs]], axis=1))
            outs['sv'][g].append(jnp.concatenate([vcl[g][:, 1:], v3[:, None, hs]], axis=1))
        spool.append(jnp.concatenate([state_pool[l][:, 1:], us[0, :db, None, :]], axis=1))

        h2 = jnp.concatenate([h2p.reshape(n_prompt * ROW_SLABS, LANES), h2s.reshape(SAMPLE_PAD * ROW_SLABS, LANES)], axis=0)
        nt = n_prompt + SAMPLE_PAD
        idx, wts, rank, counts = _router_call(h2, router_w[l], router_bias[l])
        offs, blk_exp, n_used, fill_chunks, n_blocks = _block_schedule(counts[0], nt * TOP_K)
        pos = _slot_call(idx, rank, offs.astype(F32).reshape(1, -1))[:, :TOP_K]
        xs_sorted = _dispatch_call(fill_chunks, pos.reshape(nt // ROW_TILE, 1, ROW_TILE * TOP_K), h2, n_blocks)
        y_sorted = _expert_call(blk_exp, n_used, xs_sorted, exp_w_gate[l], exp_w_up[l], exp_w_down[l])
        tc = COMBINE_TILE
        pos_tiles = pos.reshape(nt // tc, tc, TOP_K).transpose(0, 2, 1).reshape(nt // tc, 1, TOP_K * tc)
        sg, su, sd = sh_w_gate[l].astype(BF16), sh_w_up[l].astype(BF16), sh_w_down[l].astype(BF16)
        xp = _combine_call(pos_tiles, 0, y_sorted, wts, h2, x1p, mod_p[5], sg, su, sd)
        xs = _combine_call(pos_tiles, n_prompt // tc, y_sorted, wts, h2, x1s, mod_s[5], sg, su, sd)

    stack = lambda parts: jnp.stack(parts)
    res = [xp, xs[0, :db].reshape(db, 1, d)]
    for g in range(N_GROUPS):
        res += [stack(outs['pk'][g]), stack(outs['pv'][g])]
    res.append(stack(ppool))
    for g in range(N_GROUPS):
        res += [stack(outs['sk'][g]), stack(outs['sv'][g])]
    res.append(stack(spool))
    return tuple(res)
```

```python
import numpy as np
import jax
import jax.numpy as jnp
from jax import lax
from jax.experimental import pallas as pl
from jax.experimental.pallas import tpu as pltpu

F32 = jnp.float32
BF16 = jnp.bfloat16
HIGHEST = lax.Precision.HIGHEST

HEAD_DIM = 64
HEADS = 8
GROUP_W = HEADS * HEAD_DIM
ATTN_GROUPS = ((128, 1), (512, 4), (2048, 16))
N_GROUPS = len(ATTN_GROUPS)
ATTN_STEPS = 128
ATTN_BLOCK = 128
POOL_WINDOWS = (2, 4, 8, 16)
POOL_GROUP = 128
POOL_STATE = 15
N_BUCKETS = 32
REL_MAX_DIST = 2048
N_EXPERTS = 256
TOP_K = 8
N_EXPERT_GROUPS = 8
TOPK_GROUPS = 4
ROUTED_SCALE = 2.5
EPS = 1e-6
NEG = -1e30

LANES = 128
MXU_DIM = 256
VMEM_LIMIT = 56 * 1024 * 1024

INPROJ_TILE = 512
ROW_TILE = 256
EXPERT_TILE = 512
FILL_ROWS = 256
EXPERT_RING = 3
COMBINE_TILE = 128
SAMPLE_PAD = 256


def _sigmoid(x):
    return 1.0 / (1.0 + jnp.exp(-x))


def _params(sem):
    return pltpu.CompilerParams(dimension_semantics=sem, vmem_limit_bytes=VMEM_LIMIT)


ROW_SLABS = 8


def _load_rows(ref, n, first=0):
    return jnp.concatenate([ref[pl.ds(first * ROW_SLABS + s, n, stride=ROW_SLABS), :]
                            for s in range(ROW_SLABS)], axis=1)


def _store_rows(ref, val):
    n = val.shape[0]
    for s in range(ROW_SLABS):
        ref[pl.ds(s, n, stride=ROW_SLABS), :] = val[:, s * LANES:(s + 1) * LANES]


def _ada_body(c_ref, w_ref, b_ref, o_ref):
    c = c_ref[...]
    s = c * _sigmoid(c)
    o_ref[...] = jnp.dot(s, w_ref[...], precision=HIGHEST, preferred_element_type=F32) + b_ref[...]


def _ada_call(c, w, b):
    n, d = c.shape
    cols = w.shape[1]
    tn = 512
    return pl.pallas_call(
        _ada_body,
        grid=(cols // tn,),
        in_specs=[pl.BlockSpec((n, d), lambda j: (0, 0)),
                  pl.BlockSpec((d, tn), lambda j: (0, j)),
                  pl.BlockSpec((1, tn), lambda j: (0, j))],
        out_specs=pl.BlockSpec((n, tn), lambda j: (0, j)),
        out_shape=jax.ShapeDtypeStruct((n, cols), F32),
        compiler_params=_params(("arbitrary",)),
        name="adaln",
    )(c, w, b.reshape(1, cols))


def _inproj_body(x_ref, shift_ref, scale_ref, g_ref, w_ref, qg_ref, kg_ref, bd_ref,
                 q_ref, k_ref, v_ref, u_ref, gt_ref):
    x = x_ref[0]
    ms = jnp.mean(x * x, axis=-1, keepdims=True)
    h = x * lax.rsqrt(ms + EPS) * g_ref[...]
    h = h * (1.0 + scale_ref[0]) + shift_ref[0]
    hb = h.astype(BF16)
    n_chunks = w_ref.shape[1] // GROUP_W
    for c in range(n_chunks):
        z = jnp.dot(hb, w_ref[:, c * GROUP_W:(c + 1) * GROUP_W], preferred_element_type=F32)
        if c < 2 * N_GROUPS:
            zz = (z * z).astype(BF16)
            hms = jnp.concatenate(
                [jnp.dot(zz[:, s:s + MXU_DIM], bd_ref[...], preferred_element_type=F32)
                 for s in range(0, GROUP_W, MXU_DIM)], axis=1)
            g = c % N_GROUPS
            gain = (qg_ref if c < N_GROUPS else kg_ref)[:, g * GROUP_W:(g + 1) * GROUP_W]
            y = z * lax.rsqrt(hms + EPS) * gain
            if c < N_GROUPS:
                q_ref[0, :, g * GROUP_W:(g + 1) * GROUP_W] = y
            else:
                k_ref[0, :, g * GROUP_W:(g + 1) * GROUP_W] = y
        elif c < 3 * N_GROUPS:
            g = c - 2 * N_GROUPS
            v_ref[0, :, g * GROUP_W:(g + 1) * GROUP_W] = z
        elif c == 3 * N_GROUPS:
            u_ref[0] = z
        else:
            g = c - 3 * N_GROUPS - 1
            gt_ref[0, :, g * GROUP_W:(g + 1) * GROUP_W] = z.astype(BF16)


def _mod_spec(mod, tm, d):
    if mod.shape[1] == 1:
        return pl.BlockSpec((1, 1, d), lambda b, i: (b, 0, 0))
    return pl.BlockSpec((1, tm, d), lambda b, i: (b, i, 0))


def _inproj_call(x, shift, scale, g1, w_bf, qg, kg, bd):
    bsz, s, d = x.shape
    tm = min(INPROJ_TILE, s)
    qkv = N_GROUPS * GROUP_W
    n_gate = w_bf.shape[1] - 3 * qkv - GROUP_W
    const = lambda b, i: (0, 0)
    row = lambda width: pl.BlockSpec((1, tm, width), lambda b, i: (b, i, 0))
    return pl.pallas_call(
        _inproj_body,
        grid=(bsz, s // tm),
        in_specs=[row(d), _mod_spec(shift, tm, d), _mod_spec(scale, tm, d),
                  pl.BlockSpec((1, d), const),
                  pl.BlockSpec(w_bf.shape, const, pipeline_mode=pl.Buffered(1)),
                  pl.BlockSpec((1, qkv), const), pl.BlockSpec((1, qkv), const),
                  pl.BlockSpec((MXU_DIM, MXU_DIM), const)],
        out_specs=[row(qkv), row(qkv), row(qkv), row(GROUP_W), row(n_gate)],
        out_shape=[jax.ShapeDtypeStruct((bsz, s, qkv), F32),
                   jax.ShapeDtypeStruct((bsz, s, qkv), F32),
                   jax.ShapeDtypeStruct((bsz, s, qkv), F32),
                   jax.ShapeDtypeStruct((bsz, s, GROUP_W), F32),
                   jax.ShapeDtypeStruct((bsz, s, n_gate), BF16)],
        compiler_params=_params(("arbitrary", "arbitrary")),
        name="inproj",
    )(x, shift, scale, g1, w_bf, qg, kg, bd)


ATTN_ROWS = ATTN_BLOCK * ATTN_GROUPS[-1][1]
HEADS_PER_SLAB = LANES // HEAD_DIM


def _attn_body(*refs):
    ng = N_GROUPS
    qkv = refs[:3 * ng]
    bias_ref, o_ref = refs[3 * ng], refs[3 * ng + 1]
    scr = refs[3 * ng + 2:]
    first = pl.program_id(2) == 0
    blk = ATTN_BLOCK
    pb = ATTN_ROWS
    hs = HEADS_PER_SLAB
    lane = lax.broadcasted_iota(jnp.int32, (blk, LANES), 1)
    col = lax.broadcasted_iota(jnp.int32, (hs * blk, 2 * blk), 1)
    for g, (_, dil) in enumerate(ATTN_GROUPS):
        q_ref, k_ref, v_ref = qkv[3 * g:3 * g + 3]
        kext, vext, og, lg = scr[4 * g:4 * g + 4]
        span = blk * dil
        bias = bias_ref[g, 0]

        @pl.when(first)
        def _():
            kext[0:span, :] = jnp.zeros((span, LANES), F32)
            vext[0:span, :] = jnp.zeros((span, LANES), F32)
        kext[span:span + pb, :] = k_ref[0]
        vext[span:span + pb, :] = v_ref[0]

        rows = lambda start: pl.ds(start, blk, stride=dil) if dil > 1 else pl.ds(pl.multiple_of(start, blk), blk)
        per = 4

        def units(i, carry):
            bases, v2s, scores = [], [], []
            for j in range(per):
                u = per * i + j
                qb = u // dil
                base = qb * span + (u % dil)
                q = q_ref[0, rows(base), :]
                k2 = jnp.concatenate([kext[rows(base), :], kext[rows(base + span), :]], axis=0).astype(BF16)
                v2s.append(jnp.concatenate([vext[rows(base), :], vext[rows(base + span), :]], axis=0).astype(BF16))
                qs = jnp.concatenate([jnp.where(lane // HEAD_DIM == h, q, 0.0) for h in range(hs)],
                                     axis=0).astype(BF16)
                s = lax.dot_general(qs, k2, (((1,), (1,)), ((), ())), preferred_element_type=F32) + bias
                scores.append(jnp.where(jnp.logical_and(jnp.logical_and(first, qb == 0), col < blk), NEG, s))
                bases.append(base)
            s = jnp.concatenate(scores, axis=0)
            m = jnp.max(s, axis=-1, keepdims=True)
            p = jnp.exp(s - m)
            l = jnp.sum(p, axis=-1, keepdims=True)
            pb16 = p.astype(BF16)
            lse = m + jnp.log(l)
            for j in range(per):
                r0 = j * hs * blk
                o2 = jnp.dot(pb16[r0:r0 + hs * blk], v2s[j], preferred_element_type=F32) / l[r0:r0 + hs * blk]
                o = o2[0:blk]
                lw = jnp.broadcast_to(lse[r0:r0 + blk], (blk, LANES))
                for h in range(1, hs):
                    o = jnp.where(lane // HEAD_DIM == h, o2[h * blk:(h + 1) * blk], o)
                    lw = jnp.where(lane // HEAD_DIM == h, lse[r0 + h * blk:r0 + (h + 1) * blk], lw)
                og[rows(bases[j]), :] = o
                lg[rows(bases[j]), :] = lw
            return carry
        lax.fori_loop(0, pb // blk // per, units, 0)
        kext[0:span, :] = kext[pb:pb + span, :]
        vext[0:span, :] = vext[pb:pb + span, :]

    def merge(c, carry):
        rs = pl.ds(pl.multiple_of(c * blk, blk), blk)
        ls = [scr[4 * g + 3][rs, :] for g in range(ng)]
        mx = jnp.maximum(jnp.maximum(ls[0], ls[1]), ls[2])
        es = [jnp.exp(t - mx) for t in ls]
        den = es[0] + es[1] + es[2]
        acc = (es[0] / den) * scr[2][rs, :]
        for g in range(1, ng):
            acc = acc + (es[g] / den) * scr[4 * g + 2][rs, :]
        o_ref[0, rs, :] = acc.astype(o_ref.dtype)
        return carry
    lax.fori_loop(0, pb // blk, merge, 0)


def _attn_call(q, k, v, bias_tabs):
    bsz, s, qkv = q.shape
    pb = ATTN_ROWS
    n_slab = GROUP_W // LANES
    in_specs, args = [], []
    for g in range(N_GROUPS):
        spec = pl.BlockSpec((1, pb, LANES), lambda b, c, i, g=g: (b, i, g * n_slab + c))
        in_specs += [spec, spec, spec]
        args += [q, k, v]
    in_specs.append(pl.BlockSpec((N_GROUPS, 1) + bias_tabs.shape[2:], lambda b, c, i: (0, c, 0, 0)))
    scratch = []
    for _, dil in ATTN_GROUPS:
        ext = pltpu.VMEM((ATTN_BLOCK * dil + pb, LANES), F32)
        scratch += [ext, ext, pltpu.VMEM((pb, LANES), F32), pltpu.VMEM((pb, LANES), F32)]
    return pl.pallas_call(
        _attn_body,
        grid=(bsz, n_slab, s // pb),
        in_specs=in_specs,
        out_specs=pl.BlockSpec((1, pb, LANES), lambda b, c, i: (b, i, c)),
        out_shape=jax.ShapeDtypeStruct((bsz, s, GROUP_W), BF16),
        scratch_shapes=scratch,
        compiler_params=_params(("arbitrary", "arbitrary", "arbitrary")),
        name="attn",
    )(*args, bias_tabs)


def _decode_body(q_ref, kn_ref, vn_ref, u_ref, pool_ref, kc0, vc0, kc1, vc1, kc2, vc2,
                 bdec_ref, bnew_ref, pw_ref, oa_ref, pooled_ref):
    kcs = (kc0, kc1, kc2)
    vcs = (vc0, vc1, vc2)
    outs, lses = [], []
    for g in range(N_GROUPS):
        hs = slice(g * HEADS, (g + 1) * HEADS)
        q = q_ref[0, hs, :]
        kn = kn_ref[0, hs, :]
        vn = vn_ref[0, hs, :]
        kc = kcs[g][0]
        vc = vcs[g][0]
        s_old = jnp.sum(kc * q[None], axis=-1, keepdims=True) + bdec_ref[g]
        s_new = jnp.sum(kn * q, axis=-1, keepdims=True) + bnew_ref[g]
        m = jnp.maximum(jnp.max(s_old, axis=0), s_new)
        p_old = jnp.exp(s_old - m[None])
        p_new = jnp.exp(s_new - m)
        l = jnp.sum(p_old, axis=0) + p_new
        outs.append((jnp.sum(p_old * vc, axis=0) + p_new * vn) / l)
        lses.append(m + jnp.log(l))
    mx = jnp.maximum(jnp.maximum(lses[0], lses[1]), lses[2])
    es = [jnp.exp(t - mx) for t in lses]
    den = es[0] + es[1] + es[2]
    oa_ref[0] = (es[0] / den) * outs[0] + (es[1] / den) * outs[1] + (es[2] / den) * outs[2]
    u = u_ref[0]
    stored = jnp.sum(pool_ref[0] * pw_ref[0:POOL_STATE, :], axis=0, keepdims=True)
    pooled_ref[0] = stored + u * pw_ref[POOL_STATE:POOL_STATE + 1, :] - u


def _decode_call(q, kn, vn, u, pool_prev, k_caches, v_caches, bdec, bnew, pw):
    db = pool_prev.shape[0]
    per_b = lambda t: pl.BlockSpec((1,) + t.shape[1:], lambda b: (b,) + (0,) * (t.ndim - 1))
    const = lambda t: pl.BlockSpec(t.shape, lambda b: (0,) * t.ndim)
    cache_specs, cache_args = [], []
    for g, (_, dil) in enumerate(ATTN_GROUPS):
        for c in (k_caches[g], v_caches[g]):
            rows = c.shape[1]
            cache_args.append(c.reshape(db, rows // dil, dil, HEADS, HEAD_DIM)[:, :, 0])
            cache_specs.append(pl.BlockSpec((1, ATTN_STEPS, HEADS, HEAD_DIM), lambda b: (b, 0, 0, 0)))
    u3 = u.reshape(db, 1, u.shape[-1])
    return pl.pallas_call(
        _decode_body,
        grid=(db,),
        in_specs=[per_b(q), per_b(kn), per_b(vn), per_b(u3), per_b(pool_prev)] + cache_specs +
                 [const(bdec), const(bnew), const(pw)],
        out_specs=[pl.BlockSpec((1, HEADS, HEAD_DIM), lambda b: (b, 0, 0)),
                   pl.BlockSpec((1, 1, GROUP_W), lambda b: (b, 0, 0))],
        out_shape=[jax.ShapeDtypeStruct((db, HEADS, HEAD_DIM), F32),
                   jax.ShapeDtypeStruct((db, 1, GROUP_W), F32)],
        compiler_params=_params(("arbitrary",)),
        name="decode_mix",
    )(q, kn, vn, u3, pool_prev, *cache_args, bdec, bnew, pw)


def _merge_tail(x, oa, pooled, gates, gate_msa, shift2, scale2, g2, wa, wb, wo, pw_ref, ps,
                x1_ref, h2_ref):
    mixed = jnp.concatenate(
        [jnp.dot(pooled[:, j * POOL_GROUP:(j + 1) * POOL_GROUP].astype(BF16), pw_ref[j],
                 preferred_element_type=F32) for j in range(len(POOL_WINDOWS))], axis=1) * ps
    d = x.shape[-1]
    ga = gates[:, :d].astype(F32)
    gb = gates[:, d:].astype(F32)
    merged = (_sigmoid(ga) * jnp.dot(oa.astype(BF16), wa, preferred_element_type=F32) +
              _sigmoid(gb) * jnp.dot(mixed.astype(BF16), wb, preferred_element_type=F32))
    y = jnp.dot(merged.astype(BF16), wo, preferred_element_type=F32)
    x1 = x + gate_msa * y
    x1_ref[0] = x1
    ms = jnp.mean(x1 * x1, axis=-1, keepdims=True)
    h2 = x1 * lax.rsqrt(ms + EPS) * g2
    _store_rows(h2_ref.at[0], h2 * (1.0 + scale2) + shift2)


def _merge_prompt_body(x_ref, gt_ref, oa_ref, u_ref, uh_ref,
                       gm_ref, sh_ref, sc_ref, g2_ref, wa_ref, wb_ref, wo_ref, pw_ref, ps_ref,
                       x1_ref, h2_ref):
    i = pl.program_id(1)
    tm = x_ref.shape[1]
    oa = oa_ref[0]
    u = u_ref[0]
    halo = jnp.where(i == 0, 0.0, uh_ref[0])
    ext = jnp.concatenate([halo, u], axis=0)
    hw = halo.shape[0]
    acc = ext
    parts = []
    for j, w in enumerate(POOL_WINDOWS):
        acc = acc[:, POOL_GROUP * (1 if j else 0):]
        acc = acc + pltpu.roll(acc, w // 2, axis=0)
        parts.append(acc[hw:, :POOL_GROUP])
    tot = jnp.concatenate(parts, axis=1)
    lane = lax.broadcasted_iota(jnp.int32, (tm, GROUP_W), 1)
    wcol = jnp.left_shift(2, lane // POOL_GROUP)
    pos = i * tm + lax.broadcasted_iota(jnp.int32, (tm, GROUP_W), 0)
    cnt = jnp.minimum(pos + 1, wcol).astype(F32)
    pooled = tot / cnt - u
    _merge_tail(x_ref[0], oa, pooled, gt_ref[0], gm_ref[0], sh_ref[0], sc_ref[0], g2_ref[...],
                wa_ref[...], wb_ref[...], wo_ref[...], pw_ref, ps_ref[...], x1_ref, h2_ref)


def _merge_sample_body(x_ref, gt_ref, oa_ref, pooled_ref,
                       gm_ref, sh_ref, sc_ref, g2_ref, wa_ref, wb_ref, wo_ref, pw_ref, ps_ref,
                       x1_ref, h2_ref):
    _merge_tail(x_ref[0], oa_ref[0], pooled_ref[0], gt_ref[0], gm_ref[0], sh_ref[0], sc_ref[0], g2_ref[...],
                wa_ref[...], wb_ref[...], wo_ref[...], pw_ref, ps_ref[...], x1_ref, h2_ref)


def _merge_call(x, gates, mixer_inputs, mods, g2, wa, wb, wo, pw, ps, prompt):
    bsz, s, d = x.shape
    assert d == ROW_SLABS * LANES
    tm = min(ROW_TILE, s)
    row = lambda t: pl.BlockSpec((1, tm, t.shape[-1]), lambda b, i: (b, i, 0))
    const = lambda t: pl.BlockSpec(t.shape, lambda b, i: (0,) * t.ndim)
    gm, sh, sc = mods
    in_specs = [row(x), row(gates)]
    args = [x, gates]
    if prompt:
        oa, u = mixer_inputs
        hw = 16
        in_specs += [row(oa), row(u),
                     pl.BlockSpec((1, hw, u.shape[-1]), lambda b, i: (b, jnp.maximum(i * (tm // hw) - 1, 0), 0))]
        args += [oa, u, u]
        body = _merge_prompt_body
    else:
        in_specs += [row(t) for t in mixer_inputs]
        args += list(mixer_inputs)
        body = _merge_sample_body
    in_specs += [_mod_spec(m, tm, d) for m in (gm, sh, sc)]
    in_specs += [const(t) for t in (g2, wa, wb, wo, pw, ps)]
    args += [gm, sh, sc, g2, wa, wb, wo, pw, ps]
    return pl.pallas_call(
        body,
        grid=(bsz, s // tm),
        in_specs=in_specs,
        out_specs=[row(x), pl.BlockSpec((1, tm * ROW_SLABS, LANES), lambda b, i: (b, i, 0))],
        out_shape=[jax.ShapeDtypeStruct((bsz, s, d), F32),
                   jax.ShapeDtypeStruct((bsz, s * ROW_SLABS, LANES), F32)],
        compiler_params=_params(("arbitrary", "arbitrary")),
        name="merge_prompt" if prompt else "merge_sample",
    )(*args)


def _router_body(h_ref, rw_ref, rb_ref, tri_ref, idx_ref, wt_ref, rank_ref, cnt_ref, carry):
    tm = h_ref.shape[0] // ROW_SLABS

    @pl.when(pl.program_id(0) == 0)
    def _():
        carry[...] = jnp.zeros_like(carry)

    ne = rw_ref.shape[1]
    gsz = ne // N_EXPERT_GROUPS
    logits = jnp.dot(_load_rows(h_ref, tm), rw_ref[...], precision=HIGHEST, preferred_element_type=F32)
    scores = _sigmoid(logits)
    sel = scores + rb_ref[...]
    lane = lax.broadcasted_iota(jnp.int32, (tm, ne), 1)
    lanef = lane.astype(F32)
    grp = lane // gsz
    ninf = -jnp.inf
    far = float(ne)

    def first_max(x):
        m = jnp.max(x, axis=-1, keepdims=True)
        at = jnp.min(jnp.where(x == m, lanef, far), axis=-1, keepdims=True)
        return m, at

    gs = []
    for g in range(N_EXPERT_GROUPS):
        xg = jnp.where(grp == g, sel, ninf)
        m1, a1 = first_max(xg)
        m2 = jnp.max(jnp.where(lanef == a1, ninf, xg), axis=-1, keepdims=True)
        gs.append(m1 + m2)
    allowed = jnp.zeros((tm, ne), jnp.bool_)
    for g in range(N_EXPERT_GROUPS):
        ahead = jnp.zeros((tm, 1), F32)
        for o in range(N_EXPERT_GROUPS):
            if o == g:
                continue
            beats = (gs[o] > gs[g]) | ((gs[o] == gs[g]) & (o < g))
            ahead = ahead + beats.astype(F32)
        allowed = allowed | ((grp == g) & (ahead < TOPK_GROUPS))
    masked = jnp.where(allowed, sel, ninf)
    lane_o = lax.broadcasted_iota(jnp.int32, (tm, LANES), 1)
    idx_t = jnp.zeros((tm, LANES), F32)
    wt_t = jnp.zeros((tm, LANES), F32)
    wsum = jnp.zeros((tm, 1), F32)
    picked = jnp.zeros((tm, ne), F32)
    ats = []
    for k in range(TOP_K):
        _, at = first_max(masked)
        hit = lanef == at
        wk = jnp.sum(jnp.where(hit, scores, 0.0), axis=-1, keepdims=True)
        masked = jnp.where(hit, ninf, masked)
        picked = picked + hit.astype(F32)
        ats.append(at)
        idx_t = jnp.where(lane_o == k, at, idx_t)
        wt_t = jnp.where(lane_o == k, wk, wt_t)
        wsum = wsum + wk
    idx_ref[...] = idx_t.astype(jnp.int32)
    wt_ref[...] = wt_t / wsum * ROUTED_SCALE
    before = jnp.dot(tri_ref[...], picked.astype(BF16), preferred_element_type=F32) + carry[...]
    rank_t = jnp.zeros((tm, LANES), F32)
    for k in range(TOP_K):
        rk = jnp.sum(jnp.where(lanef == ats[k], before, 0.0), axis=-1, keepdims=True)
        rank_t = jnp.where(lane_o == k, rk, rank_t)
    rank_ref[...] = rank_t.astype(jnp.int32)
    carry[...] = carry[...] + jnp.sum(picked, axis=0, keepdims=True)
    cnt_ref[...] = carry[...].astype(jnp.int32)


def _router_call(h2, rw, rb):
    nt = h2.shape[0] // ROW_SLABS
    tm = ROW_TILE
    d, ne = rw.shape
    tri = jnp.asarray(np.tril(np.ones((tm, tm), np.float32), -1), BF16)
    tile = pl.BlockSpec((tm, LANES), lambda i: (i, 0))
    return pl.pallas_call(
        _router_body,
        grid=(nt // tm,),
        in_specs=[pl.BlockSpec((tm * ROW_SLABS, LANES), lambda i: (i, 0)),
                  pl.BlockSpec((d, ne), lambda i: (0, 0)),
                  pl.BlockSpec((1, ne), lambda i: (0, 0)),
                  pl.BlockSpec((tm, tm), lambda i: (0, 0))],
        out_specs=[tile, tile, tile, pl.BlockSpec((1, ne), lambda i: (0, 0))],
        out_shape=[jax.ShapeDtypeStruct((nt, LANES), jnp.int32),
                   jax.ShapeDtypeStruct((nt, LANES), F32),
                   jax.ShapeDtypeStruct((nt, LANES), jnp.int32),
                   jax.ShapeDtypeStruct((1, ne), jnp.int32)],
        scratch_shapes=[pltpu.VMEM((1, ne), F32)],
        compiler_params=_params(("arbitrary",)),
        name="router",
    )(h2, rw, rb.reshape(1, ne), tri)


def _slot_body(idx_ref, rank_ref, offs_ref, pos_ref):
    tm = idx_ref.shape[0]
    ne = offs_ref.shape[1]
    idx = idx_ref[...]
    rank = rank_ref[...]
    lane = lax.broadcasted_iota(jnp.int32, (tm, ne), 1)
    lane_o = lax.broadcasted_iota(jnp.int32, (tm, LANES), 1)
    offs = offs_ref[...]
    pos = jnp.zeros((tm, LANES), F32)
    for k in range(TOP_K):
        start = jnp.sum(jnp.where(lane == idx[:, k:k + 1], offs, 0.0), axis=-1, keepdims=True)
        pos = jnp.where(lane_o == k, start, pos)
    pos_ref[...] = pos.astype(jnp.int32) + rank


def _slot_call(idx, rank, offs):
    nt = idx.shape[0]
    tm = ROW_TILE
    ne = offs.shape[1]
    tile = pl.BlockSpec((tm, LANES), lambda i: (i, 0))
    return pl.pallas_call(
        _slot_body,
        grid=(nt // tm,),
        in_specs=[tile, tile, pl.BlockSpec((1, ne), lambda i: (0, 0))],
        out_specs=tile,
        out_shape=jax.ShapeDtypeStruct((nt, LANES), jnp.int32),
        compiler_params=_params(("arbitrary",)),
        name="slots",
    )(idx, rank, offs)


def _row_tile(ref, r):
    return ref.at[pl.ds(pl.multiple_of(r * ROW_SLABS, ROW_SLABS), ROW_SLABS), :]


def _dispatch_body(fill_ref, pos_ref, h_ref, xs_hbm, zeros, zsem, tsem, sem):
    td = h_ref.shape[0] // ROW_SLABS
    chunk = zeros.shape[0]
    n_ent = fill_ref.shape[0] - 1
    n_chunks = xs_hbm.shape[0] // chunk

    def fill_chunk(c, sem_, wait):
        cp = pltpu.make_async_copy(zeros, xs_hbm.at[pl.ds(pl.multiple_of(c * chunk, chunk), chunk), :], sem_)
        if wait:
            cp.wait()
        else:
            cp.start()

    def fill_tail(wait):
        lax.fori_loop(fill_ref[n_ent], n_chunks, lambda b, c: (fill_chunk(b, tsem, wait), c)[1], 0)

    @pl.when(pl.program_id(0) == 0)
    def _():
        zeros[...] = jnp.zeros_like(zeros)

        def fill(j, wait):
            @pl.when(fill_ref[j] >= 0)
            def _():
                fill_chunk(fill_ref[j], zsem, wait)
        fill_tail(False)
        for wait in (False, True):
            lax.fori_loop(0, n_ent, lambda j, c, wait=wait: (fill(j, wait), c)[1], 0)

    @pl.when(pl.program_id(0) == pl.num_programs(0) - 1)
    def _():
        fill_tail(True)

    def one(t, carry):
        for k in range(TOP_K):
            pltpu.make_async_copy(_row_tile(h_ref, t), _row_tile(xs_hbm, pos_ref[0, 0, t * TOP_K + k]),
                                  sem).start(priority=k % 2)
        return carry
    lax.fori_loop(0, td, one, 0, unroll=2)
    for k in range(TOP_K):
        pltpu.make_async_copy(h_ref, xs_hbm.at[pl.ds(0, td * ROW_SLABS), :], sem).wait()


def _dispatch_call(fill_chunks, pos_tok, h2, n_blocks):
    nt = h2.shape[0] // ROW_SLABS
    td = ROW_TILE
    te = EXPERT_TILE
    grid_spec = pltpu.PrefetchScalarGridSpec(
        num_scalar_prefetch=1,
        grid=(nt // td,),
        in_specs=[pl.BlockSpec((1, 1, td * TOP_K), lambda i, lb: (i, 0, 0), memory_space=pltpu.SMEM),
                  pl.BlockSpec((td * ROW_SLABS, LANES), lambda i, lb: (i, 0))],
        out_specs=pl.BlockSpec(memory_space=pl.ANY),
        scratch_shapes=[pltpu.VMEM((FILL_ROWS * ROW_SLABS, LANES), F32), pltpu.SemaphoreType.DMA,
                        pltpu.SemaphoreType.DMA, pltpu.SemaphoreType.DMA],
    )
    return pl.pallas_call(
        _dispatch_body,
        grid_spec=grid_spec,
        out_shape=jax.ShapeDtypeStruct((n_blocks * te * ROW_SLABS, LANES), F32),
        compiler_params=_params(("arbitrary",)),
        name="dispatch",
    )(fill_chunks, pos_tok, h2)


def _expert_body(exp_ref, nxt_ref, par_ref, nu_ref, xs_hbm, wg_hbm, wu_hbm, wd_hbm, y_ref,
                 xbuf, xsem, wgf, wuf, wdf, wsem, wgb, wub, wdb):
    b = pl.program_id(0)
    n_used = nu_ref[0]

    def weights(e, slot, wait):
        for src, dst in ((wg_hbm, wgf), (wu_hbm, wuf), (wd_hbm, wdf)):
            cp = pltpu.make_async_copy(src.at[e], dst.at[slot], wsem.at[slot])
            if wait:
                cp.wait()
            else:
                cp.start()
    live = b < n_used
    new_expert = jnp.logical_or(b == 0, exp_ref[b] != exp_ref[jnp.maximum(b - 1, 0)])
    ring, rows = xbuf.shape[0], xbuf.shape[1]

    def fetch(blk):
        slot = blk % ring
        return pltpu.make_async_copy(xs_hbm.at[pl.ds(pl.multiple_of(blk * rows, rows), rows), :],
                                     xbuf.at[slot], xsem.at[slot])

    @pl.when(b == 0)
    def _():
        for j in range(ring - 1):
            @pl.when(j < n_used)
            def _():
                fetch(j).start()

    @pl.when(b + ring - 1 < n_used)
    def _():
        fetch(b + ring - 1).start()

    @pl.when(jnp.logical_and(live, new_expert))
    def _():
        slot = par_ref[b]

        @pl.when(b == 0)
        def _():
            weights(exp_ref[b], slot, wait=False)
        weights(exp_ref[b], slot, wait=True)

        @pl.when(nxt_ref[b] >= 0)
        def _():
            weights(nxt_ref[b], 1 - slot, wait=False)
        wgb[...] = wgf[slot].astype(BF16)
        wub[...] = wuf[slot].astype(BF16)
        wdb[...] = wdf[slot].astype(BF16)

    @pl.when(live)
    def _():
        te = rows // ROW_SLABS
        fetch(b).wait()
        x = _load_rows(xbuf.at[b % ring], te).astype(BF16)
        gt = jnp.dot(x, wgb[...], preferred_element_type=F32)
        up = jnp.dot(x, wub[...], preferred_element_type=F32)
        act = (gt * _sigmoid(gt)) * up
        _store_rows(y_ref, jnp.dot(act.astype(BF16), wdb[...], preferred_element_type=F32))

    @pl.when(jnp.logical_not(live))
    def _():
        y_ref[...] = jnp.zeros_like(y_ref)


def _expert_call(blk_exp, blk_next, blk_par, n_used, xs, wg, wu, wd):
    n_blocks = blk_exp.shape[0]
    te = EXPERT_TILE
    d, hid = wg.shape[-2:]
    any_spec = pl.BlockSpec(memory_space=pl.ANY)
    grid_spec = pltpu.PrefetchScalarGridSpec(
        num_scalar_prefetch=4,
        grid=(n_blocks,),
        in_specs=[any_spec, any_spec, any_spec, any_spec],
        out_specs=pl.BlockSpec((te * ROW_SLABS, LANES), lambda b, e, nx, pr, nu: (b, 0)),
        scratch_shapes=[pltpu.VMEM((EXPERT_RING, te * ROW_SLABS, LANES), F32), pltpu.SemaphoreType.DMA((EXPERT_RING,)),
                        pltpu.VMEM((2, d, hid), F32), pltpu.VMEM((2, d, hid), F32), pltpu.VMEM((2, hid, d), F32),
                        pltpu.SemaphoreType.DMA((2,)),
                        pltpu.VMEM((d, hid), BF16), pltpu.VMEM((d, hid), BF16), pltpu.VMEM((hid, d), BF16)],
    )
    return pl.pallas_call(
        _expert_body,
        grid_spec=grid_spec,
        out_shape=jax.ShapeDtypeStruct(xs.shape, F32),
        compiler_params=_params(("arbitrary",)),
        name="expert_ffn",
    )(blk_exp, blk_next, blk_par, n_used, xs, wg, wu, wd)


def _gather_rows(idx_ref, src_hbm, dst, sem, n):
    def pair(i, carry):
        for p in range(2):
            j = 2 * i + p
            pltpu.make_async_copy(_row_tile(src_hbm, idx_ref[0, 0, j]), _row_tile(dst, j), sem).start(priority=p)
        return carry
    lax.fori_loop(0, n // 2, pair, 0, unroll=4)


def _combine_body(pos_ref, posn_ref, y_hbm, wt_ref, h_ref, x1_ref, gate_ref, sg_ref, su_ref, sd_ref,
                  out_ref, ybuf, sem):
    nb, ni = pl.num_programs(0), pl.num_programs(1)
    step = pl.program_id(0) * ni + pl.program_id(1)
    slot = step % 2
    tc = h_ref.shape[0] // ROW_SLABS
    rows = tc * TOP_K

    @pl.when(step == 0)
    def _():
        _gather_rows(pos_ref, y_hbm, ybuf.at[0], sem.at[0], rows)

    @pl.when(step + 1 < nb * ni)
    def _():
        _gather_rows(posn_ref, y_hbm, ybuf.at[1 - slot], sem.at[1 - slot], rows)

    pltpu.make_async_copy(y_hbm.at[pl.ds(0, rows * ROW_SLABS), :], ybuf.at[slot], sem.at[slot]).wait()
    wt = wt_ref[...]
    routed = jnp.zeros(out_ref.shape[1:], F32)
    for k in range(TOP_K):
        routed = routed + wt[:, k:k + 1] * _load_rows(ybuf.at[slot], tc, first=k * tc)
    hb = _load_rows(h_ref, tc).astype(BF16)
    gt = jnp.dot(hb, sg_ref[...], preferred_element_type=F32)
    up = jnp.dot(hb, su_ref[...], preferred_element_type=F32)
    shared = jnp.dot(((gt * _sigmoid(gt)) * up).astype(BF16), sd_ref[...], preferred_element_type=F32)
    out_ref[0] = x1_ref[0] + gate_ref[0] * (routed + shared)


def _combine_call(pos_tiles, tile_off, y_sorted, wts, h2, x1, gate, sg, su, sd):
    bsz, s, d = x1.shape
    tc = min(COMBINE_TILE, s)
    ni = s // tc
    n_tiles = pos_tiles.shape[0]
    lin = lambda b, i: tile_off + b * ni + i
    const = lambda t: pl.BlockSpec(t.shape, lambda b, i: (0,) * t.ndim)
    return pl.pallas_call(
        _combine_body,
        grid=(bsz, ni),
        in_specs=[
            pl.BlockSpec((1, 1, tc * TOP_K), lambda b, i: (lin(b, i), 0, 0), memory_space=pltpu.SMEM),
            pl.BlockSpec((1, 1, tc * TOP_K), lambda b, i: (jnp.minimum(lin(b, i) + 1, n_tiles - 1), 0, 0),
                         memory_space=pltpu.SMEM),
            pl.BlockSpec(memory_space=pl.ANY),
            pl.BlockSpec((tc, LANES), lambda b, i: (lin(b, i), 0)),
            pl.BlockSpec((tc * ROW_SLABS, LANES), lambda b, i: (lin(b, i), 0)),
            pl.BlockSpec((1, tc, d), lambda b, i: (b, i, 0)),
            _mod_spec(gate, tc, d),
            const(sg), const(su), const(sd),
        ],
        out_specs=pl.BlockSpec((1, tc, d), lambda b, i: (b, i, 0)),
        out_shape=jax.ShapeDtypeStruct((bsz, s, d), F32),
        scratch_shapes=[pltpu.VMEM((2, tc * TOP_K * ROW_SLABS, LANES), F32), pltpu.SemaphoreType.DMA((2,))],
        compiler_params=_params(("arbitrary", "arbitrary")),
        name="combine",
    )(pos_tiles, pos_tiles, y_sorted, wts, h2, x1, gate, sg, su, sd)


def _t5_bucket(dist):
    exact = N_BUCKETS // 2
    d = np.asarray(dist)
    large = exact + (np.log(np.maximum(d, 1) / exact) / np.log(REL_MAX_DIST / exact) * (N_BUCKETS - exact)).astype(np.int32)
    large = np.minimum(large, N_BUCKETS - 1)
    return np.where(d < exact, d, large).astype(np.int32)


def _group_bias(rel_bias, gi):
    window, dil = ATTN_GROUPS[gi]
    bkt = _t5_bucket(np.arange(window // dil + 1) * dil)
    onehot = jnp.asarray(bkt[:, None] == np.arange(N_BUCKETS)[None, :], F32)
    cols = rel_bias[:, gi * HEADS:(gi + 1) * HEADS].astype(F32)
    return jnp.dot(onehot, cols, precision=HIGHEST).T


def _prompt_bias_table(bias_j):
    blk = ATTN_BLOCK
    n = bias_j.shape[1] - 1
    period = 3 * blk
    row0 = jnp.concatenate([jnp.flip(bias_j, axis=1), jnp.full((HEADS, period - n - 1), NEG, F32)], axis=1)
    flat = jnp.tile(row0, (1, blk))[:, :blk * (period - 1)]
    tab = flat.reshape(HEADS, blk, period - 1)[:, :, :2 * blk]
    return tab.reshape(HEADS // HEADS_PER_SLAB, HEADS_PER_SLAB * blk, 2 * blk)


def _decode_bias_tables(bias_j):
    old = jnp.flip(bias_j[:, 1:], axis=1).T[:, :, None]
    new = bias_j[:, 0][:, None]
    return old, new


def _block_schedule(counts, na):
    te = EXPERT_TILE
    n_blocks = na // te + N_EXPERTS
    pcounts = (counts + te - 1) // te * te
    pends = jnp.cumsum(pcounts)
    offs = pends - pcounts
    blk_start = jnp.arange(n_blocks, dtype=jnp.int32) * te
    blk_exp = jnp.minimum(jnp.sum(pends[None, :] <= blk_start[:, None], axis=1), N_EXPERTS - 1).astype(jnp.int32)
    n_used = (pends[-1] // te).astype(jnp.int32).reshape(1)
    fr = FILL_ROWS
    fill_start = offs + counts // fr * fr
    n_fill = (pends - fill_start) // fr
    first = fill_start // fr
    chunks = jnp.concatenate([jnp.where(n_fill > j, first + j, -1) for j in range(te // fr)]
                             + [n_used * (te // fr)]).astype(jnp.int32)
    ids = jnp.arange(N_EXPERTS, dtype=jnp.int32)
    later = jnp.flip(lax.cummin(jnp.flip(jnp.where(counts > 0, ids, N_EXPERTS))))
    nxt = jnp.concatenate([later[1:], jnp.full((1,), N_EXPERTS, later.dtype)])
    nxt = jnp.where(nxt < N_EXPERTS, nxt, -1).astype(jnp.int32)
    parity = ((jnp.cumsum(counts > 0) - 1) % 2).astype(jnp.int32)
    return offs, (blk_exp, nxt[blk_exp], parity[blk_exp]), n_used, chunks, n_blocks


def kernel(x_prompt, x_sample, cache_k_w128, cache_v_w128, cache_k_w512, cache_v_w512, cache_k_w2048, cache_v_w2048, state_pool, c_prompt, c_sample, ada_w, ada_b, norm1, norm2, w_in, q_gain, k_gain, rel_bias, pool_w, pool_scale, w_br_a, w_br_b, w_out, router_w, router_bias, exp_w_gate, exp_w_up, exp_w_down, sh_w_gate, sh_w_up, sh_w_down):
    depth = ada_w.shape[0]
    bsz, seq, d = x_prompt.shape
    db = x_sample.shape[0]
    assert x_sample.shape[1] == 1 and db <= SAMPLE_PAD and seq % (ATTN_BLOCK * ATTN_GROUPS[-1][1]) == 0
    k_caches = (cache_k_w128, cache_k_w512, cache_k_w2048)
    v_caches = (cache_v_w128, cache_v_w512, cache_v_w2048)
    for c, (w, dil) in zip(k_caches, ATTN_GROUPS):
        assert c.shape[2] == w == ATTN_STEPS * dil

    bias_js = [_group_bias(rel_bias, g) for g in range(N_GROUPS)]
    bias_tabs = jnp.stack([_prompt_bias_table(b) for b in bias_js])
    dec = [_decode_bias_tables(b) for b in bias_js]
    bdec = jnp.stack([t[0] for t in dec])
    bnew = jnp.stack([t[1] for t in dec])
    bd = jnp.asarray((np.arange(MXU_DIM)[:, None] // HEAD_DIM == np.arange(MXU_DIM)[None, :] // HEAD_DIM)
                     / HEAD_DIM, BF16)
    wcols = np.repeat(np.asarray(POOL_WINDOWS), POOL_GROUP)
    pw_dec = jnp.asarray(np.where(np.arange(POOL_STATE + 1)[:, None] >= POOL_STATE + 1 - wcols[None, :],
                                  1.0 / wcols[None, :], 0.0), F32)

    xp = x_prompt
    xs = jnp.pad(x_sample.reshape(db, d), ((0, SAMPLE_PAD - db), (0, 0))).reshape(1, SAMPLE_PAD, d)
    n_prompt = bsz * seq
    outs = {name: [] for name in ('pk', 'pv', 'sk', 'sv')}
    outs['pk'] = [[] for _ in range(N_GROUPS)]
    outs['pv'] = [[] for _ in range(N_GROUPS)]
    outs['sk'] = [[] for _ in range(N_GROUPS)]
    outs['sv'] = [[] for _ in range(N_GROUPS)]
    ppool, spool = [], []
    for l in range(depth):
        c_all = jnp.concatenate([c_prompt, c_sample], axis=0)
        mod = _ada_call(c_all, ada_w[l], ada_b[l])
        mod_p = [m.reshape(bsz, 1, d) for m in jnp.split(mod[:bsz], 6, axis=-1)]
        mod_s = [jnp.pad(m, ((0, SAMPLE_PAD - db), (0, 0))).reshape(1, SAMPLE_PAD, d)
                 for m in jnp.split(mod[bsz:], 6, axis=-1)]
        w_in_bf = w_in[l].astype(BF16)
        qg = (jnp.tile(q_gain[l], (1, HEADS)) * (HEAD_DIM ** -0.5)).reshape(1, -1)
        kg = jnp.tile(k_gain[l], (1, HEADS)).reshape(1, -1)
        g1 = norm1[l].reshape(1, d)
        g2 = norm2[l].reshape(1, d)
        wa, wb, wo = (w_br_a[l].astype(BF16), w_br_b[l].astype(BF16), w_out[l].astype(BF16))
        pw = pool_w[l].astype(BF16)
        ps = pool_scale[l].reshape(1, -1)

        q, k, v, u, gates = _inproj_call(xp, mod_p[0], mod_p[1], g1, w_in_bf, qg, kg, bd)
        oa_p = _attn_call(q, k, v, bias_tabs)
        for g, (window, dil) in enumerate(ATTN_GROUPS):
            keep = min(window, seq)
            cs = slice(g * GROUP_W, (g + 1) * GROUP_W)
            outs['pk'][g].append(k[:, seq - keep:, cs].reshape(bsz, keep, HEADS, HEAD_DIM))
            outs['pv'][g].append(v[:, seq - keep:, cs].reshape(bsz, keep, HEADS, HEAD_DIM))
        ppool.append(u[:, seq - POOL_STATE:])
        x1p, h2p = _merge_call(xp, gates, (oa_p, u), (mod_p[2], mod_p[3], mod_p[4]),
                               g2, wa, wb, wo, pw, ps, prompt=True)

        qs, ks, vs, us, gates_s = _inproj_call(xs, mod_s[0], mod_s[1], g1, w_in_bf, qg, kg, bd)
        heads3 = lambda t: t[0, :db].reshape(db, N_GROUPS * HEADS, HEAD_DIM)
        q3, k3, v3 = heads3(qs), heads3(ks), heads3(vs)
        kcl = [c[l] for c in k_caches]
        vcl = [c[l] for c in v_caches]
        oa_s, pooled_s = _decode_call(q3, k3, v3, us[0, :db], state_pool[l], kcl, vcl, bdec, bnew, pw_dec)
        padrows = lambda t: jnp.pad(t.reshape(db, GROUP_W), ((0, SAMPLE_PAD - db), (0, 0))).reshape(1, SAMPLE_PAD, GROUP_W)
        x1s, h2s = _merge_call(xs, gates_s, (padrows(oa_s), padrows(pooled_s)), (mod_s[2], mod_s[3], mod_s[4]),
                               g2, wa, wb, wo, pw, ps, prompt=False)
        for g in range(N_GROUPS):
            hs = slice(g * HEADS, (g + 1) * HEADS)
            outs['sk'][g].append(jnp.concatenate([kcl[g][:, 1:], k3[:, None, hs]], axis=1))
            outs['sv'][g].append(jnp.concatenate([vcl[g][:, 1:], v3[:, None, hs]], axis=1))
        spool.append(jnp.concatenate([state_pool[l][:, 1:], us[0, :db, None, :]], axis=1))

        h2 = jnp.concatenate([h2p.reshape(n_prompt * ROW_SLABS, LANES), h2s.reshape(SAMPLE_PAD * ROW_SLABS, LANES)], axis=0)
        nt = n_prompt + SAMPLE_PAD
        idx, wts, rank, counts = _router_call(h2, router_w[l], router_bias[l])
        offs, blk_meta, n_used, fill_chunks, n_blocks = _block_schedule(counts[0], nt * TOP_K)
        pos = _slot_call(idx, rank, offs.astype(F32).reshape(1, -1))[:, :TOP_K]
        xs_sorted = _dispatch_call(fill_chunks, pos.reshape(nt // ROW_TILE, 1, ROW_TILE * TOP_K), h2, n_blocks)
        y_sorted = _expert_call(*blk_meta, n_used, xs_sorted, exp_w_gate[l], exp_w_up[l], exp_w_down[l])
        tc = COMBINE_TILE
        pos_tiles = pos.reshape(nt // tc, tc, TOP_K).transpose(0, 2, 1).reshape(nt // tc, 1, TOP_K * tc)
        sg, su, sd = sh_w_gate[l].astype(BF16), sh_w_up[l].astype(BF16), sh_w_down[l].astype(BF16)
        xp = _combine_call(pos_tiles, 0, y_sorted, wts, h2, x1p, mod_p[5], sg, su, sd)
        xs = _combine_call(pos_tiles, n_prompt // tc, y_sorted, wts, h2, x1s, mod_s[5], sg, su, sd)

    stack = lambda parts: jnp.stack(parts)
    res = [xp, xs[0, :db].reshape(db, 1, d)]
    for g in range(N_GROUPS):
        res += [stack(outs['pk'][g]), stack(outs['pv'][g])]
    res.append(stack(ppool))
    for g in range(N_GROUPS):
        res += [stack(outs['sk'][g]), stack(outs['sv'][g])]
    res.append(stack(spool))
    return tuple(res)
```
